```python
import math
import jax
import jax.numpy as jnp
from jax import lax
import numpy as np

D_MODEL = 2048
BATCH = 2
SEQ = 4096
DEPTH = 2

GRID_W = 64
CTX_LEN = 256
N_HEADS = 8
N_KV_HEADS = 2
HEAD_DIM = 128
Q_BLOCK = 128
ROPE_THETA = 10000.0
SSM_WIDTH = 512
SSM_GROUP = 16
SSM_GROUPS = SSM_WIDTH // SSM_GROUP
SSM_STATE = 64
SSM_DT_MIN = 0.001
SSM_DT_MAX = 0.1
POOL_WIDTH = 512
POOL_WINDOWS = (2, 4, 8, 16)
POOL_GROUP = POOL_WIDTH // len(POOL_WINDOWS)
CONV_WIDTH = 512
CONV_TAPS = 31
N_BRANCHES = 4
N_EXPERTS = 64
N_EXPERT_GROUPS = 8
EXPERTS_PER_GROUP = N_EXPERTS // N_EXPERT_GROUPS
GROUP_SCORE_TOP = 2
TOP_K = 2
EXPERT_FF = 512
MOE_BLOCK = 128
EPS = 1e-6

Q_W = N_HEADS * HEAD_DIM
KV_W = N_KV_HEADS * HEAD_DIM
IN_WIDTHS = (KV_W, KV_W, SSM_WIDTH, Q_W, POOL_WIDTH, 2 * CONV_WIDTH, N_BRANCHES * D_MODEL)
IN_SPLITS = tuple(sum(IN_WIDTHS[:i + 1]) for i in range(len(IN_WIDTHS) - 1))
IN_W = sum(IN_WIDTHS)
CTX_STATE_W = 2 * KV_W + SSM_WIDTH

kernel_name = 'hybrid_gated_dit_moe_block'


def rms_norm(x, g):
    xf = x.astype(jnp.float32)
    y = xf * lax.rsqrt(jnp.mean(xf * xf, axis=-1, keepdims=True) + EPS)
    return (y * g.astype(jnp.float32)).astype(x.dtype)


def layer_norm(x, g, b):
    xf = x.astype(jnp.float32)
    xc = xf - jnp.mean(xf, axis=-1, keepdims=True)
    y = xc * lax.rsqrt(jnp.mean(xc * xc, axis=-1, keepdims=True) + EPS)
    return (y * g.astype(jnp.float32) + b.astype(jnp.float32)).astype(x.dtype)


def split_heads(t, n_heads):
    return t.reshape(t.shape[:-1] + (n_heads, HEAD_DIM))


def rope_axis(x, pos):
    half = x.shape[-1] // 2
    inv_freq = ROPE_THETA ** (-jnp.arange(half, dtype=jnp.float32) / half)
    ang = pos.astype(jnp.float32)[:, None] * inv_freq[None, :]
    cos = jnp.cos(ang)[None, :, None, :]
    sin = jnp.sin(ang)[None, :, None, :]
    xf = x.astype(jnp.float32)
    x1, x2 = xf[..., :half], xf[..., half:]
    return jnp.concatenate([x1 * cos - x2 * sin, x2 * cos + x1 * sin], axis=-1).astype(x.dtype)


def rope_2d(x, rows, cols):
    h = HEAD_DIM // 2
    return jnp.concatenate([rope_axis(x[..., :h], rows), rope_axis(x[..., h:], cols)], axis=-1)


def block_attention(q, k, v):
    b, l = q.shape[0], q.shape[1]
    n_blk = l // Q_BLOCK
    grp = N_HEADS // N_KV_HEADS
    qb = q.reshape(b, n_blk, Q_BLOCK, N_KV_HEADS, grp, HEAD_DIM).transpose(1, 0, 2, 3, 4, 5)
    scale = HEAD_DIM ** -0.5

    def one_block(q_blk):
        s = jnp.einsum('bqkgd,bskd->bkgqs', q_blk, k, preferred_element_type=jnp.float32) * scale
        p = jax.nn.softmax(s, axis=-1)
        return jnp.einsum('bkgqs,bskd->bqkgd', p.astype(v.dtype), v)

    o = lax.map(one_block, qb)
    return o.transpose(1, 0, 2, 3, 4, 5).reshape(b, l, Q_W)


def ssm_zoh(a_re, a_im, log_dt, b_re, b_im):
    f32 = jnp.float32
    lam = lax.complex(a_re.astype(f32), a_im.astype(f32))
    dt = jnp.exp(log_dt.astype(f32))[:, None]
    lam_bar = jnp.exp(lam * dt)
    b_mat = lax.complex(b_re.astype(f32), b_im.astype(f32))
    b_bar = ((lam_bar - 1.0) / lam)[..., None] * b_mat
    return lam_bar, b_bar


def ssm_scan(u, lam_bar, b_bar, h0, reverse):
    b, l = u.shape[0], u.shape[1]
    ug = u.astype(jnp.float32).reshape(b, l, SSM_GROUPS, SSM_GROUP).astype(jnp.complex64)
    bu = jnp.einsum('blgc,gpc->blgp', ug, b_bar)
    a = jnp.broadcast_to(lam_bar, bu.shape)

    def combine(e1, e2):
        a1, b1 = e1
        a2, b2 = e2
        return a1 * a2, a2 * b1 + b2

    a_cum, b_cum = lax.associative_scan(combine, (a, bu), axis=1, reverse=reverse)
    return a_cum * h0[:, None] + b_cum


def ssm_readout(states, c_mat):
    b, l = states.shape[0], states.shape[1]
    y = jnp.einsum('blgp,gcp->blgc', states, c_mat).real
    return y.reshape(b, l, SSM_WIDTH)


def ssm_mixer(u_ctx, u_lat, lp, with_ctx):
    f32 = jnp.float32
    d_skip = lp['ssm_d'].astype(f32)
    y_lat = d_skip * u_lat.astype(f32)
    y_ctx = d_skip * u_ctx.astype(f32) if with_ctx else None
    h_zero = jnp.zeros((u_lat.shape[0], SSM_GROUPS, SSM_STATE), jnp.complex64)
    for dr, rev in enumerate((False, True)):
        lam_bar, b_bar = ssm_zoh(lp['ssm_a_re'][dr], lp['ssm_a_im'][dr], lp['ssm_log_dt'][dr],
                                 lp['ssm_b_re'][dr], lp['ssm_b_im'][dr])
        c_mat = lax.complex(lp['ssm_c_re'][dr].astype(f32), lp['ssm_c_im'][dr].astype(f32))
        s_ctx = ssm_scan(u_ctx, lam_bar, b_bar, h_zero, rev)
        h_carry = s_ctx[:, 0] if rev else s_ctx[:, -1]
        s_lat = ssm_scan(u_lat, lam_bar, b_bar, h_carry, rev)
        y_lat = y_lat + ssm_readout(s_lat, c_mat)
        if with_ctx:
            y_ctx = y_ctx + ssm_readout(s_ctx, c_mat)

    def glu(y):
        y = jax.nn.gelu(y).astype(u_lat.dtype)
        return y * jax.nn.sigmoid(y @ lp['ssm_glu_w'] + lp['ssm_glu_b'])

    return (glu(y_ctx) if with_ctx else None), glu(y_lat)


def multiscale_pool(u, lp):
    b, l, _ = u.shape
    uf = u.astype(jnp.float32)
    cs = jnp.concatenate([jnp.zeros((b, 1, POOL_WIDTH), jnp.float32), jnp.cumsum(uf, axis=1)], axis=1)
    t = jnp.arange(l)
    outs = []
    for gi, w in enumerate(POOL_WINDOWS):
        sl = slice(gi * POOL_GROUP, (gi + 1) * POOL_GROUP)
        lo = jnp.clip(t - w // 2, 0, l)
        hi = jnp.clip(t - w // 2 + w, 0, l)
        cnt = (hi - lo).astype(jnp.float32)[None, :, None]
        seg = cs[..., sl]
        outs.append((seg[:, hi] - seg[:, lo]) / cnt - uf[..., sl])
    pooled = jnp.stack(outs, axis=2).astype(u.dtype)
    mixed = jnp.einsum('blgc,gcd->blgd', pooled, lp['pool_w'])
    return mixed.reshape(b, l, POOL_WIDTH) * lp['pool_scale']


def conformer_conv(u2, lp):
    a, g = jnp.split(u2, 2, axis=-1)
    u = a * jax.nn.sigmoid(g)
    w_dw = lp['conv_dw_w'][:, None, :].astype(u.dtype)
    y = lax.conv_general_dilated(u, w_dw, window_strides=(1,),
                                 padding=[(CONV_TAPS // 2, CONV_TAPS // 2)],
                                 dimension_numbers=('NWC', 'WIO', 'NWC'),
                                 feature_group_count=CONV_WIDTH)
    y = layer_norm(y + lp['conv_dw_b'], lp['conv_ln_g'], lp['conv_ln_b'])
    return jax.nn.silu(y)


def merge_branches(p_gate, o_attn, y_ssm, y_pool, y_conv, lp):
    g = jax.nn.sigmoid(p_gate.reshape(p_gate.shape[:-1] + (N_BRANCHES, D_MODEL)))
    m = (g[..., 0, :] * (o_attn @ lp['w_up_attn'])
         + g[..., 1, :] * (y_ssm @ lp['w_up_ssm'])
         + g[..., 2, :] * (y_pool @ lp['w_up_pool'])
         + g[..., 3, :] * (y_conv @ lp['w_up_conv']))
    return m @ lp['w_out']


def moe_ffn(t, router_w, router_b, w_gate, w_up, w_down):
    n, d = t.shape
    scores = jax.nn.sigmoid((t @ router_w).astype(jnp.float32))
    sel = (scores + router_b.astype(jnp.float32)).reshape(n, N_EXPERT_GROUPS, EXPERTS_PER_GROUP)
    grp_score = lax.top_k(sel, GROUP_SCORE_TOP)[0].sum(-1)
    grp = jnp.argmax(grp_score, axis=-1)
    in_grp = sel[jnp.arange(n), grp]
    _, loc = lax.top_k(in_grp, TOP_K)
    expert = grp[:, None] * EXPERTS_PER_GROUP + loc
    wts = jnp.take_along_axis(scores, expert, axis=-1)
    wts = wts / jnp.sum(wts, axis=-1, keepdims=True)

    n_assign = n * TOP_K
    flat_e = expert.reshape(-1)
    flat_tok = jnp.arange(n_assign, dtype=jnp.int32) // TOP_K
    flat_w = wts.reshape(-1)
    order = jnp.argsort(flat_e)
    sorted_e = flat_e[order]
    counts = jnp.bincount(flat_e, length=N_EXPERTS)
    start = jnp.cumsum(counts) - counts
    padded = (counts + MOE_BLOCK - 1) // MOE_BLOCK * MOE_BLOCK
    pad_end = jnp.cumsum(padded)
    pad_start = pad_end - padded
    dest = pad_start[sorted_e] + jnp.arange(n_assign) - start[sorted_e]
    n_blocks = -(-n_assign // MOE_BLOCK) + N_EXPERTS
    n_slots = n_blocks * MOE_BLOCK
    slot_tok = jnp.full((n_slots,), n, dtype=jnp.int32).at[dest].set(flat_tok[order])
    slot_w = jnp.zeros((n_slots,), jnp.float32).at[dest].set(flat_w[order])
    blk_e = jnp.minimum(jnp.searchsorted(pad_end, jnp.arange(n_blocks) * MOE_BLOCK, side='right'),
                        N_EXPERTS - 1)
    t_pad = jnp.concatenate([t, jnp.zeros((1, d), t.dtype)], axis=0)
    xs = t_pad[slot_tok].reshape(n_blocks, MOE_BLOCK, d)

    def expert_block(args):
        xb, e = args
        hdn = jax.nn.silu(xb @ w_gate[e]) * (xb @ w_up[e])
        return hdn @ w_down[e]

    ys = lax.map(expert_block, (xs, blk_e)).reshape(n_slots, d)
    out = jnp.zeros_like(t_pad).at[slot_tok].add(ys * slot_w[:, None].astype(ys.dtype))
    return out[:n]


def hybrid_layer(x, ctx, c, c_ctx, rows, cols, lp, router_w, router_b, with_ctx):
    b, l, d = x.shape
    mod = jax.nn.silu(c) @ lp['ada_w'] + lp['ada_b']
    mod_c = jax.nn.silu(c_ctx) @ lp['ada_w'] + lp['ada_b']
    sh1, sc1, g1, sh2, sc2, g2 = jnp.split(mod[:, None, :], 6, axis=-1)
    csh1, csc1, cg1, csh2, csc2, cg2 = jnp.split(mod_c, 6, axis=-1)

    h = rms_norm(x, lp['norm1_g']) * (1 + sc1) + sh1
    hc = rms_norm(ctx, lp['norm1_g']) * (1 + csc1) + csh1
    p = h @ lp['w_in']
    pc = hc @ (lp['w_in'] if with_ctx else lp['w_in'][:, :CTX_STATE_W])
    k, v, u_ssm, q, u_pool, u_conv, g_br = jnp.split(p, IN_SPLITS, axis=-1)
    kc, vc, uc_ssm = jnp.split(pc[..., :CTX_STATE_W], IN_SPLITS[:2], axis=-1)

    q = rope_2d(rms_norm(split_heads(q, N_HEADS), lp['q_norm_g']), rows, cols)
    k = rope_2d(rms_norm(split_heads(k, N_KV_HEADS), lp['k_norm_g']), rows, cols)
    v = split_heads(v, N_KV_HEADS)
    kc = rms_norm(split_heads(kc, N_KV_HEADS), lp['k_norm_g'])
    vc = split_heads(vc, N_KV_HEADS)
    o_attn = block_attention(q, jnp.concatenate([kc, k], axis=1), jnp.concatenate([vc, v], axis=1))
    ys_c, ys = ssm_mixer(uc_ssm, u_ssm, lp, with_ctx)
    mix = merge_branches(g_br, o_attn, ys, multiscale_pool(u_pool, lp), conformer_conv(u_conv, lp), lp)
    x = x + g1 * mix
    h2 = rms_norm(x, lp['norm2_g']) * (1 + sc2) + sh2

    if with_ctx:
        _, _, _, qc, uc_pool, uc_conv, gc_br = jnp.split(pc, IN_SPLITS, axis=-1)
        qc = rms_norm(split_heads(qc, N_HEADS), lp['q_norm_g'])
        oc = block_attention(qc, kc, vc)
        mix_c = merge_branches(gc_br, oc, ys_c, multiscale_pool(uc_pool, lp), conformer_conv(uc_conv, lp), lp)
        ctx = ctx + cg1 * mix_c
        h2c = rms_norm(ctx, lp['norm2_g']) * (1 + csc2) + csh2
        tokens = jnp.concatenate([h2.reshape(-1, d), h2c.reshape(-1, d)], axis=0)
        f = moe_ffn(tokens, router_w, router_b, lp['moe_w_gate'], lp['moe_w_up'], lp['moe_w_down'])
        x = x + g2 * f[:b * l].reshape(b, l, d)
        ctx = ctx + cg2 * f[b * l:].reshape(ctx.shape)
    else:
        f = moe_ffn(h2.reshape(-1, d), router_w, router_b, lp['moe_w_gate'], lp['moe_w_up'], lp['moe_w_down'])
        x = x + g2 * f.reshape(b, l, d)
    return x, ctx


def setup_inputs(seed: int = 0) -> dict:
    key = jax.random.key(seed)
    ks = iter(jax.random.split(key, 64))
    f32 = jnp.float32
    L, G, P = DEPTH, SSM_GROUPS, SSM_STATE

    def nrm(shape, scale):
        return jax.random.normal(next(ks), shape, f32) * scale

    def gain(shape):
        return 1.0 + 0.02 * jax.random.normal(next(ks), shape, f32)

    n_idx = jnp.arange(P, dtype=f32)
    return {
        'x': nrm((BATCH, SEQ, D_MODEL), 1.0),
        'c': nrm((BATCH, D_MODEL), 1.0),
        'ctx': nrm((BATCH, CTX_LEN, D_MODEL), 1.0),
        'c_ctx': nrm((D_MODEL,), 1.0),
        'ada_w': nrm((L, D_MODEL, 6 * D_MODEL), 0.5 * D_MODEL ** -0.5),
        'ada_b': nrm((L, 6 * D_MODEL), 0.02),
        'norm1_g': gain((L, D_MODEL)),
        'norm2_g': gain((L, D_MODEL)),
        'w_in': nrm((L, D_MODEL, IN_W), D_MODEL ** -0.5),
        'q_norm_g': gain((L, HEAD_DIM)),
        'k_norm_g': gain((L, HEAD_DIM)),
        'ssm_a_re': -0.5 + nrm((L, 2, G, P), 0.01),
        'ssm_a_im': math.pi * n_idx + nrm((L, 2, G, P), 0.01),
        'ssm_log_dt': jax.random.uniform(next(ks), (L, 2, G), f32,
                                         math.log(SSM_DT_MIN), math.log(SSM_DT_MAX)),
        'ssm_b_re': nrm((L, 2, G, P, SSM_GROUP), (2 * SSM_GROUP) ** -0.5),
        'ssm_b_im': nrm((L, 2, G, P, SSM_GROUP), (2 * SSM_GROUP) ** -0.5),
        'ssm_c_re': nrm((L, 2, G, SSM_GROUP, P), (2 * P) ** -0.5),
        'ssm_c_im': nrm((L, 2, G, SSM_GROUP, P), (2 * P) ** -0.5),
        'ssm_d': nrm((L, SSM_WIDTH), 1.0),
        'ssm_glu_w': nrm((L, SSM_WIDTH, SSM_WIDTH), SSM_WIDTH ** -0.5),
        'ssm_glu_b': nrm((L, SSM_WIDTH), 0.02),
        'pool_w': nrm((L, len(POOL_WINDOWS), POOL_GROUP, POOL_GROUP), POOL_GROUP ** -0.5),
        'pool_scale': 1.0 + 0.1 * jax.random.normal(next(ks), (L, POOL_WIDTH), f32),
        'conv_dw_w': nrm((L, CONV_TAPS, CONV_WIDTH), CONV_TAPS ** -0.5),
        'conv_dw_b': nrm((L, CONV_WIDTH), 0.02),
        'conv_ln_g': gain((L, CONV_WIDTH)),
        'conv_ln_b': nrm((L, CONV_WIDTH), 0.02),
        'w_up_attn': nrm((L, Q_W, D_MODEL), Q_W ** -0.5),
        'w_up_ssm': nrm((L, SSM_WIDTH, D_MODEL), SSM_WIDTH ** -0.5),
        'w_up_pool': nrm((L, POOL_WIDTH, D_MODEL), POOL_WIDTH ** -0.5),
        'w_up_conv': nrm((L, CONV_WIDTH, D_MODEL), CONV_WIDTH ** -0.5),
        'w_out': nrm((L, D_MODEL, D_MODEL), D_MODEL ** -0.5),
        'router_w': nrm((D_MODEL, N_EXPERTS), D_MODEL ** -0.5),
        'router_b': nrm((N_EXPERTS,), 0.01),
        'moe_w_gate': nrm((L, N_EXPERTS, D_MODEL, EXPERT_FF), D_MODEL ** -0.5),
        'moe_w_up': nrm((L, N_EXPERTS, D_MODEL, EXPERT_FF), D_MODEL ** -0.5),
        'moe_w_down': nrm((L, N_EXPERTS, EXPERT_FF, D_MODEL), EXPERT_FF ** -0.5),
        'final_g': gain((D_MODEL,)),
    }


def reference(x, c, ctx, c_ctx, ada_w, ada_b, norm1_g, norm2_g, w_in, q_norm_g, k_norm_g,
              ssm_a_re, ssm_a_im, ssm_log_dt, ssm_b_re, ssm_b_im, ssm_c_re, ssm_c_im, ssm_d,
              ssm_glu_w, ssm_glu_b, pool_w, pool_scale, conv_dw_w, conv_dw_b, conv_ln_g, conv_ln_b,
              w_up_attn, w_up_ssm, w_up_pool, w_up_conv, w_out, router_w, router_b,
              moe_w_gate, moe_w_up, moe_w_down, final_g):
    ROWS = x.shape[1] // GRID_W
    rows = jnp.repeat(jnp.arange(ROWS), GRID_W)
    cols = jnp.tile(jnp.arange(GRID_W), ROWS)
    for l in range(DEPTH):
        lp = {
            'ada_w': ada_w[l], 'ada_b': ada_b[l], 'norm1_g': norm1_g[l], 'norm2_g': norm2_g[l],
            'w_in': w_in[l], 'q_norm_g': q_norm_g[l], 'k_norm_g': k_norm_g[l],
            'ssm_a_re': ssm_a_re[l], 'ssm_a_im': ssm_a_im[l], 'ssm_log_dt': ssm_log_dt[l],
            'ssm_b_re': ssm_b_re[l], 'ssm_b_im': ssm_b_im[l], 'ssm_c_re': ssm_c_re[l],
            'ssm_c_im': ssm_c_im[l], 'ssm_d': ssm_d[l], 'ssm_glu_w': ssm_glu_w[l],
            'ssm_glu_b': ssm_glu_b[l], 'pool_w': pool_w[l], 'pool_scale': pool_scale[l],
            'conv_dw_w': conv_dw_w[l], 'conv_dw_b': conv_dw_b[l], 'conv_ln_g': conv_ln_g[l],
            'conv_ln_b': conv_ln_b[l], 'w_up_attn': w_up_attn[l], 'w_up_ssm': w_up_ssm[l],
            'w_up_pool': w_up_pool[l], 'w_up_conv': w_up_conv[l], 'w_out': w_out[l],
            'moe_w_gate': moe_w_gate[l], 'moe_w_up': moe_w_up[l], 'moe_w_down': moe_w_down[l],
        }
        x, ctx = hybrid_layer(x, ctx, c, c_ctx, rows, cols, lp, router_w, router_b,
                              with_ctx=(l < DEPTH - 1))
    return rms_norm(x, final_g)
```

```python
import functools
import math

import jax
import jax.numpy as jnp
from jax import lax
from jax.experimental import pallas as pl
from jax.experimental.pallas import tpu as pltpu

F32 = jnp.float32
BF16 = jnp.bfloat16

D_MODEL = 2048
BATCH = 2
SEQ = 4096
DEPTH = 2
GRID_W = 64
CTX_LEN = 256
N_HEADS = 8
N_KV_HEADS = 2
HEAD_DIM = 128
ROPE_THETA = 10000.0
SSM_WIDTH = 512
SSM_GROUP = 16
SSM_GROUPS = SSM_WIDTH // SSM_GROUP
SSM_STATE = 64
POOL_WIDTH = 512
POOL_WINDOWS = (2, 4, 8, 16)
POOL_GROUP = POOL_WIDTH // len(POOL_WINDOWS)
CONV_WIDTH = 512
CONV_TAPS = 31
N_BRANCHES = 4
N_EXPERTS = 64
N_EXPERT_GROUPS = 8
EXPERTS_PER_GROUP = N_EXPERTS // N_EXPERT_GROUPS
TOP_K = 2
EXPERT_FF = 512
EPS = 1e-6

Q_W = N_HEADS * HEAD_DIM
KV_W = N_KV_HEADS * HEAD_DIM
IN_W = 2 * KV_W + SSM_WIDTH + Q_W + POOL_WIDTH + 2 * CONV_WIDTH + N_BRANCHES * D_MODEL
CTX_STATE_W = 2 * KV_W + SSM_WIDTH
COL_K, COL_V, COL_SSM, COL_Q = 0, KV_W, 2 * KV_W, 2 * KV_W + SSM_WIDTH
COL_POOL = COL_Q + Q_W
COL_CONV = COL_POOL + POOL_WIDTH
COL_GATE = COL_CONV + 2 * CONV_WIDTH

N_LAT = BATCH * SEQ
N_CTX = BATCH * CTX_LEN
S_ALL = CTX_LEN + SEQ

V7X_VMEM_BYTES = 64 * 1024 * 1024
SUBLANES = 8
LANES = 128
BF16_ROWS = 16

TM = 512
TN = 512
TILES_PER_BATCH = SEQ // TM
TQ = 256
SSM_T = 256
SSM_TC = SSM_T // SUBLANES
SSM_HALF = SSM_WIDTH // 2
SSM_HSTATE = SSM_GROUPS // 2 * SSM_STATE
SEQ_T = 256
SEQ_PAD = 16
MOE_BLOCK = 128
ROUTE_T = 512
GATHER_T = 256


def _cparams(sem, vmem_mb):
    return pltpu.CompilerParams(dimension_semantics=sem, vmem_limit_bytes=vmem_mb * 1024 * 1024)


def _mod_id(tile):
    return jnp.minimum(tile // TILES_PER_BATCH, BATCH)


def _silu(x):
    return x * jax.nn.sigmoid(x)


def _ada_kernel(c_ref, w_ref, b_ref, o_ref):
    c = c_ref[...]
    a = _silu(c).astype(BF16)
    o_ref[0] = jnp.dot(a, w_ref[0].astype(BF16), preferred_element_type=F32) + b_ref[0]


def ada_modulation(cc, ada_w, ada_b):
    tn = 1024
    n6 = 6 * D_MODEL
    return pl.pallas_call(
        _ada_kernel,
        grid=(DEPTH, n6 // tn),
        in_specs=[
            pl.BlockSpec((SUBLANES, D_MODEL), lambda l, j: (0, 0)),
            pl.BlockSpec((1, D_MODEL, tn), lambda l, j: (l, 0, j)),
            pl.BlockSpec((1, 1, tn), lambda l, j: (l, 0, j)),
        ],
        out_specs=pl.BlockSpec((1, SUBLANES, tn), lambda l, j: (l, 0, j)),
        out_shape=jax.ShapeDtypeStruct((DEPTH, SUBLANES, n6), F32),
        compiler_params=_cparams(("arbitrary", "arbitrary"), 40),
        name="ada_modulation",
    )(cc, ada_w, ada_b.reshape(DEPTH, 1, n6))


def _inproj_kernel(x_ref, mod_ref, g_ref, w_ref, o_ref, h_ref):
    @pl.when(pl.program_id(1) == 0)
    def _():
        x = x_ref[...]
        ms = jnp.mean(x * x, axis=-1, keepdims=True)
        y = x * lax.rsqrt(ms + EPS) * g_ref[0]
        h = y * (1.0 + mod_ref[0, 1:2, :]) + mod_ref[0, 0:1, :]
        h_ref[...] = h.astype(BF16)

    o_ref[...] = jnp.dot(h_ref[...], w_ref[0], preferred_element_type=F32).astype(o_ref.dtype)


def in_projection(xa, mod, norm_g, w_in, layer, row_tile0, n_row_tiles, n_col_tiles):
    return pl.pallas_call(
        _inproj_kernel,
        grid=(n_row_tiles, n_col_tiles),
        in_specs=[
            pl.BlockSpec((TM, D_MODEL), lambda i, j: (i + row_tile0, 0)),
            pl.BlockSpec((1, 6, D_MODEL), lambda i, j: (_mod_id(i + row_tile0), 0, 0)),
            pl.BlockSpec((1, 1, D_MODEL), lambda i, j: (layer, 0, 0)),
            pl.BlockSpec((1, D_MODEL, TN), lambda i, j: (layer, 0, j)),
        ],
        out_specs=pl.BlockSpec((TM, TN), lambda i, j: (i, j)),
        out_shape=jax.ShapeDtypeStruct((n_row_tiles * TM, n_col_tiles * TN), BF16),
        scratch_shapes=[pltpu.VMEM((TM, D_MODEL), BF16)],
        compiler_params=_cparams(("arbitrary", "arbitrary"), 40),
        name="in_projection",
    )(xa, mod, norm_g.reshape(DEPTH, 1, D_MODEL), w_in)


def _rope_tables(n_ctx_rows):
    half = HEAD_DIM // 4
    inv_freq = ROPE_THETA ** (-jnp.arange(half, dtype=F32) / half)
    t = jnp.arange(SEQ)
    ang_r = (t // GRID_W).astype(F32)[:, None] * inv_freq[None, :]
    ang_c = (t % GRID_W).astype(F32)[:, None] * inv_freq[None, :]
    cos = jnp.concatenate([jnp.cos(ang_r), jnp.cos(ang_r), jnp.cos(ang_c), jnp.cos(ang_c)], axis=-1)
    sin = jnp.concatenate([-jnp.sin(ang_r), jnp.sin(ang_r), -jnp.sin(ang_c), jnp.sin(ang_c)], axis=-1)
    cos = jnp.concatenate([jnp.ones((n_ctx_rows, HEAD_DIM), F32), cos], axis=0)
    sin = jnp.concatenate([jnp.zeros((n_ctx_rows, HEAD_DIM), F32), sin], axis=0)
    return cos, sin


def _head_norm_rope(x, g, cos, sin):
    ms = jnp.mean(x * x, axis=-1, keepdims=True)
    y = x * lax.rsqrt(ms + EPS) * g
    lane = lax.broadcasted_iota(jnp.int32, y.shape, 1)
    first = (lane % (HEAD_DIM // 2)) < (HEAD_DIM // 4)
    partner = jnp.where(first, pltpu.roll(y, HEAD_DIM - HEAD_DIM // 4, 1), pltpu.roll(y, HEAD_DIM // 4, 1))
    return y * cos + partner * sin


def _attn_kernel(q_ref, k_ref, v_ref, cq_ref, sq_ref, ck_ref, sk_ref, gq_ref, gk_ref, o_ref, ks_ref, *, n_keys):
    @pl.when(pl.program_id(2) == 0)
    def _():
        def prep(c, carry):
            r0 = pl.multiple_of(c * TQ, TQ)
            kk = k_ref[0, pl.ds(r0, TQ), :].astype(F32)
            kn = _head_norm_rope(kk, gk_ref[0], ck_ref[pl.ds(r0, TQ), :], sk_ref[pl.ds(r0, TQ), :])
            ks_ref[pl.ds(r0, TQ), :] = kn.astype(BF16)
            return carry

        lax.fori_loop(0, n_keys // TQ, prep, 0)

    k = ks_ref[...]
    v = v_ref[0]
    scale = HEAD_DIM ** -0.5
    for hh in range(N_HEADS // N_KV_HEADS):
        cols = slice(hh * HEAD_DIM, (hh + 1) * HEAD_DIM)
        q = q_ref[:, cols].astype(F32)
        qn = _head_norm_rope(q, gq_ref[0], cq_ref[...], sq_ref[...]) * scale
        s = lax.dot_general(qn.astype(BF16), k, (((1,), (1,)), ((), ())), preferred_element_type=F32)
        m = jnp.max(s, axis=-1, keepdims=True)
        p = jnp.exp(s - m)
        l = jnp.sum(p, axis=-1, keepdims=True)
        o = jnp.dot(p.astype(BF16), v, preferred_element_type=F32) / l
        o_ref[:, cols] = o.astype(o_ref.dtype)


def attention(q_src, q_col0, q_row_tile0, n_q, kv, cos, sin, q_table_tile0, q_norm_g, k_norm_g, layer):
    n_keys = kv.shape[1]
    grp_w = Q_W // N_KV_HEADS
    qb = n_q // TQ
    kern = functools.partial(_attn_kernel, n_keys=n_keys)
    return pl.pallas_call(
        kern,
        grid=(BATCH, N_KV_HEADS, qb),
        in_specs=[
            pl.BlockSpec((TQ, grp_w), lambda b, g, i: (q_row_tile0 + b * qb + i, q_col0 // grp_w + g)),
            pl.BlockSpec((1, n_keys, HEAD_DIM), lambda b, g, i: (b, 0, g)),
            pl.BlockSpec((1, n_keys, HEAD_DIM), lambda b, g, i: (b, 0, N_KV_HEADS + g)),
            pl.BlockSpec((TQ, HEAD_DIM), lambda b, g, i: (q_table_tile0 + i, 0)),
            pl.BlockSpec((TQ, HEAD_DIM), lambda b, g, i: (q_table_tile0 + i, 0)),
            pl.BlockSpec((n_keys, HEAD_DIM), lambda b, g, i: (0, 0)),
            pl.BlockSpec((n_keys, HEAD_DIM), lambda b, g, i: (0, 0)),
            pl.BlockSpec((1, 1, HEAD_DIM), lambda b, g, i: (layer, 0, 0)),
            pl.BlockSpec((1, 1, HEAD_DIM), lambda b, g, i: (layer, 0, 0)),
        ],
        out_specs=pl.BlockSpec((TQ, grp_w), lambda b, g, i: (b * qb + i, g)),
        out_shape=jax.ShapeDtypeStruct((BATCH * n_q, Q_W), BF16),
        scratch_shapes=[pltpu.VMEM((n_keys, HEAD_DIM), BF16)],
        compiler_params=_cparams(("arbitrary", "arbitrary", "arbitrary"), 48),
        name="attention",
    )(q_src, kv, kv, cos, sin, cos, sin,
      q_norm_g.reshape(DEPTH, 1, HEAD_DIM), k_norm_g.reshape(DEPTH, 1, HEAD_DIM))


def _cmul(ar, ai, br, bi):
    return ar * br - ai * bi, ar * bi + ai * br


SSM_CB = 512


def _ssm_kernel(u_ref, bm_ref, cm_ref, lam_ref, pw_ref, y_ref, bu_ref, carry_ref):
    @pl.when(pl.program_id(3) == 0)
    def _():
        carry_ref[...] = jnp.zeros_like(carry_ref)

    bu_ref[...] = jnp.dot(u_ref[0, 0, 0], bm_ref[0, 0], preferred_element_type=F32)
    last = SSM_T - SUBLANES
    for cb in range(SSM_HSTATE // SSM_CB):
        re = slice(cb * SSM_CB, (cb + 1) * SSM_CB)
        im = slice(SSM_HSTATE + cb * SSM_CB, SSM_HSTATE + (cb + 1) * SSM_CB)
        lr = lam_ref[0, 0, :, re]
        li = lam_ref[0, 0, :, im]

        def local_scan(j, h, re=re, im=im, lr=lr, li=li):
            r0 = pl.multiple_of(j * SUBLANES, SUBLANES)
            pr, pi = _cmul(lr, li, h[0], h[1])
            nr = pr + bu_ref[pl.ds(r0, SUBLANES), re]
            ni = pi + bu_ref[pl.ds(r0, SUBLANES), im]
            bu_ref[pl.ds(r0, SUBLANES), re] = nr
            bu_ref[pl.ds(r0, SUBLANES), im] = ni
            return nr, ni

        zero = jnp.zeros((SUBLANES, SSM_CB), F32)
        lax.fori_loop(0, SSM_TC, local_scan, (zero, zero), unroll=4)

        er = bu_ref[last:SSM_T, re]
        ei = bu_ref[last:SSM_T, im]
        row = lax.broadcasted_iota(jnp.int32, (SUBLANES, SSM_CB), 0)
        xr = jnp.where(row == 0, carry_ref[:, re], pltpu.roll(er, 1, 0))
        xi = jnp.where(row == 0, carry_ref[:, im], pltpu.roll(ei, 1, 0))
        for lvl, k in enumerate((1, 2, 4)):
            ar = pw_ref[0, 0, lvl, :, re]
            ai = pw_ref[0, 0, lvl, :, im]
            sr = jnp.where(row >= k, pltpu.roll(xr, k, 0), 0.0)
            si = jnp.where(row >= k, pltpu.roll(xi, k, 0), 0.0)
            mr, mi = _cmul(ar, ai, sr, si)
            xr = xr + mr
            xi = xi + mi
        cr, ci = _cmul(pw_ref[0, 0, 0, :, re], pw_ref[0, 0, 0, :, im], xr, xi)
        carry_ref[:, re] = pltpu.roll(cr + er, 1, 0)
        carry_ref[:, im] = pltpu.roll(ci + ei, 1, 0)

        def add_carry(j, g, re=re, im=im, lr=lr, li=li):
            r0 = pl.multiple_of(j * SUBLANES, SUBLANES)
            bu_ref[pl.ds(r0, SUBLANES), re] = bu_ref[pl.ds(r0, SUBLANES), re] + g[0]
            bu_ref[pl.ds(r0, SUBLANES), im] = bu_ref[pl.ds(r0, SUBLANES), im] + g[1]
            return _cmul(lr, li, g[0], g[1])

        lax.fori_loop(0, SSM_TC, add_carry, _cmul(lr, li, xr, xi), unroll=4)

    y_ref[0, 0, 0] = jnp.dot(bu_ref[...].astype(BF16), cm_ref[0, 0], preferred_element_type=F32)


def _ssm_params(a_re, a_im, log_dt, b_re, b_im, c_re, c_im):
    lam = lax.complex(a_re.astype(F32), a_im.astype(F32))
    dt = jnp.exp(log_dt.astype(F32))[..., None]
    lam_bar = jnp.exp(lam * dt)
    b_bar = ((lam_bar - 1.0) / lam)[..., None] * lax.complex(b_re.astype(F32), b_im.astype(F32))
    gh = SSM_GROUPS // 2
    eye = jnp.eye(gh, dtype=F32)

    def b_block(m):
        m = m.reshape(2, 2, gh, SSM_STATE, SSM_GROUP)
        return jnp.einsum('dhgpc,gk->dhgckp', m, eye).reshape(2, 2, gh * SSM_GROUP, gh * SSM_STATE)

    def c_block(m):
        m = m.reshape(2, 2, gh, SSM_GROUP, SSM_STATE)
        return jnp.einsum('dhgcp,gk->dhkpgc', m, eye).reshape(2, 2, gh * SSM_STATE, gh * SSM_GROUP)

    bm = jnp.concatenate([b_block(jnp.real(b_bar)), b_block(jnp.imag(b_bar))], axis=-1).astype(BF16)
    cm = jnp.concatenate([c_block(c_re.astype(F32)), -c_block(c_im.astype(F32))], axis=-2).astype(BF16)

    def table(z):
        z = z.reshape(2, 2, SSM_HSTATE)
        t = jnp.concatenate([jnp.real(z), jnp.imag(z)], axis=-1)
        return jnp.broadcast_to(t[:, :, None, :], (2, 2, SUBLANES, 2 * SSM_HSTATE))

    lam_t = table(lam_bar)
    pw_t = jnp.stack([table(jnp.exp(lam * dt * (SSM_TC * k))) for k in (1, 2, 4)], axis=2)
    return bm, cm, lam_t, pw_t


def ssm_scan(u_perm, bm, cm, lam_t, pw_t):
    n_chunks = u_perm.shape[2]
    hs2 = 2 * SSM_HSTATE
    return pl.pallas_call(
        _ssm_kernel,
        grid=(2, 2, BATCH, n_chunks),
        in_specs=[
            pl.BlockSpec((1, 1, 1, SSM_T, SSM_HALF), lambda d, h, b, c: (d, b, c, 0, h)),
            pl.BlockSpec((1, 1, SSM_HALF, hs2), lambda d, h, b, c: (d, h, 0, 0)),
            pl.BlockSpec((1, 1, hs2, SSM_HALF), lambda d, h, b, c: (d, h, 0, 0)),
            pl.BlockSpec((1, 1, SUBLANES, hs2), lambda d, h, b, c: (d, h, 0, 0)),
            pl.BlockSpec((1, 1, 3, SUBLANES, hs2), lambda d, h, b, c: (d, h, 0, 0, 0)),
        ],
        out_specs=pl.BlockSpec((1, 1, 1, SSM_T, SSM_HALF), lambda d, h, b, c: (d, b, c, 0, h)),
        out_shape=jax.ShapeDtypeStruct(u_perm.shape, F32),
        scratch_shapes=[pltpu.VMEM((SSM_T, hs2), F32), pltpu.VMEM((SUBLANES, hs2), F32)],
        compiler_params=_cparams(("arbitrary",) * 4, 32),
        name="ssm_scan",
    )(u_perm, bm, cm, lam_t, pw_t)


def _to_scan_order(u):
    b, s, w = u.shape
    u = u.reshape(b, s // SSM_T, SUBLANES, SSM_TC, w).transpose(0, 1, 3, 2, 4)
    return u.reshape(b, s // SSM_T, SSM_T, w)


def _from_scan_order(y):
    b, c, _, w = y.shape
    y = y.reshape(b, c, SSM_TC, SUBLANES, w).transpose(0, 1, 3, 2, 4)
    return y.reshape(b, c * SSM_T, w)


def _gelu_tanh(x):
    return 0.5 * x * (1.0 + jnp.tanh(math.sqrt(2.0 / math.pi) * (x + 0.044715 * (x * x * x))))


def _ssm_out_kernel(u_ref, yf_ref, yb_ref, d_ref, w_ref, b_ref, o_ref):
    y = d_ref[0] * u_ref[...].astype(F32) + yf_ref[...] + yb_ref[...]
    y = _gelu_tanh(y)
    z = jnp.dot(y.astype(BF16), w_ref[0], preferred_element_type=F32) + b_ref[0]
    o_ref[...] = (y * jax.nn.sigmoid(z)).astype(o_ref.dtype)


def ssm_output(p, yf, yb, ssm_d, glu_w, glu_b, layer):
    n = yf.shape[0]
    return pl.pallas_call(
        _ssm_out_kernel,
        grid=(n // TM,),
        in_specs=[
            pl.BlockSpec((TM, SSM_WIDTH), lambda i: (i, COL_SSM // SSM_WIDTH)),
            pl.BlockSpec((TM, SSM_WIDTH), lambda i: (i, 0)),
            pl.BlockSpec((TM, SSM_WIDTH), lambda i: (i, 0)),
            pl.BlockSpec((1, 1, SSM_WIDTH), lambda i: (layer, 0, 0)),
            pl.BlockSpec((1, SSM_WIDTH, SSM_WIDTH), lambda i: (layer, 0, 0)),
            pl.BlockSpec((1, 1, SSM_WIDTH), lambda i: (layer, 0, 0)),
        ],
        out_specs=pl.BlockSpec((TM, SSM_WIDTH), lambda i: (i, 0)),
        out_shape=jax.ShapeDtypeStruct((n, SSM_WIDTH), BF16),
        compiler_params=_cparams(("arbitrary",), 32),
        name="ssm_output",
    )(p, yf, yb, ssm_d.reshape(DEPTH, 1, SSM_WIDTH), glu_w, glu_b.reshape(DEPTH, 1, SSM_WIDTH))


def _pool_kernel(u_ref, w_ref, s_ref, o_ref, *, seq_len):
    t0 = pl.multiple_of(pl.program_id(1) * SEQ_T, SEQ_T)
    halo = u_ref[0, pl.ds(t0, SEQ_T + 2 * SEQ_PAD), :]
    centre = u_ref[0, pl.ds(t0 + SEQ_PAD, SEQ_T), :].astype(F32)
    tt = lax.broadcasted_iota(jnp.int32, (SEQ_T, SEQ_T + 2 * SEQ_PAD), 0)
    rr = lax.broadcasted_iota(jnp.int32, (SEQ_T, SEQ_T + 2 * SEQ_PAD), 1) - SEQ_PAD
    tg = t0 + lax.broadcasted_iota(jnp.int32, (SEQ_T, 1), 0)
    for gi, w in enumerate(POOL_WINDOWS):
        cols = slice(gi * POOL_GROUP, (gi + 1) * POOL_GROUP)
        band = ((rr >= tt - w // 2) & (rr <= tt + w // 2 - 1)).astype(F32).astype(BF16)
        wsum = jnp.dot(band, halo[:, cols], preferred_element_type=F32)
        cnt = jnp.minimum(tg - w // 2 + w, seq_len) - jnp.maximum(tg - w // 2, 0)
        pooled = wsum / cnt.astype(F32) - centre[:, cols]
        mixed = jnp.dot(pooled.astype(BF16), w_ref[0, gi], preferred_element_type=F32)
        o_ref[0, :, cols] = (mixed * s_ref[0, :, cols]).astype(o_ref.dtype)


def pool_mixer(u_pad, pool_w, pool_scale, layer):
    nseq, lp, _ = u_pad.shape
    seq_len = lp - 2 * SEQ_PAD
    kern = functools.partial(_pool_kernel, seq_len=seq_len)
    ng = len(POOL_WINDOWS)
    return pl.pallas_call(
        kern,
        grid=(nseq, seq_len // SEQ_T),
        in_specs=[
            pl.BlockSpec((1, lp, POOL_WIDTH), lambda s, t: (s, 0, 0)),
            pl.BlockSpec((1, ng, POOL_GROUP, POOL_GROUP), lambda s, t: (layer, 0, 0, 0)),
            pl.BlockSpec((1, 1, POOL_WIDTH), lambda s, t: (layer, 0, 0)),
        ],
        out_specs=pl.BlockSpec((1, SEQ_T, POOL_WIDTH), lambda s, t: (s, t, 0)),
        out_shape=jax.ShapeDtypeStruct((nseq, seq_len, POOL_WIDTH), BF16),
        compiler_params=_cparams(("arbitrary", "arbitrary"), 32),
        name="pool_mixer",
    )(u_pad, pool_w, pool_scale.reshape(DEPTH, 1, POOL_WIDTH))


CONV_RB = 64


def _conv_kernel(u_ref, w_ref, b_ref, g_ref, beta_ref, o_ref, glu_ref):
    t0 = pl.multiple_of(pl.program_id(1) * SEQ_T, SEQ_T)
    rows = SEQ_T + 2 * SEQ_PAD
    a = u_ref[0, pl.ds(t0, rows), 0:CONV_WIDTH].astype(F32)
    g = u_ref[0, pl.ds(t0, rows), CONV_WIDTH:2 * CONV_WIDTH].astype(F32)
    glu_ref[...] = a * jax.nn.sigmoid(g)
    off = SEQ_PAD - CONV_TAPS // 2
    for rb in range(SEQ_T // CONV_RB):
        parts = []
        for cb in range(CONV_WIDTH // LANES):
            cols = slice(cb * LANES, (cb + 1) * LANES)
            acc = jnp.zeros((CONV_RB, LANES), F32)
            for k in range(CONV_TAPS):
                acc = acc + glu_ref[rb * CONV_RB + k + off:rb * CONV_RB + k + off + CONV_RB, cols] * w_ref[0, k:k + 1, cols]
            parts.append(acc)
        y = jnp.concatenate(parts, axis=-1) + b_ref[0]
        yc = y - jnp.mean(y, axis=-1, keepdims=True)
        yn = yc * lax.rsqrt(jnp.mean(yc * yc, axis=-1, keepdims=True) + EPS)
        yn = yn * g_ref[0] + beta_ref[0]
        o_ref[0, rb * CONV_RB:(rb + 1) * CONV_RB, :] = _silu(yn).astype(o_ref.dtype)


def conv_mixer(u_pad, dw_w, dw_b, ln_g, ln_b, layer):
    nseq, lp, _ = u_pad.shape
    seq_len = lp - 2 * SEQ_PAD
    vec = lambda a: a.reshape(DEPTH, 1, CONV_WIDTH)
    vspec = pl.BlockSpec((1, 1, CONV_WIDTH), lambda s, t: (layer, 0, 0))
    return pl.pallas_call(
        _conv_kernel,
        grid=(nseq, seq_len // SEQ_T),
        in_specs=[
            pl.BlockSpec((1, lp, 2 * CONV_WIDTH), lambda s, t: (s, 0, 0)),
            pl.BlockSpec((1, CONV_TAPS, CONV_WIDTH), lambda s, t: (layer, 0, 0)),
            vspec, vspec, vspec,
        ],
        out_specs=pl.BlockSpec((1, SEQ_T, CONV_WIDTH), lambda s, t: (s, t, 0)),
        out_shape=jax.ShapeDtypeStruct((nseq, seq_len, CONV_WIDTH), BF16),
        scratch_shapes=[pltpu.VMEM((SEQ_T + 2 * SEQ_PAD, CONV_WIDTH), F32)],
        compiler_params=_cparams(("arbitrary", "arbitrary"), 40),
        name="conv_mixer",
    )(u_pad, dw_w, vec(dw_b), vec(ln_g), vec(ln_b))


def _merge_kernel(oa_ref, ys_ref, yp_ref, yc_ref, g0_ref, g1_ref, g2_ref, g3_ref,
                  wa_ref, ws_ref, wp_ref, wc_ref, o_ref):
    def branch(x_ref, w_ref, g_ref):
        up = jnp.dot(x_ref[...], w_ref[0], preferred_element_type=F32)
        return jax.nn.sigmoid(g_ref[...].astype(F32)) * up

    m = (branch(oa_ref, wa_ref, g0_ref) + branch(ys_ref, ws_ref, g1_ref)
         + branch(yp_ref, wp_ref, g2_ref) + branch(yc_ref, wc_ref, g3_ref))
    o_ref[...] = m.astype(o_ref.dtype)


def merge_branches(p, o_attn, y_ssm, y_pool, y_conv, w_attn, w_ssm, w_pool, w_conv, layer):
    n = o_attn.shape[0]
    nct = D_MODEL // TN
    gate_tile0 = COL_GATE // TN

    def gate_spec(br):
        return pl.BlockSpec((TM, TN), lambda i, j: (i, gate_tile0 + br * nct + j))

    def x_spec(width):
        return pl.BlockSpec((TM, width), lambda i, j: (i, 0))

    def w_spec(width):
        return pl.BlockSpec((1, width, TN), lambda i, j: (layer, 0, j))

    return pl.pallas_call(
        _merge_kernel,
        grid=(n // TM, nct),
        in_specs=[x_spec(Q_W), x_spec(SSM_WIDTH), x_spec(POOL_WIDTH), x_spec(CONV_WIDTH),
                  gate_spec(0), gate_spec(1), gate_spec(2), gate_spec(3),
                  w_spec(Q_W), w_spec(SSM_WIDTH), w_spec(POOL_WIDTH), w_spec(CONV_WIDTH)],
        out_specs=pl.BlockSpec((TM, TN), lambda i, j: (i, j)),
        out_shape=jax.ShapeDtypeStruct((n, D_MODEL), BF16),
        compiler_params=_cparams(("arbitrary", "arbitrary"), 40),
        name="merge_branches",
    )(o_attn, y_ssm, y_pool, y_conv, p, p, p, p, w_attn, w_ssm, w_pool, w_conv)


def _split_bf16(x):
    hi = x.astype(BF16)
    lo = (x - hi.astype(F32)).astype(BF16)
    return hi, lo


def _outproj_kernel(m_ref, x_ref, mod_ref, g_ref, w_ref, rw_ref, xo_ref, h_ref, lg_ref):
    mix = jnp.dot(m_ref[...], w_ref[0], preferred_element_type=F32)
    x = x_ref[...] + mod_ref[0, 2:3, :] * mix
    xo_ref[...] = x
    ms = jnp.mean(x * x, axis=-1, keepdims=True)
    h = x * lax.rsqrt(ms + EPS) * g_ref[0]
    h = h * (1.0 + mod_ref[0, 4:5, :]) + mod_ref[0, 3:4, :]
    h_ref[...] = h
    h_hi, h_lo = _split_bf16(h)
    r_hi, r_lo = _split_bf16(rw_ref[...])
    nt = (((1,), (1,)), ((), ()))
    lg_ref[...] = (lax.dot_general(r_hi, h_hi, nt, preferred_element_type=F32)
                   + lax.dot_general(r_hi, h_lo, nt, preferred_element_type=F32)
                   + lax.dot_general(r_lo, h_hi, nt, preferred_element_type=F32))


def out_projection(m, xa, mod, norm2_g, w_out, router_wt, layer):
    n = m.shape[0]
    tmo = TM // 2
    return pl.pallas_call(
        _outproj_kernel,
        grid=(n // tmo,),
        in_specs=[
            pl.BlockSpec((tmo, D_MODEL), lambda i: (i, 0)),
            pl.BlockSpec((tmo, D_MODEL), lambda i: (i, 0)),
            pl.BlockSpec((1, 6, D_MODEL), lambda i: (_mod_id(i // 2), 0, 0)),
            pl.BlockSpec((1, 1, D_MODEL), lambda i: (layer, 0, 0)),
            pl.BlockSpec((1, D_MODEL, D_MODEL), lambda i: (layer, 0, 0)),
            pl.BlockSpec((N_EXPERTS, D_MODEL), lambda i: (0, 0)),
        ],
        out_specs=[
            pl.BlockSpec((tmo, D_MODEL), lambda i: (i, 0)),
            pl.BlockSpec((tmo, D_MODEL), lambda i: (i, 0)),
            pl.BlockSpec((N_EXPERTS, tmo), lambda i: (0, i)),
        ],
        out_shape=[
            jax.ShapeDtypeStruct((n, D_MODEL), F32),
            jax.ShapeDtypeStruct((n, D_MODEL), F32),
            jax.ShapeDtypeStruct((N_EXPERTS, n), F32),
        ],
        compiler_params=_cparams(("arbitrary",), 56),
        name="out_projection",
    )(m, xa, mod, norm2_g.reshape(DEPTH, 1, D_MODEL), w_out, router_wt)


def _first_argmax(blk, row):
    m = jnp.max(blk, axis=0, keepdims=True)
    idx = jnp.min(jnp.where(blk == m, row, EXPERTS_PER_GROUP), axis=0, keepdims=True)
    return m, idx


def _route_kernel(lg_ref, rb_ref, e_ref, w_ref, rank_ref, cnt_ref, run_ref):
    @pl.when(pl.program_id(0) == 0)
    def _():
        run_ref[...] = jnp.zeros_like(run_ref)

    t = lg_ref.shape[1]
    scores = jax.nn.sigmoid(lg_ref[...])
    sel = scores + rb_ref[...]
    row = lax.broadcasted_iota(jnp.int32, (EXPERTS_PER_GROUP, t), 0)
    neg = jnp.float32(-jnp.inf)

    best = None
    for g in range(N_EXPERT_GROUPS):
        blk = sel[g * EXPERTS_PER_GROUP:(g + 1) * EXPERTS_PER_GROUP, :]
        m1, i1 = _first_argmax(blk, row)
        m2 = jnp.max(jnp.where(row == i1, neg, blk), axis=0, keepdims=True)
        gs = m1 + m2
        if best is None:
            best, grp = gs, jnp.zeros((1, t), jnp.int32)
        else:
            better = gs > best
            best = jnp.where(better, gs, best)
            grp = jnp.where(better, g, grp)

    in_sel = jnp.zeros((EXPERTS_PER_GROUP, t), F32)
    in_sc = jnp.zeros((EXPERTS_PER_GROUP, t), F32)
    for g in range(N_EXPERT_GROUPS):
        rows = slice(g * EXPERTS_PER_GROUP, (g + 1) * EXPERTS_PER_GROUP)
        in_sel = jnp.where(grp == g, sel[rows, :], in_sel)
        in_sc = jnp.where(grp == g, scores[rows, :], in_sc)
    _, i1 = _first_argmax(in_sel, row)
    _, i2 = _first_argmax(jnp.where(row == i1, neg, in_sel), row)
    s1 = jnp.sum(jnp.where(row == i1, in_sc, 0.0), axis=0, keepdims=True)
    s2 = jnp.sum(jnp.where(row == i2, in_sc, 0.0), axis=0, keepdims=True)
    e1 = grp * EXPERTS_PER_GROUP + i1
    e2 = grp * EXPERTS_PER_GROUP + i2
    e_ref[0:1, :] = e1
    e_ref[1:2, :] = e2
    w_ref[0:1, :] = s1 / (s1 + s2)
    w_ref[1:2, :] = s2 / (s1 + s2)

    erow = lax.broadcasted_iota(jnp.int32, (N_EXPERTS, t), 0)
    oh1 = erow == e1
    oh2 = erow == e2
    cnt = jnp.where(oh1 | oh2, 1.0, 0.0)
    a = lax.broadcasted_iota(jnp.int32, (t, t), 0)
    b = lax.broadcasted_iota(jnp.int32, (t, t), 1)
    before = jnp.where(a < b, 1.0, 0.0).astype(BF16)
    excl = jnp.dot(cnt.astype(BF16), before, preferred_element_type=F32) + run_ref[...]
    rank_ref[0:1, :] = jnp.sum(jnp.where(oh1, excl, 0.0), axis=0, keepdims=True).astype(jnp.int32)
    rank_ref[1:2, :] = jnp.sum(jnp.where(oh2, excl, 0.0), axis=0, keepdims=True).astype(jnp.int32)
    run_ref[...] = run_ref[...] + jnp.sum(cnt, axis=1, keepdims=True)
    cnt_ref[...] = run_ref[...]


def route(logits_t, router_b):
    n = logits_t.shape[1]
    return pl.pallas_call(
        _route_kernel,
        grid=(n // ROUTE_T,),
        in_specs=[
            pl.BlockSpec((N_EXPERTS, ROUTE_T), lambda i: (0, i)),
            pl.BlockSpec((N_EXPERTS, 1), lambda i: (0, 0)),
        ],
        out_specs=[
            pl.BlockSpec((TOP_K, ROUTE_T), lambda i: (0, i)),
            pl.BlockSpec((TOP_K, ROUTE_T), lambda i: (0, i)),
            pl.BlockSpec((TOP_K, ROUTE_T), lambda i: (0, i)),
            pl.BlockSpec((N_EXPERTS, 1), lambda i: (0, 0)),
        ],
        out_shape=[
            jax.ShapeDtypeStruct((TOP_K, n), jnp.int32),
            jax.ShapeDtypeStruct((TOP_K, n), F32),
            jax.ShapeDtypeStruct((TOP_K, n), jnp.int32),
            jax.ShapeDtypeStruct((N_EXPERTS, 1), F32),
        ],
        scratch_shapes=[pltpu.VMEM((N_EXPERTS, 1), F32)],
        compiler_params=_cparams(("arbitrary",), 32),
        name="route",
    )(logits_t, router_b.reshape(N_EXPERTS, 1).astype(F32))


def _row_copy(src, src_row, dst, dst_row, sem):
    return pltpu.make_async_copy(src.at[pl.ds(src_row, 1)], dst.at[pl.ds(dst_row, 1)], sem)


def _dispatch_kernel(dest_ref, h_ref, xs_in_ref, xs_ref, sem):
    del xs_in_ref
    n = h_ref.shape[0]
    base = pl.program_id(0) * GATHER_T

    def start(t, carry):
        for k in range(TOP_K):
            _row_copy(h_ref, base + t, xs_ref, dest_ref[k * n + base + t], sem).start()
        return carry

    def wait(t, carry):
        for k in range(TOP_K):
            _row_copy(h_ref, base + t, xs_ref, dest_ref[k * n + base + t], sem).wait()
        return carry

    lax.fori_loop(0, GATHER_T, start, 0)
    lax.fori_loop(0, GATHER_T, wait, 0)


def dispatch(dest_flat, h2, n_slots):
    n = h2.shape[0]
    xs0 = jnp.zeros((n_slots, D_MODEL), F32)
    grid_spec = pltpu.PrefetchScalarGridSpec(
        num_scalar_prefetch=1,
        grid=(n // GATHER_T,),
        in_specs=[pl.BlockSpec(memory_space=pl.ANY), pl.BlockSpec(memory_space=pl.ANY)],
        out_specs=pl.BlockSpec(memory_space=pl.ANY),
        scratch_shapes=[pltpu.SemaphoreType.DMA(())],
    )
    return pl.pallas_call(
        _dispatch_kernel,
        grid_spec=grid_spec,
        out_shape=jax.ShapeDtypeStruct((n_slots, D_MODEL), F32),
        input_output_aliases={2: 0},
        compiler_params=_cparams(("arbitrary",), 32),
        name="dispatch",
    )(dest_flat, h2, xs0)


def _expert_kernel(be_ref, nu_ref, x_ref, wg_ref, wu_ref, wd_ref, o_ref, wgb_ref, wub_ref, wdb_ref):
    b = pl.program_id(0)
    prev = be_ref[jnp.maximum(b - 1, 0)]
    fresh = (b == 0) | (be_ref[b] != prev)

    @pl.when(fresh & (b < nu_ref[0]))
    def _():
        wgb_ref[...] = wg_ref[0].astype(BF16)
        wub_ref[...] = wu_ref[0].astype(BF16)
        wdb_ref[...] = wd_ref[0].astype(BF16)

    @pl.when(b < nu_ref[0])
    def _():
        x = x_ref[...].astype(BF16)
        gate = jnp.dot(x, wgb_ref[...], preferred_element_type=F32)
        up = jnp.dot(x, wub_ref[...], preferred_element_type=F32)
        hid = (_silu(gate) * up).astype(BF16)
        o_ref[...] = jnp.dot(hid, wdb_ref[...], preferred_element_type=F32)

    @pl.when(b >= nu_ref[0])
    def _():
        o_ref[...] = jnp.zeros_like(o_ref)


def experts(blk_e, n_used, xs, w_gate, w_up, w_down, layer):
    n_slots = xs.shape[0]
    n_blocks = n_slots // MOE_BLOCK

    def live(b, nu):
        return jnp.minimum(b, nu[0] - 1)

    grid_spec = pltpu.PrefetchScalarGridSpec(
        num_scalar_prefetch=2,
        grid=(n_blocks,),
        in_specs=[
            pl.BlockSpec((MOE_BLOCK, D_MODEL), lambda b, be, nu: (live(b, nu), 0)),
            pl.BlockSpec((1, 1, D_MODEL, EXPERT_FF), lambda b, be, nu: (layer, be[live(b, nu)], 0, 0)),
            pl.BlockSpec((1, 1, D_MODEL, EXPERT_FF), lambda b, be, nu: (layer, be[live(b, nu)], 0, 0)),
            pl.BlockSpec((1, 1, EXPERT_FF, D_MODEL), lambda b, be, nu: (layer, be[live(b, nu)], 0, 0)),
        ],
        out_specs=pl.BlockSpec((MOE_BLOCK, D_MODEL), lambda b, be, nu: (b, 0)),
        scratch_shapes=[
            pltpu.VMEM((D_MODEL, EXPERT_FF), BF16),
            pltpu.VMEM((D_MODEL, EXPERT_FF), BF16),
            pltpu.VMEM((EXPERT_FF, D_MODEL), BF16),
        ],
    )

    def kern(be_ref, nu_ref, x_ref, wg_ref, wu_ref, wd_ref, o_ref, wgb_ref, wub_ref, wdb_ref):
        _expert_kernel(be_ref, nu_ref, x_ref, wg_ref.at[0], wu_ref.at[0], wd_ref.at[0], o_ref,
                       wgb_ref, wub_ref, wdb_ref)

    return pl.pallas_call(
        kern,
        grid_spec=grid_spec,
        out_shape=jax.ShapeDtypeStruct((n_slots, D_MODEL), F32),
        compiler_params=_cparams(("arbitrary",), 56),
        name="experts",
    )(blk_e, n_used, xs, w_gate, w_up, w_down)


def _combine_kernel(dest_ref, x_ref, mod_ref, w_ref, fg_ref, ys_ref, o_ref, buf_ref, sem, *, final_norm):
    n = dest_ref.shape[0] // TOP_K
    base = pl.program_id(0) * GATHER_T

    def start(t, carry):
        for k in range(TOP_K):
            _row_copy(ys_ref, dest_ref[k * n + base + t], buf_ref.at[k], t, sem).start()
        return carry

    def wait(t, carry):
        for k in range(TOP_K):
            _row_copy(ys_ref, dest_ref[k * n + base + t], buf_ref.at[k], t, sem).wait()
        return carry

    lax.fori_loop(0, GATHER_T, start, 0)
    lax.fori_loop(0, GATHER_T, wait, 0)
    f = w_ref[:, 0:1] * buf_ref[0] + w_ref[:, 1:2] * buf_ref[1]
    x = x_ref[...] + mod_ref[0, 5:6, :] * f
    if final_norm:
        ms = jnp.mean(x * x, axis=-1, keepdims=True)
        x = x * lax.rsqrt(ms + EPS) * fg_ref[...]
    o_ref[...] = x


def combine(dest_flat, xa, mod, wts_rows, ys, final_g, final_norm):
    n = xa.shape[0]
    tiles_per_tm = TM // GATHER_T
    grid_spec = pltpu.PrefetchScalarGridSpec(
        num_scalar_prefetch=1,
        grid=(n // GATHER_T,),
        in_specs=[
            pl.BlockSpec((GATHER_T, D_MODEL), lambda i, d: (i, 0)),
            pl.BlockSpec((1, 6, D_MODEL), lambda i, d: (_mod_id(i // tiles_per_tm), 0, 0)),
            pl.BlockSpec((GATHER_T, TOP_K), lambda i, d: (i, 0)),
            pl.BlockSpec((1, D_MODEL), lambda i, d: (0, 0)),
            pl.BlockSpec(memory_space=pl.ANY),
        ],
        out_specs=pl.BlockSpec((GATHER_T, D_MODEL), lambda i, d: (i, 0)),
        scratch_shapes=[pltpu.VMEM((TOP_K, GATHER_T, D_MODEL), F32), pltpu.SemaphoreType.DMA(())],
    )
    kern = functools.partial(_combine_kernel, final_norm=final_norm)
    return pl.pallas_call(
        kern,
        grid_spec=grid_spec,
        out_shape=jax.ShapeDtypeStruct((n, D_MODEL), F32),
        compiler_params=_cparams(("arbitrary",), 32),
        name="combine",
    )(dest_flat, xa, mod, wts_rows, final_g.reshape(1, D_MODEL), ys)


def _slot_plan(e_idx, rank, counts, n_blocks):
    cnt = counts.reshape(N_EXPERTS).astype(jnp.int32)
    padded = (cnt + MOE_BLOCK - 1) // MOE_BLOCK * MOE_BLOCK
    pad_end = jnp.cumsum(padded)
    pad_start = pad_end - padded
    dest = pad_start[e_idx] + rank
    blk_start = jnp.arange(n_blocks, dtype=jnp.int32) * MOE_BLOCK
    blk_e = jnp.minimum(jnp.searchsorted(pad_end, blk_start, side='right'), N_EXPERTS - 1).astype(jnp.int32)
    n_used = (pad_end[-1] // MOE_BLOCK).astype(jnp.int32).reshape(1)
    return dest.reshape(-1).astype(jnp.int32), blk_e, n_used


def _pad_rows(u):
    return jnp.pad(u, ((0, 0), (SEQ_PAD, SEQ_PAD), (0, 0)))


def _layer(layer, xa, mod, W, with_ctx, final_norm):
    n_lat_tiles = N_LAT // TM
    n_ctx_tiles = N_CTX // TM
    n_col_tiles = IN_W // TN
    if with_ctx:
        p = in_projection(xa, mod, W['norm1_g'], W['w_in'], layer, 0, n_lat_tiles + n_ctx_tiles, n_col_tiles)
        p_lat, p_ctx = p[:N_LAT], p[N_LAT:]
    else:
        p = in_projection(xa, mod, W['norm1_g'], W['w_in'], layer, 0, n_lat_tiles, n_col_tiles)
        p_lat = p
        p_ctx = in_projection(xa, mod, W['norm1_g'], W['w_in'], layer, n_lat_tiles, n_ctx_tiles, CTX_STATE_W // TN)
    n_rows = p.shape[0]

    def seq(a, lo, hi, n_seq_rows):
        return a[:, lo:hi].reshape(BATCH, n_seq_rows, hi - lo)

    kv = jnp.concatenate([seq(p_ctx, 0, 2 * KV_W, CTX_LEN), seq(p_lat, 0, 2 * KV_W, SEQ)], axis=1)
    cos, sin = _rope_tables(CTX_LEN)
    o_attn = attention(p, COL_Q, 0, SEQ, kv, cos, sin, CTX_LEN // TQ, W['q_norm_g'], W['k_norm_g'], layer)
    if with_ctx:
        kv_c = seq(p_ctx, 0, 2 * KV_W, CTX_LEN)
        ones, zeros = jnp.ones((CTX_LEN, HEAD_DIM), F32), jnp.zeros((CTX_LEN, HEAD_DIM), F32)
        o_attn_c = attention(p, COL_Q, N_LAT // TQ, CTX_LEN, kv_c, ones, zeros, 0,
                             W['q_norm_g'], W['k_norm_g'], layer)
        o_attn = jnp.concatenate([o_attn, o_attn_c], axis=0)

    u_seq = jnp.concatenate([seq(p_ctx, COL_SSM, COL_SSM + SSM_WIDTH, CTX_LEN),
                             seq(p_lat, COL_SSM, COL_SSM + SSM_WIDTH, SEQ)], axis=1)
    u_rev = jnp.concatenate([u_seq[:, :CTX_LEN][:, ::-1], u_seq[:, CTX_LEN:][:, ::-1]], axis=1)
    u_perm = jnp.stack([_to_scan_order(u_seq), _to_scan_order(u_rev)], axis=0)
    bm, cm, lam_t, pw_t = _ssm_params(W['ssm_a_re'][layer], W['ssm_a_im'][layer], W['ssm_log_dt'][layer],
                                      W['ssm_b_re'][layer], W['ssm_b_im'][layer],
                                      W['ssm_c_re'][layer], W['ssm_c_im'][layer])
    y_perm = ssm_scan(u_perm, bm, cm, lam_t, pw_t)
    yf = _from_scan_order(y_perm[0])
    yb = _from_scan_order(y_perm[1])
    yb = jnp.concatenate([yb[:, :CTX_LEN][:, ::-1], yb[:, CTX_LEN:][:, ::-1]], axis=1)

    def rows(y):
        lat = y[:, CTX_LEN:].reshape(N_LAT, SSM_WIDTH)
        if not with_ctx:
            return lat
        return jnp.concatenate([lat, y[:, :CTX_LEN].reshape(N_CTX, SSM_WIDTH)], axis=0)

    y_ssm = ssm_output(p, rows(yf), rows(yb), W['ssm_d'], W['ssm_glu_w'], W['ssm_glu_b'], layer)

    def seq_mixers(src, n_seq_rows):
        up = _pad_rows(seq(src, COL_POOL, COL_POOL + POOL_WIDTH, n_seq_rows))
        uc = _pad_rows(seq(src, COL_CONV, COL_CONV + 2 * CONV_WIDTH, n_seq_rows))
        yp = pool_mixer(up, W['pool_w'], W['pool_scale'], layer)
        yc = conv_mixer(uc, W['conv_dw_w'], W['conv_dw_b'], W['conv_ln_g'], W['conv_ln_b'], layer)
        return yp.reshape(-1, POOL_WIDTH), yc.reshape(-1, CONV_WIDTH)

    y_pool, y_conv = seq_mixers(p_lat, SEQ)
    if with_ctx:
        yp_c, yc_c = seq_mixers(p_ctx, CTX_LEN)
        y_pool = jnp.concatenate([y_pool, yp_c], axis=0)
        y_conv = jnp.concatenate([y_conv, yc_c], axis=0)

    m = merge_branches(p, o_attn, y_ssm, y_pool, y_conv,
                       W['w_up_attn'], W['w_up_ssm'], W['w_up_pool'], W['w_up_conv'], layer)
    x1, h2, logits_t = out_projection(m, xa, mod, W['norm2_g'], W['w_out'], W['router_wt'], layer)

    e_idx, wts, rank, counts = route(logits_t, W['router_b'])
    n_blocks = -(-n_rows * TOP_K // MOE_BLOCK) + N_EXPERTS
    dest_flat, blk_e, n_used = _slot_plan(e_idx, rank, counts, n_blocks)
    xs = dispatch(dest_flat, h2, n_blocks * MOE_BLOCK)
    ys = experts(blk_e, n_used, xs, W['moe_w_gate'], W['moe_w_up'], W['moe_w_down'], layer)
    x2 = combine(dest_flat, x1, mod, wts.T, ys, W['final_g'], final_norm)
    if with_ctx:
        return x2
    return jnp.concatenate([x2, xa[N_LAT:]], axis=0)


def kernel(x, c, ctx, c_ctx, ada_w, ada_b, norm1_g, norm2_g, w_in, q_norm_g, k_norm_g, ssm_a_re, ssm_a_im, ssm_log_dt, ssm_b_re, ssm_b_im, ssm_c_re, ssm_c_im, ssm_d, ssm_glu_w, ssm_glu_b, pool_w, pool_scale, conv_dw_w, conv_dw_b, conv_ln_g, conv_ln_b, w_up_attn, w_up_ssm, w_up_pool, w_up_conv, w_out, router_w, router_b, moe_w_gate, moe_w_up, moe_w_down, final_g):
    W = dict(
        norm1_g=norm1_g, norm2_g=norm2_g, q_norm_g=q_norm_g, k_norm_g=k_norm_g,
        w_in=w_in.astype(BF16),
        ssm_a_re=ssm_a_re, ssm_a_im=ssm_a_im, ssm_log_dt=ssm_log_dt, ssm_b_re=ssm_b_re, ssm_b_im=ssm_b_im,
        ssm_c_re=ssm_c_re, ssm_c_im=ssm_c_im, ssm_d=ssm_d, ssm_glu_w=ssm_glu_w.astype(BF16), ssm_glu_b=ssm_glu_b,
        pool_w=pool_w.astype(BF16), pool_scale=pool_scale,
        conv_dw_w=conv_dw_w, conv_dw_b=conv_dw_b, conv_ln_g=conv_ln_g, conv_ln_b=conv_ln_b,
        w_up_attn=w_up_attn.astype(BF16), w_up_ssm=w_up_ssm.astype(BF16), w_up_pool=w_up_pool.astype(BF16),
        w_up_conv=w_up_conv.astype(BF16), w_out=w_out.astype(BF16),
        router_wt=router_w.T, router_b=router_b,
        moe_w_gate=moe_w_gate, moe_w_up=moe_w_up, moe_w_down=moe_w_down, final_g=final_g,
    )
    cc = jnp.concatenate([c, c_ctx[None, :], jnp.zeros((SUBLANES - BATCH - 1, D_MODEL), F32)], axis=0)
    mod_all = ada_modulation(cc, ada_w, ada_b)
    xa = jnp.concatenate([x.reshape(N_LAT, D_MODEL), ctx.reshape(N_CTX, D_MODEL)], axis=0)
    for layer in range(DEPTH):
        mod = mod_all[layer, :BATCH + 1].reshape(BATCH + 1, 6, D_MODEL)
        xa = _layer(layer, xa, mod, W, with_ctx=(layer < DEPTH - 1), final_norm=(layer == DEPTH - 1))
    return xa[:N_LAT].reshape(BATCH, SEQ, D_MODEL)
```

```python
import functools
import math

import jax
import jax.numpy as jnp
from jax import lax
from jax.experimental import pallas as pl
from jax.experimental.pallas import tpu as pltpu

F32 = jnp.float32
BF16 = jnp.bfloat16

D_MODEL = 2048
BATCH = 2
SEQ = 4096
DEPTH = 2
GRID_W = 64
CTX_LEN = 256
N_HEADS = 8
N_KV_HEADS = 2
HEAD_DIM = 128
ROPE_THETA = 10000.0
SSM_WIDTH = 512
SSM_GROUP = 16
SSM_GROUPS = SSM_WIDTH // SSM_GROUP
SSM_STATE = 64
POOL_WIDTH = 512
POOL_WINDOWS = (2, 4, 8, 16)
POOL_GROUP = POOL_WIDTH // len(POOL_WINDOWS)
CONV_WIDTH = 512
CONV_TAPS = 31
N_BRANCHES = 4
N_EXPERTS = 64
N_EXPERT_GROUPS = 8
EXPERTS_PER_GROUP = N_EXPERTS // N_EXPERT_GROUPS
TOP_K = 2
EXPERT_FF = 512
EPS = 1e-6

Q_W = N_HEADS * HEAD_DIM
KV_W = N_KV_HEADS * HEAD_DIM
IN_W = 2 * KV_W + SSM_WIDTH + Q_W + POOL_WIDTH + 2 * CONV_WIDTH + N_BRANCHES * D_MODEL
CTX_STATE_W = 2 * KV_W + SSM_WIDTH
COL_K, COL_V, COL_SSM, COL_Q = 0, KV_W, 2 * KV_W, 2 * KV_W + SSM_WIDTH
COL_POOL = COL_Q + Q_W
COL_CONV = COL_POOL + POOL_WIDTH
COL_GATE = COL_CONV + 2 * CONV_WIDTH

N_LAT = BATCH * SEQ
N_CTX = BATCH * CTX_LEN
S_ALL = CTX_LEN + SEQ

V7X_VMEM_BYTES = 64 * 1024 * 1024
SUBLANES = 8
LANES = 128
BF16_ROWS = 16

TM = 512
TN = 512
TILES_PER_BATCH = SEQ // TM
TQ = 256
SSM_T = 256
SSM_TC = SSM_T // SUBLANES
SSM_HALF = SSM_WIDTH // 2
SSM_HSTATE = SSM_GROUPS // 2 * SSM_STATE
SEQ_T = 256
SEQ_PAD = 16
MOE_BLOCK = 128
ROUTE_T = 512
GATHER_T = 256


def _cparams(sem, vmem_mb):
    return pltpu.CompilerParams(dimension_semantics=sem, vmem_limit_bytes=vmem_mb * 1024 * 1024)


def _mod_id(tile):
    return jnp.minimum(tile // TILES_PER_BATCH, BATCH)


def _silu(x):
    return x * jax.nn.sigmoid(x)


def _ada_kernel(c_ref, w_ref, b_ref, o_ref):
    c = c_ref[...]
    a = _silu(c).astype(BF16)
    o_ref[0] = jnp.dot(a, w_ref[0].astype(BF16), preferred_element_type=F32) + b_ref[0]


def ada_modulation(cc, ada_w, ada_b):
    tn = 1024
    n6 = 6 * D_MODEL
    return pl.pallas_call(
        _ada_kernel,
        grid=(DEPTH, n6 // tn),
        in_specs=[
            pl.BlockSpec((SUBLANES, D_MODEL), lambda l, j: (0, 0)),
            pl.BlockSpec((1, D_MODEL, tn), lambda l, j: (l, 0, j)),
            pl.BlockSpec((1, 1, tn), lambda l, j: (l, 0, j)),
        ],
        out_specs=pl.BlockSpec((1, SUBLANES, tn), lambda l, j: (l, 0, j)),
        out_shape=jax.ShapeDtypeStruct((DEPTH, SUBLANES, n6), F32),
        compiler_params=_cparams(("arbitrary", "arbitrary"), 40),
        name="ada_modulation",
    )(cc, ada_w, ada_b.reshape(DEPTH, 1, n6))


def _inproj_kernel(x_ref, mod_ref, g_ref, w_ref, o_ref, h_ref):
    @pl.when(pl.program_id(1) == 0)
    def _():
        x = x_ref[...]
        ms = jnp.mean(x * x, axis=-1, keepdims=True)
        y = x * lax.rsqrt(ms + EPS) * g_ref[0]
        h = y * (1.0 + mod_ref[0, 1:2, :]) + mod_ref[0, 0:1, :]
        h_ref[...] = h.astype(BF16)

    o_ref[...] = jnp.dot(h_ref[...], w_ref[0], preferred_element_type=F32).astype(o_ref.dtype)


def in_projection(xa, mod, norm_g, w_in, layer, row_tile0, n_row_tiles, n_col_tiles):
    return pl.pallas_call(
        _inproj_kernel,
        grid=(n_row_tiles, n_col_tiles),
        in_specs=[
            pl.BlockSpec((TM, D_MODEL), lambda i, j: (i + row_tile0, 0)),
            pl.BlockSpec((1, 6, D_MODEL), lambda i, j: (_mod_id(i + row_tile0), 0, 0)),
            pl.BlockSpec((1, 1, D_MODEL), lambda i, j: (layer, 0, 0)),
            pl.BlockSpec((1, D_MODEL, TN), lambda i, j: (layer, 0, j)),
        ],
        out_specs=pl.BlockSpec((TM, TN), lambda i, j: (i, j)),
        out_shape=jax.ShapeDtypeStruct((n_row_tiles * TM, n_col_tiles * TN), BF16),
        scratch_shapes=[pltpu.VMEM((TM, D_MODEL), BF16)],
        compiler_params=_cparams(("arbitrary", "arbitrary"), 40),
        name="in_projection",
    )(xa, mod, norm_g.reshape(DEPTH, 1, D_MODEL), w_in)


def _rope_tables(n_ctx_rows):
    half = HEAD_DIM // 4
    inv_freq = ROPE_THETA ** (-jnp.arange(half, dtype=F32) / half)
    t = jnp.arange(SEQ)
    ang_r = (t // GRID_W).astype(F32)[:, None] * inv_freq[None, :]
    ang_c = (t % GRID_W).astype(F32)[:, None] * inv_freq[None, :]
    cos = jnp.concatenate([jnp.cos(ang_r), jnp.cos(ang_r), jnp.cos(ang_c), jnp.cos(ang_c)], axis=-1)
    sin = jnp.concatenate([-jnp.sin(ang_r), jnp.sin(ang_r), -jnp.sin(ang_c), jnp.sin(ang_c)], axis=-1)
    cos = jnp.concatenate([jnp.ones((n_ctx_rows, HEAD_DIM), F32), cos], axis=0)
    sin = jnp.concatenate([jnp.zeros((n_ctx_rows, HEAD_DIM), F32), sin], axis=0)
    return cos, sin


def _head_norm_rope(x, g, cos, sin):
    ms = jnp.mean(x * x, axis=-1, keepdims=True)
    y = x * lax.rsqrt(ms + EPS) * g
    lane = lax.broadcasted_iota(jnp.int32, y.shape, 1)
    first = (lane % (HEAD_DIM // 2)) < (HEAD_DIM // 4)
    partner = jnp.where(first, pltpu.roll(y, HEAD_DIM - HEAD_DIM // 4, 1), pltpu.roll(y, HEAD_DIM // 4, 1))
    return y * cos + partner * sin


def _attn_kernel(q_ref, k_ref, v_ref, cq_ref, sq_ref, ck_ref, sk_ref, gq_ref, gk_ref, o_ref, ks_ref, *, n_keys):
    @pl.when(pl.program_id(2) == 0)
    def _():
        def prep(c, carry):
            r0 = pl.multiple_of(c * TQ, TQ)
            kk = k_ref[0, pl.ds(r0, TQ), :].astype(F32)
            kn = _head_norm_rope(kk, gk_ref[0], ck_ref[pl.ds(r0, TQ), :], sk_ref[pl.ds(r0, TQ), :])
            ks_ref[pl.ds(r0, TQ), :] = kn.astype(BF16)
            return carry

        lax.fori_loop(0, n_keys // TQ, prep, 0)

    k = ks_ref[...]
    v = v_ref[0]
    scale = HEAD_DIM ** -0.5
    for hh in range(N_HEADS // N_KV_HEADS):
        cols = slice(hh * HEAD_DIM, (hh + 1) * HEAD_DIM)
        q = q_ref[:, cols].astype(F32)
        qn = _head_norm_rope(q, gq_ref[0], cq_ref[...], sq_ref[...]) * scale
        s = lax.dot_general(qn.astype(BF16), k, (((1,), (1,)), ((), ())), preferred_element_type=F32)
        m = jnp.max(s, axis=-1, keepdims=True)
        p = jnp.exp(s - m)
        l = jnp.sum(p, axis=-1, keepdims=True)
        o = jnp.dot(p.astype(BF16), v, preferred_element_type=F32) / l
        o_ref[:, cols] = o.astype(o_ref.dtype)


def attention(q_src, q_col0, q_row_tile0, n_q, kv, cos, sin, q_table_tile0, q_norm_g, k_norm_g, layer):
    n_keys = kv.shape[1]
    grp_w = Q_W // N_KV_HEADS
    qb = n_q // TQ
    kern = functools.partial(_attn_kernel, n_keys=n_keys)
    return pl.pallas_call(
        kern,
        grid=(BATCH, N_KV_HEADS, qb),
        in_specs=[
            pl.BlockSpec((TQ, grp_w), lambda b, g, i: (q_row_tile0 + b * qb + i, q_col0 // grp_w + g)),
            pl.BlockSpec((1, n_keys, HEAD_DIM), lambda b, g, i: (b, 0, g)),
            pl.BlockSpec((1, n_keys, HEAD_DIM), lambda b, g, i: (b, 0, N_KV_HEADS + g)),
            pl.BlockSpec((TQ, HEAD_DIM), lambda b, g, i: (q_table_tile0 + i, 0)),
            pl.BlockSpec((TQ, HEAD_DIM), lambda b, g, i: (q_table_tile0 + i, 0)),
            pl.BlockSpec((n_keys, HEAD_DIM), lambda b, g, i: (0, 0)),
            pl.BlockSpec((n_keys, HEAD_DIM), lambda b, g, i: (0, 0)),
            pl.BlockSpec((1, 1, HEAD_DIM), lambda b, g, i: (layer, 0, 0)),
            pl.BlockSpec((1, 1, HEAD_DIM), lambda b, g, i: (layer, 0, 0)),
        ],
        out_specs=pl.BlockSpec((TQ, grp_w), lambda b, g, i: (b * qb + i, g)),
        out_shape=jax.ShapeDtypeStruct((BATCH * n_q, Q_W), BF16),
        scratch_shapes=[pltpu.VMEM((n_keys, HEAD_DIM), BF16)],
        compiler_params=_cparams(("arbitrary", "arbitrary", "arbitrary"), 48),
        name="attention",
    )(q_src, kv, kv, cos, sin, cos, sin,
      q_norm_g.reshape(DEPTH, 1, HEAD_DIM), k_norm_g.reshape(DEPTH, 1, HEAD_DIM))


def _cmul(ar, ai, br, bi):
    return ar * br - ai * bi, ar * bi + ai * br


SSM_CB = 512


def _ssm_kernel(u_ref, bm_ref, cm_ref, lam_ref, pw_ref, y_ref, bu_ref, yv_ref, carry_ref, *, reverse):
    @pl.when(pl.program_id(2) == 0)
    def _():
        carry_ref[...] = jnp.zeros_like(carry_ref)

    rr = lax.broadcasted_iota(jnp.int32, (SSM_T, SSM_T), 0)
    tt = lax.broadcasted_iota(jnp.int32, (SSM_T, SSM_T), 1)
    perm = jnp.where(tt == (rr % SUBLANES) * SSM_TC + rr // SUBLANES, 1.0, 0.0).astype(BF16)
    u_scan = jnp.dot(perm, u_ref[0], preferred_element_type=F32).astype(BF16)
    bu_ref[...] = jnp.dot(u_scan, bm_ref[0], preferred_element_type=F32)

    last_row = 0 if reverse else (SSM_T - SUBLANES)
    edge = (SUBLANES - 1) if reverse else 0
    toward = (lambda x, k: pltpu.roll(x, SUBLANES - k, 0)) if reverse else (lambda x, k: pltpu.roll(x, k, 0))

    def step_rows(j):
        jj = (SSM_TC - 1 - j) if reverse else j
        return pl.ds(pl.multiple_of(jj * SUBLANES, SUBLANES), SUBLANES)

    for cb in range(SSM_HSTATE // SSM_CB):
        re = slice(cb * SSM_CB, (cb + 1) * SSM_CB)
        im = slice(SSM_HSTATE + cb * SSM_CB, SSM_HSTATE + (cb + 1) * SSM_CB)
        lr = lam_ref[0, :, re]
        li = lam_ref[0, :, im]

        def local_scan(j, h, re=re, im=im, lr=lr, li=li):
            rows = step_rows(j)
            pr, pi = _cmul(lr, li, h[0], h[1])
            nr = pr + bu_ref[rows, re]
            ni = pi + bu_ref[rows, im]
            bu_ref[rows, re] = nr
            bu_ref[rows, im] = ni
            return nr, ni

        zero = jnp.zeros((SUBLANES, SSM_CB), F32)
        lax.fori_loop(0, SSM_TC, local_scan, (zero, zero), unroll=4)

        er = bu_ref[last_row:last_row + SUBLANES, re]
        ei = bu_ref[last_row:last_row + SUBLANES, im]
        row = lax.broadcasted_iota(jnp.int32, (SUBLANES, SSM_CB), 0)
        dist = (SUBLANES - 1 - row) if reverse else row
        xr = jnp.where(row == edge, carry_ref[:, re], toward(er, 1))
        xi = jnp.where(row == edge, carry_ref[:, im], toward(ei, 1))
        for lvl, k in enumerate((1, 2, 4)):
            ar = pw_ref[0, lvl, :, re]
            ai = pw_ref[0, lvl, :, im]
            sr = jnp.where(dist >= k, toward(xr, k), 0.0)
            si = jnp.where(dist >= k, toward(xi, k), 0.0)
            mr, mi = _cmul(ar, ai, sr, si)
            xr = xr + mr
            xi = xi + mi
        cr, ci = _cmul(pw_ref[0, 0, :, re], pw_ref[0, 0, :, im], xr, xi)
        carry_ref[:, re] = toward(cr + er, 1)
        carry_ref[:, im] = toward(ci + ei, 1)

        def add_carry(j, g, re=re, im=im, lr=lr, li=li):
            rows = step_rows(j)
            bu_ref[rows, re] = bu_ref[rows, re] + g[0]
            bu_ref[rows, im] = bu_ref[rows, im] + g[1]
            return _cmul(lr, li, g[0], g[1])

        lax.fori_loop(0, SSM_TC, add_carry, _cmul(lr, li, xr, xi), unroll=4)

    y = jnp.dot(bu_ref[...].astype(BF16), cm_ref[0], preferred_element_type=F32)
    for cb in range(SSM_HALF // LANES):
        cols = slice(cb * LANES, (cb + 1) * LANES)
        yv_ref[cb] = y[:, cols]
        for s in range(SUBLANES):
            y_ref[0, s * SSM_TC:(s + 1) * SSM_TC, cols] = yv_ref[cb, pl.ds(s, SSM_TC, stride=SUBLANES), :]


def _ssm_params(a_re, a_im, log_dt, b_re, b_im, c_re, c_im):
    lam = lax.complex(a_re.astype(F32), a_im.astype(F32))
    dt = jnp.exp(log_dt.astype(F32))[..., None]
    lam_bar = jnp.exp(lam * dt)
    b_bar = ((lam_bar - 1.0) / lam)[..., None] * lax.complex(b_re.astype(F32), b_im.astype(F32))
    gh = SSM_GROUPS // 2
    eye = jnp.eye(gh, dtype=F32)

    def b_block(m):
        m = m.reshape(2, 2, gh, SSM_STATE, SSM_GROUP)
        return jnp.einsum('dhgpc,gk->dhgckp', m, eye).reshape(2, 2, gh * SSM_GROUP, gh * SSM_STATE)

    def c_block(m):
        m = m.reshape(2, 2, gh, SSM_GROUP, SSM_STATE)
        return jnp.einsum('dhgcp,gk->dhkpgc', m, eye).reshape(2, 2, gh * SSM_STATE, gh * SSM_GROUP)

    bm = jnp.concatenate([b_block(jnp.real(b_bar)), b_block(jnp.imag(b_bar))], axis=-1).astype(BF16)
    cm = jnp.concatenate([c_block(c_re.astype(F32)), -c_block(c_im.astype(F32))], axis=-2).astype(BF16)

    def table(z):
        z = z.reshape(2, 2, SSM_HSTATE)
        t = jnp.concatenate([jnp.real(z), jnp.imag(z)], axis=-1)
        return jnp.broadcast_to(t[:, :, None, :], (2, 2, SUBLANES, 2 * SSM_HSTATE))

    lam_t = table(lam_bar)
    pw_t = jnp.stack([table(jnp.exp(lam * dt * (SSM_TC * k))) for k in (1, 2, 4)], axis=2)
    return bm, cm, lam_t, pw_t


SSM_CHUNKS = S_ALL // SSM_T


def ssm_scan(u_chunks, bm, cm, lam_t, pw_t, reverse):
    hs2 = 2 * SSM_HSTATE

    def chunk(b, c):
        if reverse:
            c = jnp.where(c == 0, 0, SSM_CHUNKS - c)
        return b * SSM_CHUNKS + c

    kern = functools.partial(_ssm_kernel, reverse=reverse)
    return pl.pallas_call(
        kern,
        grid=(2, BATCH, SSM_CHUNKS),
        in_specs=[
            pl.BlockSpec((1, SSM_T, SSM_HALF), lambda h, b, c: (chunk(b, c), 0, h)),
            pl.BlockSpec((1, SSM_HALF, hs2), lambda h, b, c: (h, 0, 0)),
            pl.BlockSpec((1, hs2, SSM_HALF), lambda h, b, c: (h, 0, 0)),
            pl.BlockSpec((1, SUBLANES, hs2), lambda h, b, c: (h, 0, 0)),
            pl.BlockSpec((1, 3, SUBLANES, hs2), lambda h, b, c: (h, 0, 0, 0)),
        ],
        out_specs=pl.BlockSpec((1, SSM_T, SSM_HALF), lambda h, b, c: (chunk(b, c), 0, h)),
        out_shape=jax.ShapeDtypeStruct(u_chunks.shape, F32),
        scratch_shapes=[pltpu.VMEM((SSM_T, hs2), F32), pltpu.VMEM((SSM_HALF // LANES, SSM_T, LANES), F32),
                        pltpu.VMEM((SUBLANES, hs2), F32)],
        compiler_params=_cparams(("arbitrary",) * 3, 32),
        name="ssm_scan",
    )(u_chunks, bm, cm, lam_t, pw_t)


def _ssm_chunk_of_tile(i):
    lat_tiles = SEQ // SSM_T
    return jnp.where(i < BATCH * lat_tiles,
                     (i // lat_tiles) * SSM_CHUNKS + 1 + i % lat_tiles,
                     (i - BATCH * lat_tiles) * SSM_CHUNKS)


def _gelu_tanh(x):
    return 0.5 * x * (1.0 + jnp.tanh(math.sqrt(2.0 / math.pi) * (x + 0.044715 * (x * x * x))))


def _ssm_out_kernel(u_ref, yf_ref, yb_ref, d_ref, w_ref, b_ref, o_ref):
    y = d_ref[0] * u_ref[...].astype(F32) + yf_ref[0] + yb_ref[0]
    y = _gelu_tanh(y)
    z = jnp.dot(y.astype(BF16), w_ref[0], preferred_element_type=F32) + b_ref[0]
    o_ref[...] = (y * jax.nn.sigmoid(z)).astype(o_ref.dtype)


def ssm_output(p, yf, yb, ssm_d, glu_w, glu_b, layer):
    n = p.shape[0]
    yspec = pl.BlockSpec((1, SSM_T, SSM_WIDTH), lambda i: (_ssm_chunk_of_tile(i), 0, 0))
    return pl.pallas_call(
        _ssm_out_kernel,
        grid=(n // SSM_T,),
        in_specs=[
            pl.BlockSpec((SSM_T, SSM_WIDTH), lambda i: (i, COL_SSM // SSM_WIDTH)),
            yspec,
            yspec,
            pl.BlockSpec((1, 1, SSM_WIDTH), lambda i: (layer, 0, 0)),
            pl.BlockSpec((1, SSM_WIDTH, SSM_WIDTH), lambda i: (layer, 0, 0)),
            pl.BlockSpec((1, 1, SSM_WIDTH), lambda i: (layer, 0, 0)),
        ],
        out_specs=pl.BlockSpec((SSM_T, SSM_WIDTH), lambda i: (i, 0)),
        out_shape=jax.ShapeDtypeStruct((n, SSM_WIDTH), BF16),
        compiler_params=_cparams(("arbitrary",), 32),
        name="ssm_output",
    )(p, yf, yb, ssm_d.reshape(DEPTH, 1, SSM_WIDTH), glu_w, glu_b.reshape(DEPTH, 1, SSM_WIDTH))


def _pool_kernel(u_ref, w_ref, s_ref, o_ref, *, seq_len):
    t0 = pl.multiple_of(pl.program_id(1) * SEQ_T, SEQ_T)
    halo = u_ref[0, pl.ds(t0, SEQ_T + 2 * SEQ_PAD), :]
    centre = u_ref[0, pl.ds(t0 + SEQ_PAD, SEQ_T), :].astype(F32)
    tt = lax.broadcasted_iota(jnp.int32, (SEQ_T, SEQ_T + 2 * SEQ_PAD), 0)
    rr = lax.broadcasted_iota(jnp.int32, (SEQ_T, SEQ_T + 2 * SEQ_PAD), 1) - SEQ_PAD
    tg = t0 + lax.broadcasted_iota(jnp.int32, (SEQ_T, 1), 0)
    for gi, w in enumerate(POOL_WINDOWS):
        cols = slice(gi * POOL_GROUP, (gi + 1) * POOL_GROUP)
        band = ((rr >= tt - w // 2) & (rr <= tt + w // 2 - 1)).astype(F32).astype(BF16)
        wsum = jnp.dot(band, halo[:, cols], preferred_element_type=F32)
        cnt = jnp.minimum(tg - w // 2 + w, seq_len) - jnp.maximum(tg - w // 2, 0)
        pooled = wsum / cnt.astype(F32) - centre[:, cols]
        mixed = jnp.dot(pooled.astype(BF16), w_ref[0, gi], preferred_element_type=F32)
        o_ref[0, :, cols] = (mixed * s_ref[0, :, cols]).astype(o_ref.dtype)


def pool_mixer(u_pad, pool_w, pool_scale, layer):
    nseq, lp, _ = u_pad.shape
    seq_len = lp - 2 * SEQ_PAD
    kern = functools.partial(_pool_kernel, seq_len=seq_len)
    ng = len(POOL_WINDOWS)
    return pl.pallas_call(
        kern,
        grid=(nseq, seq_len // SEQ_T),
        in_specs=[
            pl.BlockSpec((1, lp, POOL_WIDTH), lambda s, t: (s, 0, 0)),
            pl.BlockSpec((1, ng, POOL_GROUP, POOL_GROUP), lambda s, t: (layer, 0, 0, 0)),
            pl.BlockSpec((1, 1, POOL_WIDTH), lambda s, t: (layer, 0, 0)),
        ],
        out_specs=pl.BlockSpec((1, SEQ_T, POOL_WIDTH), lambda s, t: (s, t, 0)),
        out_shape=jax.ShapeDtypeStruct((nseq, seq_len, POOL_WIDTH), BF16),
        compiler_params=_cparams(("arbitrary", "arbitrary"), 32),
        name="pool_mixer",
    )(u_pad, pool_w, pool_scale.reshape(DEPTH, 1, POOL_WIDTH))


CONV_RB = 64


def _conv_kernel(u_ref, w_ref, b_ref, g_ref, beta_ref, o_ref, glu_ref):
    t0 = pl.multiple_of(pl.program_id(1) * SEQ_T, SEQ_T)
    rows = SEQ_T + 2 * SEQ_PAD
    a = u_ref[0, pl.ds(t0, rows), 0:CONV_WIDTH].astype(F32)
    g = u_ref[0, pl.ds(t0, rows), CONV_WIDTH:2 * CONV_WIDTH].astype(F32)
    glu_ref[...] = a * jax.nn.sigmoid(g)
    off = SEQ_PAD - CONV_TAPS // 2
    for rb in range(SEQ_T // CONV_RB):
        parts = []
        for cb in range(CONV_WIDTH // LANES):
            cols = slice(cb * LANES, (cb + 1) * LANES)
            acc = jnp.zeros((CONV_RB, LANES), F32)
            for k in range(CONV_TAPS):
                acc = acc + glu_ref[rb * CONV_RB + k + off:rb * CONV_RB + k + off + CONV_RB, cols] * w_ref[0, k:k + 1, cols]
            parts.append(acc)
        y = jnp.concatenate(parts, axis=-1) + b_ref[0]
        yc = y - jnp.mean(y, axis=-1, keepdims=True)
        yn = yc * lax.rsqrt(jnp.mean(yc * yc, axis=-1, keepdims=True) + EPS)
        yn = yn * g_ref[0] + beta_ref[0]
        o_ref[0, rb * CONV_RB:(rb + 1) * CONV_RB, :] = _silu(yn).astype(o_ref.dtype)


def conv_mixer(u_pad, dw_w, dw_b, ln_g, ln_b, layer):
    nseq, lp, _ = u_pad.shape
    seq_len = lp - 2 * SEQ_PAD
    vec = lambda a: a.reshape(DEPTH, 1, CONV_WIDTH)
    vspec = pl.BlockSpec((1, 1, CONV_WIDTH), lambda s, t: (layer, 0, 0))
    return pl.pallas_call(
        _conv_kernel,
        grid=(nseq, seq_len // SEQ_T),
        in_specs=[
            pl.BlockSpec((1, lp, 2 * CONV_WIDTH), lambda s, t: (s, 0, 0)),
            pl.BlockSpec((1, CONV_TAPS, CONV_WIDTH), lambda s, t: (layer, 0, 0)),
            vspec, vspec, vspec,
        ],
        out_specs=pl.BlockSpec((1, SEQ_T, CONV_WIDTH), lambda s, t: (s, t, 0)),
        out_shape=jax.ShapeDtypeStruct((nseq, seq_len, CONV_WIDTH), BF16),
        scratch_shapes=[pltpu.VMEM((SEQ_T + 2 * SEQ_PAD, CONV_WIDTH), F32)],
        compiler_params=_cparams(("arbitrary", "arbitrary"), 40),
        name="conv_mixer",
    )(u_pad, dw_w, vec(dw_b), vec(ln_g), vec(ln_b))


def _merge_kernel(oa_ref, ys_ref, yp_ref, yc_ref, g0_ref, g1_ref, g2_ref, g3_ref,
                  wa_ref, ws_ref, wp_ref, wc_ref, o_ref):
    def branch(x_ref, w_ref, g_ref):
        up = jnp.dot(x_ref[...], w_ref[0], preferred_element_type=F32)
        return jax.nn.sigmoid(g_ref[...].astype(F32)) * up

    m = (branch(oa_ref, wa_ref, g0_ref) + branch(ys_ref, ws_ref, g1_ref)
         + branch(yp_ref, wp_ref, g2_ref) + branch(yc_ref, wc_ref, g3_ref))
    o_ref[...] = m.astype(o_ref.dtype)


def merge_branches(p, o_attn, y_ssm, y_pool, y_conv, w_attn, w_ssm, w_pool, w_conv, layer):
    n = o_attn.shape[0]
    nct = D_MODEL // TN
    gate_tile0 = COL_GATE // TN

    def gate_spec(br):
        return pl.BlockSpec((TM, TN), lambda i, j: (i, gate_tile0 + br * nct + j))

    def x_spec(width):
        return pl.BlockSpec((TM, width), lambda i, j: (i, 0))

    def w_spec(width):
        return pl.BlockSpec((1, width, TN), lambda i, j: (layer, 0, j))

    return pl.pallas_call(
        _merge_kernel,
        grid=(n // TM, nct),
        in_specs=[x_spec(Q_W), x_spec(SSM_WIDTH), x_spec(POOL_WIDTH), x_spec(CONV_WIDTH),
                  gate_spec(0), gate_spec(1), gate_spec(2), gate_spec(3),
                  w_spec(Q_W), w_spec(SSM_WIDTH), w_spec(POOL_WIDTH), w_spec(CONV_WIDTH)],
        out_specs=pl.BlockSpec((TM, TN), lambda i, j: (i, j)),
        out_shape=jax.ShapeDtypeStruct((n, D_MODEL), BF16),
        compiler_params=_cparams(("arbitrary", "arbitrary"), 40),
        name="merge_branches",
    )(o_attn, y_ssm, y_pool, y_conv, p, p, p, p, w_attn, w_ssm, w_pool, w_conv)


def _split_bf16(x):
    hi = x.astype(BF16)
    lo = (x - hi.astype(F32)).astype(BF16)
    return hi, lo


def _outproj_kernel(m_ref, x_ref, mod_ref, g_ref, w_ref, rw_ref, xo_ref, h_ref, lg_ref):
    mix = jnp.dot(m_ref[...], w_ref[0], preferred_element_type=F32)
    x = x_ref[...] + mod_ref[0, 2:3, :] * mix
    xo_ref[...] = x
    ms = jnp.mean(x * x, axis=-1, keepdims=True)
    h = x * lax.rsqrt(ms + EPS) * g_ref[0]
    h = h * (1.0 + mod_ref[0, 4:5, :]) + mod_ref[0, 3:4, :]
    h_ref[...] = h
    h_hi, h_lo = _split_bf16(h)
    r_hi, r_lo = _split_bf16(rw_ref[...])
    nt = (((1,), (1,)), ((), ()))
    lg_ref[...] = (lax.dot_general(r_hi, h_hi, nt, preferred_element_type=F32)
                   + lax.dot_general(r_hi, h_lo, nt, preferred_element_type=F32)
                   + lax.dot_general(r_lo, h_hi, nt, preferred_element_type=F32))


def out_projection(m, xa, mod, norm2_g, w_out, router_wt, layer):
    n = m.shape[0]
    tmo = TM // 2
    return pl.pallas_call(
        _outproj_kernel,
        grid=(n // tmo,),
        in_specs=[
            pl.BlockSpec((tmo, D_MODEL), lambda i: (i, 0)),
            pl.BlockSpec((tmo, D_MODEL), lambda i: (i, 0)),
            pl.BlockSpec((1, 6, D_MODEL), lambda i: (_mod_id(i // 2), 0, 0)),
            pl.BlockSpec((1, 1, D_MODEL), lambda i: (layer, 0, 0)),
            pl.BlockSpec((1, D_MODEL, D_MODEL), lambda i: (layer, 0, 0)),
            pl.BlockSpec((N_EXPERTS, D_MODEL), lambda i: (0, 0)),
        ],
        out_specs=[
            pl.BlockSpec((tmo, D_MODEL), lambda i: (i, 0)),
            pl.BlockSpec((tmo, D_MODEL), lambda i: (i, 0)),
            pl.BlockSpec((N_EXPERTS, tmo), lambda i: (0, i)),
        ],
        out_shape=[
            jax.ShapeDtypeStruct((n, D_MODEL), F32),
            jax.ShapeDtypeStruct((n, D_MODEL), F32),
            jax.ShapeDtypeStruct((N_EXPERTS, n), F32),
        ],
        compiler_params=_cparams(("arbitrary",), 56),
        name="out_projection",
    )(m, xa, mod, norm2_g.reshape(DEPTH, 1, D_MODEL), w_out, router_wt)


def _first_argmax(blk, row):
    m = jnp.max(blk, axis=0, keepdims=True)
    idx = jnp.min(jnp.where(blk == m, row, EXPERTS_PER_GROUP), axis=0, keepdims=True)
    return m, idx


def _route_kernel(lg_ref, rb_ref, e_ref, w_ref, rank_ref, cnt_ref, run_ref):
    @pl.when(pl.program_id(0) == 0)
    def _():
        run_ref[...] = jnp.zeros_like(run_ref)

    t = lg_ref.shape[1]
    scores = jax.nn.sigmoid(lg_ref[...])
    sel = scores + rb_ref[...]
    row = lax.broadcasted_iota(jnp.int32, (EXPERTS_PER_GROUP, t), 0)
    neg = jnp.float32(-jnp.inf)

    best = None
    for g in range(N_EXPERT_GROUPS):
        blk = sel[g * EXPERTS_PER_GROUP:(g + 1) * EXPERTS_PER_GROUP, :]
        m1, i1 = _first_argmax(blk, row)
        m2 = jnp.max(jnp.where(row == i1, neg, blk), axis=0, keepdims=True)
        gs = m1 + m2
        if best is None:
            best, grp = gs, jnp.zeros((1, t), jnp.int32)
        else:
            better = gs > best
            best = jnp.where(better, gs, best)
            grp = jnp.where(better, g, grp)

    in_sel = jnp.zeros((EXPERTS_PER_GROUP, t), F32)
    in_sc = jnp.zeros((EXPERTS_PER_GROUP, t), F32)
    for g in range(N_EXPERT_GROUPS):
        rows = slice(g * EXPERTS_PER_GROUP, (g + 1) * EXPERTS_PER_GROUP)
        in_sel = jnp.where(grp == g, sel[rows, :], in_sel)
        in_sc = jnp.where(grp == g, scores[rows, :], in_sc)
    _, i1 = _first_argmax(in_sel, row)
    _, i2 = _first_argmax(jnp.where(row == i1, neg, in_sel), row)
    s1 = jnp.sum(jnp.where(row == i1, in_sc, 0.0), axis=0, keepdims=True)
    s2 = jnp.sum(jnp.where(row == i2, in_sc, 0.0), axis=0, keepdims=True)
    e1 = grp * EXPERTS_PER_GROUP + i1
    e2 = grp * EXPERTS_PER_GROUP + i2
    e_ref[0:1, :] = e1
    e_ref[1:2, :] = e2
    w_ref[0:1, :] = s1 / (s1 + s2)
    w_ref[1:2, :] = s2 / (s1 + s2)

    erow = lax.broadcasted_iota(jnp.int32, (N_EXPERTS, t), 0)
    oh1 = erow == e1
    oh2 = erow == e2
    cnt = jnp.where(oh1 | oh2, 1.0, 0.0)
    a = lax.broadcasted_iota(jnp.int32, (t, t), 0)
    b = lax.broadcasted_iota(jnp.int32, (t, t), 1)
    before = jnp.where(a < b, 1.0, 0.0).astype(BF16)
    excl = jnp.dot(cnt.astype(BF16), before, preferred_element_type=F32) + run_ref[...]
    rank_ref[0:1, :] = jnp.sum(jnp.where(oh1, excl, 0.0), axis=0, keepdims=True).astype(jnp.int32)
    rank_ref[1:2, :] = jnp.sum(jnp.where(oh2, excl, 0.0), axis=0, keepdims=True).astype(jnp.int32)
    run_ref[...] = run_ref[...] + jnp.sum(cnt, axis=1, keepdims=True)
    cnt_ref[...] = run_ref[...]


def route(logits_t, router_b):
    n = logits_t.shape[1]
    return pl.pallas_call(
        _route_kernel,
        grid=(n // ROUTE_T,),
        in_specs=[
            pl.BlockSpec((N_EXPERTS, ROUTE_T), lambda i: (0, i)),
            pl.BlockSpec((N_EXPERTS, 1), lambda i: (0, 0)),
        ],
        out_specs=[
            pl.BlockSpec((TOP_K, ROUTE_T), lambda i: (0, i)),
            pl.BlockSpec((TOP_K, ROUTE_T), lambda i: (0, i)),
            pl.BlockSpec((TOP_K, ROUTE_T), lambda i: (0, i)),
            pl.BlockSpec((N_EXPERTS, 1), lambda i: (0, 0)),
        ],
        out_shape=[
            jax.ShapeDtypeStruct((TOP_K, n), jnp.int32),
            jax.ShapeDtypeStruct((TOP_K, n), F32),
            jax.ShapeDtypeStruct((TOP_K, n), jnp.int32),
            jax.ShapeDtypeStruct((N_EXPERTS, 1), F32),
        ],
        scratch_shapes=[pltpu.VMEM((N_EXPERTS, 1), F32)],
        compiler_params=_cparams(("arbitrary",), 32),
        name="route",
    )(logits_t, router_b.reshape(N_EXPERTS, 1).astype(F32))


def _row_copy(src, src_row, dst, dst_row, sem):
    return pltpu.make_async_copy(src.at[pl.ds(src_row, 1)], dst.at[pl.ds(dst_row, 1)], sem)


def _dispatch_kernel(dest_ref, h_ref, xs_in_ref, xs_ref, sem):
    del xs_in_ref
    n = dest_ref.shape[0] // TOP_K
    base = pl.program_id(0) * GATHER_T

    def start(t, carry):
        for k in range(TOP_K):
            _row_copy(h_ref, t, xs_ref, dest_ref[k * n + base + t], sem).start()
        return carry

    def wait(t, carry):
        for k in range(TOP_K):
            _row_copy(h_ref, t, xs_ref, dest_ref[k * n + base + t], sem).wait()
        return carry

    lax.fori_loop(0, GATHER_T, start, 0)
    lax.fori_loop(0, GATHER_T, wait, 0)


def dispatch(dest_flat, h2, n_slots):
    n = h2.shape[0]
    xs0 = jnp.zeros((n_slots, D_MODEL), F32)
    grid_spec = pltpu.PrefetchScalarGridSpec(
        num_scalar_prefetch=1,
        grid=(n // GATHER_T,),
        in_specs=[pl.BlockSpec((GATHER_T, D_MODEL), lambda i, d: (i, 0)), pl.BlockSpec(memory_space=pl.ANY)],
        out_specs=pl.BlockSpec(memory_space=pl.ANY),
        scratch_shapes=[pltpu.SemaphoreType.DMA(())],
    )
    return pl.pallas_call(
        _dispatch_kernel,
        grid_spec=grid_spec,
        out_shape=jax.ShapeDtypeStruct((n_slots, D_MODEL), F32),
        input_output_aliases={2: 0},
        compiler_params=_cparams(("arbitrary",), 32),
        name="dispatch",
    )(dest_flat, h2, xs0)


def _expert_kernel(be_ref, nu_ref, x_ref, wg_ref, wu_ref, wd_ref, o_ref, wgb_ref, wub_ref, wdb_ref):
    b = pl.program_id(0)
    prev = be_ref[jnp.maximum(b - 1, 0)]
    fresh = (b == 0) | (be_ref[b] != prev)

    @pl.when(fresh & (b < nu_ref[0]))
    def _():
        wgb_ref[...] = wg_ref[0].astype(BF16)
        wub_ref[...] = wu_ref[0].astype(BF16)
        wdb_ref[...] = wd_ref[0].astype(BF16)

    @pl.when(b < nu_ref[0])
    def _():
        x = x_ref[...].astype(BF16)
        gate = jnp.dot(x, wgb_ref[...], preferred_element_type=F32)
        up = jnp.dot(x, wub_ref[...], preferred_element_type=F32)
        hid = (_silu(gate) * up).astype(BF16)
        o_ref[...] = jnp.dot(hid, wdb_ref[...], preferred_element_type=F32)

    @pl.when(b >= nu_ref[0])
    def _():
        o_ref[...] = jnp.zeros_like(o_ref)


def experts(blk_e, n_used, xs, w_gate, w_up, w_down, layer):
    n_slots = xs.shape[0]
    n_blocks = n_slots // MOE_BLOCK

    def live(b, nu):
        return jnp.minimum(b, nu[0] - 1)

    grid_spec = pltpu.PrefetchScalarGridSpec(
        num_scalar_prefetch=2,
        grid=(n_blocks,),
        in_specs=[
            pl.BlockSpec((MOE_BLOCK, D_MODEL), lambda b, be, nu: (live(b, nu), 0)),
            pl.BlockSpec((1, 1, D_MODEL, EXPERT_FF), lambda b, be, nu: (layer, be[live(b, nu)], 0, 0)),
            pl.BlockSpec((1, 1, D_MODEL, EXPERT_FF), lambda b, be, nu: (layer, be[live(b, nu)], 0, 0)),
            pl.BlockSpec((1, 1, EXPERT_FF, D_MODEL), lambda b, be, nu: (layer, be[live(b, nu)], 0, 0)),
        ],
        out_specs=pl.BlockSpec((MOE_BLOCK, D_MODEL), lambda b, be, nu: (b, 0)),
        scratch_shapes=[
            pltpu.VMEM((D_MODEL, EXPERT_FF), BF16),
            pltpu.VMEM((D_MODEL, EXPERT_FF), BF16),
            pltpu.VMEM((EXPERT_FF, D_MODEL), BF16),
        ],
    )

    def kern(be_ref, nu_ref, x_ref, wg_ref, wu_ref, wd_ref, o_ref, wgb_ref, wub_ref, wdb_ref):
        _expert_kernel(be_ref, nu_ref, x_ref, wg_ref.at[0], wu_ref.at[0], wd_ref.at[0], o_ref,
                       wgb_ref, wub_ref, wdb_ref)

    return pl.pallas_call(
        kern,
        grid_spec=grid_spec,
        out_shape=jax.ShapeDtypeStruct((n_slots, D_MODEL), F32),
        compiler_params=_cparams(("arbitrary",), 56),
        name="experts",
    )(blk_e, n_used, xs, w_gate, w_up, w_down)


def _combine_kernel(dest_ref, x_ref, mod_ref, w_ref, fg_ref, ys_ref, o_ref, buf_ref, sem, *, final_norm):
    n = dest_ref.shape[0] // TOP_K
    base = pl.program_id(0) * GATHER_T

    def start(t, carry):
        for k in range(TOP_K):
            _row_copy(ys_ref, dest_ref[k * n + base + t], buf_ref.at[k], t, sem).start()
        return carry

    def wait(t, carry):
        for k in range(TOP_K):
            _row_copy(ys_ref, dest_ref[k * n + base + t], buf_ref.at[k], t, sem).wait()
        return carry

    lax.fori_loop(0, GATHER_T, start, 0)
    lax.fori_loop(0, GATHER_T, wait, 0)
    f = w_ref[:, 0:1] * buf_ref[0] + w_ref[:, 1:2] * buf_ref[1]
    x = x_ref[...] + mod_ref[0, 5:6, :] * f
    if final_norm:
        ms = jnp.mean(x * x, axis=-1, keepdims=True)
        x = x * lax.rsqrt(ms + EPS) * fg_ref[...]
    o_ref[...] = x


def combine(dest_flat, xa, mod, wts_rows, ys, final_g, final_norm):
    n = xa.shape[0]
    tiles_per_tm = TM // GATHER_T
    grid_spec = pltpu.PrefetchScalarGridSpec(
        num_scalar_prefetch=1,
        grid=(n // GATHER_T,),
        in_specs=[
            pl.BlockSpec((GATHER_T, D_MODEL), lambda i, d: (i, 0)),
            pl.BlockSpec((1, 6, D_MODEL), lambda i, d: (_mod_id(i // tiles_per_tm), 0, 0)),
            pl.BlockSpec((GATHER_T, TOP_K), lambda i, d: (i, 0)),
            pl.BlockSpec((1, D_MODEL), lambda i, d: (0, 0)),
            pl.BlockSpec(memory_space=pl.ANY),
        ],
        out_specs=pl.BlockSpec((GATHER_T, D_MODEL), lambda i, d: (i, 0)),
        scratch_shapes=[pltpu.VMEM((TOP_K, GATHER_T, D_MODEL), F32), pltpu.SemaphoreType.DMA(())],
    )
    kern = functools.partial(_combine_kernel, final_norm=final_norm)
    return pl.pallas_call(
        kern,
        grid_spec=grid_spec,
        out_shape=jax.ShapeDtypeStruct((n, D_MODEL), F32),
        compiler_params=_cparams(("arbitrary",), 32),
        name="combine",
    )(dest_flat, xa, mod, wts_rows, final_g.reshape(1, D_MODEL), ys)


def _slot_plan(e_idx, rank, counts, n_blocks):
    cnt = counts.reshape(N_EXPERTS).astype(jnp.int32)
    padded = (cnt + MOE_BLOCK - 1) // MOE_BLOCK * MOE_BLOCK
    pad_end = jnp.cumsum(padded)
    pad_start = pad_end - padded
    dest = pad_start[e_idx] + rank
    blk_start = jnp.arange(n_blocks, dtype=jnp.int32) * MOE_BLOCK
    blk_e = jnp.minimum(jnp.searchsorted(pad_end, blk_start, side='right'), N_EXPERTS - 1).astype(jnp.int32)
    n_used = (pad_end[-1] // MOE_BLOCK).astype(jnp.int32).reshape(1)
    return dest.reshape(-1).astype(jnp.int32), blk_e, n_used


def _pad_rows(u):
    return jnp.pad(u, ((0, 0), (SEQ_PAD, SEQ_PAD), (0, 0)))


def _layer(layer, xa, mod, W, with_ctx, final_norm):
    n_lat_tiles = N_LAT // TM
    n_ctx_tiles = N_CTX // TM
    n_col_tiles = IN_W // TN
    if with_ctx:
        p = in_projection(xa, mod, W['norm1_g'], W['w_in'], layer, 0, n_lat_tiles + n_ctx_tiles, n_col_tiles)
        p_lat, p_ctx = p[:N_LAT], p[N_LAT:]
    else:
        p = in_projection(xa, mod, W['norm1_g'], W['w_in'], layer, 0, n_lat_tiles, n_col_tiles)
        p_lat = p
        p_ctx = in_projection(xa, mod, W['norm1_g'], W['w_in'], layer, n_lat_tiles, n_ctx_tiles, CTX_STATE_W // TN)
    n_rows = p.shape[0]

    def seq(a, lo, hi, n_seq_rows):
        return a[:, lo:hi].reshape(BATCH, n_seq_rows, hi - lo)

    kv = jnp.concatenate([seq(p_ctx, 0, 2 * KV_W, CTX_LEN), seq(p_lat, 0, 2 * KV_W, SEQ)], axis=1)
    cos, sin = _rope_tables(CTX_LEN)
    o_attn = attention(p, COL_Q, 0, SEQ, kv, cos, sin, CTX_LEN // TQ, W['q_norm_g'], W['k_norm_g'], layer)
    if with_ctx:
        kv_c = seq(p_ctx, 0, 2 * KV_W, CTX_LEN)
        ones, zeros = jnp.ones((CTX_LEN, HEAD_DIM), F32), jnp.zeros((CTX_LEN, HEAD_DIM), F32)
        o_attn_c = attention(p, COL_Q, N_LAT // TQ, CTX_LEN, kv_c, ones, zeros, 0,
                             W['q_norm_g'], W['k_norm_g'], layer)
        o_attn = jnp.concatenate([o_attn, o_attn_c], axis=0)

    u_seq = jnp.concatenate([seq(p_ctx, COL_SSM, COL_SSM + SSM_WIDTH, CTX_LEN),
                             seq(p_lat, COL_SSM, COL_SSM + SSM_WIDTH, SEQ)], axis=1)
    u_chunks = u_seq.reshape(BATCH * SSM_CHUNKS, SSM_T, SSM_WIDTH)
    bm, cm, lam_t, pw_t = _ssm_params(W['ssm_a_re'][layer], W['ssm_a_im'][layer], W['ssm_log_dt'][layer],
                                      W['ssm_b_re'][layer], W['ssm_b_im'][layer],
                                      W['ssm_c_re'][layer], W['ssm_c_im'][layer])
    yf = ssm_scan(u_chunks, bm[0], cm[0], lam_t[0], pw_t[0], reverse=False)
    yb = ssm_scan(u_chunks, bm[1], cm[1], lam_t[1], pw_t[1], reverse=True)
    y_ssm = ssm_output(p, yf, yb, W['ssm_d'], W['ssm_glu_w'], W['ssm_glu_b'], layer)

    def seq_mixers(src, n_seq_rows):
        up = _pad_rows(seq(src, COL_POOL, COL_POOL + POOL_WIDTH, n_seq_rows))
        uc = _pad_rows(seq(src, COL_CONV, COL_CONV + 2 * CONV_WIDTH, n_seq_rows))
        yp = pool_mixer(up, W['pool_w'], W['pool_scale'], layer)
        yc = conv_mixer(uc, W['conv_dw_w'], W['conv_dw_b'], W['conv_ln_g'], W['conv_ln_b'], layer)
        return yp.reshape(-1, POOL_WIDTH), yc.reshape(-1, CONV_WIDTH)

    y_pool, y_conv = seq_mixers(p_lat, SEQ)
    if with_ctx:
        yp_c, yc_c = seq_mixers(p_ctx, CTX_LEN)
        y_pool = jnp.concatenate([y_pool, yp_c], axis=0)
        y_conv = jnp.concatenate([y_conv, yc_c], axis=0)

    m = merge_branches(p, o_attn, y_ssm, y_pool, y_conv,
                       W['w_up_attn'], W['w_up_ssm'], W['w_up_pool'], W['w_up_conv'], layer)
    x1, h2, logits_t = out_projection(m, xa, mod, W['norm2_g'], W['w_out'], W['router_wt'], layer)

    e_idx, wts, rank, counts = route(logits_t, W['router_b'])
    n_blocks = -(-n_rows * TOP_K // MOE_BLOCK) + N_EXPERTS
    dest_flat, blk_e, n_used = _slot_plan(e_idx, rank, counts, n_blocks)
    xs = dispatch(dest_flat, h2, n_blocks * MOE_BLOCK)
    ys = experts(blk_e, n_used, xs, W['moe_w_gate'], W['moe_w_up'], W['moe_w_down'], layer)
    x2 = combine(dest_flat, x1, mod, wts.T, ys, W['final_g'], final_norm)
    if with_ctx:
        return x2
    return jnp.concatenate([x2, xa[N_LAT:]], axis=0)


def kernel(x, c, ctx, c_ctx, ada_w, ada_b, norm1_g, norm2_g, w_in, q_norm_g, k_norm_g, ssm_a_re, ssm_a_im, ssm_log_dt, ssm_b_re, ssm_b_im, ssm_c_re, ssm_c_im, ssm_d, ssm_glu_w, ssm_glu_b, pool_w, pool_scale, conv_dw_w, conv_dw_b, conv_ln_g, conv_ln_b, w_up_attn, w_up_ssm, w_up_pool, w_up_conv, w_out, router_w, router_b, moe_w_gate, moe_w_up, moe_w_down, final_g):
    W = dict(
        norm1_g=norm1_g, norm2_g=norm2_g, q_norm_g=q_norm_g, k_norm_g=k_norm_g,
        w_in=w_in.astype(BF16),
        ssm_a_re=ssm_a_re, ssm_a_im=ssm_a_im, ssm_log_dt=ssm_log_dt, ssm_b_re=ssm_b_re, ssm_b_im=ssm_b_im,
        ssm_c_re=ssm_c_re, ssm_c_im=ssm_c_im, ssm_d=ssm_d, ssm_glu_w=ssm_glu_w.astype(BF16), ssm_glu_b=ssm_glu_b,
        pool_w=pool_w.astype(BF16), pool_scale=pool_scale,
        conv_dw_w=conv_dw_w, conv_dw_b=conv_dw_b, conv_ln_g=conv_ln_g, conv_ln_b=conv_ln_b,
        w_up_attn=w_up_attn.astype(BF16), w_up_ssm=w_up_ssm.astype(BF16), w_up_pool=w_up_pool.astype(BF16),
        w_up_conv=w_up_conv.astype(BF16), w_out=w_out.astype(BF16),
        router_wt=router_w.T, router_b=router_b,
        moe_w_gate=moe_w_gate, moe_w_up=moe_w_up, moe_w_down=moe_w_down, final_g=final_g,
    )
    cc = jnp.concatenate([c, c_ctx[None, :], jnp.zeros((SUBLANES - BATCH - 1, D_MODEL), F32)], axis=0)
    mod_all = ada_modulation(cc, ada_w, ada_b)
    xa = jnp.concatenate([x.reshape(N_LAT, D_MODEL), ctx.reshape(N_CTX, D_MODEL)], axis=0)
    for layer in range(DEPTH):
        mod = mod_all[layer, :BATCH + 1].reshape(BATCH + 1, 6, D_MODEL)
        xa = _layer(layer, xa, mod, W, with_ctx=(layer < DEPTH - 1), final_norm=(layer == DEPTH - 1))
    return xa[:N_LAT].reshape(BATCH, SEQ, D_MODEL)
```

```python
import functools
import math

import jax
import jax.numpy as jnp
from jax import lax
from jax.experimental import pallas as pl
from jax.experimental.pallas import tpu as pltpu

F32 = jnp.float32
BF16 = jnp.bfloat16

D_MODEL = 2048
BATCH = 2
SEQ = 4096
DEPTH = 2
GRID_W = 64
CTX_LEN = 256
N_HEADS = 8
N_KV_HEADS = 2
HEAD_DIM = 128
ROPE_THETA = 10000.0
SSM_WIDTH = 512
SSM_GROUP = 16
SSM_GROUPS = SSM_WIDTH // SSM_GROUP
SSM_STATE = 64
POOL_WIDTH = 512
POOL_WINDOWS = (2, 4, 8, 16)
POOL_GROUP = POOL_WIDTH // len(POOL_WINDOWS)
CONV_WIDTH = 512
CONV_TAPS = 31
N_BRANCHES = 4
N_EXPERTS = 64
N_EXPERT_GROUPS = 8
EXPERTS_PER_GROUP = N_EXPERTS // N_EXPERT_GROUPS
TOP_K = 2
EXPERT_FF = 512
EPS = 1e-6

Q_W = N_HEADS * HEAD_DIM
KV_W = N_KV_HEADS * HEAD_DIM
IN_W = 2 * KV_W + SSM_WIDTH + Q_W + POOL_WIDTH + 2 * CONV_WIDTH + N_BRANCHES * D_MODEL
CTX_STATE_W = 2 * KV_W + SSM_WIDTH
COL_K, COL_V, COL_SSM, COL_Q = 0, KV_W, 2 * KV_W, 2 * KV_W + SSM_WIDTH
COL_POOL = COL_Q + Q_W
COL_CONV = COL_POOL + POOL_WIDTH
COL_GATE = COL_CONV + 2 * CONV_WIDTH

N_LAT = BATCH * SEQ
N_CTX = BATCH * CTX_LEN
S_ALL = CTX_LEN + SEQ

V7X_VMEM_BYTES = 64 * 1024 * 1024
SUBLANES = 8
LANES = 128
BF16_ROWS = 16

TM = 512
TN = 512
TILES_PER_BATCH = SEQ // TM
TQ = 256
SSM_T = 256
SSM_TC = SSM_T // SUBLANES
SSM_HALF = SSM_WIDTH // 2
SSM_HSTATE = SSM_GROUPS // 2 * SSM_STATE
SEQ_T = 256
SEQ_PAD = 16
MOE_BLOCK = 128
ROUTE_T = 512
GATHER_T = 256


def _cparams(sem, vmem_mb):
    return pltpu.CompilerParams(dimension_semantics=sem, vmem_limit_bytes=vmem_mb * 1024 * 1024)


def _mod_id(tile):
    return jnp.minimum(tile // TILES_PER_BATCH, BATCH)


def _silu(x):
    return x * jax.nn.sigmoid(x)


def _ada_kernel(c_ref, w_ref, b_ref, o_ref):
    c = c_ref[...]
    a = _silu(c).astype(BF16)
    o_ref[0] = jnp.dot(a, w_ref[0].astype(BF16), preferred_element_type=F32) + b_ref[0]


def ada_modulation(cc, ada_w, ada_b):
    tn = 1024
    n6 = 6 * D_MODEL
    return pl.pallas_call(
        _ada_kernel,
        grid=(DEPTH, n6 // tn),
        in_specs=[
            pl.BlockSpec((SUBLANES, D_MODEL), lambda l, j: (0, 0)),
            pl.BlockSpec((1, D_MODEL, tn), lambda l, j: (l, 0, j)),
            pl.BlockSpec((1, 1, tn), lambda l, j: (l, 0, j)),
        ],
        out_specs=pl.BlockSpec((1, SUBLANES, tn), lambda l, j: (l, 0, j)),
        out_shape=jax.ShapeDtypeStruct((DEPTH, SUBLANES, n6), F32),
        compiler_params=_cparams(("arbitrary", "arbitrary"), 40),
        name="ada_modulation",
    )(cc, ada_w, ada_b.reshape(DEPTH, 1, n6))


def _inproj_kernel(x_ref, mod_ref, g_ref, w_ref, o_ref, h_ref):
    @pl.when(pl.program_id(1) == 0)
    def _():
        x = x_ref[...]
        ms = jnp.mean(x * x, axis=-1, keepdims=True)
        y = x * lax.rsqrt(ms + EPS) * g_ref[0]
        h = y * (1.0 + mod_ref[0, 1:2, :]) + mod_ref[0, 0:1, :]
        h_ref[...] = h.astype(BF16)

    o_ref[...] = jnp.dot(h_ref[...], w_ref[0, 0], preferred_element_type=F32).astype(o_ref.dtype)


def in_projection(xa, mod, norm_g, w_in, layer, row_tile0, n_row_tiles, n_col_tiles):
    return pl.pallas_call(
        _inproj_kernel,
        grid=(n_row_tiles, n_col_tiles),
        in_specs=[
            pl.BlockSpec((TM, D_MODEL), lambda i, j: (i + row_tile0, 0)),
            pl.BlockSpec((1, 6, D_MODEL), lambda i, j: (_mod_id(i + row_tile0), 0, 0)),
            pl.BlockSpec((1, 1, D_MODEL), lambda i, j: (layer, 0, 0)),
            pl.BlockSpec((1, 1, D_MODEL, TN), lambda i, j: (layer, j, 0, 0)),
        ],
        out_specs=pl.BlockSpec((TM, TN), lambda i, j: (i, j)),
        out_shape=jax.ShapeDtypeStruct((n_row_tiles * TM, n_col_tiles * TN), BF16),
        scratch_shapes=[pltpu.VMEM((TM, D_MODEL), BF16)],
        compiler_params=_cparams(("arbitrary", "arbitrary"), 40),
        name="in_projection",
    )(xa, mod, norm_g.reshape(DEPTH, 1, D_MODEL), w_in)


def _rope_tables(n_ctx_rows):
    half = HEAD_DIM // 4
    inv_freq = ROPE_THETA ** (-jnp.arange(half, dtype=F32) / half)
    t = jnp.arange(SEQ)
    ang_r = (t // GRID_W).astype(F32)[:, None] * inv_freq[None, :]
    ang_c = (t % GRID_W).astype(F32)[:, None] * inv_freq[None, :]
    cos = jnp.concatenate([jnp.cos(ang_r), jnp.cos(ang_r), jnp.cos(ang_c), jnp.cos(ang_c)], axis=-1)
    sin = jnp.concatenate([-jnp.sin(ang_r), jnp.sin(ang_r), -jnp.sin(ang_c), jnp.sin(ang_c)], axis=-1)
    cos = jnp.concatenate([jnp.ones((n_ctx_rows, HEAD_DIM), F32), cos], axis=0)
    sin = jnp.concatenate([jnp.zeros((n_ctx_rows, HEAD_DIM), F32), sin], axis=0)
    return cos, sin


def _head_norm_rope(x, g, cos, sin):
    ms = jnp.mean(x * x, axis=-1, keepdims=True)
    y = x * lax.rsqrt(ms + EPS) * g
    lane = lax.broadcasted_iota(jnp.int32, y.shape, 1)
    first = (lane % (HEAD_DIM // 2)) < (HEAD_DIM // 4)
    partner = jnp.where(first, pltpu.roll(y, HEAD_DIM - HEAD_DIM // 4, 1), pltpu.roll(y, HEAD_DIM // 4, 1))
    return y * cos + partner * sin


def _attn_kernel(q_ref, k_ref, v_ref, cq_ref, sq_ref, ck_ref, sk_ref, gq_ref, gk_ref, o_ref, ks_ref, *, n_keys):
    @pl.when(pl.program_id(2) == 0)
    def _():
        def prep(c, carry):
            r0 = pl.multiple_of(c * TQ, TQ)
            kk = k_ref[0, pl.ds(r0, TQ), :].astype(F32)
            kn = _head_norm_rope(kk, gk_ref[0], ck_ref[pl.ds(r0, TQ), :], sk_ref[pl.ds(r0, TQ), :])
            ks_ref[pl.ds(r0, TQ), :] = kn.astype(BF16)
            return carry

        lax.fori_loop(0, n_keys // TQ, prep, 0)

    k = ks_ref[...]
    v = v_ref[0]
    scale = HEAD_DIM ** -0.5
    for hh in range(N_HEADS // N_KV_HEADS):
        cols = slice(hh * HEAD_DIM, (hh + 1) * HEAD_DIM)
        q = q_ref[:, cols].astype(F32)
        qn = _head_norm_rope(q, gq_ref[0], cq_ref[...], sq_ref[...]) * scale
        s = lax.dot_general(qn.astype(BF16), k, (((1,), (1,)), ((), ())), preferred_element_type=F32)
        m = jnp.max(s, axis=-1, keepdims=True)
        p = jnp.exp(s - m)
        l = jnp.sum(p, axis=-1, keepdims=True)
        o = jnp.dot(p.astype(BF16), v, preferred_element_type=F32) / l
        o_ref[:, cols] = o.astype(o_ref.dtype)


def attention(q_src, q_col0, q_row_tile0, n_q, kv, cos, sin, q_table_tile0, q_norm_g, k_norm_g, layer):
    n_keys = kv.shape[1]
    grp_w = Q_W // N_KV_HEADS
    qb = n_q // TQ
    kern = functools.partial(_attn_kernel, n_keys=n_keys)
    return pl.pallas_call(
        kern,
        grid=(BATCH, N_KV_HEADS, qb),
        in_specs=[
            pl.BlockSpec((TQ, grp_w), lambda b, g, i: (q_row_tile0 + b * qb + i, q_col0 // grp_w + g)),
            pl.BlockSpec((1, n_keys, HEAD_DIM), lambda b, g, i: (b, 0, g)),
            pl.BlockSpec((1, n_keys, HEAD_DIM), lambda b, g, i: (b, 0, N_KV_HEADS + g)),
            pl.BlockSpec((TQ, HEAD_DIM), lambda b, g, i: (q_table_tile0 + i, 0)),
            pl.BlockSpec((TQ, HEAD_DIM), lambda b, g, i: (q_table_tile0 + i, 0)),
            pl.BlockSpec((n_keys, HEAD_DIM), lambda b, g, i: (0, 0)),
            pl.BlockSpec((n_keys, HEAD_DIM), lambda b, g, i: (0, 0)),
            pl.BlockSpec((1, 1, HEAD_DIM), lambda b, g, i: (layer, 0, 0)),
            pl.BlockSpec((1, 1, HEAD_DIM), lambda b, g, i: (layer, 0, 0)),
        ],
        out_specs=pl.BlockSpec((TQ, grp_w), lambda b, g, i: (b * qb + i, g)),
        out_shape=jax.ShapeDtypeStruct((BATCH * n_q, Q_W), BF16),
        scratch_shapes=[pltpu.VMEM((n_keys, HEAD_DIM), BF16)],
        compiler_params=_cparams(("arbitrary", "arbitrary", "arbitrary"), 48),
        name="attention",
    )(q_src, kv, kv, cos, sin, cos, sin,
      q_norm_g.reshape(DEPTH, 1, HEAD_DIM), k_norm_g.reshape(DEPTH, 1, HEAD_DIM))


def _cmul(ar, ai, br, bi):
    return ar * br - ai * bi, ar * bi + ai * br


SSM_CB = 512


def _ssm_kernel(u_ref, bm_ref, cm_ref, lam_ref, pw_ref, y_ref, bu_ref, yv_ref, carry_ref, *, reverse):
    @pl.when(pl.program_id(1) == 0)
    def _():
        carry_ref[...] = jnp.zeros_like(carry_ref)

    rr = lax.broadcasted_iota(jnp.int32, (SSM_T, SSM_T), 0)
    tt = lax.broadcasted_iota(jnp.int32, (SSM_T, SSM_T), 1)
    perm = jnp.where(tt == (rr % SUBLANES) * SSM_TC + rr // SUBLANES, 1.0, 0.0).astype(BF16)
    u_scan = jnp.dot(perm, u_ref[0], preferred_element_type=F32).astype(BF16)
    for h in range(2):
        bu_ref[h] = jnp.dot(u_scan[:, h * SSM_HALF:(h + 1) * SSM_HALF], bm_ref[h], preferred_element_type=F32)

    last_row = 0 if reverse else (SSM_T - SUBLANES)
    edge = (SUBLANES - 1) if reverse else 0
    toward = (lambda x, k: pltpu.roll(x, SUBLANES - k, 0)) if reverse else (lambda x, k: pltpu.roll(x, k, 0))

    def step_rows(j):
        jj = (SSM_TC - 1 - j) if reverse else j
        return pl.ds(pl.multiple_of(jj * SUBLANES, SUBLANES), SUBLANES)

    for h, cb in [(h, cb) for h in range(2) for cb in range(SSM_HSTATE // SSM_CB)]:
        re = slice(cb * SSM_CB, (cb + 1) * SSM_CB)
        im = slice(SSM_HSTATE + cb * SSM_CB, SSM_HSTATE + (cb + 1) * SSM_CB)
        lr = lam_ref[h, :, re]
        li = lam_ref[h, :, im]

        def local_scan(j, st, h=h, re=re, im=im, lr=lr, li=li):
            rows = step_rows(j)
            pr, pi = _cmul(lr, li, st[0], st[1])
            nr = pr + bu_ref[h, rows, re]
            ni = pi + bu_ref[h, rows, im]
            bu_ref[h, rows, re] = nr
            bu_ref[h, rows, im] = ni
            return nr, ni

        zero = jnp.zeros((SUBLANES, SSM_CB), F32)
        lax.fori_loop(0, SSM_TC, local_scan, (zero, zero), unroll=4)

        er = bu_ref[h, last_row:last_row + SUBLANES, re]
        ei = bu_ref[h, last_row:last_row + SUBLANES, im]
        row = lax.broadcasted_iota(jnp.int32, (SUBLANES, SSM_CB), 0)
        dist = (SUBLANES - 1 - row) if reverse else row
        xr = jnp.where(row == edge, carry_ref[h, :, re], toward(er, 1))
        xi = jnp.where(row == edge, carry_ref[h, :, im], toward(ei, 1))
        for lvl, k in enumerate((1, 2, 4)):
            ar = pw_ref[h, lvl, :, re]
            ai = pw_ref[h, lvl, :, im]
            sr = jnp.where(dist >= k, toward(xr, k), 0.0)
            si = jnp.where(dist >= k, toward(xi, k), 0.0)
            mr, mi = _cmul(ar, ai, sr, si)
            xr = xr + mr
            xi = xi + mi
        cr, ci = _cmul(pw_ref[h, 0, :, re], pw_ref[h, 0, :, im], xr, xi)
        carry_ref[h, :, re] = toward(cr + er, 1)
        carry_ref[h, :, im] = toward(ci + ei, 1)

        def add_carry(j, g, h=h, re=re, im=im, lr=lr, li=li):
            rows = step_rows(j)
            bu_ref[h, rows, re] = bu_ref[h, rows, re] + g[0]
            bu_ref[h, rows, im] = bu_ref[h, rows, im] + g[1]
            return _cmul(lr, li, g[0], g[1])

        lax.fori_loop(0, SSM_TC, add_carry, _cmul(lr, li, xr, xi), unroll=4)

    for h in range(2):
        y = jnp.dot(bu_ref[h].astype(BF16), cm_ref[h], preferred_element_type=F32)
        for cb in range(SSM_HALF // LANES):
            cols = slice(cb * LANES, (cb + 1) * LANES)
            out_cols = slice(h * SSM_HALF + cb * LANES, h * SSM_HALF + (cb + 1) * LANES)
            slab = h * (SSM_HALF // LANES) + cb
            yv_ref[slab] = y[:, cols]
            for s in range(SUBLANES):
                y_ref[0, s * SSM_TC:(s + 1) * SSM_TC, out_cols] = yv_ref[slab, pl.ds(s, SSM_TC, stride=SUBLANES), :]


def _ssm_params(a_re, a_im, log_dt, b_re, b_im, c_re, c_im):
    lam = lax.complex(a_re.astype(F32), a_im.astype(F32))
    dt = jnp.exp(log_dt.astype(F32))[..., None]
    lam_bar = jnp.exp(lam * dt)
    b_bar = ((lam_bar - 1.0) / lam)[..., None] * lax.complex(b_re.astype(F32), b_im.astype(F32))
    gh = SSM_GROUPS // 2
    eye = jnp.eye(gh, dtype=F32)

    def b_block(m):
        m = m.reshape(2, 2, gh, SSM_STATE, SSM_GROUP)
        return jnp.einsum('dhgpc,gk->dhgckp', m, eye).reshape(2, 2, gh * SSM_GROUP, gh * SSM_STATE)

    def c_block(m):
        m = m.reshape(2, 2, gh, SSM_GROUP, SSM_STATE)
        return jnp.einsum('dhgcp,gk->dhkpgc', m, eye).reshape(2, 2, gh * SSM_STATE, gh * SSM_GROUP)

    bm = jnp.concatenate([b_block(jnp.real(b_bar)), b_block(jnp.imag(b_bar))], axis=-1).astype(BF16)
    cm = jnp.concatenate([c_block(c_re.astype(F32)), -c_block(c_im.astype(F32))], axis=-2).astype(BF16)

    def table(z):
        z = z.reshape(2, 2, SSM_HSTATE)
        t = jnp.concatenate([jnp.real(z), jnp.imag(z)], axis=-1)
        return jnp.broadcast_to(t[:, :, None, :], (2, 2, SUBLANES, 2 * SSM_HSTATE))

    lam_t = table(lam_bar)
    pw_t = jnp.stack([table(jnp.exp(lam * dt * (SSM_TC * k))) for k in (1, 2, 4)], axis=2)
    return bm, cm, lam_t, pw_t


SSM_CHUNKS = S_ALL // SSM_T


def ssm_scan(u_chunks, bm, cm, lam_t, pw_t, reverse):
    hs2 = 2 * SSM_HSTATE

    def chunk(b, c):
        if reverse:
            c = jnp.where(c == 0, 0, SSM_CHUNKS - c)
        return b * SSM_CHUNKS + c

    kern = functools.partial(_ssm_kernel, reverse=reverse)
    return pl.pallas_call(
        kern,
        grid=(BATCH, SSM_CHUNKS),
        in_specs=[
            pl.BlockSpec((1, SSM_T, SSM_WIDTH), lambda b, c: (chunk(b, c), 0, 0)),
            pl.BlockSpec((2, SSM_HALF, hs2), lambda b, c: (0, 0, 0)),
            pl.BlockSpec((2, hs2, SSM_HALF), lambda b, c: (0, 0, 0)),
            pl.BlockSpec((2, SUBLANES, hs2), lambda b, c: (0, 0, 0)),
            pl.BlockSpec((2, 3, SUBLANES, hs2), lambda b, c: (0, 0, 0, 0)),
        ],
        out_specs=pl.BlockSpec((1, SSM_T, SSM_WIDTH), lambda b, c: (chunk(b, c), 0, 0)),
        out_shape=jax.ShapeDtypeStruct(u_chunks.shape, F32),
        scratch_shapes=[pltpu.VMEM((2, SSM_T, hs2), F32), pltpu.VMEM((SSM_WIDTH // LANES, SSM_T, LANES), F32),
                        pltpu.VMEM((2, SUBLANES, hs2), F32)],
        compiler_params=_cparams(("arbitrary",) * 2, 32),
        name="ssm_scan",
    )(u_chunks, bm, cm, lam_t, pw_t)


def _ssm_chunk_of_tile(i):
    lat_tiles = SEQ // SSM_T
    return jnp.where(i < BATCH * lat_tiles,
                     (i // lat_tiles) * SSM_CHUNKS + 1 + i % lat_tiles,
                     (i - BATCH * lat_tiles) * SSM_CHUNKS)


def _gelu_tanh(x):
    return 0.5 * x * (1.0 + jnp.tanh(math.sqrt(2.0 / math.pi) * (x + 0.044715 * (x * x * x))))


def _ssm_out_kernel(u_ref, yf_ref, yb_ref, d_ref, w_ref, b_ref, o_ref):
    y = d_ref[0] * u_ref[...].astype(F32) + yf_ref[0] + yb_ref[0]
    y = _gelu_tanh(y)
    z = jnp.dot(y.astype(BF16), w_ref[0], preferred_element_type=F32) + b_ref[0]
    o_ref[...] = (y * jax.nn.sigmoid(z)).astype(o_ref.dtype)


def ssm_output(p, yf, yb, ssm_d, glu_w, glu_b, layer):
    n = p.shape[0]
    yspec = pl.BlockSpec((1, SSM_T, SSM_WIDTH), lambda i: (_ssm_chunk_of_tile(i), 0, 0))
    return pl.pallas_call(
        _ssm_out_kernel,
        grid=(n // SSM_T,),
        in_specs=[
            pl.BlockSpec((SSM_T, SSM_WIDTH), lambda i: (i, COL_SSM // SSM_WIDTH)),
            yspec,
            yspec,
            pl.BlockSpec((1, 1, SSM_WIDTH), lambda i: (layer, 0, 0)),
            pl.BlockSpec((1, SSM_WIDTH, SSM_WIDTH), lambda i: (layer, 0, 0)),
            pl.BlockSpec((1, 1, SSM_WIDTH), lambda i: (layer, 0, 0)),
        ],
        out_specs=pl.BlockSpec((SSM_T, SSM_WIDTH), lambda i: (i, 0)),
        out_shape=jax.ShapeDtypeStruct((n, SSM_WIDTH), BF16),
        compiler_params=_cparams(("arbitrary",), 32),
        name="ssm_output",
    )(p, yf, yb, ssm_d.reshape(DEPTH, 1, SSM_WIDTH), glu_w, glu_b.reshape(DEPTH, 1, SSM_WIDTH))


def _pool_kernel(u_ref, w_ref, s_ref, o_ref, *, seq_len):
    t0 = pl.multiple_of(pl.program_id(1) * SEQ_T, SEQ_T)
    halo = u_ref[0, pl.ds(t0, SEQ_T + 2 * SEQ_PAD), :]
    centre = u_ref[0, pl.ds(t0 + SEQ_PAD, SEQ_T), :].astype(F32)
    tt = lax.broadcasted_iota(jnp.int32, (SEQ_T, SEQ_T + 2 * SEQ_PAD), 0)
    rr = lax.broadcasted_iota(jnp.int32, (SEQ_T, SEQ_T + 2 * SEQ_PAD), 1) - SEQ_PAD
    tg = t0 + lax.broadcasted_iota(jnp.int32, (SEQ_T, 1), 0)
    for gi, w in enumerate(POOL_WINDOWS):
        cols = slice(gi * POOL_GROUP, (gi + 1) * POOL_GROUP)
        band = ((rr >= tt - w // 2) & (rr <= tt + w // 2 - 1)).astype(F32).astype(BF16)
        wsum = jnp.dot(band, halo[:, cols], preferred_element_type=F32)
        cnt = jnp.minimum(tg - w // 2 + w, seq_len) - jnp.maximum(tg - w // 2, 0)
        pooled = wsum / cnt.astype(F32) - centre[:, cols]
        mixed = jnp.dot(pooled.astype(BF16), w_ref[0, gi], preferred_element_type=F32)
        o_ref[0, :, cols] = (mixed * s_ref[0, :, cols]).astype(o_ref.dtype)


def pool_mixer(u_pad, pool_w, pool_scale, layer):
    nseq, lp, _ = u_pad.shape
    seq_len = lp - 2 * SEQ_PAD
    kern = functools.partial(_pool_kernel, seq_len=seq_len)
    ng = len(POOL_WINDOWS)
    return pl.pallas_call(
        kern,
        grid=(nseq, seq_len // SEQ_T),
        in_specs=[
            pl.BlockSpec((1, lp, POOL_WIDTH), lambda s, t: (s, 0, 0)),
            pl.BlockSpec((1, ng, POOL_GROUP, POOL_GROUP), lambda s, t: (layer, 0, 0, 0)),
            pl.BlockSpec((1, 1, POOL_WIDTH), lambda s, t: (layer, 0, 0)),
        ],
        out_specs=pl.BlockSpec((1, SEQ_T, POOL_WIDTH), lambda s, t: (s, t, 0)),
        out_shape=jax.ShapeDtypeStruct((nseq, seq_len, POOL_WIDTH), BF16),
        compiler_params=_cparams(("arbitrary", "arbitrary"), 32),
        name="pool_mixer",
    )(u_pad, pool_w, pool_scale.reshape(DEPTH, 1, POOL_WIDTH))


CONV_RB = 64


def _conv_kernel(u_ref, w_ref, b_ref, g_ref, beta_ref, o_ref, glu_ref):
    t0 = pl.multiple_of(pl.program_id(1) * SEQ_T, SEQ_T)
    rows = SEQ_T + 2 * SEQ_PAD
    a = u_ref[0, pl.ds(t0, rows), 0:CONV_WIDTH].astype(F32)
    g = u_ref[0, pl.ds(t0, rows), CONV_WIDTH:2 * CONV_WIDTH].astype(F32)
    glu_ref[...] = a * jax.nn.sigmoid(g)
    off = SEQ_PAD - CONV_TAPS // 2
    for rb in range(SEQ_T // CONV_RB):
        parts = []
        for cb in range(CONV_WIDTH // LANES):
            cols = slice(cb * LANES, (cb + 1) * LANES)
            acc = jnp.zeros((CONV_RB, LANES), F32)
            for k in range(CONV_TAPS):
                acc = acc + glu_ref[rb * CONV_RB + k + off:rb * CONV_RB + k + off + CONV_RB, cols] * w_ref[0, k:k + 1, cols]
            parts.append(acc)
        y = jnp.concatenate(parts, axis=-1) + b_ref[0]
        yc = y - jnp.mean(y, axis=-1, keepdims=True)
        yn = yc * lax.rsqrt(jnp.mean(yc * yc, axis=-1, keepdims=True) + EPS)
        yn = yn * g_ref[0] + beta_ref[0]
        o_ref[0, rb * CONV_RB:(rb + 1) * CONV_RB, :] = _silu(yn).astype(o_ref.dtype)


def conv_mixer(u_pad, dw_w, dw_b, ln_g, ln_b, layer):
    nseq, lp, _ = u_pad.shape
    seq_len = lp - 2 * SEQ_PAD
    vec = lambda a: a.reshape(DEPTH, 1, CONV_WIDTH)
    vspec = pl.BlockSpec((1, 1, CONV_WIDTH), lambda s, t: (layer, 0, 0))
    return pl.pallas_call(
        _conv_kernel,
        grid=(nseq, seq_len // SEQ_T),
        in_specs=[
            pl.BlockSpec((1, lp, 2 * CONV_WIDTH), lambda s, t: (s, 0, 0)),
            pl.BlockSpec((1, CONV_TAPS, CONV_WIDTH), lambda s, t: (layer, 0, 0)),
            vspec, vspec, vspec,
        ],
        out_specs=pl.BlockSpec((1, SEQ_T, CONV_WIDTH), lambda s, t: (s, t, 0)),
        out_shape=jax.ShapeDtypeStruct((nseq, seq_len, CONV_WIDTH), BF16),
        scratch_shapes=[pltpu.VMEM((SEQ_T + 2 * SEQ_PAD, CONV_WIDTH), F32)],
        compiler_params=_cparams(("arbitrary", "arbitrary"), 40),
        name="conv_mixer",
    )(u_pad, dw_w, vec(dw_b), vec(ln_g), vec(ln_b))


def _merge_kernel(oa_ref, ys_ref, yp_ref, yc_ref, g0_ref, g1_ref, g2_ref, g3_ref,
                  wa_ref, ws_ref, wp_ref, wc_ref, o_ref):
    def branch(x_ref, w_ref, g_ref):
        up = jnp.dot(x_ref[...], w_ref[0, 0], preferred_element_type=F32)
        return jax.nn.sigmoid(g_ref[...].astype(F32)) * up

    m = (branch(oa_ref, wa_ref, g0_ref) + branch(ys_ref, ws_ref, g1_ref)
         + branch(yp_ref, wp_ref, g2_ref) + branch(yc_ref, wc_ref, g3_ref))
    o_ref[...] = m.astype(o_ref.dtype)


def merge_branches(p, o_attn, y_ssm, y_pool, y_conv, w_attn, w_ssm, w_pool, w_conv, layer):
    n = o_attn.shape[0]
    nct = D_MODEL // TN
    gate_tile0 = COL_GATE // TN

    def gate_spec(br):
        return pl.BlockSpec((TM, TN), lambda i, j: (i, gate_tile0 + br * nct + j))

    def x_spec(width):
        return pl.BlockSpec((TM, width), lambda i, j: (i, 0))

    def w_spec(width):
        return pl.BlockSpec((1, 1, width, TN), lambda i, j: (layer, j, 0, 0))

    return pl.pallas_call(
        _merge_kernel,
        grid=(n // TM, nct),
        in_specs=[x_spec(Q_W), x_spec(SSM_WIDTH), x_spec(POOL_WIDTH), x_spec(CONV_WIDTH),
                  gate_spec(0), gate_spec(1), gate_spec(2), gate_spec(3),
                  w_spec(Q_W), w_spec(SSM_WIDTH), w_spec(POOL_WIDTH), w_spec(CONV_WIDTH)],
        out_specs=pl.BlockSpec((TM, TN), lambda i, j: (i, j)),
        out_shape=jax.ShapeDtypeStruct((n, D_MODEL), BF16),
        compiler_params=_cparams(("arbitrary", "arbitrary"), 40),
        name="merge_branches",
    )(o_attn, y_ssm, y_pool, y_conv, p, p, p, p, w_attn, w_ssm, w_pool, w_conv)


def _split_bf16(x):
    hi = x.astype(BF16)
    lo = (x - hi.astype(F32)).astype(BF16)
    return hi, lo


def _outproj_kernel(m_ref, x_ref, mod_ref, g_ref, w_ref, rw_ref, xo_ref, h_ref, lg_ref):
    mix = jnp.dot(m_ref[...], w_ref[0], preferred_element_type=F32)
    x = x_ref[...] + mod_ref[0, 2:3, :] * mix
    xo_ref[...] = x
    ms = jnp.mean(x * x, axis=-1, keepdims=True)
    h = x * lax.rsqrt(ms + EPS) * g_ref[0]
    h = h * (1.0 + mod_ref[0, 4:5, :]) + mod_ref[0, 3:4, :]
    h_ref[...] = h
    h_hi, h_lo = _split_bf16(h)
    r_hi, r_lo = _split_bf16(rw_ref[...])
    nt = (((1,), (1,)), ((), ()))
    lg_ref[...] = (lax.dot_general(r_hi, h_hi, nt, preferred_element_type=F32)
                   + lax.dot_general(r_hi, h_lo, nt, preferred_element_type=F32)
                   + lax.dot_general(r_lo, h_hi, nt, preferred_element_type=F32))


def out_projection(m, xa, mod, norm2_g, w_out, router_wt, layer):
    n = m.shape[0]
    tmo = TM // 2
    return pl.pallas_call(
        _outproj_kernel,
        grid=(n // tmo,),
        in_specs=[
            pl.BlockSpec((tmo, D_MODEL), lambda i: (i, 0)),
            pl.BlockSpec((tmo, D_MODEL), lambda i: (i, 0)),
            pl.BlockSpec((1, 6, D_MODEL), lambda i: (_mod_id(i // 2), 0, 0)),
            pl.BlockSpec((1, 1, D_MODEL), lambda i: (layer, 0, 0)),
            pl.BlockSpec((1, D_MODEL, D_MODEL), lambda i: (layer, 0, 0)),
            pl.BlockSpec((N_EXPERTS, D_MODEL), lambda i: (0, 0)),
        ],
        out_specs=[
            pl.BlockSpec((tmo, D_MODEL), lambda i: (i, 0)),
            pl.BlockSpec((tmo, D_MODEL), lambda i: (i, 0)),
            pl.BlockSpec((N_EXPERTS, tmo), lambda i: (0, i)),
        ],
        out_shape=[
            jax.ShapeDtypeStruct((n, D_MODEL), F32),
            jax.ShapeDtypeStruct((n, D_MODEL), F32),
            jax.ShapeDtypeStruct((N_EXPERTS, n), F32),
        ],
        compiler_params=_cparams(("arbitrary",), 56),
        name="out_projection",
    )(m, xa, mod, norm2_g.reshape(DEPTH, 1, D_MODEL), w_out, router_wt)


def _first_argmax(blk, row):
    m = jnp.max(blk, axis=0, keepdims=True)
    idx = jnp.min(jnp.where(blk == m, row, EXPERTS_PER_GROUP), axis=0, keepdims=True)
    return m, idx


def _route_kernel(lg_ref, rb_ref, e_ref, w_ref, rank_ref, cnt_ref, run_ref):
    @pl.when(pl.program_id(0) == 0)
    def _():
        run_ref[...] = jnp.zeros_like(run_ref)

    t = lg_ref.shape[1]
    scores = jax.nn.sigmoid(lg_ref[...])
    sel = scores + rb_ref[...]
    row = lax.broadcasted_iota(jnp.int32, (EXPERTS_PER_GROUP, t), 0)
    neg = jnp.float32(-jnp.inf)

    best = None
    for g in range(N_EXPERT_GROUPS):
        blk = sel[g * EXPERTS_PER_GROUP:(g + 1) * EXPERTS_PER_GROUP, :]
        m1, i1 = _first_argmax(blk, row)
        m2 = jnp.max(jnp.where(row == i1, neg, blk), axis=0, keepdims=True)
        gs = m1 + m2
        if best is None:
            best, grp = gs, jnp.zeros((1, t), jnp.int32)
        else:
            better = gs > best
            best = jnp.where(better, gs, best)
            grp = jnp.where(better, g, grp)

    in_sel = jnp.zeros((EXPERTS_PER_GROUP, t), F32)
    in_sc = jnp.zeros((EXPERTS_PER_GROUP, t), F32)
    for g in range(N_EXPERT_GROUPS):
        rows = slice(g * EXPERTS_PER_GROUP, (g + 1) * EXPERTS_PER_GROUP)
        in_sel = jnp.where(grp == g, sel[rows, :], in_sel)
        in_sc = jnp.where(grp == g, scores[rows, :], in_sc)
    _, i1 = _first_argmax(in_sel, row)
    _, i2 = _first_argmax(jnp.where(row == i1, neg, in_sel), row)
    s1 = jnp.sum(jnp.where(row == i1, in_sc, 0.0), axis=0, keepdims=True)
    s2 = jnp.sum(jnp.where(row == i2, in_sc, 0.0), axis=0, keepdims=True)
    e1 = grp * EXPERTS_PER_GROUP + i1
    e2 = grp * EXPERTS_PER_GROUP + i2
    e_ref[0:1, :] = e1
    e_ref[1:2, :] = e2
    w_ref[0:1, :] = s1 / (s1 + s2)
    w_ref[1:2, :] = s2 / (s1 + s2)

    erow = lax.broadcasted_iota(jnp.int32, (N_EXPERTS, t), 0)
    oh1 = erow == e1
    oh2 = erow == e2
    cnt = jnp.where(oh1 | oh2, 1.0, 0.0)
    a = lax.broadcasted_iota(jnp.int32, (t, t), 0)
    b = lax.broadcasted_iota(jnp.int32, (t, t), 1)
    before = jnp.where(a < b, 1.0, 0.0).astype(BF16)
    excl = jnp.dot(cnt.astype(BF16), before, preferred_element_type=F32) + run_ref[...]
    rank_ref[0:1, :] = jnp.sum(jnp.where(oh1, excl, 0.0), axis=0, keepdims=True).astype(jnp.int32)
    rank_ref[1:2, :] = jnp.sum(jnp.where(oh2, excl, 0.0), axis=0, keepdims=True).astype(jnp.int32)
    run_ref[...] = run_ref[...] + jnp.sum(cnt, axis=1, keepdims=True)
    cnt_ref[...] = run_ref[...]


def route(logits_t, router_b):
    n = logits_t.shape[1]
    return pl.pallas_call(
        _route_kernel,
        grid=(n // ROUTE_T,),
        in_specs=[
            pl.BlockSpec((N_EXPERTS, ROUTE_T), lambda i: (0, i)),
            pl.BlockSpec((N_EXPERTS, 1), lambda i: (0, 0)),
        ],
        out_specs=[
            pl.BlockSpec((TOP_K, ROUTE_T), lambda i: (0, i)),
            pl.BlockSpec((TOP_K, ROUTE_T), lambda i: (0, i)),
            pl.BlockSpec((TOP_K, ROUTE_T), lambda i: (0, i)),
            pl.BlockSpec((N_EXPERTS, 1), lambda i: (0, 0)),
        ],
        out_shape=[
            jax.ShapeDtypeStruct((TOP_K, n), jnp.int32),
            jax.ShapeDtypeStruct((TOP_K, n), F32),
            jax.ShapeDtypeStruct((TOP_K, n), jnp.int32),
            jax.ShapeDtypeStruct((N_EXPERTS, 1), F32),
        ],
        scratch_shapes=[pltpu.VMEM((N_EXPERTS, 1), F32)],
        compiler_params=_cparams(("arbitrary",), 32),
        name="route",
    )(logits_t, router_b.reshape(N_EXPERTS, 1).astype(F32))


def _row_copy(src, src_row, dst, dst_row, sem):
    return pltpu.make_async_copy(src.at[pl.ds(src_row, 1)], dst.at[pl.ds(dst_row, 1)], sem)


def _dispatch_kernel(dest_ref, h_ref, xs_in_ref, xs_ref, sem):
    del xs_in_ref
    n = dest_ref.shape[0] // TOP_K
    base = pl.program_id(0) * GATHER_T

    def start(t, carry):
        for k in range(TOP_K):
            _row_copy(h_ref, t, xs_ref, dest_ref[k * n + base + t], sem).start()
        return carry

    def wait(t, carry):
        for k in range(TOP_K):
            _row_copy(h_ref, t, xs_ref, dest_ref[k * n + base + t], sem).wait()
        return carry

    lax.fori_loop(0, GATHER_T, start, 0)
    lax.fori_loop(0, GATHER_T, wait, 0)


def dispatch(dest_flat, h2, n_slots):
    n = h2.shape[0]
    xs0 = jnp.zeros((n_slots, D_MODEL), F32)
    grid_spec = pltpu.PrefetchScalarGridSpec(
        num_scalar_prefetch=1,
        grid=(n // GATHER_T,),
        in_specs=[pl.BlockSpec((GATHER_T, D_MODEL), lambda i, d: (i, 0)), pl.BlockSpec(memory_space=pl.ANY)],
        out_specs=pl.BlockSpec(memory_space=pl.ANY),
        scratch_shapes=[pltpu.SemaphoreType.DMA(())],
    )
    return pl.pallas_call(
        _dispatch_kernel,
        grid_spec=grid_spec,
        out_shape=jax.ShapeDtypeStruct((n_slots, D_MODEL), F32),
        input_output_aliases={2: 0},
        compiler_params=_cparams(("arbitrary",), 32),
        name="dispatch",
    )(dest_flat, h2, xs0)


def _expert_kernel(be_ref, slot_ref, nxt_ref, nu_ref, x_ref, wg_hbm, wu_hbm, wd_hbm, o_ref,
                   wgf_ref, wuf_ref, wdf_ref, wgb_ref, wub_ref, wdb_ref, sem, *, layer):
    b = pl.program_id(0)
    e = be_ref[b]
    fresh = (b == 0) | (e != be_ref[jnp.maximum(b - 1, 0)])
    live = b < nu_ref[0]

    def weight_copies(expert, slot):
        return (pltpu.make_async_copy(wg_hbm.at[layer, expert], wgf_ref.at[slot], sem.at[slot]),
                pltpu.make_async_copy(wu_hbm.at[layer, expert], wuf_ref.at[slot], sem.at[slot]),
                pltpu.make_async_copy(wd_hbm.at[layer, expert], wdf_ref.at[slot], sem.at[slot]))

    @pl.when(b == 0)
    def _():
        for cp in weight_copies(e, 0):
            cp.start()

    @pl.when(fresh & live)
    def _():
        slot = slot_ref[b]
        nxt = nxt_ref[b]

        @pl.when(nxt >= 0)
        def _():
            for cp in weight_copies(nxt, 1 - slot):
                cp.start()

        for cp in weight_copies(e, slot):
            cp.wait()
        wgb_ref[...] = wgf_ref[slot].astype(BF16)
        wub_ref[...] = wuf_ref[slot].astype(BF16)
        wdb_ref[...] = wdf_ref[slot].astype(BF16)

    @pl.when(live)
    def _():
        x = x_ref[...].astype(BF16)
        gate = jnp.dot(x, wgb_ref[...], preferred_element_type=F32)
        up = jnp.dot(x, wub_ref[...], preferred_element_type=F32)
        hid = (_silu(gate) * up).astype(BF16)
        o_ref[...] = jnp.dot(hid, wdb_ref[...], preferred_element_type=F32)

    @pl.when(b >= nu_ref[0])
    def _():
        o_ref[...] = jnp.zeros_like(o_ref)


def experts(blk_e, blk_slot, blk_next, n_used, xs, w_gate, w_up, w_down, layer):
    n_slots = xs.shape[0]
    n_blocks = n_slots // MOE_BLOCK

    def live(b, nu):
        return jnp.minimum(b, nu[0] - 1)

    grid_spec = pltpu.PrefetchScalarGridSpec(
        num_scalar_prefetch=4,
        grid=(n_blocks,),
        in_specs=[
            pl.BlockSpec((MOE_BLOCK, D_MODEL), lambda b, be, sl, nx, nu: (live(b, nu), 0)),
            pl.BlockSpec(memory_space=pl.ANY),
            pl.BlockSpec(memory_space=pl.ANY),
            pl.BlockSpec(memory_space=pl.ANY),
        ],
        out_specs=pl.BlockSpec((MOE_BLOCK, D_MODEL), lambda b, be, sl, nx, nu: (b, 0)),
        scratch_shapes=[
            pltpu.VMEM((2, D_MODEL, EXPERT_FF), F32),
            pltpu.VMEM((2, D_MODEL, EXPERT_FF), F32),
            pltpu.VMEM((2, EXPERT_FF, D_MODEL), F32),
            pltpu.VMEM((D_MODEL, EXPERT_FF), BF16),
            pltpu.VMEM((D_MODEL, EXPERT_FF), BF16),
            pltpu.VMEM((EXPERT_FF, D_MODEL), BF16),
            pltpu.SemaphoreType.DMA((2,)),
        ],
    )
    return pl.pallas_call(
        functools.partial(_expert_kernel, layer=layer),
        grid_spec=grid_spec,
        out_shape=jax.ShapeDtypeStruct((n_slots, D_MODEL), F32),
        compiler_params=_cparams(("arbitrary",), 56),
        name="experts",
    )(blk_e, blk_slot, blk_next, n_used, xs, w_gate, w_up, w_down)


def _combine_kernel(dest_ref, x_ref, mod_ref, w_ref, fg_ref, ys_ref, o_ref, buf_ref, sem, *, final_norm):
    n = dest_ref.shape[0] // TOP_K
    base = pl.program_id(0) * GATHER_T

    def start(t, carry):
        for k in range(TOP_K):
            _row_copy(ys_ref, dest_ref[k * n + base + t], buf_ref.at[k], t, sem).start()
        return carry

    def wait(t, carry):
        for k in range(TOP_K):
            _row_copy(ys_ref, dest_ref[k * n + base + t], buf_ref.at[k], t, sem).wait()
        return carry

    lax.fori_loop(0, GATHER_T, start, 0)
    lax.fori_loop(0, GATHER_T, wait, 0)
    f = w_ref[:, 0:1] * buf_ref[0] + w_ref[:, 1:2] * buf_ref[1]
    x = x_ref[...] + mod_ref[0, 5:6, :] * f
    if final_norm:
        ms = jnp.mean(x * x, axis=-1, keepdims=True)
        x = x * lax.rsqrt(ms + EPS) * fg_ref[...]
    o_ref[...] = x


def combine(dest_flat, xa, mod, wts_rows, ys, final_g, final_norm):
    n = xa.shape[0]
    tiles_per_tm = TM // GATHER_T
    grid_spec = pltpu.PrefetchScalarGridSpec(
        num_scalar_prefetch=1,
        grid=(n // GATHER_T,),
        in_specs=[
            pl.BlockSpec((GATHER_T, D_MODEL), lambda i, d: (i, 0)),
            pl.BlockSpec((1, 6, D_MODEL), lambda i, d: (_mod_id(i // tiles_per_tm), 0, 0)),
            pl.BlockSpec((GATHER_T, TOP_K), lambda i, d: (i, 0)),
            pl.BlockSpec((1, D_MODEL), lambda i, d: (0, 0)),
            pl.BlockSpec(memory_space=pl.ANY),
        ],
        out_specs=pl.BlockSpec((GATHER_T, D_MODEL), lambda i, d: (i, 0)),
        scratch_shapes=[pltpu.VMEM((TOP_K, GATHER_T, D_MODEL), F32), pltpu.SemaphoreType.DMA(())],
    )
    kern = functools.partial(_combine_kernel, final_norm=final_norm)
    return pl.pallas_call(
        kern,
        grid_spec=grid_spec,
        out_shape=jax.ShapeDtypeStruct((n, D_MODEL), F32),
        compiler_params=_cparams(("arbitrary",), 32),
        name="combine",
    )(dest_flat, xa, mod, wts_rows, final_g.reshape(1, D_MODEL), ys)


def _slot_plan(e_idx, rank, counts, n_blocks):
    i32 = jnp.int32
    cnt = counts.reshape(N_EXPERTS).astype(i32)
    padded = (cnt + MOE_BLOCK - 1) // MOE_BLOCK * MOE_BLOCK
    pad_end = jnp.cumsum(padded)
    pad_start = pad_end - padded
    eids = jnp.arange(N_EXPERTS, dtype=i32)

    def lookup(table, idx):
        return jnp.sum(jnp.where(idx[..., None] == eids, table, 0), axis=-1).astype(i32)

    dest = lookup(pad_start, e_idx) + rank
    blk_start = jnp.arange(n_blocks, dtype=i32) * MOE_BLOCK
    blk_e = jnp.minimum(jnp.sum(pad_end[None, :] <= blk_start[:, None], axis=1), N_EXPERTS - 1).astype(i32)
    n_used = (pad_end[-1] // MOE_BLOCK).astype(i32).reshape(1)
    has = cnt > 0
    run = jnp.cumsum(has.astype(i32)) - 1
    first_from = lax.cummin(jnp.where(has, eids, N_EXPERTS)[::-1])[::-1]
    nxt = jnp.concatenate([first_from[1:], jnp.full((1,), N_EXPERTS, i32)])
    nxt = jnp.where(nxt >= N_EXPERTS, -1, nxt)
    return dest.reshape(-1).astype(i32), blk_e, lookup(run % 2, blk_e), lookup(nxt, blk_e), n_used


def _col_tiles(w):
    d, k, n = w.shape
    return w.astype(BF16).reshape(d, k, n // TN, TN).transpose(0, 2, 1, 3)


def _pad_rows(u):
    return jnp.pad(u, ((0, 0), (SEQ_PAD, SEQ_PAD), (0, 0)))


def _layer(layer, xa, mod, W, with_ctx, final_norm):
    n_lat_tiles = N_LAT // TM
    n_ctx_tiles = N_CTX // TM
    n_col_tiles = IN_W // TN
    if with_ctx:
        p = in_projection(xa, mod, W['norm1_g'], W['w_in'], layer, 0, n_lat_tiles + n_ctx_tiles, n_col_tiles)
        p_lat, p_ctx = p[:N_LAT], p[N_LAT:]
    else:
        p = in_projection(xa, mod, W['norm1_g'], W['w_in'], layer, 0, n_lat_tiles, n_col_tiles)
        p_lat = p
        p_ctx = in_projection(xa, mod, W['norm1_g'], W['w_in'], layer, n_lat_tiles, n_ctx_tiles, CTX_STATE_W // TN)
    n_rows = p.shape[0]

    def seq(a, lo, hi, n_seq_rows):
        return a[:, lo:hi].reshape(BATCH, n_seq_rows, hi - lo)

    kv = jnp.concatenate([seq(p_ctx, 0, 2 * KV_W, CTX_LEN), seq(p_lat, 0, 2 * KV_W, SEQ)], axis=1)
    cos, sin = _rope_tables(CTX_LEN)
    o_attn = attention(p, COL_Q, 0, SEQ, kv, cos, sin, CTX_LEN // TQ, W['q_norm_g'], W['k_norm_g'], layer)
    if with_ctx:
        kv_c = seq(p_ctx, 0, 2 * KV_W, CTX_LEN)
        ones, zeros = jnp.ones((CTX_LEN, HEAD_DIM), F32), jnp.zeros((CTX_LEN, HEAD_DIM), F32)
        o_attn_c = attention(p, COL_Q, N_LAT // TQ, CTX_LEN, kv_c, ones, zeros, 0,
                             W['q_norm_g'], W['k_norm_g'], layer)
        o_attn = jnp.concatenate([o_attn, o_attn_c], axis=0)

    u_seq = jnp.concatenate([seq(p_ctx, COL_SSM, COL_SSM + SSM_WIDTH, CTX_LEN),
                             seq(p_lat, COL_SSM, COL_SSM + SSM_WIDTH, SEQ)], axis=1)
    u_chunks = u_seq.reshape(BATCH * SSM_CHUNKS, SSM_T, SSM_WIDTH)
    bm, cm, lam_t, pw_t = _ssm_params(W['ssm_a_re'][layer], W['ssm_a_im'][layer], W['ssm_log_dt'][layer],
                                      W['ssm_b_re'][layer], W['ssm_b_im'][layer],
                                      W['ssm_c_re'][layer], W['ssm_c_im'][layer])
    yf = ssm_scan(u_chunks, bm[0], cm[0], lam_t[0], pw_t[0], reverse=False)
    yb = ssm_scan(u_chunks, bm[1], cm[1], lam_t[1], pw_t[1], reverse=True)
    y_ssm = ssm_output(p, yf, yb, W['ssm_d'], W['ssm_glu_w'], W['ssm_glu_b'], layer)

    def seq_mixers(src, n_seq_rows):
        up = _pad_rows(seq(src, COL_POOL, COL_POOL + POOL_WIDTH, n_seq_rows))
        uc = _pad_rows(seq(src, COL_CONV, COL_CONV + 2 * CONV_WIDTH, n_seq_rows))
        yp = pool_mixer(up, W['pool_w'], W['pool_scale'], layer)
        yc = conv_mixer(uc, W['conv_dw_w'], W['conv_dw_b'], W['conv_ln_g'], W['conv_ln_b'], layer)
        return yp.reshape(-1, POOL_WIDTH), yc.reshape(-1, CONV_WIDTH)

    y_pool, y_conv = seq_mixers(p_lat, SEQ)
    if with_ctx:
        yp_c, yc_c = seq_mixers(p_ctx, CTX_LEN)
        y_pool = jnp.concatenate([y_pool, yp_c], axis=0)
        y_conv = jnp.concatenate([y_conv, yc_c], axis=0)

    m = merge_branches(p, o_attn, y_ssm, y_pool, y_conv,
                       W['w_up_attn'], W['w_up_ssm'], W['w_up_pool'], W['w_up_conv'], layer)
    x1, h2, logits_t = out_projection(m, xa, mod, W['norm2_g'], W['w_out'], W['router_wt'], layer)

    e_idx, wts, rank, counts = route(logits_t, W['router_b'])
    n_blocks = -(-n_rows * TOP_K // MOE_BLOCK) + N_EXPERTS
    dest_flat, blk_e, blk_slot, blk_next, n_used = _slot_plan(e_idx, rank, counts, n_blocks)
    xs = dispatch(dest_flat, h2, n_blocks * MOE_BLOCK)
    ys = experts(blk_e, blk_slot, blk_next, n_used, xs, W['moe_w_gate'], W['moe_w_up'], W['moe_w_down'], layer)
    x2 = combine(dest_flat, x1, mod, wts.T, ys, W['final_g'], final_norm)
    if with_ctx:
        return x2
    return jnp.concatenate([x2, xa[N_LAT:]], axis=0)


def kernel(x, c, ctx, c_ctx, ada_w, ada_b, norm1_g, norm2_g, w_in, q_norm_g, k_norm_g, ssm_a_re, ssm_a_im, ssm_log_dt, ssm_b_re, ssm_b_im, ssm_c_re, ssm_c_im, ssm_d, ssm_glu_w, ssm_glu_b, pool_w, pool_scale, conv_dw_w, conv_dw_b, conv_ln_g, conv_ln_b, w_up_attn, w_up_ssm, w_up_pool, w_up_conv, w_out, router_w, router_b, moe_w_gate, moe_w_up, moe_w_down, final_g):
    W = dict(
        norm1_g=norm1_g, norm2_g=norm2_g, q_norm_g=q_norm_g, k_norm_g=k_norm_g,
        w_in=_col_tiles(w_in),
        ssm_a_re=ssm_a_re, ssm_a_im=ssm_a_im, ssm_log_dt=ssm_log_dt, ssm_b_re=ssm_b_re, ssm_b_im=ssm_b_im,
        ssm_c_re=ssm_c_re, ssm_c_im=ssm_c_im, ssm_d=ssm_d, ssm_glu_w=ssm_glu_w.astype(BF16), ssm_glu_b=ssm_glu_b,
        pool_w=pool_w.astype(BF16), pool_scale=pool_scale,
        conv_dw_w=conv_dw_w, conv_dw_b=conv_dw_b, conv_ln_g=conv_ln_g, conv_ln_b=conv_ln_b,
        w_up_attn=_col_tiles(w_up_attn), w_up_ssm=_col_tiles(w_up_ssm), w_up_pool=_col_tiles(w_up_pool),
        w_up_conv=_col_tiles(w_up_conv), w_out=w_out.astype(BF16),
        router_wt=router_w.T, router_b=router_b,
        moe_w_gate=moe_w_gate, moe_w_up=moe_w_up, moe_w_down=moe_w_down, final_g=final_g,
    )
    cc = jnp.concatenate([c, c_ctx[None, :], jnp.zeros((SUBLANES - BATCH - 1, D_MODEL), F32)], axis=0)
    mod_all = ada_modulation(cc, ada_w, ada_b)
    xa = jnp.concatenate([x.reshape(N_LAT, D_MODEL), ctx.reshape(N_CTX, D_MODEL)], axis=0)
    for layer in range(DEPTH):
        mod = mod_all[layer, :BATCH + 1].reshape(BATCH + 1, 6, D_MODEL)
        xa = _layer(layer, xa, mod, W, with_ctx=(layer < DEPTH - 1), final_norm=(layer == DEPTH - 1))
    return xa[:N_LAT].reshape(BATCH, SEQ, D_MODEL)
```

```python
import functools
import math

import jax
import jax.numpy as jnp
from jax import lax
from jax.experimental import pallas as pl
from jax.experimental.pallas import tpu as pltpu

F32 = jnp.float32
BF16 = jnp.bfloat16

D_MODEL = 2048
BATCH = 2
SEQ = 4096
DEPTH = 2
GRID_W = 64
CTX_LEN = 256
N_HEADS = 8
N_KV_HEADS = 2
HEAD_DIM = 128
ROPE_THETA = 10000.0
SSM_WIDTH = 512
SSM_GROUP = 16
SSM_GROUPS = SSM_WIDTH // SSM_GROUP
SSM_STATE = 64
POOL_WIDTH = 512
POOL_WINDOWS = (2, 4, 8, 16)
POOL_GROUP = POOL_WIDTH // len(POOL_WINDOWS)
CONV_WIDTH = 512
CONV_TAPS = 31
N_BRANCHES = 4
N_EXPERTS = 64
N_EXPERT_GROUPS = 8
EXPERTS_PER_GROUP = N_EXPERTS // N_EXPERT_GROUPS
TOP_K = 2
EXPERT_FF = 512
EPS = 1e-6

Q_W = N_HEADS * HEAD_DIM
KV_W = N_KV_HEADS * HEAD_DIM
IN_W = 2 * KV_W + SSM_WIDTH + Q_W + POOL_WIDTH + 2 * CONV_WIDTH + N_BRANCHES * D_MODEL
CTX_STATE_W = 2 * KV_W + SSM_WIDTH
COL_K, COL_V, COL_SSM, COL_Q = 0, KV_W, 2 * KV_W, 2 * KV_W + SSM_WIDTH
COL_POOL = COL_Q + Q_W
COL_CONV = COL_POOL + POOL_WIDTH
COL_GATE = COL_CONV + 2 * CONV_WIDTH

N_LAT = BATCH * SEQ
N_CTX = BATCH * CTX_LEN
S_ALL = CTX_LEN + SEQ

V7X_VMEM_BYTES = 64 * 1024 * 1024
SUBLANES = 8
LANES = 128
BF16_ROWS = 16

TM = 512
TN = 512
TN_IN = IN_W // 4
TILES_PER_BATCH = SEQ // TM
TQ = 256
SSM_T = 256
SSM_TC = SSM_T // SUBLANES
SSM_HALF = SSM_WIDTH // 2
SSM_HSTATE = SSM_GROUPS // 2 * SSM_STATE
SEQ_T = 256
SEQ_PAD = 16
MOE_BLOCK = 128
ROUTE_T = 512
GATHER_T = 256


def _cparams(sem, vmem_mb):
    return pltpu.CompilerParams(dimension_semantics=sem, vmem_limit_bytes=vmem_mb * 1024 * 1024)


def _mod_id(tile):
    return jnp.minimum(tile // TILES_PER_BATCH, BATCH)


def _silu(x):
    return x * jax.nn.sigmoid(x)


def _ada_kernel(c_ref, w_ref, b_ref, o_ref):
    c = c_ref[...]
    a = _silu(c).astype(BF16)
    o_ref[0] = jnp.dot(a, w_ref[0].astype(BF16), preferred_element_type=F32) + b_ref[0]


def ada_modulation(cc, ada_w, ada_b):
    tn = 1024
    n6 = 6 * D_MODEL
    return pl.pallas_call(
        _ada_kernel,
        grid=(DEPTH, n6 // tn),
        in_specs=[
            pl.BlockSpec((SUBLANES, D_MODEL), lambda l, j: (0, 0)),
            pl.BlockSpec((1, D_MODEL, tn), lambda l, j: (l, 0, j)),
            pl.BlockSpec((1, 1, tn), lambda l, j: (l, 0, j)),
        ],
        out_specs=pl.BlockSpec((1, SUBLANES, tn), lambda l, j: (l, 0, j)),
        out_shape=jax.ShapeDtypeStruct((DEPTH, SUBLANES, n6), F32),
        compiler_params=_cparams(("arbitrary", "arbitrary"), 40),
        name="ada_modulation",
    )(cc, ada_w, ada_b.reshape(DEPTH, 1, n6))


def _inproj_kernel(x_ref, mod_ref, g_ref, w_ref, o_ref, h_ref):
    @pl.when(pl.program_id(1) == 0)
    def _():
        x = x_ref[...]
        ms = jnp.mean(x * x, axis=-1, keepdims=True)
        y = x * lax.rsqrt(ms + EPS) * g_ref[0]
        h = y * (1.0 + mod_ref[0, 1:2, :]) + mod_ref[0, 0:1, :]
        h_ref[...] = h.astype(BF16)

    o_ref[...] = jnp.dot(h_ref[...], w_ref[0], preferred_element_type=F32).astype(o_ref.dtype)


def in_projection(xa, mod, norm_g, w_in, layer, row_tile0, n_row_tiles, n_cols, tn):
    return pl.pallas_call(
        _inproj_kernel,
        grid=(n_row_tiles, n_cols // tn),
        in_specs=[
            pl.BlockSpec((TM, D_MODEL), lambda i, j: (i + row_tile0, 0)),
            pl.BlockSpec((1, 6, D_MODEL), lambda i, j: (_mod_id(i + row_tile0), 0, 0)),
            pl.BlockSpec((1, 1, D_MODEL), lambda i, j: (layer, 0, 0)),
            pl.BlockSpec((1, D_MODEL, tn), lambda i, j: (layer, 0, j)),
        ],
        out_specs=pl.BlockSpec((TM, tn), lambda i, j: (i, j)),
        out_shape=jax.ShapeDtypeStruct((n_row_tiles * TM, n_cols), BF16),
        scratch_shapes=[pltpu.VMEM((TM, D_MODEL), BF16)],
        compiler_params=_cparams(("arbitrary", "arbitrary"), 56),
        name="in_projection",
    )(xa, mod, norm_g.reshape(DEPTH, 1, D_MODEL), w_in)


def _rope_tables(n_ctx_rows):
    half = HEAD_DIM // 4
    inv_freq = ROPE_THETA ** (-jnp.arange(half, dtype=F32) / half)
    t = jnp.arange(SEQ)
    ang_r = (t // GRID_W).astype(F32)[:, None] * inv_freq[None, :]
    ang_c = (t % GRID_W).astype(F32)[:, None] * inv_freq[None, :]
    cos = jnp.concatenate([jnp.cos(ang_r), jnp.cos(ang_r), jnp.cos(ang_c), jnp.cos(ang_c)], axis=-1)
    sin = jnp.concatenate([-jnp.sin(ang_r), jnp.sin(ang_r), -jnp.sin(ang_c), jnp.sin(ang_c)], axis=-1)
    cos = jnp.concatenate([jnp.ones((n_ctx_rows, HEAD_DIM), F32), cos], axis=0)
    sin = jnp.concatenate([jnp.zeros((n_ctx_rows, HEAD_DIM), F32), sin], axis=0)
    return cos, sin


def _head_norm_rope(x, g, cos, sin):
    ms = jnp.mean(x * x, axis=-1, keepdims=True)
    y = x * lax.rsqrt(ms + EPS) * g
    lane = lax.broadcasted_iota(jnp.int32, y.shape, 1)
    first = (lane % (HEAD_DIM // 2)) < (HEAD_DIM // 4)
    partner = jnp.where(first, pltpu.roll(y, HEAD_DIM - HEAD_DIM // 4, 1), pltpu.roll(y, HEAD_DIM // 4, 1))
    return y * cos + partner * sin


def _attn_kernel(q_ref, k_ref, v_ref, cq_ref, sq_ref, ck_ref, sk_ref, gq_ref, gk_ref, o_ref, ks_ref, va_ref, *,
                 n_keys):
    @pl.when(pl.program_id(2) == 0)
    def _():
        def prep(c, carry):
            r0 = pl.multiple_of(c * TQ, TQ)
            kk = k_ref[0, pl.ds(r0, TQ), :].astype(F32)
            kn = _head_norm_rope(kk, gk_ref[0], ck_ref[pl.ds(r0, TQ), :], sk_ref[pl.ds(r0, TQ), :])
            ks_ref[pl.ds(r0, TQ), :] = kn.astype(BF16)
            return carry

        lax.fori_loop(0, n_keys // TQ, prep, 0)
        va_ref[:, 0:HEAD_DIM] = v_ref[0]
        va_ref[:, HEAD_DIM:2 * HEAD_DIM] = jnp.ones((n_keys, HEAD_DIM), BF16)

    k = ks_ref[...]
    va = va_ref[...]
    scale = HEAD_DIM ** -0.5 * math.log2(math.e)
    for hh in range(N_HEADS // N_KV_HEADS):
        cols = slice(hh * HEAD_DIM, (hh + 1) * HEAD_DIM)
        q = q_ref[:, cols].astype(F32)
        qn = _head_norm_rope(q, gq_ref[0], cq_ref[...], sq_ref[...]) * scale
        s = lax.dot_general(qn.astype(BF16), k, (((1,), (1,)), ((), ())), preferred_element_type=F32)
        m = jnp.max(s, axis=-1, keepdims=True)
        p = jnp.exp2(s - m)
        oa = jnp.dot(p.astype(BF16), va, preferred_element_type=F32)
        o = oa[:, 0:HEAD_DIM] / oa[:, HEAD_DIM:HEAD_DIM + 1]
        o_ref[:, cols] = o.astype(o_ref.dtype)


def attention(q_src, q_col0, q_row_tile0, n_q, kv, cos, sin, q_table_tile0, q_norm_g, k_norm_g, layer):
    n_keys = kv.shape[1]
    grp_w = Q_W // N_KV_HEADS
    qb = n_q // TQ
    kern = functools.partial(_attn_kernel, n_keys=n_keys)
    return pl.pallas_call(
        kern,
        grid=(BATCH, N_KV_HEADS, qb),
        in_specs=[
            pl.BlockSpec((TQ, grp_w), lambda b, g, i: (q_row_tile0 + b * qb + i, q_col0 // grp_w + g)),
            pl.BlockSpec((1, n_keys, HEAD_DIM), lambda b, g, i: (b, 0, g)),
            pl.BlockSpec((1, n_keys, HEAD_DIM), lambda b, g, i: (b, 0, N_KV_HEADS + g)),
            pl.BlockSpec((TQ, HEAD_DIM), lambda b, g, i: (q_table_tile0 + i, 0)),
            pl.BlockSpec((TQ, HEAD_DIM), lambda b, g, i: (q_table_tile0 + i, 0)),
            pl.BlockSpec((n_keys, HEAD_DIM), lambda b, g, i: (0, 0)),
            pl.BlockSpec((n_keys, HEAD_DIM), lambda b, g, i: (0, 0)),
            pl.BlockSpec((1, 1, HEAD_DIM), lambda b, g, i: (layer, 0, 0)),
            pl.BlockSpec((1, 1, HEAD_DIM), lambda b, g, i: (layer, 0, 0)),
        ],
        out_specs=pl.BlockSpec((TQ, grp_w), lambda b, g, i: (b * qb + i, g)),
        out_shape=jax.ShapeDtypeStruct((BATCH * n_q, Q_W), BF16),
        scratch_shapes=[pltpu.VMEM((n_keys, HEAD_DIM), BF16), pltpu.VMEM((n_keys, 2 * HEAD_DIM), BF16)],
        compiler_params=_cparams(("arbitrary", "arbitrary", "arbitrary"), 48),
        name="attention",
    )(q_src, kv, kv, cos, sin, cos, sin,
      q_norm_g.reshape(DEPTH, 1, HEAD_DIM), k_norm_g.reshape(DEPTH, 1, HEAD_DIM))


def _cmul(ar, ai, br, bi):
    return ar * br - ai * bi, ar * bi + ai * br


SSM_CB = 512


def _ssm_kernel(u_ref, bm_ref, cm_ref, lam_ref, pw_ref, y_ref, bu_ref, yv_ref, carry_ref, *, reverse):
    @pl.when(pl.program_id(1) == 0)
    def _():
        carry_ref[...] = jnp.zeros_like(carry_ref)

    rr = lax.broadcasted_iota(jnp.int32, (SSM_T, SSM_T), 0)
    tt = lax.broadcasted_iota(jnp.int32, (SSM_T, SSM_T), 1)
    perm = jnp.where(tt == (rr % SUBLANES) * SSM_TC + rr // SUBLANES, 1.0, 0.0).astype(BF16)
    u_scan = jnp.dot(perm, u_ref[0], preferred_element_type=F32).astype(BF16)
    for h in range(2):
        bu_ref[h] = jnp.dot(u_scan[:, h * SSM_HALF:(h + 1) * SSM_HALF], bm_ref[h], preferred_element_type=F32)

    last_row = 0 if reverse else (SSM_T - SUBLANES)
    edge = (SUBLANES - 1) if reverse else 0
    toward = (lambda x, k: pltpu.roll(x, SUBLANES - k, 0)) if reverse else (lambda x, k: pltpu.roll(x, k, 0))

    def step_rows(j):
        jj = (SSM_TC - 1 - j) if reverse else j
        return pl.ds(pl.multiple_of(jj * SUBLANES, SUBLANES), SUBLANES)

    for h, cb in [(h, cb) for h in range(2) for cb in range(SSM_HSTATE // SSM_CB)]:
        re = slice(cb * SSM_CB, (cb + 1) * SSM_CB)
        im = slice(SSM_HSTATE + cb * SSM_CB, SSM_HSTATE + (cb + 1) * SSM_CB)
        lr = lam_ref[h, :, re]
        li = lam_ref[h, :, im]

        def local_scan(j, st, h=h, re=re, im=im, lr=lr, li=li):
            rows = step_rows(j)
            pr, pi = _cmul(lr, li, st[0], st[1])
            nr = pr + bu_ref[h, rows, re]
            ni = pi + bu_ref[h, rows, im]
            bu_ref[h, rows, re] = nr
            bu_ref[h, rows, im] = ni
            return nr, ni

        zero = jnp.zeros((SUBLANES, SSM_CB), F32)
        lax.fori_loop(0, SSM_TC, local_scan, (zero, zero), unroll=4)

        er = bu_ref[h, last_row:last_row + SUBLANES, re]
        ei = bu_ref[h, last_row:last_row + SUBLANES, im]
        row = lax.broadcasted_iota(jnp.int32, (SUBLANES, SSM_CB), 0)
        dist = (SUBLANES - 1 - row) if reverse else row
        xr = jnp.where(row == edge, carry_ref[h, :, re], toward(er, 1))
        xi = jnp.where(row == edge, carry_ref[h, :, im], toward(ei, 1))
        for lvl, k in enumerate((1, 2, 4)):
            ar = pw_ref[h, lvl, :, re]
            ai = pw_ref[h, lvl, :, im]
            sr = jnp.where(dist >= k, toward(xr, k), 0.0)
            si = jnp.where(dist >= k, toward(xi, k), 0.0)
            mr, mi = _cmul(ar, ai, sr, si)
            xr = xr + mr
            xi = xi + mi
        cr, ci = _cmul(pw_ref[h, 0, :, re], pw_ref[h, 0, :, im], xr, xi)
        carry_ref[h, :, re] = toward(cr + er, 1)
        carry_ref[h, :, im] = toward(ci + ei, 1)

        def add_carry(j, g, h=h, re=re, im=im, lr=lr, li=li):
            rows = step_rows(j)
            bu_ref[h, rows, re] = bu_ref[h, rows, re] + g[0]
            bu_ref[h, rows, im] = bu_ref[h, rows, im] + g[1]
            return _cmul(lr, li, g[0], g[1])

        lax.fori_loop(0, SSM_TC, add_carry, _cmul(lr, li, xr, xi), unroll=4)

    for h in range(2):
        y = jnp.dot(bu_ref[h].astype(BF16), cm_ref[h], preferred_element_type=F32)
        for cb in range(SSM_HALF // LANES):
            cols = slice(cb * LANES, (cb + 1) * LANES)
            out_cols = slice(h * SSM_HALF + cb * LANES, h * SSM_HALF + (cb + 1) * LANES)
            slab = h * (SSM_HALF // LANES) + cb
            yv_ref[slab] = y[:, cols]
            for s in range(SUBLANES):
                y_ref[0, s * SSM_TC:(s + 1) * SSM_TC, out_cols] = yv_ref[slab, pl.ds(s, SSM_TC, stride=SUBLANES), :]


def _ssm_params(a_re, a_im, log_dt, b_re, b_im, c_re, c_im):
    lam = lax.complex(a_re.astype(F32), a_im.astype(F32))
    dt = jnp.exp(log_dt.astype(F32))[..., None]
    lam_bar = jnp.exp(lam * dt)
    b_bar = ((lam_bar - 1.0) / lam)[..., None] * lax.complex(b_re.astype(F32), b_im.astype(F32))
    gh = SSM_GROUPS // 2
    eye = jnp.eye(gh, dtype=F32)

    def b_block(m):
        m = m.reshape(2, 2, gh, SSM_STATE, SSM_GROUP)
        return jnp.einsum('dhgpc,gk->dhgckp', m, eye).reshape(2, 2, gh * SSM_GROUP, gh * SSM_STATE)

    def c_block(m):
        m = m.reshape(2, 2, gh, SSM_GROUP, SSM_STATE)
        return jnp.einsum('dhgcp,gk->dhkpgc', m, eye).reshape(2, 2, gh * SSM_STATE, gh * SSM_GROUP)

    bm = jnp.concatenate([b_block(jnp.real(b_bar)), b_block(jnp.imag(b_bar))], axis=-1).astype(BF16)
    cm = jnp.concatenate([c_block(c_re.astype(F32)), -c_block(c_im.astype(F32))], axis=-2).astype(BF16)

    def table(z):
        z = z.reshape(2, 2, SSM_HSTATE)
        t = jnp.concatenate([jnp.real(z), jnp.imag(z)], axis=-1)
        return jnp.broadcast_to(t[:, :, None, :], (2, 2, SUBLANES, 2 * SSM_HSTATE))

    lam_t = table(lam_bar)
    pw_t = jnp.stack([table(jnp.exp(lam * dt * (SSM_TC * k))) for k in (1, 2, 4)], axis=2)
    return bm, cm, lam_t, pw_t


SSM_CHUNKS = S_ALL // SSM_T


def ssm_scan(u_chunks, bm, cm, lam_t, pw_t, reverse):
    hs2 = 2 * SSM_HSTATE

    def chunk(b, c):
        if reverse:
            c = jnp.where(c == 0, 0, SSM_CHUNKS - c)
        return b * SSM_CHUNKS + c

    kern = functools.partial(_ssm_kernel, reverse=reverse)
    return pl.pallas_call(
        kern,
        grid=(BATCH, SSM_CHUNKS),
        in_specs=[
            pl.BlockSpec((1, SSM_T, SSM_WIDTH), lambda b, c: (chunk(b, c), 0, 0)),
            pl.BlockSpec((2, SSM_HALF, hs2), lambda b, c: (0, 0, 0)),
            pl.BlockSpec((2, hs2, SSM_HALF), lambda b, c: (0, 0, 0)),
            pl.BlockSpec((2, SUBLANES, hs2), lambda b, c: (0, 0, 0)),
            pl.BlockSpec((2, 3, SUBLANES, hs2), lambda b, c: (0, 0, 0, 0)),
        ],
        out_specs=pl.BlockSpec((1, SSM_T, SSM_WIDTH), lambda b, c: (chunk(b, c), 0, 0)),
        out_shape=jax.ShapeDtypeStruct(u_chunks.shape, F32),
        scratch_shapes=[pltpu.VMEM((2, SSM_T, hs2), F32), pltpu.VMEM((SSM_WIDTH // LANES, SSM_T, LANES), F32),
                        pltpu.VMEM((2, SUBLANES, hs2), F32)],
        compiler_params=_cparams(("arbitrary",) * 2, 32),
        name="ssm_scan",
    )(u_chunks, bm, cm, lam_t, pw_t)


def _ssm_chunk_of_tile(i):
    lat_tiles = SEQ // SSM_T
    return jnp.where(i < BATCH * lat_tiles,
                     (i // lat_tiles) * SSM_CHUNKS + 1 + i % lat_tiles,
                     (i - BATCH * lat_tiles) * SSM_CHUNKS)


def _gelu_tanh(x):
    return 0.5 * x * (1.0 + jnp.tanh(math.sqrt(2.0 / math.pi) * (x + 0.044715 * (x * x * x))))


def _ssm_out_kernel(u_ref, yf_ref, yb_ref, d_ref, w_ref, b_ref, o_ref):
    y = d_ref[0] * u_ref[...].astype(F32) + yf_ref[0] + yb_ref[0]
    y = _gelu_tanh(y)
    z = jnp.dot(y.astype(BF16), w_ref[0], preferred_element_type=F32) + b_ref[0]
    o_ref[...] = (y * jax.nn.sigmoid(z)).astype(o_ref.dtype)


def ssm_output(p, yf, yb, ssm_d, glu_w, glu_b, layer):
    n = p.shape[0]
    yspec = pl.BlockSpec((1, SSM_T, SSM_WIDTH), lambda i: (_ssm_chunk_of_tile(i), 0, 0))
    return pl.pallas_call(
        _ssm_out_kernel,
        grid=(n // SSM_T,),
        in_specs=[
            pl.BlockSpec((SSM_T, SSM_WIDTH), lambda i: (i, COL_SSM // SSM_WIDTH)),
            yspec,
            yspec,
            pl.BlockSpec((1, 1, SSM_WIDTH), lambda i: (layer, 0, 0)),
            pl.BlockSpec((1, SSM_WIDTH, SSM_WIDTH), lambda i: (layer, 0, 0)),
            pl.BlockSpec((1, 1, SSM_WIDTH), lambda i: (layer, 0, 0)),
        ],
        out_specs=pl.BlockSpec((SSM_T, SSM_WIDTH), lambda i: (i, 0)),
        out_shape=jax.ShapeDtypeStruct((n, SSM_WIDTH), BF16),
        compiler_params=_cparams(("arbitrary",), 32),
        name="ssm_output",
    )(p, yf, yb, ssm_d.reshape(DEPTH, 1, SSM_WIDTH), glu_w, glu_b.reshape(DEPTH, 1, SSM_WIDTH))


def _pool_kernel(u_ref, w_ref, s_ref, o_ref, *, seq_len):
    t0 = pl.multiple_of(pl.program_id(1) * SEQ_T, SEQ_T)
    halo = u_ref[0, pl.ds(t0, SEQ_T + 2 * SEQ_PAD), :]
    centre = u_ref[0, pl.ds(t0 + SEQ_PAD, SEQ_T), :].astype(F32)
    tt = lax.broadcasted_iota(jnp.int32, (SEQ_T, SEQ_T + 2 * SEQ_PAD), 0)
    rr = lax.broadcasted_iota(jnp.int32, (SEQ_T, SEQ_T + 2 * SEQ_PAD), 1) - SEQ_PAD
    tg = t0 + lax.broadcasted_iota(jnp.int32, (SEQ_T, 1), 0)
    for gi, w in enumerate(POOL_WINDOWS):
        cols = slice(gi * POOL_GROUP, (gi + 1) * POOL_GROUP)
        band = ((rr >= tt - w // 2) & (rr <= tt + w // 2 - 1)).astype(F32).astype(BF16)
        wsum = jnp.dot(band, halo[:, cols], preferred_element_type=F32)
        cnt = jnp.minimum(tg - w // 2 + w, seq_len) - jnp.maximum(tg - w // 2, 0)
        pooled = wsum / cnt.astype(F32) - centre[:, cols]
        mixed = jnp.dot(pooled.astype(BF16), w_ref[0, gi], preferred_element_type=F32)
        o_ref[0, :, cols] = (mixed * s_ref[0, :, cols]).astype(o_ref.dtype)


def pool_mixer(u_pad, pool_w, pool_scale, layer):
    nseq, lp, _ = u_pad.shape
    seq_len = lp - 2 * SEQ_PAD
    kern = functools.partial(_pool_kernel, seq_len=seq_len)
    ng = len(POOL_WINDOWS)
    return pl.pallas_call(
        kern,
        grid=(nseq, seq_len // SEQ_T),
        in_specs=[
            pl.BlockSpec((1, lp, POOL_WIDTH), lambda s, t: (s, 0, 0)),
            pl.BlockSpec((1, ng, POOL_GROUP, POOL_GROUP), lambda s, t: (layer, 0, 0, 0)),
            pl.BlockSpec((1, 1, POOL_WIDTH), lambda s, t: (layer, 0, 0)),
        ],
        out_specs=pl.BlockSpec((1, SEQ_T, POOL_WIDTH), lambda s, t: (s, t, 0)),
        out_shape=jax.ShapeDtypeStruct((nseq, seq_len, POOL_WIDTH), BF16),
        compiler_params=_cparams(("arbitrary", "arbitrary"), 32),
        name="pool_mixer",
    )(u_pad, pool_w, pool_scale.reshape(DEPTH, 1, POOL_WIDTH))


CONV_RB = 64


def _conv_kernel(u_ref, w_ref, b_ref, g_ref, beta_ref, o_ref, glu_ref):
    t0 = pl.multiple_of(pl.program_id(1) * SEQ_T, SEQ_T)
    rows = SEQ_T + 2 * SEQ_PAD
    a = u_ref[0, pl.ds(t0, rows), 0:CONV_WIDTH].astype(F32)
    g = u_ref[0, pl.ds(t0, rows), CONV_WIDTH:2 * CONV_WIDTH].astype(F32)
    glu_ref[...] = a * jax.nn.sigmoid(g)
    off = SEQ_PAD - CONV_TAPS // 2
    for rb in range(SEQ_T // CONV_RB):
        parts = []
        for cb in range(CONV_WIDTH // LANES):
            cols = slice(cb * LANES, (cb + 1) * LANES)
            acc = jnp.zeros((CONV_RB, LANES), F32)
            for k in range(CONV_TAPS):
                acc = acc + glu_ref[rb * CONV_RB + k + off:rb * CONV_RB + k + off + CONV_RB, cols] * w_ref[0, k:k + 1, cols]
            parts.append(acc)
        y = jnp.concatenate(parts, axis=-1) + b_ref[0]
        yc = y - jnp.mean(y, axis=-1, keepdims=True)
        yn = yc * lax.rsqrt(jnp.mean(yc * yc, axis=-1, keepdims=True) + EPS)
        yn = yn * g_ref[0] + beta_ref[0]
        o_ref[0, rb * CONV_RB:(rb + 1) * CONV_RB, :] = _silu(yn).astype(o_ref.dtype)


def conv_mixer(u_pad, dw_w, dw_b, ln_g, ln_b, layer):
    nseq, lp, _ = u_pad.shape
    seq_len = lp - 2 * SEQ_PAD
    vec = lambda a: a.reshape(DEPTH, 1, CONV_WIDTH)
    vspec = pl.BlockSpec((1, 1, CONV_WIDTH), lambda s, t: (layer, 0, 0))
    return pl.pallas_call(
        _conv_kernel,
        grid=(nseq, seq_len // SEQ_T),
        in_specs=[
            pl.BlockSpec((1, lp, 2 * CONV_WIDTH), lambda s, t: (s, 0, 0)),
            pl.BlockSpec((1, CONV_TAPS, CONV_WIDTH), lambda s, t: (layer, 0, 0)),
            vspec, vspec, vspec,
        ],
        out_specs=pl.BlockSpec((1, SEQ_T, CONV_WIDTH), lambda s, t: (s, t, 0)),
        out_shape=jax.ShapeDtypeStruct((nseq, seq_len, CONV_WIDTH), BF16),
        scratch_shapes=[pltpu.VMEM((SEQ_T + 2 * SEQ_PAD, CONV_WIDTH), F32)],
        compiler_params=_cparams(("arbitrary", "arbitrary"), 40),
        name="conv_mixer",
    )(u_pad, dw_w, vec(dw_b), vec(ln_g), vec(ln_b))


def _merge_kernel(oa_ref, ys_ref, yp_ref, yc_ref, g0_ref, g1_ref, g2_ref, g3_ref,
                  wa_ref, ws_ref, wp_ref, wc_ref, o_ref):
    def branch(x_ref, w_ref, g_ref):
        up = jnp.dot(x_ref[...], w_ref[0, 0], preferred_element_type=F32)
        return jax.nn.sigmoid(g_ref[...].astype(F32)) * up

    m = (branch(oa_ref, wa_ref, g0_ref) + branch(ys_ref, ws_ref, g1_ref)
         + branch(yp_ref, wp_ref, g2_ref) + branch(yc_ref, wc_ref, g3_ref))
    o_ref[...] = m.astype(o_ref.dtype)


def merge_branches(p, o_attn, y_ssm, y_pool, y_conv, w_attn, w_ssm, w_pool, w_conv, layer):
    n = o_attn.shape[0]
    nct = D_MODEL // TN
    gate_tile0 = COL_GATE // TN

    def gate_spec(br):
        return pl.BlockSpec((TM, TN), lambda i, j: (i, gate_tile0 + br * nct + j))

    def x_spec(width):
        return pl.BlockSpec((TM, width), lambda i, j: (i, 0))

    def w_spec(width):
        return pl.BlockSpec((1, 1, width, TN), lambda i, j: (layer, j, 0, 0))

    return pl.pallas_call(
        _merge_kernel,
        grid=(n // TM, nct),
        in_specs=[x_spec(Q_W), x_spec(SSM_WIDTH), x_spec(POOL_WIDTH), x_spec(CONV_WIDTH),
                  gate_spec(0), gate_spec(1), gate_spec(2), gate_spec(3),
                  w_spec(Q_W), w_spec(SSM_WIDTH), w_spec(POOL_WIDTH), w_spec(CONV_WIDTH)],
        out_specs=pl.BlockSpec((TM, TN), lambda i, j: (i, j)),
        out_shape=jax.ShapeDtypeStruct((n, D_MODEL), BF16),
        compiler_params=_cparams(("arbitrary", "arbitrary"), 40),
        name="merge_branches",
    )(o_attn, y_ssm, y_pool, y_conv, p, p, p, p, w_attn, w_ssm, w_pool, w_conv)


def _split_bf16(x):
    hi = x.astype(BF16)
    lo = (x - hi.astype(F32)).astype(BF16)
    return hi, lo


def _outproj_kernel(m_ref, x_ref, mod_ref, g_ref, w_ref, rw_ref, xo_ref, h_ref, lg_ref):
    r_hi, r_lo = _split_bf16(rw_ref[...])
    nt = (((1,), (1,)), ((), ()))
    half = m_ref.shape[0] // 2
    for c in range(2):
        rows = slice(c * half, (c + 1) * half)
        mix = jnp.dot(m_ref[rows, :], w_ref[0], preferred_element_type=F32)
        x = x_ref[rows, :] + mod_ref[0, 2:3, :] * mix
        xo_ref[rows, :] = x
        ms = jnp.mean(x * x, axis=-1, keepdims=True)
        h = x * lax.rsqrt(ms + EPS) * g_ref[0]
        h = h * (1.0 + mod_ref[0, 4:5, :]) + mod_ref[0, 3:4, :]
        h_ref[rows, :] = h
        h_hi, h_lo = _split_bf16(h)
        lg_ref[:, rows] = (lax.dot_general(r_hi, h_hi, nt, preferred_element_type=F32)
                           + lax.dot_general(r_hi, h_lo, nt, preferred_element_type=F32)
                           + lax.dot_general(r_lo, h_hi, nt, preferred_element_type=F32))


def out_projection(m, xa, mod, norm2_g, w_out, router_wt, layer):
    n = m.shape[0]
    tmo = TM
    return pl.pallas_call(
        _outproj_kernel,
        grid=(n // tmo,),
        in_specs=[
            pl.BlockSpec((tmo, D_MODEL), lambda i: (i, 0)),
            pl.BlockSpec((tmo, D_MODEL), lambda i: (i, 0)),
            pl.BlockSpec((1, 6, D_MODEL), lambda i: (_mod_id(i), 0, 0)),
            pl.BlockSpec((1, 1, D_MODEL), lambda i: (layer, 0, 0)),
            pl.BlockSpec((1, D_MODEL, D_MODEL), lambda i: (layer, 0, 0)),
            pl.BlockSpec((N_EXPERTS, D_MODEL), lambda i: (0, 0)),
        ],
        out_specs=[
            pl.BlockSpec((tmo, D_MODEL), lambda i: (i, 0)),
            pl.BlockSpec((tmo, D_MODEL), lambda i: (i, 0)),
            pl.BlockSpec((N_EXPERTS, tmo), lambda i: (0, i)),
        ],
        out_shape=[
            jax.ShapeDtypeStruct((n, D_MODEL), F32),
            jax.ShapeDtypeStruct((n, D_MODEL), F32),
            jax.ShapeDtypeStruct((N_EXPERTS, n), F32),
        ],
        compiler_params=_cparams(("arbitrary",), 60),
        name="out_projection",
    )(m, xa, mod, norm2_g.reshape(DEPTH, 1, D_MODEL), w_out, router_wt)


def _first_argmax(blk, row):
    m = jnp.max(blk, axis=0, keepdims=True)
    idx = jnp.min(jnp.where(blk == m, row, EXPERTS_PER_GROUP), axis=0, keepdims=True)
    return m, idx


def _route_kernel(lg_ref, rb_ref, e_ref, w_ref, rank_ref, cnt_ref, run_ref):
    @pl.when(pl.program_id(0) == 0)
    def _():
        run_ref[...] = jnp.zeros_like(run_ref)

    t = lg_ref.shape[1]
    scores = jax.nn.sigmoid(lg_ref[...])
    sel = scores + rb_ref[...]
    row = lax.broadcasted_iota(jnp.int32, (EXPERTS_PER_GROUP, t), 0)
    neg = jnp.float32(-jnp.inf)

    best = None
    for g in range(N_EXPERT_GROUPS):
        blk = sel[g * EXPERTS_PER_GROUP:(g + 1) * EXPERTS_PER_GROUP, :]
        m1, i1 = _first_argmax(blk, row)
        m2 = jnp.max(jnp.where(row == i1, neg, blk), axis=0, keepdims=True)
        gs = m1 + m2
        if best is None:
            best, grp = gs, jnp.zeros((1, t), jnp.int32)
        else:
            better = gs > best
            best = jnp.where(better, gs, best)
            grp = jnp.where(better, g, grp)

    in_sel = jnp.zeros((EXPERTS_PER_GROUP, t), F32)
    in_sc = jnp.zeros((EXPERTS_PER_GROUP, t), F32)
    for g in range(N_EXPERT_GROUPS):
        rows = slice(g * EXPERTS_PER_GROUP, (g + 1) * EXPERTS_PER_GROUP)
        in_sel = jnp.where(grp == g, sel[rows, :], in_sel)
        in_sc = jnp.where(grp == g, scores[rows, :], in_sc)
    _, i1 = _first_argmax(in_sel, row)
    _, i2 = _first_argmax(jnp.where(row == i1, neg, in_sel), row)
    s1 = jnp.sum(jnp.where(row == i1, in_sc, 0.0), axis=0, keepdims=True)
    s2 = jnp.sum(jnp.where(row == i2, in_sc, 0.0), axis=0, keepdims=True)
    e1 = grp * EXPERTS_PER_GROUP + i1
    e2 = grp * EXPERTS_PER_GROUP + i2
    e_ref[0:1, :] = e1
    e_ref[1:2, :] = e2
    w_ref[0:1, :] = s1 / (s1 + s2)
    w_ref[1:2, :] = s2 / (s1 + s2)

    erow = lax.broadcasted_iota(jnp.int32, (N_EXPERTS, t), 0)
    oh1 = erow == e1
    oh2 = erow == e2
    cnt = jnp.where(oh1 | oh2, 1.0, 0.0)
    a = lax.broadcasted_iota(jnp.int32, (t, t), 0)
    b = lax.broadcasted_iota(jnp.int32, (t, t), 1)
    before = jnp.where(a < b, 1.0, 0.0).astype(BF16)
    excl = jnp.dot(cnt.astype(BF16), before, preferred_element_type=F32) + run_ref[...]
    rank_ref[0:1, :] = jnp.sum(jnp.where(oh1, excl, 0.0), axis=0, keepdims=True).astype(jnp.int32)
    rank_ref[1:2, :] = jnp.sum(jnp.where(oh2, excl, 0.0), axis=0, keepdims=True).astype(jnp.int32)
    run_ref[...] = run_ref[...] + jnp.sum(cnt, axis=1, keepdims=True)
    cnt_ref[...] = run_ref[...]


def route(logits_t, router_b):
    n = logits_t.shape[1]
    return pl.pallas_call(
        _route_kernel,
        grid=(n // ROUTE_T,),
        in_specs=[
            pl.BlockSpec((N_EXPERTS, ROUTE_T), lambda i: (0, i)),
            pl.BlockSpec((N_EXPERTS, 1), lambda i: (0, 0)),
        ],
        out_specs=[
            pl.BlockSpec((TOP_K, ROUTE_T), lambda i: (0, i)),
            pl.BlockSpec((TOP_K, ROUTE_T), lambda i: (0, i)),
            pl.BlockSpec((TOP_K, ROUTE_T), lambda i: (0, i)),
            pl.BlockSpec((N_EXPERTS, 1), lambda i: (0, 0)),
        ],
        out_shape=[
            jax.ShapeDtypeStruct((TOP_K, n), jnp.int32),
            jax.ShapeDtypeStruct((TOP_K, n), F32),
            jax.ShapeDtypeStruct((TOP_K, n), jnp.int32),
            jax.ShapeDtypeStruct((N_EXPERTS, 1), F32),
        ],
        scratch_shapes=[pltpu.VMEM((N_EXPERTS, 1), F32)],
        compiler_params=_cparams(("arbitrary",), 32),
        name="route",
    )(logits_t, router_b.reshape(N_EXPERTS, 1).astype(F32))


def _row_copy(src, src_row, dst, dst_row, sem):
    return pltpu.make_async_copy(src.at[pl.ds(src_row, 1)], dst.at[pl.ds(dst_row, 1)], sem)


def _dispatch_kernel(dest_ref, h_ref, xs_in_ref, xs_ref, sem):
    del xs_in_ref
    n = dest_ref.shape[0] // TOP_K
    base = pl.program_id(0) * GATHER_T

    def start(t, carry):
        for k in range(TOP_K):
            _row_copy(h_ref, t, xs_ref, dest_ref[k * n + base + t], sem).start()
        return carry

    def wait(t, carry):
        for k in range(TOP_K):
            _row_copy(h_ref, t, xs_ref, dest_ref[k * n + base + t], sem).wait()
        return carry

    lax.fori_loop(0, GATHER_T, start, 0)
    lax.fori_loop(0, GATHER_T, wait, 0)


def dispatch(dest_flat, h2, n_slots):
    n = h2.shape[0]
    xs0 = jnp.zeros((n_slots, D_MODEL), F32)
    grid_spec = pltpu.PrefetchScalarGridSpec(
        num_scalar_prefetch=1,
        grid=(n // GATHER_T,),
        in_specs=[pl.BlockSpec((GATHER_T, D_MODEL), lambda i, d: (i, 0)), pl.BlockSpec(memory_space=pl.ANY)],
        out_specs=pl.BlockSpec(memory_space=pl.ANY),
        scratch_shapes=[pltpu.SemaphoreType.DMA(())],
    )
    return pl.pallas_call(
        _dispatch_kernel,
        grid_spec=grid_spec,
        out_shape=jax.ShapeDtypeStruct((n_slots, D_MODEL), F32),
        input_output_aliases={2: 0},
        compiler_params=_cparams(("arbitrary",), 32),
        name="dispatch",
    )(dest_flat, h2, xs0)


def _expert_kernel(be_ref, slot_ref, nxt_ref, nu_ref, x_ref, wg_hbm, wu_hbm, wd_hbm, o_ref,
                   wgf_ref, wuf_ref, wdf_ref, wgb_ref, wub_ref, wdb_ref, sem, *, layer):
    b = pl.program_id(0)
    e = be_ref[b]
    fresh = (b == 0) | (e != be_ref[jnp.maximum(b - 1, 0)])
    live = b < nu_ref[0]

    def weight_copies(expert, slot):
        return (pltpu.make_async_copy(wg_hbm.at[layer, expert], wgf_ref.at[slot], sem.at[slot]),
                pltpu.make_async_copy(wu_hbm.at[layer, expert], wuf_ref.at[slot], sem.at[slot]),
                pltpu.make_async_copy(wd_hbm.at[layer, expert], wdf_ref.at[slot], sem.at[slot]))

    @pl.when(b == 0)
    def _():
        for cp in weight_copies(e, 0):
            cp.start()

    @pl.when(fresh & live)
    def _():
        slot = slot_ref[b]
        nxt = nxt_ref[b]

        @pl.when(nxt >= 0)
        def _():
            for cp in weight_copies(nxt, 1 - slot):
                cp.start()

        for cp in weight_copies(e, slot):
            cp.wait()
        wgb_ref[...] = wgf_ref[slot].astype(BF16)
        wub_ref[...] = wuf_ref[slot].astype(BF16)
        wdb_ref[...] = wdf_ref[slot].astype(BF16)

    @pl.when(live)
    def _():
        x = x_ref[...].astype(BF16)
        gate = jnp.dot(x, wgb_ref[...], preferred_element_type=F32)
        up = jnp.dot(x, wub_ref[...], preferred_element_type=F32)
        hid = (_silu(gate) * up).astype(BF16)
        o_ref[...] = jnp.dot(hid, wdb_ref[...], preferred_element_type=F32)

    @pl.when(b >= nu_ref[0])
    def _():
        o_ref[...] = jnp.zeros_like(o_ref)


def experts(blk_e, blk_slot, blk_next, n_used, xs, w_gate, w_up, w_down, layer):
    n_slots = xs.shape[0]
    n_blocks = n_slots // MOE_BLOCK

    def live(b, nu):
        return jnp.minimum(b, nu[0] - 1)

    grid_spec = pltpu.PrefetchScalarGridSpec(
        num_scalar_prefetch=4,
        grid=(n_blocks,),
        in_specs=[
            pl.BlockSpec((MOE_BLOCK, D_MODEL), lambda b, be, sl, nx, nu: (live(b, nu), 0)),
            pl.BlockSpec(memory_space=pl.ANY),
            pl.BlockSpec(memory_space=pl.ANY),
            pl.BlockSpec(memory_space=pl.ANY),
        ],
        out_specs=pl.BlockSpec((MOE_BLOCK, D_MODEL), lambda b, be, sl, nx, nu: (b, 0)),
        scratch_shapes=[
            pltpu.VMEM((2, D_MODEL, EXPERT_FF), F32),
            pltpu.VMEM((2, D_MODEL, EXPERT_FF), F32),
            pltpu.VMEM((2, EXPERT_FF, D_MODEL), F32),
            pltpu.VMEM((D_MODEL, EXPERT_FF), BF16),
            pltpu.VMEM((D_MODEL, EXPERT_FF), BF16),
            pltpu.VMEM((EXPERT_FF, D_MODEL), BF16),
            pltpu.SemaphoreType.DMA((2,)),
        ],
    )
    return pl.pallas_call(
        functools.partial(_expert_kernel, layer=layer),
        grid_spec=grid_spec,
        out_shape=jax.ShapeDtypeStruct((n_slots, D_MODEL), F32),
        compiler_params=_cparams(("arbitrary",), 56),
        name="experts",
    )(blk_e, blk_slot, blk_next, n_used, xs, w_gate, w_up, w_down)


def _combine_kernel(dest_ref, x_ref, mod_ref, w_ref, fg_ref, ys_ref, o_ref, buf_ref, sem, *, final_norm):
    n = dest_ref.shape[0] // TOP_K
    base = pl.program_id(0) * GATHER_T

    def start(t, carry):
        for k in range(TOP_K):
            _row_copy(ys_ref, dest_ref[k * n + base + t], buf_ref.at[k], t, sem).start()
        return carry

    def wait(t, carry):
        for k in range(TOP_K):
            _row_copy(ys_ref, dest_ref[k * n + base + t], buf_ref.at[k], t, sem).wait()
        return carry

    lax.fori_loop(0, GATHER_T, start, 0)
    lax.fori_loop(0, GATHER_T, wait, 0)
    f = w_ref[:, 0:1] * buf_ref[0] + w_ref[:, 1:2] * buf_ref[1]
    x = x_ref[...] + mod_ref[0, 5:6, :] * f
    if final_norm:
        ms = jnp.mean(x * x, axis=-1, keepdims=True)
        x = x * lax.rsqrt(ms + EPS) * fg_ref[...]
    o_ref[...] = x


def combine(dest_flat, xa, mod, wts_rows, ys, final_g, final_norm):
    n = xa.shape[0]
    tiles_per_tm = TM // GATHER_T
    grid_spec = pltpu.PrefetchScalarGridSpec(
        num_scalar_prefetch=1,
        grid=(n // GATHER_T,),
        in_specs=[
            pl.BlockSpec((GATHER_T, D_MODEL), lambda i, d: (i, 0)),
            pl.BlockSpec((1, 6, D_MODEL), lambda i, d: (_mod_id(i // tiles_per_tm), 0, 0)),
            pl.BlockSpec((GATHER_T, TOP_K), lambda i, d: (i, 0)),
            pl.BlockSpec((1, D_MODEL), lambda i, d: (0, 0)),
            pl.BlockSpec(memory_space=pl.ANY),
        ],
        out_specs=pl.BlockSpec((GATHER_T, D_MODEL), lambda i, d: (i, 0)),
        scratch_shapes=[pltpu.VMEM((TOP_K, GATHER_T, D_MODEL), F32), pltpu.SemaphoreType.DMA(())],
    )
    kern = functools.partial(_combine_kernel, final_norm=final_norm)
    return pl.pallas_call(
        kern,
        grid_spec=grid_spec,
        out_shape=jax.ShapeDtypeStruct((n, D_MODEL), F32),
        compiler_params=_cparams(("arbitrary",), 32),
        name="combine",
    )(dest_flat, xa, mod, wts_rows, final_g.reshape(1, D_MODEL), ys)


def _slot_plan(e_idx, rank, counts, n_blocks):
    i32 = jnp.int32
    cnt = counts.reshape(N_EXPERTS).astype(i32)
    padded = (cnt + MOE_BLOCK - 1) // MOE_BLOCK * MOE_BLOCK
    pad_end = jnp.cumsum(padded)
    pad_start = pad_end - padded
    eids = jnp.arange(N_EXPERTS, dtype=i32)

    def lookup(table, idx):
        return jnp.sum(jnp.where(idx[..., None] == eids, table, 0), axis=-1).astype(i32)

    dest = lookup(pad_start, e_idx) + rank
    blk_start = jnp.arange(n_blocks, dtype=i32) * MOE_BLOCK
    blk_e = jnp.minimum(jnp.sum(pad_end[None, :] <= blk_start[:, None], axis=1), N_EXPERTS - 1).astype(i32)
    n_used = (pad_end[-1] // MOE_BLOCK).astype(i32).reshape(1)
    has = cnt > 0
    run = jnp.cumsum(has.astype(i32)) - 1
    first_from = lax.cummin(jnp.where(has, eids, N_EXPERTS)[::-1])[::-1]
    nxt = jnp.concatenate([first_from[1:], jnp.full((1,), N_EXPERTS, i32)])
    nxt = jnp.where(nxt >= N_EXPERTS, -1, nxt)
    return dest.reshape(-1).astype(i32), blk_e, lookup(run % 2, blk_e), lookup(nxt, blk_e), n_used


def _col_tiles(w):
    d, k, n = w.shape
    return w.astype(BF16).reshape(d, k, n // TN, TN).transpose(0, 2, 1, 3)


def _pad_rows(u):
    return jnp.pad(u, ((0, 0), (SEQ_PAD, SEQ_PAD), (0, 0)))


def _layer(layer, xa, mod, W, with_ctx, final_norm):
    n_lat_tiles = N_LAT // TM
    n_ctx_tiles = N_CTX // TM
    if with_ctx:
        p = in_projection(xa, mod, W['norm1_g'], W['w_in'], layer, 0, n_lat_tiles + n_ctx_tiles, IN_W, TN_IN)
        p_lat, p_ctx = p[:N_LAT], p[N_LAT:]
    else:
        p = in_projection(xa, mod, W['norm1_g'], W['w_in'], layer, 0, n_lat_tiles, IN_W, TN_IN)
        p_lat = p
        p_ctx = in_projection(xa, mod, W['norm1_g'], W['w_in'], layer, n_lat_tiles, n_ctx_tiles, CTX_STATE_W, TN)
    n_rows = p.shape[0]

    def seq(a, lo, hi, n_seq_rows):
        return a[:, lo:hi].reshape(BATCH, n_seq_rows, hi - lo)

    kv = jnp.concatenate([seq(p_ctx, 0, 2 * KV_W, CTX_LEN), seq(p_lat, 0, 2 * KV_W, SEQ)], axis=1)
    cos, sin = _rope_tables(CTX_LEN)
    o_attn = attention(p, COL_Q, 0, SEQ, kv, cos, sin, CTX_LEN // TQ, W['q_norm_g'], W['k_norm_g'], layer)
    if with_ctx:
        kv_c = seq(p_ctx, 0, 2 * KV_W, CTX_LEN)
        ones, zeros = jnp.ones((CTX_LEN, HEAD_DIM), F32), jnp.zeros((CTX_LEN, HEAD_DIM), F32)
        o_attn_c = attention(p, COL_Q, N_LAT // TQ, CTX_LEN, kv_c, ones, zeros, 0,
                             W['q_norm_g'], W['k_norm_g'], layer)
        o_attn = jnp.concatenate([o_attn, o_attn_c], axis=0)

    u_seq = jnp.concatenate([seq(p_ctx, COL_SSM, COL_SSM + SSM_WIDTH, CTX_LEN),
                             seq(p_lat, COL_SSM, COL_SSM + SSM_WIDTH, SEQ)], axis=1)
    u_chunks = u_seq.reshape(BATCH * SSM_CHUNKS, SSM_T, SSM_WIDTH)
    bm, cm, lam_t, pw_t = _ssm_params(W['ssm_a_re'][layer], W['ssm_a_im'][layer], W['ssm_log_dt'][layer],
                                      W['ssm_b_re'][layer], W['ssm_b_im'][layer],
                                      W['ssm_c_re'][layer], W['ssm_c_im'][layer])
    yf = ssm_scan(u_chunks, bm[0], cm[0], lam_t[0], pw_t[0], reverse=False)
    yb = ssm_scan(u_chunks, bm[1], cm[1], lam_t[1], pw_t[1], reverse=True)
    y_ssm = ssm_output(p, yf, yb, W['ssm_d'], W['ssm_glu_w'], W['ssm_glu_b'], layer)

    def seq_mixers(src, n_seq_rows):
        up = _pad_rows(seq(src, COL_POOL, COL_POOL + POOL_WIDTH, n_seq_rows))
        uc = _pad_rows(seq(src, COL_CONV, COL_CONV + 2 * CONV_WIDTH, n_seq_rows))
        yp = pool_mixer(up, W['pool_w'], W['pool_scale'], layer)
        yc = conv_mixer(uc, W['conv_dw_w'], W['conv_dw_b'], W['conv_ln_g'], W['conv_ln_b'], layer)
        return yp.reshape(-1, POOL_WIDTH), yc.reshape(-1, CONV_WIDTH)

    y_pool, y_conv = seq_mixers(p_lat, SEQ)
    if with_ctx:
        yp_c, yc_c = seq_mixers(p_ctx, CTX_LEN)
        y_pool = jnp.concatenate([y_pool, yp_c], axis=0)
        y_conv = jnp.concatenate([y_conv, yc_c], axis=0)

    m = merge_branches(p, o_attn, y_ssm, y_pool, y_conv,
                       W['w_up_attn'], W['w_up_ssm'], W['w_up_pool'], W['w_up_conv'], layer)
    x1, h2, logits_t = out_projection(m, xa, mod, W['norm2_g'], W['w_out'], W['router_wt'], layer)

    e_idx, wts, rank, counts = route(logits_t, W['router_b'])
    n_blocks = -(-n_rows * TOP_K // MOE_BLOCK) + N_EXPERTS
    dest_flat, blk_e, blk_slot, blk_next, n_used = _slot_plan(e_idx, rank, counts, n_blocks)
    xs = dispatch(dest_flat, h2, n_blocks * MOE_BLOCK)
    ys = experts(blk_e, blk_slot, blk_next, n_used, xs, W['moe_w_gate'], W['moe_w_up'], W['moe_w_down'], layer)
    x2 = combine(dest_flat, x1, mod, wts.T, ys, W['final_g'], final_norm)
    if with_ctx:
        return x2
    return jnp.concatenate([x2, xa[N_LAT:]], axis=0)


def kernel(x, c, ctx, c_ctx, ada_w, ada_b, norm1_g, norm2_g, w_in, q_norm_g, k_norm_g, ssm_a_re, ssm_a_im, ssm_log_dt, ssm_b_re, ssm_b_im, ssm_c_re, ssm_c_im, ssm_d, ssm_glu_w, ssm_glu_b, pool_w, pool_scale, conv_dw_w, conv_dw_b, conv_ln_g, conv_ln_b, w_up_attn, w_up_ssm, w_up_pool, w_up_conv, w_out, router_w, router_b, moe_w_gate, moe_w_up, moe_w_down, final_g):
    W = dict(
        norm1_g=norm1_g, norm2_g=norm2_g, q_norm_g=q_norm_g, k_norm_g=k_norm_g,
        w_in=w_in.astype(BF16),
        ssm_a_re=ssm_a_re, ssm_a_im=ssm_a_im, ssm_log_dt=ssm_log_dt, ssm_b_re=ssm_b_re, ssm_b_im=ssm_b_im,
        ssm_c_re=ssm_c_re, ssm_c_im=ssm_c_im, ssm_d=ssm_d, ssm_glu_w=ssm_glu_w.astype(BF16), ssm_glu_b=ssm_glu_b,
        pool_w=pool_w.astype(BF16), pool_scale=pool_scale,
        conv_dw_w=conv_dw_w, conv_dw_b=conv_dw_b, conv_ln_g=conv_ln_g, conv_ln_b=conv_ln_b,
        w_up_attn=_col_tiles(w_up_attn), w_up_ssm=_col_tiles(w_up_ssm), w_up_pool=_col_tiles(w_up_pool),
        w_up_conv=_col_tiles(w_up_conv), w_out=w_out.astype(BF16),
        router_wt=router_w.T, router_b=router_b,
        moe_w_gate=moe_w_gate, moe_w_up=moe_w_up, moe_w_down=moe_w_down, final_g=final_g,
    )
    cc = jnp.concatenate([c, c_ctx[None, :], jnp.zeros((SUBLANES - BATCH - 1, D_MODEL), F32)], axis=0)
    mod_all = ada_modulation(cc, ada_w, ada_b)
    xa = jnp.concatenate([x.reshape(N_LAT, D_MODEL), ctx.reshape(N_CTX, D_MODEL)], axis=0)
    for layer in range(DEPTH):
        mod = mod_all[layer, :BATCH + 1].reshape(BATCH + 1, 6, D_MODEL)
        xa = _layer(layer, xa, mod, W, with_ctx=(layer < DEPTH - 1), final_norm=(layer == DEPTH - 1))
    return xa[:N_LAT].reshape(BATCH, SEQ, D_MODEL)
```

```python
import functools
import math

import jax
import jax.numpy as jnp
from jax import lax
from jax.experimental import pallas as pl
from jax.experimental.pallas import tpu as pltpu

F32 = jnp.float32
BF16 = jnp.bfloat16

D_MODEL = 2048
BATCH = 2
SEQ = 4096
DEPTH = 2
GRID_W = 64
CTX_LEN = 256
N_HEADS = 8
N_KV_HEADS = 2
HEAD_DIM = 128
ROPE_THETA = 10000.0
SSM_WIDTH = 512
SSM_GROUP = 16
SSM_GROUPS = SSM_WIDTH // SSM_GROUP
SSM_STATE = 64
POOL_WIDTH = 512
POOL_WINDOWS = (2, 4, 8, 16)
POOL_GROUP = POOL_WIDTH // len(POOL_WINDOWS)
CONV_WIDTH = 512
CONV_TAPS = 31
N_BRANCHES = 4
N_EXPERTS = 64
N_EXPERT_GROUPS = 8
EXPERTS_PER_GROUP = N_EXPERTS // N_EXPERT_GROUPS
TOP_K = 2
EXPERT_FF = 512
EPS = 1e-6

Q_W = N_HEADS * HEAD_DIM
KV_W = N_KV_HEADS * HEAD_DIM
IN_W = 2 * KV_W + SSM_WIDTH + Q_W + POOL_WIDTH + 2 * CONV_WIDTH + N_BRANCHES * D_MODEL
CTX_STATE_W = 2 * KV_W + SSM_WIDTH
COL_K, COL_V, COL_SSM, COL_Q = 0, KV_W, 2 * KV_W, 2 * KV_W + SSM_WIDTH
COL_POOL = COL_Q + Q_W
COL_CONV = COL_POOL + POOL_WIDTH
COL_GATE = COL_CONV + 2 * CONV_WIDTH

N_LAT = BATCH * SEQ
N_CTX = BATCH * CTX_LEN
S_ALL = CTX_LEN + SEQ

V7X_VMEM_BYTES = 64 * 1024 * 1024
SUBLANES = 8
LANES = 128
BF16_ROWS = 16

TM = 512
TN = 512
TN_IN = IN_W // 4
TILES_PER_BATCH = SEQ // TM
TQ = 512
KEY_PREP_T = 256
SSM_T = 256
SSM_TC = SSM_T // SUBLANES
SSM_HALF = SSM_WIDTH // 2
SSM_HSTATE = SSM_GROUPS // 2 * SSM_STATE
SEQ_T = 256
SEQ_PAD = 16
MOE_BLOCK = 128
ROUTE_T = 512


def _cparams(sem, vmem_mb):
    return pltpu.CompilerParams(dimension_semantics=sem, vmem_limit_bytes=vmem_mb * 1024 * 1024)


def _mod_id(tile):
    return jnp.minimum(tile // TILES_PER_BATCH, BATCH)


def _silu(x):
    return x * jax.nn.sigmoid(x)


def _ada_kernel(c_ref, w_ref, b_ref, o_ref):
    c = c_ref[...]
    a = _silu(c).astype(BF16)
    o_ref[0] = jnp.dot(a, w_ref[0].astype(BF16), preferred_element_type=F32) + b_ref[0]


def ada_modulation(cc, ada_w, ada_b):
    tn = 1024
    n6 = 6 * D_MODEL
    return pl.pallas_call(
        _ada_kernel,
        grid=(DEPTH, n6 // tn),
        in_specs=[
            pl.BlockSpec((SUBLANES, D_MODEL), lambda l, j: (0, 0)),
            pl.BlockSpec((1, D_MODEL, tn), lambda l, j: (l, 0, j)),
            pl.BlockSpec((1, 1, tn), lambda l, j: (l, 0, j)),
        ],
        out_specs=pl.BlockSpec((1, SUBLANES, tn), lambda l, j: (l, 0, j)),
        out_shape=jax.ShapeDtypeStruct((DEPTH, SUBLANES, n6), F32),
        compiler_params=_cparams(("arbitrary", "arbitrary"), 40),
        name="ada_modulation",
    )(cc, ada_w, ada_b.reshape(DEPTH, 1, n6))


def _inproj_kernel(x_ref, mod_ref, g_ref, w_ref, o_ref, h_ref):
    @pl.when(pl.program_id(1) == 0)
    def _():
        x = x_ref[...]
        ms = jnp.mean(x * x, axis=-1, keepdims=True)
        y = x * lax.rsqrt(ms + EPS) * g_ref[0]
        h = y * (1.0 + mod_ref[0, 1:2, :]) + mod_ref[0, 0:1, :]
        h_ref[...] = h.astype(BF16)

    o_ref[...] = jnp.dot(h_ref[...], w_ref[0], preferred_element_type=F32).astype(o_ref.dtype)


def in_projection(xa, mod, norm_g, w_in, layer, row_tile0, n_row_tiles, n_cols, tn):
    return pl.pallas_call(
        _inproj_kernel,
        grid=(n_row_tiles, n_cols // tn),
        in_specs=[
            pl.BlockSpec((TM, D_MODEL), lambda i, j: (i + row_tile0, 0)),
            pl.BlockSpec((1, 6, D_MODEL), lambda i, j: (_mod_id(i + row_tile0), 0, 0)),
            pl.BlockSpec((1, 1, D_MODEL), lambda i, j: (layer, 0, 0)),
            pl.BlockSpec((1, D_MODEL, tn), lambda i, j: (layer, 0, j)),
        ],
        out_specs=pl.BlockSpec((TM, tn), lambda i, j: (i, j)),
        out_shape=jax.ShapeDtypeStruct((n_row_tiles * TM, n_cols), BF16),
        scratch_shapes=[pltpu.VMEM((TM, D_MODEL), BF16)],
        compiler_params=_cparams(("arbitrary", "arbitrary"), 56),
        name="in_projection",
    )(xa, mod, norm_g.reshape(DEPTH, 1, D_MODEL), w_in)


def _rope_tables(n_ctx_rows):
    half = HEAD_DIM // 4
    inv_freq = ROPE_THETA ** (-jnp.arange(half, dtype=F32) / half)
    t = jnp.arange(SEQ)
    ang_r = (t // GRID_W).astype(F32)[:, None] * inv_freq[None, :]
    ang_c = (t % GRID_W).astype(F32)[:, None] * inv_freq[None, :]
    cos = jnp.concatenate([jnp.cos(ang_r), jnp.cos(ang_r), jnp.cos(ang_c), jnp.cos(ang_c)], axis=-1)
    sin = jnp.concatenate([-jnp.sin(ang_r), jnp.sin(ang_r), -jnp.sin(ang_c), jnp.sin(ang_c)], axis=-1)
    cos = jnp.concatenate([jnp.ones((n_ctx_rows, HEAD_DIM), F32), cos], axis=0)
    sin = jnp.concatenate([jnp.zeros((n_ctx_rows, HEAD_DIM), F32), sin], axis=0)
    return cos, sin


def _head_norm_rope(x, g, cos, sin):
    ms = jnp.mean(x * x, axis=-1, keepdims=True)
    y = x * lax.rsqrt(ms + EPS) * g
    lane = lax.broadcasted_iota(jnp.int32, y.shape, 1)
    first = (lane % (HEAD_DIM // 2)) < (HEAD_DIM // 4)
    partner = jnp.where(first, pltpu.roll(y, HEAD_DIM - HEAD_DIM // 4, 1), pltpu.roll(y, HEAD_DIM // 4, 1))
    return y * cos + partner * sin


def _attn_kernel(q_ref, k_ref, v_ref, cq_ref, sq_ref, ck_ref, sk_ref, gq_ref, gk_ref, o_ref, ks_ref, va_ref, *,
                 n_keys):
    @pl.when(pl.program_id(2) == 0)
    def _():
        def prep(c, carry):
            r0 = pl.multiple_of(c * KEY_PREP_T, KEY_PREP_T)
            rows = pl.ds(r0, KEY_PREP_T)
            kk = k_ref[0, rows, :].astype(F32)
            kn = _head_norm_rope(kk, gk_ref[0], ck_ref[rows, :], sk_ref[rows, :])
            ks_ref[rows, :] = kn.astype(BF16)
            return carry

        lax.fori_loop(0, n_keys // KEY_PREP_T, prep, 0)
        va_ref[:, 0:HEAD_DIM] = v_ref[0]
        va_ref[:, HEAD_DIM:2 * HEAD_DIM] = jnp.ones((n_keys, HEAD_DIM), BF16)

    k = ks_ref[...]
    va = va_ref[...]
    scale = HEAD_DIM ** -0.5 * math.log2(math.e)
    for hh in range(N_HEADS // N_KV_HEADS):
        cols = slice(hh * HEAD_DIM, (hh + 1) * HEAD_DIM)
        q = q_ref[:, cols].astype(F32)
        qn = _head_norm_rope(q, gq_ref[0], cq_ref[...], sq_ref[...]) * scale
        s = lax.dot_general(qn.astype(BF16), k, (((1,), (1,)), ((), ())), preferred_element_type=F32)
        m = jnp.max(s, axis=-1, keepdims=True)
        p = jnp.exp2(s - m)
        oa = jnp.dot(p.astype(BF16), va, preferred_element_type=F32)
        o = oa[:, 0:HEAD_DIM] / oa[:, HEAD_DIM:HEAD_DIM + 1]
        o_ref[:, cols] = o.astype(o_ref.dtype)


def attention(q_src, q_col0, q_row0, n_q, kv, cos_q, sin_q, cos_k, sin_k, q_norm_g, k_norm_g, layer):
    n_keys = kv.shape[1]
    grp_w = Q_W // N_KV_HEADS
    tq = min(TQ, n_q)
    qb = n_q // tq
    kern = functools.partial(_attn_kernel, n_keys=n_keys)
    return pl.pallas_call(
        kern,
        grid=(BATCH, N_KV_HEADS, qb),
        in_specs=[
            pl.BlockSpec((tq, grp_w), lambda b, g, i: (q_row0 // tq + b * qb + i, q_col0 // grp_w + g)),
            pl.BlockSpec((1, n_keys, HEAD_DIM), lambda b, g, i: (b, 0, g)),
            pl.BlockSpec((1, n_keys, HEAD_DIM), lambda b, g, i: (b, 0, N_KV_HEADS + g)),
            pl.BlockSpec((tq, HEAD_DIM), lambda b, g, i: (i, 0)),
            pl.BlockSpec((tq, HEAD_DIM), lambda b, g, i: (i, 0)),
            pl.BlockSpec((n_keys, HEAD_DIM), lambda b, g, i: (0, 0)),
            pl.BlockSpec((n_keys, HEAD_DIM), lambda b, g, i: (0, 0)),
            pl.BlockSpec((1, 1, HEAD_DIM), lambda b, g, i: (layer, 0, 0)),
            pl.BlockSpec((1, 1, HEAD_DIM), lambda b, g, i: (layer, 0, 0)),
        ],
        out_specs=pl.BlockSpec((tq, grp_w), lambda b, g, i: (b * qb + i, g)),
        out_shape=jax.ShapeDtypeStruct((BATCH * n_q, Q_W), BF16),
        scratch_shapes=[pltpu.VMEM((n_keys, HEAD_DIM), BF16), pltpu.VMEM((n_keys, 2 * HEAD_DIM), BF16)],
        compiler_params=_cparams(("arbitrary", "arbitrary", "arbitrary"), 56),
        name="attention",
    )(q_src, kv, kv, cos_q, sin_q, cos_k, sin_k,
      q_norm_g.reshape(DEPTH, 1, HEAD_DIM), k_norm_g.reshape(DEPTH, 1, HEAD_DIM))


def _cmul(ar, ai, br, bi):
    return ar * br - ai * bi, ar * bi + ai * br


SSM_CB = 512


def _ssm_kernel(u_ref, bm_ref, cm_ref, lam_ref, pw_ref, y_ref, bu_ref, yv_ref, carry_ref, *, reverse):
    @pl.when(pl.program_id(1) == 0)
    def _():
        carry_ref[...] = jnp.zeros_like(carry_ref)

    rr = lax.broadcasted_iota(jnp.int32, (SSM_T, SSM_T), 0)
    tt = lax.broadcasted_iota(jnp.int32, (SSM_T, SSM_T), 1)
    perm = jnp.where(tt == (rr % SUBLANES) * SSM_TC + rr // SUBLANES, 1.0, 0.0).astype(BF16)
    u_scan = jnp.dot(perm, u_ref[0], preferred_element_type=F32).astype(BF16)
    for h in range(2):
        bu_ref[h] = jnp.dot(u_scan[:, h * SSM_HALF:(h + 1) * SSM_HALF], bm_ref[h], preferred_element_type=F32)

    last_row = 0 if reverse else (SSM_T - SUBLANES)
    edge = (SUBLANES - 1) if reverse else 0
    toward = (lambda x, k: pltpu.roll(x, SUBLANES - k, 0)) if reverse else (lambda x, k: pltpu.roll(x, k, 0))

    def step_rows(j):
        jj = (SSM_TC - 1 - j) if reverse else j
        return pl.ds(pl.multiple_of(jj * SUBLANES, SUBLANES), SUBLANES)

    for h, cb in [(h, cb) for h in range(2) for cb in range(SSM_HSTATE // SSM_CB)]:
        re = slice(cb * SSM_CB, (cb + 1) * SSM_CB)
        im = slice(SSM_HSTATE + cb * SSM_CB, SSM_HSTATE + (cb + 1) * SSM_CB)
        lr = lam_ref[h, :, re]
        li = lam_ref[h, :, im]

        def local_scan(j, st, h=h, re=re, im=im, lr=lr, li=li):
            rows = step_rows(j)
            pr, pi = _cmul(lr, li, st[0], st[1])
            nr = pr + bu_ref[h, rows, re]
            ni = pi + bu_ref[h, rows, im]
            bu_ref[h, rows, re] = nr
            bu_ref[h, rows, im] = ni
            return nr, ni

        zero = jnp.zeros((SUBLANES, SSM_CB), F32)
        lax.fori_loop(0, SSM_TC, local_scan, (zero, zero), unroll=4)

        er = bu_ref[h, last_row:last_row + SUBLANES, re]
        ei = bu_ref[h, last_row:last_row + SUBLANES, im]
        row = lax.broadcasted_iota(jnp.int32, (SUBLANES, SSM_CB), 0)
        dist = (SUBLANES - 1 - row) if reverse else row
        xr = jnp.where(row == edge, carry_ref[h, :, re], toward(er, 1))
        xi = jnp.where(row == edge, carry_ref[h, :, im], toward(ei, 1))
        for lvl, k in enumerate((1, 2, 4)):
            ar = pw_ref[h, lvl, :, re]
            ai = pw_ref[h, lvl, :, im]
            sr = jnp.where(dist >= k, toward(xr, k), 0.0)
            si = jnp.where(dist >= k, toward(xi, k), 0.0)
            mr, mi = _cmul(ar, ai, sr, si)
            xr = xr + mr
            xi = xi + mi
        cr, ci = _cmul(pw_ref[h, 0, :, re], pw_ref[h, 0, :, im], xr, xi)
        carry_ref[h, :, re] = toward(cr + er, 1)
        carry_ref[h, :, im] = toward(ci + ei, 1)

        def add_carry(j, g, h=h, re=re, im=im, lr=lr, li=li):
            rows = step_rows(j)
            bu_ref[h, rows, re] = bu_ref[h, rows, re] + g[0]
            bu_ref[h, rows, im] = bu_ref[h, rows, im] + g[1]
            return _cmul(lr, li, g[0], g[1])

        lax.fori_loop(0, SSM_TC, add_carry, _cmul(lr, li, xr, xi), unroll=4)

    for h in range(2):
        y = jnp.dot(bu_ref[h].astype(BF16), cm_ref[h], preferred_element_type=F32)
        for cb in range(SSM_HALF // LANES):
            cols = slice(cb * LANES, (cb + 1) * LANES)
            out_cols = slice(h * SSM_HALF + cb * LANES, h * SSM_HALF + (cb + 1) * LANES)
            slab = h * (SSM_HALF // LANES) + cb
            yv_ref[slab] = y[:, cols]
            for s in range(SUBLANES):
                y_ref[0, s * SSM_TC:(s + 1) * SSM_TC, out_cols] = yv_ref[slab, pl.ds(s, SSM_TC, stride=SUBLANES), :]


def _ssm_params(a_re, a_im, log_dt, b_re, b_im, c_re, c_im):
    lam = lax.complex(a_re.astype(F32), a_im.astype(F32))
    dt = jnp.exp(log_dt.astype(F32))[..., None]
    lam_bar = jnp.exp(lam * dt)
    b_bar = ((lam_bar - 1.0) / lam)[..., None] * lax.complex(b_re.astype(F32), b_im.astype(F32))
    gh = SSM_GROUPS // 2
    eye = jnp.eye(gh, dtype=F32)

    def b_block(m):
        m = m.reshape(2, 2, gh, SSM_STATE, SSM_GROUP)
        return jnp.einsum('dhgpc,gk->dhgckp', m, eye).reshape(2, 2, gh * SSM_GROUP, gh * SSM_STATE)

    def c_block(m):
        m = m.reshape(2, 2, gh, SSM_GROUP, SSM_STATE)
        return jnp.einsum('dhgcp,gk->dhkpgc', m, eye).reshape(2, 2, gh * SSM_STATE, gh * SSM_GROUP)

    bm = jnp.concatenate([b_block(jnp.real(b_bar)), b_block(jnp.imag(b_bar))], axis=-1).astype(BF16)
    cm = jnp.concatenate([c_block(c_re.astype(F32)), -c_block(c_im.astype(F32))], axis=-2).astype(BF16)

    def table(z):
        z = z.reshape(2, 2, SSM_HSTATE)
        t = jnp.concatenate([jnp.real(z), jnp.imag(z)], axis=-1)
        return jnp.broadcast_to(t[:, :, None, :], (2, 2, SUBLANES, 2 * SSM_HSTATE))

    lam_t = table(lam_bar)
    pw_t = jnp.stack([table(jnp.exp(lam * dt * (SSM_TC * k))) for k in (1, 2, 4)], axis=2)
    return bm, cm, lam_t, pw_t


SSM_CHUNKS = S_ALL // SSM_T


def ssm_scan(u_chunks, bm, cm, lam_t, pw_t, reverse):
    hs2 = 2 * SSM_HSTATE

    def chunk(b, c):
        if reverse:
            c = jnp.where(c == 0, 0, SSM_CHUNKS - c)
        return b * SSM_CHUNKS + c

    kern = functools.partial(_ssm_kernel, reverse=reverse)
    return pl.pallas_call(
        kern,
        grid=(BATCH, SSM_CHUNKS),
        in_specs=[
            pl.BlockSpec((1, SSM_T, SSM_WIDTH), lambda b, c: (chunk(b, c), 0, 0)),
            pl.BlockSpec((2, SSM_HALF, hs2), lambda b, c: (0, 0, 0)),
            pl.BlockSpec((2, hs2, SSM_HALF), lambda b, c: (0, 0, 0)),
            pl.BlockSpec((2, SUBLANES, hs2), lambda b, c: (0, 0, 0)),
            pl.BlockSpec((2, 3, SUBLANES, hs2), lambda b, c: (0, 0, 0, 0)),
        ],
        out_specs=pl.BlockSpec((1, SSM_T, SSM_WIDTH), lambda b, c: (chunk(b, c), 0, 0)),
        out_shape=jax.ShapeDtypeStruct(u_chunks.shape, F32),
        scratch_shapes=[pltpu.VMEM((2, SSM_T, hs2), F32), pltpu.VMEM((SSM_WIDTH // LANES, SSM_T, LANES), F32),
                        pltpu.VMEM((2, SUBLANES, hs2), F32)],
        compiler_params=_cparams(("arbitrary",) * 2, 32),
        name="ssm_scan",
    )(u_chunks, bm, cm, lam_t, pw_t)


def _ssm_chunk_of_tile(i):
    lat_tiles = SEQ // SSM_T
    return jnp.where(i < BATCH * lat_tiles,
                     (i // lat_tiles) * SSM_CHUNKS + 1 + i % lat_tiles,
                     (i - BATCH * lat_tiles) * SSM_CHUNKS)


def _gelu_tanh(x):
    return 0.5 * x * (1.0 + jnp.tanh(math.sqrt(2.0 / math.pi) * (x + 0.044715 * (x * x * x))))


def _ssm_out_kernel(u_ref, yf_ref, yb_ref, d_ref, w_ref, b_ref, o_ref):
    y = d_ref[0] * u_ref[...].astype(F32) + yf_ref[0] + yb_ref[0]
    y = _gelu_tanh(y)
    z = jnp.dot(y.astype(BF16), w_ref[0], preferred_element_type=F32) + b_ref[0]
    o_ref[...] = (y * jax.nn.sigmoid(z)).astype(o_ref.dtype)


def ssm_output(p, yf, yb, ssm_d, glu_w, glu_b, layer):
    n = p.shape[0]
    yspec = pl.BlockSpec((1, SSM_T, SSM_WIDTH), lambda i: (_ssm_chunk_of_tile(i), 0, 0))
    return pl.pallas_call(
        _ssm_out_kernel,
        grid=(n // SSM_T,),
        in_specs=[
            pl.BlockSpec((SSM_T, SSM_WIDTH), lambda i: (i, COL_SSM // SSM_WIDTH)),
            yspec,
            yspec,
            pl.BlockSpec((1, 1, SSM_WIDTH), lambda i: (layer, 0, 0)),
            pl.BlockSpec((1, SSM_WIDTH, SSM_WIDTH), lambda i: (layer, 0, 0)),
            pl.BlockSpec((1, 1, SSM_WIDTH), lambda i: (layer, 0, 0)),
        ],
        out_specs=pl.BlockSpec((SSM_T, SSM_WIDTH), lambda i: (i, 0)),
        out_shape=jax.ShapeDtypeStruct((n, SSM_WIDTH), BF16),
        compiler_params=_cparams(("arbitrary",), 32),
        name="ssm_output",
    )(p, yf, yb, ssm_d.reshape(DEPTH, 1, SSM_WIDTH), glu_w, glu_b.reshape(DEPTH, 1, SSM_WIDTH))


def _pool_kernel(u_ref, w_ref, s_ref, o_ref, *, seq_len):
    t0 = pl.multiple_of(pl.program_id(1) * SEQ_T, SEQ_T)
    halo = u_ref[0, pl.ds(t0, SEQ_T + 2 * SEQ_PAD), :]
    centre = u_ref[0, pl.ds(t0 + SEQ_PAD, SEQ_T), :].astype(F32)
    tt = lax.broadcasted_iota(jnp.int32, (SEQ_T, SEQ_T + 2 * SEQ_PAD), 0)
    rr = lax.broadcasted_iota(jnp.int32, (SEQ_T, SEQ_T + 2 * SEQ_PAD), 1) - SEQ_PAD
    tg = t0 + lax.broadcasted_iota(jnp.int32, (SEQ_T, 1), 0)
    for gi, w in enumerate(POOL_WINDOWS):
        cols = slice(gi * POOL_GROUP, (gi + 1) * POOL_GROUP)
        band = ((rr >= tt - w // 2) & (rr <= tt + w // 2 - 1)).astype(F32).astype(BF16)
        wsum = jnp.dot(band, halo[:, cols], preferred_element_type=F32)
        cnt = jnp.minimum(tg - w // 2 + w, seq_len) - jnp.maximum(tg - w // 2, 0)
        pooled = wsum / cnt.astype(F32) - centre[:, cols]
        mixed = jnp.dot(pooled.astype(BF16), w_ref[0, gi], preferred_element_type=F32)
        o_ref[0, :, cols] = (mixed * s_ref[0, :, cols]).astype(o_ref.dtype)


def pool_mixer(u_pad, pool_w, pool_scale, layer):
    nseq, lp, _ = u_pad.shape
    seq_len = lp - 2 * SEQ_PAD
    kern = functools.partial(_pool_kernel, seq_len=seq_len)
    ng = len(POOL_WINDOWS)
    return pl.pallas_call(
        kern,
        grid=(nseq, seq_len // SEQ_T),
        in_specs=[
            pl.BlockSpec((1, lp, POOL_WIDTH), lambda s, t: (s, 0, 0)),
            pl.BlockSpec((1, ng, POOL_GROUP, POOL_GROUP), lambda s, t: (layer, 0, 0, 0)),
            pl.BlockSpec((1, 1, POOL_WIDTH), lambda s, t: (layer, 0, 0)),
        ],
        out_specs=pl.BlockSpec((1, SEQ_T, POOL_WIDTH), lambda s, t: (s, t, 0)),
        out_shape=jax.ShapeDtypeStruct((nseq, seq_len, POOL_WIDTH), BF16),
        compiler_params=_cparams(("arbitrary", "arbitrary"), 32),
        name="pool_mixer",
    )(u_pad, pool_w, pool_scale.reshape(DEPTH, 1, POOL_WIDTH))


CONV_RB = 64


def _conv_kernel(u_ref, w_ref, b_ref, g_ref, beta_ref, o_ref, glu_ref):
    t0 = pl.multiple_of(pl.program_id(1) * SEQ_T, SEQ_T)
    rows = SEQ_T + 2 * SEQ_PAD
    a = u_ref[0, pl.ds(t0, rows), 0:CONV_WIDTH].astype(F32)
    g = u_ref[0, pl.ds(t0, rows), CONV_WIDTH:2 * CONV_WIDTH].astype(F32)
    glu_ref[...] = a * jax.nn.sigmoid(g)
    off = SEQ_PAD - CONV_TAPS // 2
    for rb in range(SEQ_T // CONV_RB):
        parts = []
        for cb in range(CONV_WIDTH // LANES):
            cols = slice(cb * LANES, (cb + 1) * LANES)
            acc = jnp.zeros((CONV_RB, LANES), F32)
            for k in range(CONV_TAPS):
                acc = acc + glu_ref[rb * CONV_RB + k + off:rb * CONV_RB + k + off + CONV_RB, cols] * w_ref[0, k:k + 1, cols]
            parts.append(acc)
        y = jnp.concatenate(parts, axis=-1) + b_ref[0]
        yc = y - jnp.mean(y, axis=-1, keepdims=True)
        yn = yc * lax.rsqrt(jnp.mean(yc * yc, axis=-1, keepdims=True) + EPS)
        yn = yn * g_ref[0] + beta_ref[0]
        o_ref[0, rb * CONV_RB:(rb + 1) * CONV_RB, :] = _silu(yn).astype(o_ref.dtype)


def conv_mixer(u_pad, dw_w, dw_b, ln_g, ln_b, layer):
    nseq, lp, _ = u_pad.shape
    seq_len = lp - 2 * SEQ_PAD
    vec = lambda a: a.reshape(DEPTH, 1, CONV_WIDTH)
    vspec = pl.BlockSpec((1, 1, CONV_WIDTH), lambda s, t: (layer, 0, 0))
    return pl.pallas_call(
        _conv_kernel,
        grid=(nseq, seq_len // SEQ_T),
        in_specs=[
            pl.BlockSpec((1, lp, 2 * CONV_WIDTH), lambda s, t: (s, 0, 0)),
            pl.BlockSpec((1, CONV_TAPS, CONV_WIDTH), lambda s, t: (layer, 0, 0)),
            vspec, vspec, vspec,
        ],
        out_specs=pl.BlockSpec((1, SEQ_T, CONV_WIDTH), lambda s, t: (s, t, 0)),
        out_shape=jax.ShapeDtypeStruct((nseq, seq_len, CONV_WIDTH), BF16),
        scratch_shapes=[pltpu.VMEM((SEQ_T + 2 * SEQ_PAD, CONV_WIDTH), F32)],
        compiler_params=_cparams(("arbitrary", "arbitrary"), 40),
        name="conv_mixer",
    )(u_pad, dw_w, vec(dw_b), vec(ln_g), vec(ln_b))


def _merge_kernel(oa_ref, ys_ref, yp_ref, yc_ref, g0_ref, g1_ref, g2_ref, g3_ref,
                  wa_ref, ws_ref, wp_ref, wc_ref, o_ref):
    def branch(x_ref, w_ref, g_ref):
        up = jnp.dot(x_ref[...], w_ref[0, 0], preferred_element_type=F32)
        return jax.nn.sigmoid(g_ref[...].astype(F32)) * up

    m = (branch(oa_ref, wa_ref, g0_ref) + branch(ys_ref, ws_ref, g1_ref)
         + branch(yp_ref, wp_ref, g2_ref) + branch(yc_ref, wc_ref, g3_ref))
    o_ref[...] = m.astype(o_ref.dtype)


def merge_branches(p, o_attn, y_ssm, y_pool, y_conv, w_attn, w_ssm, w_pool, w_conv, layer):
    n = o_attn.shape[0]
    nct = D_MODEL // TN
    gate_tile0 = COL_GATE // TN

    def gate_spec(br):
        return pl.BlockSpec((TM, TN), lambda i, j: (i, gate_tile0 + br * nct + j))

    def x_spec(width):
        return pl.BlockSpec((TM, width), lambda i, j: (i, 0))

    def w_spec(width):
        return pl.BlockSpec((1, 1, width, TN), lambda i, j: (layer, j, 0, 0))

    return pl.pallas_call(
        _merge_kernel,
        grid=(n // TM, nct),
        in_specs=[x_spec(Q_W), x_spec(SSM_WIDTH), x_spec(POOL_WIDTH), x_spec(CONV_WIDTH),
                  gate_spec(0), gate_spec(1), gate_spec(2), gate_spec(3),
                  w_spec(Q_W), w_spec(SSM_WIDTH), w_spec(POOL_WIDTH), w_spec(CONV_WIDTH)],
        out_specs=pl.BlockSpec((TM, TN), lambda i, j: (i, j)),
        out_shape=jax.ShapeDtypeStruct((n, D_MODEL), BF16),
        compiler_params=_cparams(("arbitrary", "arbitrary"), 40),
        name="merge_branches",
    )(o_attn, y_ssm, y_pool, y_conv, p, p, p, p, w_attn, w_ssm, w_pool, w_conv)


def _split_bf16(x):
    hi = x.astype(BF16)
    lo = (x - hi.astype(F32)).astype(BF16)
    return hi, lo


def _outproj_kernel(m_ref, x_ref, mod_ref, g_ref, w_ref, rw_ref, xo_ref, h_ref, lg_ref):
    r_hi, r_lo = _split_bf16(rw_ref[...])
    nt = (((1,), (1,)), ((), ()))
    half = m_ref.shape[0] // 2
    for c in range(2):
        rows = slice(c * half, (c + 1) * half)
        mix = jnp.dot(m_ref[rows, :], w_ref[0], preferred_element_type=F32)
        x = x_ref[rows, :] + mod_ref[0, 2:3, :] * mix
        xo_ref[rows, :] = x
        ms = jnp.mean(x * x, axis=-1, keepdims=True)
        h = x * lax.rsqrt(ms + EPS) * g_ref[0]
        h = h * (1.0 + mod_ref[0, 4:5, :]) + mod_ref[0, 3:4, :]
        h_ref[rows, :] = h
        h_hi, h_lo = _split_bf16(h)
        lg_ref[:, rows] = (lax.dot_general(r_hi, h_hi, nt, preferred_element_type=F32)
                           + lax.dot_general(r_hi, h_lo, nt, preferred_element_type=F32)
                           + lax.dot_general(r_lo, h_hi, nt, preferred_element_type=F32))


def out_projection(m, xa, mod, norm2_g, w_out, router_wt, layer):
    n = m.shape[0]
    tmo = TM
    return pl.pallas_call(
        _outproj_kernel,
        grid=(n // tmo,),
        in_specs=[
            pl.BlockSpec((tmo, D_MODEL), lambda i: (i, 0)),
            pl.BlockSpec((tmo, D_MODEL), lambda i: (i, 0)),
            pl.BlockSpec((1, 6, D_MODEL), lambda i: (_mod_id(i), 0, 0)),
            pl.BlockSpec((1, 1, D_MODEL), lambda i: (layer, 0, 0)),
            pl.BlockSpec((1, D_MODEL, D_MODEL), lambda i: (layer, 0, 0)),
            pl.BlockSpec((N_EXPERTS, D_MODEL), lambda i: (0, 0)),
        ],
        out_specs=[
            pl.BlockSpec((tmo, D_MODEL), lambda i: (i, 0)),
            pl.BlockSpec((tmo, D_MODEL), lambda i: (i, 0)),
            pl.BlockSpec((N_EXPERTS, tmo), lambda i: (0, i)),
        ],
        out_shape=[
            jax.ShapeDtypeStruct((n, D_MODEL), F32),
            jax.ShapeDtypeStruct((n, D_MODEL), F32),
            jax.ShapeDtypeStruct((N_EXPERTS, n), F32),
        ],
        compiler_params=_cparams(("arbitrary",), 60),
        name="out_projection",
    )(m, xa, mod, norm2_g.reshape(DEPTH, 1, D_MODEL), w_out, router_wt)


def _first_argmax(blk, row):
    m = jnp.max(blk, axis=0, keepdims=True)
    idx = jnp.min(jnp.where(blk == m, row, EXPERTS_PER_GROUP), axis=0, keepdims=True)
    return m, idx


def _route_kernel(lg_ref, rb_ref, e_ref, w_ref, rank_ref, cnt_ref, run_ref):
    @pl.when(pl.program_id(0) == 0)
    def _():
        run_ref[...] = jnp.zeros_like(run_ref)

    t = lg_ref.shape[1]
    scores = jax.nn.sigmoid(lg_ref[...])
    sel = scores + rb_ref[...]
    row = lax.broadcasted_iota(jnp.int32, (EXPERTS_PER_GROUP, t), 0)
    neg = jnp.float32(-jnp.inf)

    best = None
    for g in range(N_EXPERT_GROUPS):
        blk = sel[g * EXPERTS_PER_GROUP:(g + 1) * EXPERTS_PER_GROUP, :]
        m1, i1 = _first_argmax(blk, row)
        m2 = jnp.max(jnp.where(row == i1, neg, blk), axis=0, keepdims=True)
        gs = m1 + m2
        if best is None:
            best, grp = gs, jnp.zeros((1, t), jnp.int32)
        else:
            better = gs > best
            best = jnp.where(better, gs, best)
            grp = jnp.where(better, g, grp)

    in_sel = jnp.zeros((EXPERTS_PER_GROUP, t), F32)
    in_sc = jnp.zeros((EXPERTS_PER_GROUP, t), F32)
    for g in range(N_EXPERT_GROUPS):
        rows = slice(g * EXPERTS_PER_GROUP, (g + 1) * EXPERTS_PER_GROUP)
        in_sel = jnp.where(grp == g, sel[rows, :], in_sel)
        in_sc = jnp.where(grp == g, scores[rows, :], in_sc)
    _, i1 = _first_argmax(in_sel, row)
    _, i2 = _first_argmax(jnp.where(row == i1, neg, in_sel), row)
    s1 = jnp.sum(jnp.where(row == i1, in_sc, 0.0), axis=0, keepdims=True)
    s2 = jnp.sum(jnp.where(row == i2, in_sc, 0.0), axis=0, keepdims=True)
    e1 = grp * EXPERTS_PER_GROUP + i1
    e2 = grp * EXPERTS_PER_GROUP + i2
    e_ref[0:1, :] = e1
    e_ref[1:2, :] = e2
    w_ref[0:1, :] = s1 / (s1 + s2)
    w_ref[1:2, :] = s2 / (s1 + s2)

    erow = lax.broadcasted_iota(jnp.int32, (N_EXPERTS, t), 0)
    oh1 = erow == e1
    oh2 = erow == e2
    cnt = jnp.where(oh1 | oh2, 1.0, 0.0)
    a = lax.broadcasted_iota(jnp.int32, (t, t), 0)
    b = lax.broadcasted_iota(jnp.int32, (t, t), 1)
    before = jnp.where(a < b, 1.0, 0.0).astype(BF16)
    excl = jnp.dot(cnt.astype(BF16), before, preferred_element_type=F32) + run_ref[...]
    rank_ref[0:1, :] = jnp.sum(jnp.where(oh1, excl, 0.0), axis=0, keepdims=True).astype(jnp.int32)
    rank_ref[1:2, :] = jnp.sum(jnp.where(oh2, excl, 0.0), axis=0, keepdims=True).astype(jnp.int32)
    run_ref[...] = run_ref[...] + jnp.sum(cnt, axis=1, keepdims=True)
    cnt_ref[...] = run_ref[...]


def route(logits_t, router_b):
    n = logits_t.shape[1]
    return pl.pallas_call(
        _route_kernel,
        grid=(n // ROUTE_T,),
        in_specs=[
            pl.BlockSpec((N_EXPERTS, ROUTE_T), lambda i: (0, i)),
            pl.BlockSpec((N_EXPERTS, 1), lambda i: (0, 0)),
        ],
        out_specs=[
            pl.BlockSpec((TOP_K, ROUTE_T), lambda i: (0, i)),
            pl.BlockSpec((TOP_K, ROUTE_T), lambda i: (0, i)),
            pl.BlockSpec((TOP_K, ROUTE_T), lambda i: (0, i)),
            pl.BlockSpec((N_EXPERTS, 1), lambda i: (0, 0)),
        ],
        out_shape=[
            jax.ShapeDtypeStruct((TOP_K, n), jnp.int32),
            jax.ShapeDtypeStruct((TOP_K, n), F32),
            jax.ShapeDtypeStruct((TOP_K, n), jnp.int32),
            jax.ShapeDtypeStruct((N_EXPERTS, 1), F32),
        ],
        scratch_shapes=[pltpu.VMEM((N_EXPERTS, 1), F32)],
        compiler_params=_cparams(("arbitrary",), 32),
        name="route",
    )(logits_t, router_b.reshape(N_EXPERTS, 1).astype(F32))


def _expert_kernel(be_ref, slot_ref, nxt_ref, nu_ref, tok_ref, dst_ref, h_hbm, wg_hbm, wu_hbm, wd_hbm, o_hbm,
                   xbuf_ref, ybuf_ref, wgf_ref, wuf_ref, wdf_ref, wgb_ref, wub_ref, wdb_ref, sem, xsem, ysem,
                   *, layer):
    b = pl.program_id(0)
    n_used = nu_ref[0]
    e = be_ref[b]
    fresh = (b == 0) | (e != be_ref[jnp.maximum(b - 1, 0)])
    live = b < n_used
    has_next = b + 1 < n_used
    cur = b % 2

    def row_gathers(block, buf):
        return [pltpu.make_async_copy(h_hbm.at[pl.ds(tok_ref[block * MOE_BLOCK + r], 1)],
                                      xbuf_ref.at[buf, pl.ds(r, 1)], xsem.at[buf]) for r in range(MOE_BLOCK)]

    def row_scatters(block, buf):
        return [pltpu.make_async_copy(ybuf_ref.at[buf, pl.ds(r, 1)],
                                      o_hbm.at[pl.ds(dst_ref[block * MOE_BLOCK + r], 1)], ysem.at[buf])
                for r in range(MOE_BLOCK)]

    @pl.when(b == 0)
    def _():
        for cp in row_gathers(0, 0):
            cp.start()

    def weight_copies(expert, slot):
        return (pltpu.make_async_copy(wg_hbm.at[layer, expert], wgf_ref.at[slot], sem.at[slot]),
                pltpu.make_async_copy(wu_hbm.at[layer, expert], wuf_ref.at[slot], sem.at[slot]),
                pltpu.make_async_copy(wd_hbm.at[layer, expert], wdf_ref.at[slot], sem.at[slot]))

    @pl.when(b == 0)
    def _():
        for cp in weight_copies(e, 0):
            cp.start()

    @pl.when(fresh & live)
    def _():
        slot = slot_ref[b]
        nxt = nxt_ref[b]

        @pl.when(nxt >= 0)
        def _():
            for cp in weight_copies(nxt, 1 - slot):
                cp.start()

        for cp in weight_copies(e, slot):
            cp.wait()
        wgb_ref[...] = wgf_ref[slot].astype(BF16)
        wub_ref[...] = wuf_ref[slot].astype(BF16)
        wdb_ref[...] = wdf_ref[slot].astype(BF16)

    @pl.when(live & (b >= 2))
    def _():
        for cp in row_scatters(b - 2, cur):
            cp.wait()

    def block(scatter_prev, gather_next):
        for cp in row_gathers(b, cur):
            cp.wait()
        if scatter_prev:
            for cp in row_scatters(b - 1, 1 - cur):
                cp.start()
        if gather_next:
            for cp in row_gathers(b + 1, 1 - cur):
                cp.start()
        x = xbuf_ref[cur].astype(BF16)
        gate = jnp.dot(x, wgb_ref[...], preferred_element_type=F32)
        up = jnp.dot(x, wub_ref[...], preferred_element_type=F32)
        hid = (_silu(gate) * up).astype(BF16)
        ybuf_ref[cur] = jnp.dot(hid, wdb_ref[...], preferred_element_type=F32)

    @pl.when(b == 0)
    def _():
        block(False, True)

    @pl.when(live & (b >= 1) & has_next)
    def _():
        block(True, True)

    @pl.when(live & (b >= 1) & jnp.logical_not(has_next))
    def _():
        block(True, False)

    @pl.when(b == n_used)
    def _():
        last = n_used - 1
        for cp in row_scatters(last, last % 2):
            cp.start()
        for cp in row_scatters(last - 1, 1 - last % 2):
            cp.wait()
        for cp in row_scatters(last, last % 2):
            cp.wait()

    @pl.when(jnp.logical_not(live))
    def _():
        ybuf_ref[0] = jnp.zeros((MOE_BLOCK, D_MODEL), F32)
        rows = pl.ds(pl.multiple_of(b * MOE_BLOCK, MOE_BLOCK), MOE_BLOCK)
        cp = pltpu.make_async_copy(ybuf_ref.at[0], o_hbm.at[rows], ysem.at[0])
        cp.start()
        cp.wait()


def _invert_kernel(dest_ref, tok_ref, dst_ref):
    n = dest_ref.shape[0] // TOP_K

    def clear(s, carry):
        tok_ref[s] = 0
        dst_ref[s] = -1
        return carry

    def place(i, carry):
        for k in range(TOP_K):
            tok_ref[dest_ref[k * n + i]] = i
            dst_ref[dest_ref[k * n + i]] = k * n + i
        return carry

    lax.fori_loop(0, tok_ref.shape[0], clear, 0)
    lax.fori_loop(0, n, place, 0)


def slot_tokens(dest_flat, n_slots):
    smem = pl.BlockSpec(memory_space=pltpu.SMEM)
    shape = jax.ShapeDtypeStruct((n_slots,), jnp.int32)
    return pl.pallas_call(
        _invert_kernel,
        in_specs=[smem],
        out_specs=[smem, smem],
        out_shape=[shape, shape],
        name="slot_tokens",
    )(dest_flat)


def experts(blk_e, blk_slot, blk_next, n_used, slot_tok, slot_dst, h2, w_gate, w_up, w_down, layer):
    n = h2.shape[0]
    n_slots = slot_tok.shape[0]
    n_blocks = n_slots // MOE_BLOCK
    assert n_blocks - N_EXPERTS >= 2
    assert n_slots == TOP_K * n + N_EXPERTS * MOE_BLOCK
    grid_spec = pltpu.PrefetchScalarGridSpec(
        num_scalar_prefetch=6,
        grid=(n_blocks,),
        in_specs=[
            pl.BlockSpec(memory_space=pl.ANY),
            pl.BlockSpec(memory_space=pl.ANY),
            pl.BlockSpec(memory_space=pl.ANY),
            pl.BlockSpec(memory_space=pl.ANY),
        ],
        out_specs=pl.BlockSpec(memory_space=pl.ANY),
        scratch_shapes=[
            pltpu.VMEM((2, MOE_BLOCK, D_MODEL), F32),
            pltpu.VMEM((2, MOE_BLOCK, D_MODEL), F32),
            pltpu.VMEM((2, D_MODEL, EXPERT_FF), F32),
            pltpu.VMEM((2, D_MODEL, EXPERT_FF), F32),
            pltpu.VMEM((2, EXPERT_FF, D_MODEL), F32),
            pltpu.VMEM((D_MODEL, EXPERT_FF), BF16),
            pltpu.VMEM((D_MODEL, EXPERT_FF), BF16),
            pltpu.VMEM((EXPERT_FF, D_MODEL), BF16),
            pltpu.SemaphoreType.DMA((2,)),
            pltpu.SemaphoreType.DMA((2,)),
            pltpu.SemaphoreType.DMA((2,)),
        ],
    )
    return pl.pallas_call(
        functools.partial(_expert_kernel, layer=layer),
        grid_spec=grid_spec,
        out_shape=jax.ShapeDtypeStruct((n_slots, D_MODEL), F32),
        compiler_params=_cparams(("arbitrary",), 56),
        name="experts",
    )(blk_e, blk_slot, blk_next, n_used, slot_tok, slot_dst, h2, w_gate, w_up, w_down)


def _combine_kernel(x_ref, mod_ref, w_ref, fg_ref, y0_ref, y1_ref, o_ref, *, final_norm):
    f = w_ref[:, 0:1] * y0_ref[...] + w_ref[:, 1:2] * y1_ref[...]
    x = x_ref[...] + mod_ref[0, 5:6, :] * f
    if final_norm:
        ms = jnp.mean(x * x, axis=-1, keepdims=True)
        x = x * lax.rsqrt(ms + EPS) * fg_ref[...]
    o_ref[...] = x


def combine(xa, mod, wts_rows, ys, final_g, final_norm):
    n = xa.shape[0]
    n_tiles = n // TM
    kern = functools.partial(_combine_kernel, final_norm=final_norm)
    return pl.pallas_call(
        kern,
        grid=(n_tiles,),
        in_specs=[
            pl.BlockSpec((TM, D_MODEL), lambda i: (i, 0)),
            pl.BlockSpec((1, 6, D_MODEL), lambda i: (_mod_id(i), 0, 0)),
            pl.BlockSpec((TM, TOP_K), lambda i: (i, 0)),
            pl.BlockSpec((1, D_MODEL), lambda i: (0, 0)),
            pl.BlockSpec((TM, D_MODEL), lambda i: (i, 0)),
            pl.BlockSpec((TM, D_MODEL), lambda i: (n_tiles + i, 0)),
        ],
        out_specs=pl.BlockSpec((TM, D_MODEL), lambda i: (i, 0)),
        out_shape=jax.ShapeDtypeStruct((n, D_MODEL), F32),
        compiler_params=_cparams(("arbitrary",), 48),
        name="combine",
    )(xa, mod, wts_rows, final_g.reshape(1, D_MODEL), ys, ys)


def _slot_plan(e_idx, rank, counts, n_blocks):
    i32 = jnp.int32
    cnt = counts.reshape(N_EXPERTS).astype(i32)
    padded = (cnt + MOE_BLOCK - 1) // MOE_BLOCK * MOE_BLOCK
    pad_end = jnp.cumsum(padded)
    pad_start = pad_end - padded
    eids = jnp.arange(N_EXPERTS, dtype=i32)

    def lookup(table, idx):
        return jnp.sum(jnp.where(idx[..., None] == eids, table, 0), axis=-1).astype(i32)

    dest = lookup(pad_start, e_idx) + rank
    blk_start = jnp.arange(n_blocks, dtype=i32) * MOE_BLOCK
    blk_e = jnp.minimum(jnp.sum(pad_end[None, :] <= blk_start[:, None], axis=1), N_EXPERTS - 1).astype(i32)
    n_used = (pad_end[-1] // MOE_BLOCK).astype(i32).reshape(1)
    has = cnt > 0
    run = jnp.cumsum(has.astype(i32)) - 1
    first_from = lax.cummin(jnp.where(has, eids, N_EXPERTS)[::-1])[::-1]
    nxt = jnp.concatenate([first_from[1:], jnp.full((1,), N_EXPERTS, i32)])
    nxt = jnp.where(nxt >= N_EXPERTS, -1, nxt)
    return dest.reshape(-1).astype(i32), blk_e, lookup(run % 2, blk_e), lookup(nxt, blk_e), n_used


def _col_tiles(w):
    d, k, n = w.shape
    return w.astype(BF16).reshape(d, k, n // TN, TN).transpose(0, 2, 1, 3)


def _pad_rows(u):
    return jnp.pad(u, ((0, 0), (SEQ_PAD, SEQ_PAD), (0, 0)))


def _layer(layer, xa, mod, W, with_ctx, final_norm):
    n_lat_tiles = N_LAT // TM
    n_ctx_tiles = N_CTX // TM
    if with_ctx:
        p = in_projection(xa, mod, W['norm1_g'], W['w_in'], layer, 0, n_lat_tiles + n_ctx_tiles, IN_W, TN_IN)
        p_lat, p_ctx = p[:N_LAT], p[N_LAT:]
    else:
        p = in_projection(xa, mod, W['norm1_g'], W['w_in'], layer, 0, n_lat_tiles, IN_W, TN_IN)
        p_lat = p
        p_ctx = in_projection(xa, mod, W['norm1_g'], W['w_in'], layer, n_lat_tiles, n_ctx_tiles, CTX_STATE_W, TN)
    n_rows = p.shape[0]

    def seq(a, lo, hi, n_seq_rows):
        return a[:, lo:hi].reshape(BATCH, n_seq_rows, hi - lo)

    kv = jnp.concatenate([seq(p_ctx, 0, 2 * KV_W, CTX_LEN), seq(p_lat, 0, 2 * KV_W, SEQ)], axis=1)
    cos, sin = _rope_tables(CTX_LEN)
    o_attn = attention(p, COL_Q, 0, SEQ, kv, cos[CTX_LEN:], sin[CTX_LEN:], cos, sin,
                       W['q_norm_g'], W['k_norm_g'], layer)
    if with_ctx:
        kv_c = seq(p_ctx, 0, 2 * KV_W, CTX_LEN)
        ones, zeros = jnp.ones((CTX_LEN, HEAD_DIM), F32), jnp.zeros((CTX_LEN, HEAD_DIM), F32)
        o_attn_c = attention(p, COL_Q, N_LAT, CTX_LEN, kv_c, ones, zeros, ones, zeros,
                             W['q_norm_g'], W['k_norm_g'], layer)
        o_attn = jnp.concatenate([o_attn, o_attn_c], axis=0)

    u_seq = jnp.concatenate([seq(p_ctx, COL_SSM, COL_SSM + SSM_WIDTH, CTX_LEN),
                             seq(p_lat, COL_SSM, COL_SSM + SSM_WIDTH, SEQ)], axis=1)
    u_chunks = u_seq.reshape(BATCH * SSM_CHUNKS, SSM_T, SSM_WIDTH)
    bm, cm, lam_t, pw_t = _ssm_params(W['ssm_a_re'][layer], W['ssm_a_im'][layer], W['ssm_log_dt'][layer],
                                      W['ssm_b_re'][layer], W['ssm_b_im'][layer],
                                      W['ssm_c_re'][layer], W['ssm_c_im'][layer])
    yf = ssm_scan(u_chunks, bm[0], cm[0], lam_t[0], pw_t[0], reverse=False)
    yb = ssm_scan(u_chunks, bm[1], cm[1], lam_t[1], pw_t[1], reverse=True)
    y_ssm = ssm_output(p, yf, yb, W['ssm_d'], W['ssm_glu_w'], W['ssm_glu_b'], layer)

    def seq_mixers(src, n_seq_rows):
        up = _pad_rows(seq(src, COL_POOL, COL_POOL + POOL_WIDTH, n_seq_rows))
        uc = _pad_rows(seq(src, COL_CONV, COL_CONV + 2 * CONV_WIDTH, n_seq_rows))
        yp = pool_mixer(up, W['pool_w'], W['pool_scale'], layer)
        yc = conv_mixer(uc, W['conv_dw_w'], W['conv_dw_b'], W['conv_ln_g'], W['conv_ln_b'], layer)
        return yp.reshape(-1, POOL_WIDTH), yc.reshape(-1, CONV_WIDTH)

    y_pool, y_conv = seq_mixers(p_lat, SEQ)
    if with_ctx:
        yp_c, yc_c = seq_mixers(p_ctx, CTX_LEN)
        y_pool = jnp.concatenate([y_pool, yp_c], axis=0)
        y_conv = jnp.concatenate([y_conv, yc_c], axis=0)

    m = merge_branches(p, o_attn, y_ssm, y_pool, y_conv,
                       W['w_up_attn'], W['w_up_ssm'], W['w_up_pool'], W['w_up_conv'], layer)
    x1, h2, logits_t = out_projection(m, xa, mod, W['norm2_g'], W['w_out'], W['router_wt'], layer)

    e_idx, wts, rank, counts = route(logits_t, W['router_b'])
    n_blocks = -(-n_rows * TOP_K // MOE_BLOCK) + N_EXPERTS
    dest_flat, blk_e, blk_slot, blk_next, n_used = _slot_plan(e_idx, rank, counts, n_blocks)
    slot_tok, slot_dst = slot_tokens(dest_flat, n_blocks * MOE_BLOCK)
    is_pad = slot_dst < 0
    slot_dst = jnp.where(is_pad, TOP_K * n_rows + jnp.cumsum(is_pad.astype(jnp.int32)) - 1, slot_dst)
    ys = experts(blk_e, blk_slot, blk_next, n_used, slot_tok, slot_dst, h2,
                 W['moe_w_gate'], W['moe_w_up'], W['moe_w_down'], layer)
    x2 = combine(x1, mod, wts.T, ys, W['final_g'], final_norm)
    if with_ctx:
        return x2
    return jnp.concatenate([x2, xa[N_LAT:]], axis=0)


def kernel(x, c, ctx, c_ctx, ada_w, ada_b, norm1_g, norm2_g, w_in, q_norm_g, k_norm_g, ssm_a_re, ssm_a_im, ssm_log_dt, ssm_b_re, ssm_b_im, ssm_c_re, ssm_c_im, ssm_d, ssm_glu_w, ssm_glu_b, pool_w, pool_scale, conv_dw_w, conv_dw_b, conv_ln_g, conv_ln_b, w_up_attn, w_up_ssm, w_up_pool, w_up_conv, w_out, router_w, router_b, moe_w_gate, moe_w_up, moe_w_down, final_g):
    W = dict(
        norm1_g=norm1_g, norm2_g=norm2_g, q_norm_g=q_norm_g, k_norm_g=k_norm_g,
        w_in=w_in.astype(BF16),
        ssm_a_re=ssm_a_re, ssm_a_im=ssm_a_im, ssm_log_dt=ssm_log_dt, ssm_b_re=ssm_b_re, ssm_b_im=ssm_b_im,
        ssm_c_re=ssm_c_re, ssm_c_im=ssm_c_im, ssm_d=ssm_d, ssm_glu_w=ssm_glu_w.astype(BF16), ssm_glu_b=ssm_glu_b,
        pool_w=pool_w.astype(BF16), pool_scale=pool_scale,
        conv_dw_w=conv_dw_w, conv_dw_b=conv_dw_b, conv_ln_g=conv_ln_g, conv_ln_b=conv_ln_b,
        w_up_attn=_col_tiles(w_up_attn), w_up_ssm=_col_tiles(w_up_ssm), w_up_pool=_col_tiles(w_up_pool),
        w_up_conv=_col_tiles(w_up_conv), w_out=w_out.astype(BF16),
        router_wt=router_w.T, router_b=router_b,
        moe_w_gate=moe_w_gate, moe_w_up=moe_w_up, moe_w_down=moe_w_down, final_g=final_g,
    )
    cc = jnp.concatenate([c, c_ctx[None, :], jnp.zeros((SUBLANES - BATCH - 1, D_MODEL), F32)], axis=0)
    mod_all = ada_modulation(cc, ada_w, ada_b)
    xa = jnp.concatenate([x.reshape(N_LAT, D_MODEL), ctx.reshape(N_CTX, D_MODEL)], axis=0)
    for layer in range(DEPTH):
        mod = mod_all[layer, :BATCH + 1].reshape(BATCH + 1, 6, D_MODEL)
        xa = _layer(layer, xa, mod, W, with_ctx=(layer < DEPTH - 1), final_norm=(layer == DEPTH - 1))
    return xa[:N_LAT].reshape(BATCH, SEQ, D_MODEL)
```

```python
import functools
import math

import jax
import jax.numpy as jnp
from jax import lax
from jax.experimental import pallas as pl
from jax.experimental.pallas import tpu as pltpu

F32 = jnp.float32
BF16 = jnp.bfloat16

D_MODEL = 2048
BATCH = 2
SEQ = 4096
DEPTH = 2
GRID_W = 64
CTX_LEN = 256
N_HEADS = 8
N_KV_HEADS = 2
HEAD_DIM = 128
ROPE_THETA = 10000.0
SSM_WIDTH = 512
SSM_GROUP = 16
SSM_GROUPS = SSM_WIDTH // SSM_GROUP
SSM_STATE = 64
POOL_WIDTH = 512
POOL_WINDOWS = (2, 4, 8, 16)
POOL_GROUP = POOL_WIDTH // len(POOL_WINDOWS)
CONV_WIDTH = 512
CONV_TAPS = 31
N_BRANCHES = 4
N_EXPERTS = 64
N_EXPERT_GROUPS = 8
EXPERTS_PER_GROUP = N_EXPERTS // N_EXPERT_GROUPS
TOP_K = 2
EXPERT_FF = 512
EPS = 1e-6

Q_W = N_HEADS * HEAD_DIM
KV_W = N_KV_HEADS * HEAD_DIM
IN_W = 2 * KV_W + SSM_WIDTH + Q_W + POOL_WIDTH + 2 * CONV_WIDTH + N_BRANCHES * D_MODEL
CTX_STATE_W = 2 * KV_W + SSM_WIDTH
COL_K, COL_V, COL_SSM, COL_Q = 0, KV_W, 2 * KV_W, 2 * KV_W + SSM_WIDTH
COL_POOL = COL_Q + Q_W
COL_CONV = COL_POOL + POOL_WIDTH
COL_GATE = COL_CONV + 2 * CONV_WIDTH

N_LAT = BATCH * SEQ
N_CTX = BATCH * CTX_LEN
S_ALL = CTX_LEN + SEQ

V7X_VMEM_BYTES = 64 * 1024 * 1024
SUBLANES = 8
LANES = 128
BF16_ROWS = 16

TM = 512
TN = 512
TN_IN = IN_W // 4
TILES_PER_BATCH = SEQ // TM
TQ = 512
KEY_PREP_T = 256
SSM_T = 256
SSM_TC = SSM_T // SUBLANES
SSM_HALF = SSM_WIDTH // 2
SSM_HSTATE = SSM_GROUPS // 2 * SSM_STATE
SEQ_T = 256
SEQ_PAD = 16
MOE_BLOCK = 128
ROUTE_T = 512


def _cparams(sem, vmem_mb):
    return pltpu.CompilerParams(dimension_semantics=sem, vmem_limit_bytes=vmem_mb * 1024 * 1024)


def _mod_id(tile):
    return jnp.minimum(tile // TILES_PER_BATCH, BATCH)


def _silu(x):
    return x * jax.nn.sigmoid(x)


def _ada_kernel(c_ref, w_ref, b_ref, o_ref):
    c = c_ref[...]
    a = _silu(c).astype(BF16)
    o_ref[0] = jnp.dot(a, w_ref[0].astype(BF16), preferred_element_type=F32) + b_ref[0]


def ada_modulation(cc, ada_w, ada_b):
    tn = 1024
    n6 = 6 * D_MODEL
    return pl.pallas_call(
        _ada_kernel,
        grid=(DEPTH, n6 // tn),
        in_specs=[
            pl.BlockSpec((SUBLANES, D_MODEL), lambda l, j: (0, 0)),
            pl.BlockSpec((1, D_MODEL, tn), lambda l, j: (l, 0, j)),
            pl.BlockSpec((1, 1, tn), lambda l, j: (l, 0, j)),
        ],
        out_specs=pl.BlockSpec((1, SUBLANES, tn), lambda l, j: (l, 0, j)),
        out_shape=jax.ShapeDtypeStruct((DEPTH, SUBLANES, n6), F32),
        compiler_params=_cparams(("arbitrary", "arbitrary"), 40),
        name="ada_modulation",
    )(cc, ada_w, ada_b.reshape(DEPTH, 1, n6))


def _inproj_kernel(x_ref, mod_ref, g_ref, w_ref, o_ref, h_ref):
    @pl.when(pl.program_id(1) == 0)
    def _():
        x = x_ref[...]
        ms = jnp.mean(x * x, axis=-1, keepdims=True)
        y = x * lax.rsqrt(ms + EPS) * g_ref[0]
        h = y * (1.0 + mod_ref[0, 1:2, :]) + mod_ref[0, 0:1, :]
        h_ref[...] = h.astype(BF16)

    o_ref[...] = jnp.dot(h_ref[...], w_ref[0], preferred_element_type=F32).astype(o_ref.dtype)


def in_projection(xa, mod, norm_g, w_in, layer, row_tile0, n_row_tiles, n_cols, tn):
    return pl.pallas_call(
        _inproj_kernel,
        grid=(n_row_tiles, n_cols // tn),
        in_specs=[
            pl.BlockSpec((TM, D_MODEL), lambda i, j: (i + row_tile0, 0)),
            pl.BlockSpec((1, 6, D_MODEL), lambda i, j: (_mod_id(i + row_tile0), 0, 0)),
            pl.BlockSpec((1, 1, D_MODEL), lambda i, j: (layer, 0, 0)),
            pl.BlockSpec((1, D_MODEL, tn), lambda i, j: (layer, 0, j)),
        ],
        out_specs=pl.BlockSpec((TM, tn), lambda i, j: (i, j)),
        out_shape=jax.ShapeDtypeStruct((n_row_tiles * TM, n_cols), BF16),
        scratch_shapes=[pltpu.VMEM((TM, D_MODEL), BF16)],
        compiler_params=_cparams(("arbitrary", "arbitrary"), 56),
        name="in_projection",
    )(xa, mod, norm_g.reshape(DEPTH, 1, D_MODEL), w_in)


def _rope_tables(n_ctx_rows):
    half = HEAD_DIM // 4
    inv_freq = ROPE_THETA ** (-jnp.arange(half, dtype=F32) / half)
    t = jnp.arange(SEQ)
    ang_r = (t // GRID_W).astype(F32)[:, None] * inv_freq[None, :]
    ang_c = (t % GRID_W).astype(F32)[:, None] * inv_freq[None, :]
    cos = jnp.concatenate([jnp.cos(ang_r), jnp.cos(ang_r), jnp.cos(ang_c), jnp.cos(ang_c)], axis=-1)
    sin = jnp.concatenate([-jnp.sin(ang_r), jnp.sin(ang_r), -jnp.sin(ang_c), jnp.sin(ang_c)], axis=-1)
    cos = jnp.concatenate([jnp.ones((n_ctx_rows, HEAD_DIM), F32), cos], axis=0)
    sin = jnp.concatenate([jnp.zeros((n_ctx_rows, HEAD_DIM), F32), sin], axis=0)
    return cos, sin


def _head_norm_rope(x, g, cos, sin):
    ms = jnp.mean(x * x, axis=-1, keepdims=True)
    y = x * lax.rsqrt(ms + EPS) * g
    lane = lax.broadcasted_iota(jnp.int32, y.shape, 1)
    first = (lane % (HEAD_DIM // 2)) < (HEAD_DIM // 4)
    partner = jnp.where(first, pltpu.roll(y, HEAD_DIM - HEAD_DIM // 4, 1), pltpu.roll(y, HEAD_DIM // 4, 1))
    return y * cos + partner * sin


def _attn_kernel(q_ref, k_ref, v_ref, cq_ref, sq_ref, ck_ref, sk_ref, gq_ref, gk_ref, o_ref, ks_ref, va_ref, *,
                 n_keys):
    @pl.when(pl.program_id(2) == 0)
    def _():
        def prep(c, carry):
            r0 = pl.multiple_of(c * KEY_PREP_T, KEY_PREP_T)
            rows = pl.ds(r0, KEY_PREP_T)
            kk = k_ref[0, rows, :].astype(F32)
            kn = _head_norm_rope(kk, gk_ref[0], ck_ref[rows, :], sk_ref[rows, :])
            ks_ref[rows, :] = kn.astype(BF16)
            return carry

        lax.fori_loop(0, n_keys // KEY_PREP_T, prep, 0)
        va_ref[:, 0:HEAD_DIM] = v_ref[0]
        va_ref[:, HEAD_DIM:2 * HEAD_DIM] = jnp.ones((n_keys, HEAD_DIM), BF16)

    k = ks_ref[...]
    va = va_ref[...]
    scale = HEAD_DIM ** -0.5 * math.log2(math.e)
    for hh in range(N_HEADS // N_KV_HEADS):
        cols = slice(hh * HEAD_DIM, (hh + 1) * HEAD_DIM)
        q = q_ref[:, cols].astype(F32)
        qn = _head_norm_rope(q, gq_ref[0], cq_ref[...], sq_ref[...]) * scale
        s = lax.dot_general(qn.astype(BF16), k, (((1,), (1,)), ((), ())), preferred_element_type=F32)
        m = jnp.max(s, axis=-1, keepdims=True)
        p = jnp.exp2(s - m)
        oa = jnp.dot(p.astype(BF16), va, preferred_element_type=F32)
        o = oa[:, 0:HEAD_DIM] / oa[:, HEAD_DIM:HEAD_DIM + 1]
        o_ref[:, cols] = o.astype(o_ref.dtype)


def attention(q_src, q_col0, q_row0, n_q, kv, cos_q, sin_q, cos_k, sin_k, q_norm_g, k_norm_g, layer):
    n_keys = kv.shape[1]
    grp_w = Q_W // N_KV_HEADS
    tq = min(TQ, n_q)
    qb = n_q // tq
    kern = functools.partial(_attn_kernel, n_keys=n_keys)
    return pl.pallas_call(
        kern,
        grid=(BATCH, N_KV_HEADS, qb),
        in_specs=[
            pl.BlockSpec((tq, grp_w), lambda b, g, i: (q_row0 // tq + b * qb + i, q_col0 // grp_w + g)),
            pl.BlockSpec((1, n_keys, HEAD_DIM), lambda b, g, i: (b, 0, g)),
            pl.BlockSpec((1, n_keys, HEAD_DIM), lambda b, g, i: (b, 0, N_KV_HEADS + g)),
            pl.BlockSpec((tq, HEAD_DIM), lambda b, g, i: (i, 0)),
            pl.BlockSpec((tq, HEAD_DIM), lambda b, g, i: (i, 0)),
            pl.BlockSpec((n_keys, HEAD_DIM), lambda b, g, i: (0, 0)),
            pl.BlockSpec((n_keys, HEAD_DIM), lambda b, g, i: (0, 0)),
            pl.BlockSpec((1, 1, HEAD_DIM), lambda b, g, i: (layer, 0, 0)),
            pl.BlockSpec((1, 1, HEAD_DIM), lambda b, g, i: (layer, 0, 0)),
        ],
        out_specs=pl.BlockSpec((tq, grp_w), lambda b, g, i: (b * qb + i, g)),
        out_shape=jax.ShapeDtypeStruct((BATCH * n_q, Q_W), BF16),
        scratch_shapes=[pltpu.VMEM((n_keys, HEAD_DIM), BF16), pltpu.VMEM((n_keys, 2 * HEAD_DIM), BF16)],
        compiler_params=_cparams(("arbitrary", "arbitrary", "arbitrary"), 56),
        name="attention",
    )(q_src, kv, kv, cos_q, sin_q, cos_k, sin_k,
      q_norm_g.reshape(DEPTH, 1, HEAD_DIM), k_norm_g.reshape(DEPTH, 1, HEAD_DIM))


def _cmul(ar, ai, br, bi):
    return ar * br - ai * bi, ar * bi + ai * br


SSM_CB = 512


def _ssm_kernel(u_ref, bm_ref, cm_ref, lam_ref, pw_ref, y_ref, bu_ref, yv_ref, carry_ref, *, reverse):
    @pl.when(pl.program_id(1) == 0)
    def _():
        carry_ref[...] = jnp.zeros_like(carry_ref)

    rr = lax.broadcasted_iota(jnp.int32, (SSM_T, SSM_T), 0)
    tt = lax.broadcasted_iota(jnp.int32, (SSM_T, SSM_T), 1)
    perm = jnp.where(tt == (rr % SUBLANES) * SSM_TC + rr // SUBLANES, 1.0, 0.0).astype(BF16)
    u_scan = jnp.dot(perm, u_ref[0], preferred_element_type=F32).astype(BF16)
    for h in range(2):
        bu_ref[h] = jnp.dot(u_scan[:, h * SSM_HALF:(h + 1) * SSM_HALF], bm_ref[h], preferred_element_type=F32)

    last_row = 0 if reverse else (SSM_T - SUBLANES)
    edge = (SUBLANES - 1) if reverse else 0
    toward = (lambda x, k: pltpu.roll(x, SUBLANES - k, 0)) if reverse else (lambda x, k: pltpu.roll(x, k, 0))

    def step_rows(j):
        jj = (SSM_TC - 1 - j) if reverse else j
        return pl.ds(pl.multiple_of(jj * SUBLANES, SUBLANES), SUBLANES)

    for h, cb in [(h, cb) for h in range(2) for cb in range(SSM_HSTATE // SSM_CB)]:
        re = slice(cb * SSM_CB, (cb + 1) * SSM_CB)
        im = slice(SSM_HSTATE + cb * SSM_CB, SSM_HSTATE + (cb + 1) * SSM_CB)
        lr = lam_ref[h, :, re]
        li = lam_ref[h, :, im]

        def local_scan(j, st, h=h, re=re, im=im, lr=lr, li=li):
            rows = step_rows(j)
            pr, pi = _cmul(lr, li, st[0], st[1])
            nr = pr + bu_ref[h, rows, re]
            ni = pi + bu_ref[h, rows, im]
            bu_ref[h, rows, re] = nr
            bu_ref[h, rows, im] = ni
            return nr, ni

        zero = jnp.zeros((SUBLANES, SSM_CB), F32)
        lax.fori_loop(0, SSM_TC, local_scan, (zero, zero), unroll=4)

        er = bu_ref[h, last_row:last_row + SUBLANES, re]
        ei = bu_ref[h, last_row:last_row + SUBLANES, im]
        row = lax.broadcasted_iota(jnp.int32, (SUBLANES, SSM_CB), 0)
        dist = (SUBLANES - 1 - row) if reverse else row
        xr = jnp.where(row == edge, carry_ref[h, :, re], toward(er, 1))
        xi = jnp.where(row == edge, carry_ref[h, :, im], toward(ei, 1))
        for lvl, k in enumerate((1, 2, 4)):
            ar = pw_ref[h, lvl, :, re]
            ai = pw_ref[h, lvl, :, im]
            sr = jnp.where(dist >= k, toward(xr, k), 0.0)
            si = jnp.where(dist >= k, toward(xi, k), 0.0)
            mr, mi = _cmul(ar, ai, sr, si)
            xr = xr + mr
            xi = xi + mi
        cr, ci = _cmul(pw_ref[h, 0, :, re], pw_ref[h, 0, :, im], xr, xi)
        carry_ref[h, :, re] = toward(cr + er, 1)
        carry_ref[h, :, im] = toward(ci + ei, 1)

        def add_carry(j, g, h=h, re=re, im=im, lr=lr, li=li):
            rows = step_rows(j)
            bu_ref[h, rows, re] = bu_ref[h, rows, re] + g[0]
            bu_ref[h, rows, im] = bu_ref[h, rows, im] + g[1]
            return _cmul(lr, li, g[0], g[1])

        lax.fori_loop(0, SSM_TC, add_carry, _cmul(lr, li, xr, xi), unroll=4)

    for h in range(2):
        y = jnp.dot(bu_ref[h].astype(BF16), cm_ref[h], preferred_element_type=F32)
        for cb in range(SSM_HALF // LANES):
            cols = slice(cb * LANES, (cb + 1) * LANES)
            out_cols = slice(h * SSM_HALF + cb * LANES, h * SSM_HALF + (cb + 1) * LANES)
            slab = h * (SSM_HALF // LANES) + cb
            yv_ref[slab] = y[:, cols]
            for s in range(SUBLANES):
                y_ref[0, s * SSM_TC:(s + 1) * SSM_TC, out_cols] = yv_ref[slab, pl.ds(s, SSM_TC, stride=SUBLANES), :]


def _ssm_params(a_re, a_im, log_dt, b_re, b_im, c_re, c_im):
    lam = lax.complex(a_re.astype(F32), a_im.astype(F32))
    dt = jnp.exp(log_dt.astype(F32))[..., None]
    lam_bar = jnp.exp(lam * dt)
    b_bar = ((lam_bar - 1.0) / lam)[..., None] * lax.complex(b_re.astype(F32), b_im.astype(F32))
    gh = SSM_GROUPS // 2
    eye = jnp.eye(gh, dtype=F32)

    def b_block(m):
        m = m.reshape(2, 2, gh, SSM_STATE, SSM_GROUP)
        return jnp.einsum('dhgpc,gk->dhgckp', m, eye).reshape(2, 2, gh * SSM_GROUP, gh * SSM_STATE)

    def c_block(m):
        m = m.reshape(2, 2, gh, SSM_GROUP, SSM_STATE)
        return jnp.einsum('dhgcp,gk->dhkpgc', m, eye).reshape(2, 2, gh * SSM_STATE, gh * SSM_GROUP)

    bm = jnp.concatenate([b_block(jnp.real(b_bar)), b_block(jnp.imag(b_bar))], axis=-1).astype(BF16)
    cm = jnp.concatenate([c_block(c_re.astype(F32)), -c_block(c_im.astype(F32))], axis=-2).astype(BF16)

    def table(z):
        z = z.reshape(2, 2, SSM_HSTATE)
        t = jnp.concatenate([jnp.real(z), jnp.imag(z)], axis=-1)
        return jnp.broadcast_to(t[:, :, None, :], (2, 2, SUBLANES, 2 * SSM_HSTATE))

    lam_t = table(lam_bar)
    pw_t = jnp.stack([table(jnp.exp(lam * dt * (SSM_TC * k))) for k in (1, 2, 4)], axis=2)
    return bm, cm, lam_t, pw_t


SSM_CHUNKS = S_ALL // SSM_T


def ssm_scan(u_chunks, bm, cm, lam_t, pw_t, reverse):
    hs2 = 2 * SSM_HSTATE

    def chunk(b, c):
        if reverse:
            c = jnp.where(c == 0, 0, SSM_CHUNKS - c)
        return b * SSM_CHUNKS + c

    kern = functools.partial(_ssm_kernel, reverse=reverse)
    return pl.pallas_call(
        kern,
        grid=(BATCH, SSM_CHUNKS),
        in_specs=[
            pl.BlockSpec((1, SSM_T, SSM_WIDTH), lambda b, c: (chunk(b, c), 0, 0)),
            pl.BlockSpec((2, SSM_HALF, hs2), lambda b, c: (0, 0, 0)),
            pl.BlockSpec((2, hs2, SSM_HALF), lambda b, c: (0, 0, 0)),
            pl.BlockSpec((2, SUBLANES, hs2), lambda b, c: (0, 0, 0)),
            pl.BlockSpec((2, 3, SUBLANES, hs2), lambda b, c: (0, 0, 0, 0)),
        ],
        out_specs=pl.BlockSpec((1, SSM_T, SSM_WIDTH), lambda b, c: (chunk(b, c), 0, 0)),
        out_shape=jax.ShapeDtypeStruct(u_chunks.shape, F32),
        scratch_shapes=[pltpu.VMEM((2, SSM_T, hs2), F32), pltpu.VMEM((SSM_WIDTH // LANES, SSM_T, LANES), F32),
                        pltpu.VMEM((2, SUBLANES, hs2), F32)],
        compiler_params=_cparams(("arbitrary",) * 2, 32),
        name="ssm_scan",
    )(u_chunks, bm, cm, lam_t, pw_t)


def _ssm_chunk_of_tile(i):
    lat_tiles = SEQ // SSM_T
    return jnp.where(i < BATCH * lat_tiles,
                     (i // lat_tiles) * SSM_CHUNKS + 1 + i % lat_tiles,
                     (i - BATCH * lat_tiles) * SSM_CHUNKS)


def _gelu_tanh(x):
    return 0.5 * x * (1.0 + jnp.tanh(math.sqrt(2.0 / math.pi) * (x + 0.044715 * (x * x * x))))


def _ssm_out_kernel(u_ref, yf_ref, yb_ref, d_ref, w_ref, b_ref, o_ref):
    y = d_ref[0] * u_ref[...].astype(F32) + yf_ref[0] + yb_ref[0]
    y = _gelu_tanh(y)
    z = jnp.dot(y.astype(BF16), w_ref[0], preferred_element_type=F32) + b_ref[0]
    o_ref[...] = (y * jax.nn.sigmoid(z)).astype(o_ref.dtype)


def ssm_output(p, yf, yb, ssm_d, glu_w, glu_b, layer):
    n = p.shape[0]
    yspec = pl.BlockSpec((1, SSM_T, SSM_WIDTH), lambda i: (_ssm_chunk_of_tile(i), 0, 0))
    return pl.pallas_call(
        _ssm_out_kernel,
        grid=(n // SSM_T,),
        in_specs=[
            pl.BlockSpec((SSM_T, SSM_WIDTH), lambda i: (i, COL_SSM // SSM_WIDTH)),
            yspec,
            yspec,
            pl.BlockSpec((1, 1, SSM_WIDTH), lambda i: (layer, 0, 0)),
            pl.BlockSpec((1, SSM_WIDTH, SSM_WIDTH), lambda i: (layer, 0, 0)),
            pl.BlockSpec((1, 1, SSM_WIDTH), lambda i: (layer, 0, 0)),
        ],
        out_specs=pl.BlockSpec((SSM_T, SSM_WIDTH), lambda i: (i, 0)),
        out_shape=jax.ShapeDtypeStruct((n, SSM_WIDTH), BF16),
        compiler_params=_cparams(("arbitrary",), 32),
        name="ssm_output",
    )(p, yf, yb, ssm_d.reshape(DEPTH, 1, SSM_WIDTH), glu_w, glu_b.reshape(DEPTH, 1, SSM_WIDTH))


def _pool_kernel(u_ref, w_ref, s_ref, o_ref, *, seq_len):
    t0 = pl.multiple_of(pl.program_id(1) * SEQ_T, SEQ_T)
    halo = u_ref[0, pl.ds(t0, SEQ_T + 2 * SEQ_PAD), :]
    centre = u_ref[0, pl.ds(t0 + SEQ_PAD, SEQ_T), :].astype(F32)
    tt = lax.broadcasted_iota(jnp.int32, (SEQ_T, SEQ_T + 2 * SEQ_PAD), 0)
    rr = lax.broadcasted_iota(jnp.int32, (SEQ_T, SEQ_T + 2 * SEQ_PAD), 1) - SEQ_PAD
    tg = t0 + lax.broadcasted_iota(jnp.int32, (SEQ_T, 1), 0)
    for gi, w in enumerate(POOL_WINDOWS):
        cols = slice(gi * POOL_GROUP, (gi + 1) * POOL_GROUP)
        band = ((rr >= tt - w // 2) & (rr <= tt + w // 2 - 1)).astype(F32).astype(BF16)
        wsum = jnp.dot(band, halo[:, cols], preferred_element_type=F32)
        cnt = jnp.minimum(tg - w // 2 + w, seq_len) - jnp.maximum(tg - w // 2, 0)
        pooled = wsum / cnt.astype(F32) - centre[:, cols]
        mixed = jnp.dot(pooled.astype(BF16), w_ref[0, gi], preferred_element_type=F32)
        o_ref[0, :, cols] = (mixed * s_ref[0, :, cols]).astype(o_ref.dtype)


def pool_mixer(u_pad, pool_w, pool_scale, layer):
    nseq, lp, _ = u_pad.shape
    seq_len = lp - 2 * SEQ_PAD
    kern = functools.partial(_pool_kernel, seq_len=seq_len)
    ng = len(POOL_WINDOWS)
    return pl.pallas_call(
        kern,
        grid=(nseq, seq_len // SEQ_T),
        in_specs=[
            pl.BlockSpec((1, lp, POOL_WIDTH), lambda s, t: (s, 0, 0)),
            pl.BlockSpec((1, ng, POOL_GROUP, POOL_GROUP), lambda s, t: (layer, 0, 0, 0)),
            pl.BlockSpec((1, 1, POOL_WIDTH), lambda s, t: (layer, 0, 0)),
        ],
        out_specs=pl.BlockSpec((1, SEQ_T, POOL_WIDTH), lambda s, t: (s, t, 0)),
        out_shape=jax.ShapeDtypeStruct((nseq, seq_len, POOL_WIDTH), BF16),
        compiler_params=_cparams(("arbitrary", "arbitrary"), 32),
        name="pool_mixer",
    )(u_pad, pool_w, pool_scale.reshape(DEPTH, 1, POOL_WIDTH))


CONV_RB = 64


def _conv_kernel(u_ref, w_ref, b_ref, g_ref, beta_ref, o_ref, glu_ref):
    t0 = pl.multiple_of(pl.program_id(1) * SEQ_T, SEQ_T)
    rows = SEQ_T + 2 * SEQ_PAD
    a = u_ref[0, pl.ds(t0, rows), 0:CONV_WIDTH].astype(F32)
    g = u_ref[0, pl.ds(t0, rows), CONV_WIDTH:2 * CONV_WIDTH].astype(F32)
    glu_ref[...] = a * jax.nn.sigmoid(g)
    off = SEQ_PAD - CONV_TAPS // 2
    for rb in range(SEQ_T // CONV_RB):
        parts = []
        for cb in range(CONV_WIDTH // LANES):
            cols = slice(cb * LANES, (cb + 1) * LANES)
            acc = jnp.zeros((CONV_RB, LANES), F32)
            for k in range(CONV_TAPS):
                acc = acc + glu_ref[rb * CONV_RB + k + off:rb * CONV_RB + k + off + CONV_RB, cols] * w_ref[0, k:k + 1, cols]
            parts.append(acc)
        y = jnp.concatenate(parts, axis=-1) + b_ref[0]
        yc = y - jnp.mean(y, axis=-1, keepdims=True)
        yn = yc * lax.rsqrt(jnp.mean(yc * yc, axis=-1, keepdims=True) + EPS)
        yn = yn * g_ref[0] + beta_ref[0]
        o_ref[0, rb * CONV_RB:(rb + 1) * CONV_RB, :] = _silu(yn).astype(o_ref.dtype)


def conv_mixer(u_pad, dw_w, dw_b, ln_g, ln_b, layer):
    nseq, lp, _ = u_pad.shape
    seq_len = lp - 2 * SEQ_PAD
    vec = lambda a: a.reshape(DEPTH, 1, CONV_WIDTH)
    vspec = pl.BlockSpec((1, 1, CONV_WIDTH), lambda s, t: (layer, 0, 0))
    return pl.pallas_call(
        _conv_kernel,
        grid=(nseq, seq_len // SEQ_T),
        in_specs=[
            pl.BlockSpec((1, lp, 2 * CONV_WIDTH), lambda s, t: (s, 0, 0)),
            pl.BlockSpec((1, CONV_TAPS, CONV_WIDTH), lambda s, t: (layer, 0, 0)),
            vspec, vspec, vspec,
        ],
        out_specs=pl.BlockSpec((1, SEQ_T, CONV_WIDTH), lambda s, t: (s, t, 0)),
        out_shape=jax.ShapeDtypeStruct((nseq, seq_len, CONV_WIDTH), BF16),
        scratch_shapes=[pltpu.VMEM((SEQ_T + 2 * SEQ_PAD, CONV_WIDTH), F32)],
        compiler_params=_cparams(("arbitrary", "arbitrary"), 40),
        name="conv_mixer",
    )(u_pad, dw_w, vec(dw_b), vec(ln_g), vec(ln_b))


def _merge_kernel(oa_ref, ys_ref, yp_ref, yc_ref, g0_ref, g1_ref, g2_ref, g3_ref,
                  wa_ref, ws_ref, wp_ref, wc_ref, o_ref):
    def branch(x_ref, w_ref, g_ref):
        up = jnp.dot(x_ref[...], w_ref[0, 0], preferred_element_type=F32)
        return jax.nn.sigmoid(g_ref[...].astype(F32)) * up

    m = (branch(oa_ref, wa_ref, g0_ref) + branch(ys_ref, ws_ref, g1_ref)
         + branch(yp_ref, wp_ref, g2_ref) + branch(yc_ref, wc_ref, g3_ref))
    o_ref[...] = m.astype(o_ref.dtype)


def merge_branches(p, o_attn, y_ssm, y_pool, y_conv, w_attn, w_ssm, w_pool, w_conv, layer):
    n = o_attn.shape[0]
    nct = D_MODEL // TN
    gate_tile0 = COL_GATE // TN

    def gate_spec(br):
        return pl.BlockSpec((TM, TN), lambda i, j: (i, gate_tile0 + br * nct + j))

    def x_spec(width):
        return pl.BlockSpec((TM, width), lambda i, j: (i, 0))

    def w_spec(width):
        return pl.BlockSpec((1, 1, width, TN), lambda i, j: (layer, j, 0, 0))

    return pl.pallas_call(
        _merge_kernel,
        grid=(n // TM, nct),
        in_specs=[x_spec(Q_W), x_spec(SSM_WIDTH), x_spec(POOL_WIDTH), x_spec(CONV_WIDTH),
                  gate_spec(0), gate_spec(1), gate_spec(2), gate_spec(3),
                  w_spec(Q_W), w_spec(SSM_WIDTH), w_spec(POOL_WIDTH), w_spec(CONV_WIDTH)],
        out_specs=pl.BlockSpec((TM, TN), lambda i, j: (i, j)),
        out_shape=jax.ShapeDtypeStruct((n, D_MODEL), BF16),
        compiler_params=_cparams(("arbitrary", "arbitrary"), 40),
        name="merge_branches",
    )(o_attn, y_ssm, y_pool, y_conv, p, p, p, p, w_attn, w_ssm, w_pool, w_conv)


def _split_bf16(x):
    hi = x.astype(BF16)
    lo = (x - hi.astype(F32)).astype(BF16)
    return hi, lo


SLAB = D_MODEL // LANES


def _store_row_slabs(ref, row0, x):
    n = x.shape[0]
    for j in range(SLAB):
        ref[pl.ds(row0 * SLAB + j, n, stride=SLAB), :] = x[:, j * LANES:(j + 1) * LANES]


def _load_row_slabs(ref, row0, n):
    return jnp.concatenate([ref[pl.ds(row0 * SLAB + j, n, stride=SLAB), :] for j in range(SLAB)], axis=-1)


def _outproj_kernel(m_ref, x_ref, mod_ref, g_ref, w_ref, rw_ref, xo_ref, h_ref, lg_ref):
    r_hi, r_lo = _split_bf16(rw_ref[...])
    nt = (((1,), (1,)), ((), ()))
    half = m_ref.shape[0] // 2
    for c in range(2):
        rows = slice(c * half, (c + 1) * half)
        mix = jnp.dot(m_ref[rows, :], w_ref[0], preferred_element_type=F32)
        x = x_ref[rows, :] + mod_ref[0, 2:3, :] * mix
        xo_ref[rows, :] = x
        ms = jnp.mean(x * x, axis=-1, keepdims=True)
        h = x * lax.rsqrt(ms + EPS) * g_ref[0]
        h = h * (1.0 + mod_ref[0, 4:5, :]) + mod_ref[0, 3:4, :]
        _store_row_slabs(h_ref, c * half, h)
        h_hi, h_lo = _split_bf16(h)
        lg_ref[:, rows] = (lax.dot_general(r_hi, h_hi, nt, preferred_element_type=F32)
                           + lax.dot_general(r_hi, h_lo, nt, preferred_element_type=F32)
                           + lax.dot_general(r_lo, h_hi, nt, preferred_element_type=F32))


def out_projection(m, xa, mod, norm2_g, w_out, router_wt, layer):
    n = m.shape[0]
    tmo = TM
    return pl.pallas_call(
        _outproj_kernel,
        grid=(n // tmo,),
        in_specs=[
            pl.BlockSpec((tmo, D_MODEL), lambda i: (i, 0)),
            pl.BlockSpec((tmo, D_MODEL), lambda i: (i, 0)),
            pl.BlockSpec((1, 6, D_MODEL), lambda i: (_mod_id(i), 0, 0)),
            pl.BlockSpec((1, 1, D_MODEL), lambda i: (layer, 0, 0)),
            pl.BlockSpec((1, D_MODEL, D_MODEL), lambda i: (layer, 0, 0)),
            pl.BlockSpec((N_EXPERTS, D_MODEL), lambda i: (0, 0)),
        ],
        out_specs=[
            pl.BlockSpec((tmo, D_MODEL), lambda i: (i, 0)),
            pl.BlockSpec((tmo * SLAB, LANES), lambda i: (i, 0)),
            pl.BlockSpec((N_EXPERTS, tmo), lambda i: (0, i)),
        ],
        out_shape=[
            jax.ShapeDtypeStruct((n, D_MODEL), F32),
            jax.ShapeDtypeStruct((n * SLAB, LANES), F32),
            jax.ShapeDtypeStruct((N_EXPERTS, n), F32),
        ],
        compiler_params=_cparams(("arbitrary",), 60),
        name="out_projection",
    )(m, xa, mod, norm2_g.reshape(DEPTH, 1, D_MODEL), w_out, router_wt)


def _first_argmax(blk, row):
    m = jnp.max(blk, axis=0, keepdims=True)
    idx = jnp.min(jnp.where(blk == m, row, EXPERTS_PER_GROUP), axis=0, keepdims=True)
    return m, idx


def _route_kernel(lg_ref, rb_ref, e_ref, w_ref, rank_ref, cnt_ref, run_ref):
    @pl.when(pl.program_id(0) == 0)
    def _():
        run_ref[...] = jnp.zeros_like(run_ref)

    t = lg_ref.shape[1]
    scores = jax.nn.sigmoid(lg_ref[...])
    sel = scores + rb_ref[...]
    row = lax.broadcasted_iota(jnp.int32, (EXPERTS_PER_GROUP, t), 0)
    neg = jnp.float32(-jnp.inf)

    best = None
    for g in range(N_EXPERT_GROUPS):
        blk = sel[g * EXPERTS_PER_GROUP:(g + 1) * EXPERTS_PER_GROUP, :]
        m1, i1 = _first_argmax(blk, row)
        m2 = jnp.max(jnp.where(row == i1, neg, blk), axis=0, keepdims=True)
        gs = m1 + m2
        if best is None:
            best, grp = gs, jnp.zeros((1, t), jnp.int32)
        else:
            better = gs > best
            best = jnp.where(better, gs, best)
            grp = jnp.where(better, g, grp)

    in_sel = jnp.zeros((EXPERTS_PER_GROUP, t), F32)
    in_sc = jnp.zeros((EXPERTS_PER_GROUP, t), F32)
    for g in range(N_EXPERT_GROUPS):
        rows = slice(g * EXPERTS_PER_GROUP, (g + 1) * EXPERTS_PER_GROUP)
        in_sel = jnp.where(grp == g, sel[rows, :], in_sel)
        in_sc = jnp.where(grp == g, scores[rows, :], in_sc)
    _, i1 = _first_argmax(in_sel, row)
    _, i2 = _first_argmax(jnp.where(row == i1, neg, in_sel), row)
    s1 = jnp.sum(jnp.where(row == i1, in_sc, 0.0), axis=0, keepdims=True)
    s2 = jnp.sum(jnp.where(row == i2, in_sc, 0.0), axis=0, keepdims=True)
    e1 = grp * EXPERTS_PER_GROUP + i1
    e2 = grp * EXPERTS_PER_GROUP + i2
    e_ref[0:1, :] = e1
    e_ref[1:2, :] = e2
    w_ref[0:1, :] = s1 / (s1 + s2)
    w_ref[1:2, :] = s2 / (s1 + s2)

    erow = lax.broadcasted_iota(jnp.int32, (N_EXPERTS, t), 0)
    oh1 = erow == e1
    oh2 = erow == e2
    cnt = jnp.where(oh1 | oh2, 1.0, 0.0)
    a = lax.broadcasted_iota(jnp.int32, (t, t), 0)
    b = lax.broadcasted_iota(jnp.int32, (t, t), 1)
    before = jnp.where(a < b, 1.0, 0.0).astype(BF16)
    excl = jnp.dot(cnt.astype(BF16), before, preferred_element_type=F32) + run_ref[...]
    rank_ref[0:1, :] = jnp.sum(jnp.where(oh1, excl, 0.0), axis=0, keepdims=True).astype(jnp.int32)
    rank_ref[1:2, :] = jnp.sum(jnp.where(oh2, excl, 0.0), axis=0, keepdims=True).astype(jnp.int32)
    run_ref[...] = run_ref[...] + jnp.sum(cnt, axis=1, keepdims=True)
    cnt_ref[...] = run_ref[...]


def route(logits_t, router_b):
    n = logits_t.shape[1]
    return pl.pallas_call(
        _route_kernel,
        grid=(n // ROUTE_T,),
        in_specs=[
            pl.BlockSpec((N_EXPERTS, ROUTE_T), lambda i: (0, i)),
            pl.BlockSpec((N_EXPERTS, 1), lambda i: (0, 0)),
        ],
        out_specs=[
            pl.BlockSpec((TOP_K, ROUTE_T), lambda i: (0, i)),
            pl.BlockSpec((TOP_K, ROUTE_T), lambda i: (0, i)),
            pl.BlockSpec((TOP_K, ROUTE_T), lambda i: (0, i)),
            pl.BlockSpec((N_EXPERTS, 1), lambda i: (0, 0)),
        ],
        out_shape=[
            jax.ShapeDtypeStruct((TOP_K, n), jnp.int32),
            jax.ShapeDtypeStruct((TOP_K, n), F32),
            jax.ShapeDtypeStruct((TOP_K, n), jnp.int32),
            jax.ShapeDtypeStruct((N_EXPERTS, 1), F32),
        ],
        scratch_shapes=[pltpu.VMEM((N_EXPERTS, 1), F32)],
        compiler_params=_cparams(("arbitrary",), 32),
        name="route",
    )(logits_t, router_b.reshape(N_EXPERTS, 1).astype(F32))


def _expert_kernel(be_ref, slot_ref, nxt_ref, nu_ref, tok_ref, dst_ref, h_hbm, wg_hbm, wu_hbm, wd_hbm, o_hbm,
                   xbuf_ref, ybuf_ref, wgf_ref, wuf_ref, wdf_ref, wgb_ref, wub_ref, wdb_ref, sem, xsem, ysem,
                   *, layer):
    b = pl.program_id(0)
    n_used = nu_ref[0]
    e = be_ref[b]
    fresh = (b == 0) | (e != be_ref[jnp.maximum(b - 1, 0)])
    live = b < n_used
    has_next = b + 1 < n_used
    cur = b % 2

    def slab(row):
        return pl.ds(pl.multiple_of(row * SLAB, SLAB), SLAB)

    def row_gathers(block, buf):
        return [pltpu.make_async_copy(h_hbm.at[slab(tok_ref[block * MOE_BLOCK + r])],
                                      xbuf_ref.at[buf, slab(r)], xsem.at[buf]) for r in range(MOE_BLOCK)]

    def row_scatters(block, buf):
        return [pltpu.make_async_copy(ybuf_ref.at[buf, slab(r)],
                                      o_hbm.at[slab(dst_ref[block * MOE_BLOCK + r])], ysem.at[buf])
                for r in range(MOE_BLOCK)]

    @pl.when(b == 0)
    def _():
        for cp in row_gathers(0, 0):
            cp.start()

    def weight_copies(expert, slot):
        return (pltpu.make_async_copy(wg_hbm.at[layer, expert], wgf_ref.at[slot], sem.at[slot]),
                pltpu.make_async_copy(wu_hbm.at[layer, expert], wuf_ref.at[slot], sem.at[slot]),
                pltpu.make_async_copy(wd_hbm.at[layer, expert], wdf_ref.at[slot], sem.at[slot]))

    @pl.when(b == 0)
    def _():
        for cp in weight_copies(e, 0):
            cp.start()

    @pl.when(fresh & live)
    def _():
        slot = slot_ref[b]
        nxt = nxt_ref[b]

        @pl.when(nxt >= 0)
        def _():
            for cp in weight_copies(nxt, 1 - slot):
                cp.start()

        for cp in weight_copies(e, slot):
            cp.wait()
        wgb_ref[...] = wgf_ref[slot].astype(BF16)
        wub_ref[...] = wuf_ref[slot].astype(BF16)
        wdb_ref[...] = wdf_ref[slot].astype(BF16)

    @pl.when(live & (b >= 2))
    def _():
        for cp in row_scatters(b - 2, cur):
            cp.wait()

    def block(scatter_prev, gather_next):
        for cp in row_gathers(b, cur):
            cp.wait()
        if scatter_prev:
            for cp in row_scatters(b - 1, 1 - cur):
                cp.start()
        if gather_next:
            for cp in row_gathers(b + 1, 1 - cur):
                cp.start()
        x = _load_row_slabs(xbuf_ref.at[cur], 0, MOE_BLOCK).astype(BF16)
        gate = jnp.dot(x, wgb_ref[...], preferred_element_type=F32)
        up = jnp.dot(x, wub_ref[...], preferred_element_type=F32)
        hid = (_silu(gate) * up).astype(BF16)
        _store_row_slabs(ybuf_ref.at[cur], 0, jnp.dot(hid, wdb_ref[...], preferred_element_type=F32))

    @pl.when(b == 0)
    def _():
        block(False, True)

    @pl.when(live & (b >= 1) & has_next)
    def _():
        block(True, True)

    @pl.when(live & (b >= 1) & jnp.logical_not(has_next))
    def _():
        block(True, False)

    @pl.when(b == n_used)
    def _():
        last = n_used - 1
        for cp in row_scatters(last, last % 2):
            cp.start()
        for cp in row_scatters(last - 1, 1 - last % 2):
            cp.wait()
        for cp in row_scatters(last, last % 2):
            cp.wait()

    @pl.when(jnp.logical_not(live))
    def _():
        ybuf_ref[0] = jnp.zeros((MOE_BLOCK * SLAB, LANES), F32)
        rows = pl.ds(pl.multiple_of(b * (MOE_BLOCK * SLAB), MOE_BLOCK * SLAB), MOE_BLOCK * SLAB)
        cp = pltpu.make_async_copy(ybuf_ref.at[0], o_hbm.at[rows], ysem.at[0])
        cp.start()
        cp.wait()


def _slot_maps(dest_flat, n, n_slots):
    i32 = jnp.int32
    owner = jnp.full((n_slots,), -1, i32).at[dest_flat].set(jnp.arange(TOP_K * n, dtype=i32), unique_indices=True)
    is_pad = owner < 0
    tok = jnp.where(is_pad, 0, owner % n)
    dst = jnp.where(is_pad, TOP_K * n + jnp.cumsum(is_pad.astype(i32)) - 1, owner)
    return tok.astype(i32), dst.astype(i32)


def experts(blk_e, blk_slot, blk_next, n_used, slot_tok, slot_dst, h2, w_gate, w_up, w_down, layer):
    n = h2.shape[0] // SLAB
    n_slots = slot_tok.shape[0]
    n_blocks = n_slots // MOE_BLOCK
    assert n_blocks - N_EXPERTS >= 2
    assert n_slots == TOP_K * n + N_EXPERTS * MOE_BLOCK
    grid_spec = pltpu.PrefetchScalarGridSpec(
        num_scalar_prefetch=6,
        grid=(n_blocks,),
        in_specs=[
            pl.BlockSpec(memory_space=pl.ANY),
            pl.BlockSpec(memory_space=pl.ANY),
            pl.BlockSpec(memory_space=pl.ANY),
            pl.BlockSpec(memory_space=pl.ANY),
        ],
        out_specs=pl.BlockSpec(memory_space=pl.ANY),
        scratch_shapes=[
            pltpu.VMEM((2, MOE_BLOCK * SLAB, LANES), F32),
            pltpu.VMEM((2, MOE_BLOCK * SLAB, LANES), F32),
            pltpu.VMEM((2, D_MODEL, EXPERT_FF), F32),
            pltpu.VMEM((2, D_MODEL, EXPERT_FF), F32),
            pltpu.VMEM((2, EXPERT_FF, D_MODEL), F32),
            pltpu.VMEM((D_MODEL, EXPERT_FF), BF16),
            pltpu.VMEM((D_MODEL, EXPERT_FF), BF16),
            pltpu.VMEM((EXPERT_FF, D_MODEL), BF16),
            pltpu.SemaphoreType.DMA((2,)),
            pltpu.SemaphoreType.DMA((2,)),
            pltpu.SemaphoreType.DMA((2,)),
        ],
    )
    return pl.pallas_call(
        functools.partial(_expert_kernel, layer=layer),
        grid_spec=grid_spec,
        out_shape=jax.ShapeDtypeStruct((n_slots * SLAB, LANES), F32),
        compiler_params=_cparams(("arbitrary",), 56),
        name="experts",
    )(blk_e, blk_slot, blk_next, n_used, slot_tok, slot_dst, h2, w_gate, w_up, w_down)


def _combine_kernel(x_ref, mod_ref, w_ref, fg_ref, y0_ref, y1_ref, o_ref, *, final_norm):
    f = w_ref[:, 0:1] * _load_row_slabs(y0_ref, 0, TM) + w_ref[:, 1:2] * _load_row_slabs(y1_ref, 0, TM)
    x = x_ref[...] + mod_ref[0, 5:6, :] * f
    if final_norm:
        ms = jnp.mean(x * x, axis=-1, keepdims=True)
        x = x * lax.rsqrt(ms + EPS) * fg_ref[...]
    o_ref[...] = x


def combine(xa, mod, wts_rows, ys, final_g, final_norm):
    n = xa.shape[0]
    n_tiles = n // TM
    kern = functools.partial(_combine_kernel, final_norm=final_norm)
    return pl.pallas_call(
        kern,
        grid=(n_tiles,),
        in_specs=[
            pl.BlockSpec((TM, D_MODEL), lambda i: (i, 0)),
            pl.BlockSpec((1, 6, D_MODEL), lambda i: (_mod_id(i), 0, 0)),
            pl.BlockSpec((TM, TOP_K), lambda i: (i, 0)),
            pl.BlockSpec((1, D_MODEL), lambda i: (0, 0)),
            pl.BlockSpec((TM * SLAB, LANES), lambda i: (i, 0)),
            pl.BlockSpec((TM * SLAB, LANES), lambda i: (n_tiles + i, 0)),
        ],
        out_specs=pl.BlockSpec((TM, D_MODEL), lambda i: (i, 0)),
        out_shape=jax.ShapeDtypeStruct((n, D_MODEL), F32),
        compiler_params=_cparams(("arbitrary",), 48),
        name="combine",
    )(xa, mod, wts_rows, final_g.reshape(1, D_MODEL), ys, ys)


def _slot_plan(e_idx, rank, counts, n_blocks):
    i32 = jnp.int32
    cnt = counts.reshape(N_EXPERTS).astype(i32)
    padded = (cnt + MOE_BLOCK - 1) // MOE_BLOCK * MOE_BLOCK
    pad_end = jnp.cumsum(padded)
    pad_start = pad_end - padded
    eids = jnp.arange(N_EXPERTS, dtype=i32)

    def lookup(table, idx):
        return jnp.sum(jnp.where(idx[..., None] == eids, table, 0), axis=-1).astype(i32)

    dest = lookup(pad_start, e_idx) + rank
    blk_start = jnp.arange(n_blocks, dtype=i32) * MOE_BLOCK
    blk_e = jnp.minimum(jnp.sum(pad_end[None, :] <= blk_start[:, None], axis=1), N_EXPERTS - 1).astype(i32)
    n_used = (pad_end[-1] // MOE_BLOCK).astype(i32).reshape(1)
    has = cnt > 0
    run = jnp.cumsum(has.astype(i32)) - 1
    first_from = lax.cummin(jnp.where(has, eids, N_EXPERTS)[::-1])[::-1]
    nxt = jnp.concatenate([first_from[1:], jnp.full((1,), N_EXPERTS, i32)])
    nxt = jnp.where(nxt >= N_EXPERTS, -1, nxt)
    return dest.reshape(-1).astype(i32), blk_e, lookup(run % 2, blk_e), lookup(nxt, blk_e), n_used


def _col_tiles(w):
    d, k, n = w.shape
    return w.astype(BF16).reshape(d, k, n // TN, TN).transpose(0, 2, 1, 3)


def _pad_rows(u):
    return jnp.pad(u, ((0, 0), (SEQ_PAD, SEQ_PAD), (0, 0)))


def _layer(layer, xa, mod, W, with_ctx, final_norm):
    n_lat_tiles = N_LAT // TM
    n_ctx_tiles = N_CTX // TM
    if with_ctx:
        p = in_projection(xa, mod, W['norm1_g'], W['w_in'], layer, 0, n_lat_tiles + n_ctx_tiles, IN_W, TN_IN)
        p_lat, p_ctx = p[:N_LAT], p[N_LAT:]
    else:
        p = in_projection(xa, mod, W['norm1_g'], W['w_in'], layer, 0, n_lat_tiles, IN_W, TN_IN)
        p_lat = p
        p_ctx = in_projection(xa, mod, W['norm1_g'], W['w_in'], layer, n_lat_tiles, n_ctx_tiles, CTX_STATE_W, TN)
    n_rows = p.shape[0]

    def seq(a, lo, hi, n_seq_rows):
        return a[:, lo:hi].reshape(BATCH, n_seq_rows, hi - lo)

    kv = jnp.concatenate([seq(p_ctx, 0, 2 * KV_W, CTX_LEN), seq(p_lat, 0, 2 * KV_W, SEQ)], axis=1)
    cos, sin = _rope_tables(CTX_LEN)
    o_attn = attention(p, COL_Q, 0, SEQ, kv, cos[CTX_LEN:], sin[CTX_LEN:], cos, sin,
                       W['q_norm_g'], W['k_norm_g'], layer)
    if with_ctx:
        kv_c = seq(p_ctx, 0, 2 * KV_W, CTX_LEN)
        ones, zeros = jnp.ones((CTX_LEN, HEAD_DIM), F32), jnp.zeros((CTX_LEN, HEAD_DIM), F32)
        o_attn_c = attention(p, COL_Q, N_LAT, CTX_LEN, kv_c, ones, zeros, ones, zeros,
                             W['q_norm_g'], W['k_norm_g'], layer)
        o_attn = jnp.concatenate([o_attn, o_attn_c], axis=0)

    u_seq = jnp.concatenate([seq(p_ctx, COL_SSM, COL_SSM + SSM_WIDTH, CTX_LEN),
                             seq(p_lat, COL_SSM, COL_SSM + SSM_WIDTH, SEQ)], axis=1)
    u_chunks = u_seq.reshape(BATCH * SSM_CHUNKS, SSM_T, SSM_WIDTH)
    bm, cm, lam_t, pw_t = _ssm_params(W['ssm_a_re'][layer], W['ssm_a_im'][layer], W['ssm_log_dt'][layer],
                                      W['ssm_b_re'][layer], W['ssm_b_im'][layer],
                                      W['ssm_c_re'][layer], W['ssm_c_im'][layer])
    yf = ssm_scan(u_chunks, bm[0], cm[0], lam_t[0], pw_t[0], reverse=False)
    yb = ssm_scan(u_chunks, bm[1], cm[1], lam_t[1], pw_t[1], reverse=True)
    y_ssm = ssm_output(p, yf, yb, W['ssm_d'], W['ssm_glu_w'], W['ssm_glu_b'], layer)

    def seq_mixers(src, n_seq_rows):
        up = _pad_rows(seq(src, COL_POOL, COL_POOL + POOL_WIDTH, n_seq_rows))
        uc = _pad_rows(seq(src, COL_CONV, COL_CONV + 2 * CONV_WIDTH, n_seq_rows))
        yp = pool_mixer(up, W['pool_w'], W['pool_scale'], layer)
        yc = conv_mixer(uc, W['conv_dw_w'], W['conv_dw_b'], W['conv_ln_g'], W['conv_ln_b'], layer)
        return yp.reshape(-1, POOL_WIDTH), yc.reshape(-1, CONV_WIDTH)

    y_pool, y_conv = seq_mixers(p_lat, SEQ)
    if with_ctx:
        yp_c, yc_c = seq_mixers(p_ctx, CTX_LEN)
        y_pool = jnp.concatenate([y_pool, yp_c], axis=0)
        y_conv = jnp.concatenate([y_conv, yc_c], axis=0)

    m = merge_branches(p, o_attn, y_ssm, y_pool, y_conv,
                       W['w_up_attn'], W['w_up_ssm'], W['w_up_pool'], W['w_up_conv'], layer)
    x1, h2, logits_t = out_projection(m, xa, mod, W['norm2_g'], W['w_out'], W['router_wt'], layer)

    e_idx, wts, rank, counts = route(logits_t, W['router_b'])
    n_blocks = -(-n_rows * TOP_K // MOE_BLOCK) + N_EXPERTS
    dest_flat, blk_e, blk_slot, blk_next, n_used = _slot_plan(e_idx, rank, counts, n_blocks)
    slot_tok, slot_dst = _slot_maps(dest_flat, n_rows, n_blocks * MOE_BLOCK)
    ys = experts(blk_e, blk_slot, blk_next, n_used, slot_tok, slot_dst, h2,
                 W['moe_w_gate'], W['moe_w_up'], W['moe_w_down'], layer)
    x2 = combine(x1, mod, wts.T, ys, W['final_g'], final_norm)
    if with_ctx:
        return x2
    return jnp.concatenate([x2, xa[N_LAT:]], axis=0)


def kernel(x, c, ctx, c_ctx, ada_w, ada_b, norm1_g, norm2_g, w_in, q_norm_g, k_norm_g, ssm_a_re, ssm_a_im, ssm_log_dt, ssm_b_re, ssm_b_im, ssm_c_re, ssm_c_im, ssm_d, ssm_glu_w, ssm_glu_b, pool_w, pool_scale, conv_dw_w, conv_dw_b, conv_ln_g, conv_ln_b, w_up_attn, w_up_ssm, w_up_pool, w_up_conv, w_out, router_w, router_b, moe_w_gate, moe_w_up, moe_w_down, final_g):
    W = dict(
        norm1_g=norm1_g, norm2_g=norm2_g, q_norm_g=q_norm_g, k_norm_g=k_norm_g,
        w_in=w_in.astype(BF16),
        ssm_a_re=ssm_a_re, ssm_a_im=ssm_a_im, ssm_log_dt=ssm_log_dt, ssm_b_re=ssm_b_re, ssm_b_im=ssm_b_im,
        ssm_c_re=ssm_c_re, ssm_c_im=ssm_c_im, ssm_d=ssm_d, ssm_glu_w=ssm_glu_w.astype(BF16), ssm_glu_b=ssm_glu_b,
        pool_w=pool_w.astype(BF16), pool_scale=pool_scale,
        conv_dw_w=conv_dw_w, conv_dw_b=conv_dw_b, conv_ln_g=conv_ln_g, conv_ln_b=conv_ln_b,
        w_up_attn=_col_tiles(w_up_attn), w_up_ssm=_col_tiles(w_up_ssm), w_up_pool=_col_tiles(w_up_pool),
        w_up_conv=_col_tiles(w_up_conv), w_out=w_out.astype(BF16),
        router_wt=router_w.T, router_b=router_b,
        moe_w_gate=moe_w_gate, moe_w_up=moe_w_up, moe_w_down=moe_w_down, final_g=final_g,
    )
    cc = jnp.concatenate([c, c_ctx[None, :], jnp.zeros((SUBLANES - BATCH - 1, D_MODEL), F32)], axis=0)
    mod_all = ada_modulation(cc, ada_w, ada_b)
    xa = jnp.concatenate([x.reshape(N_LAT, D_MODEL), ctx.reshape(N_CTX, D_MODEL)], axis=0)
    for layer in range(DEPTH):
        mod = mod_all[layer, :BATCH + 1].reshape(BATCH + 1, 6, D_MODEL)
        xa = _layer(layer, xa, mod, W, with_ctx=(layer < DEPTH - 1), final_norm=(layer == DEPTH - 1))
    return xa[:N_LAT].reshape(BATCH, SEQ, D_MODEL)
```

```python
import functools
import math

import jax
import jax.numpy as jnp
from jax import lax
from jax.experimental import pallas as pl
from jax.experimental.pallas import tpu as pltpu

F32 = jnp.float32
BF16 = jnp.bfloat16

D_MODEL = 2048
BATCH = 2
SEQ = 4096
DEPTH = 2
GRID_W = 64
CTX_LEN = 256
N_HEADS = 8
N_KV_HEADS = 2
HEAD_DIM = 128
ROPE_THETA = 10000.0
SSM_WIDTH = 512
SSM_GROUP = 16
SSM_GROUPS = SSM_WIDTH // SSM_GROUP
SSM_STATE = 64
POOL_WIDTH = 512
POOL_WINDOWS = (2, 4, 8, 16)
POOL_GROUP = POOL_WIDTH // len(POOL_WINDOWS)
CONV_WIDTH = 512
CONV_TAPS = 31
N_BRANCHES = 4
N_EXPERTS = 64
N_EXPERT_GROUPS = 8
EXPERTS_PER_GROUP = N_EXPERTS // N_EXPERT_GROUPS
TOP_K = 2
EXPERT_FF = 512
EPS = 1e-6

Q_W = N_HEADS * HEAD_DIM
KV_W = N_KV_HEADS * HEAD_DIM
IN_W = 2 * KV_W + SSM_WIDTH + Q_W + POOL_WIDTH + 2 * CONV_WIDTH + N_BRANCHES * D_MODEL
CTX_STATE_W = 2 * KV_W + SSM_WIDTH
COL_K, COL_V, COL_SSM, COL_Q = 0, KV_W, 2 * KV_W, 2 * KV_W + SSM_WIDTH
COL_POOL = COL_Q + Q_W
COL_CONV = COL_POOL + POOL_WIDTH
COL_GATE = COL_CONV + 2 * CONV_WIDTH

N_LAT = BATCH * SEQ
N_CTX = BATCH * CTX_LEN
S_ALL = CTX_LEN + SEQ

V7X_VMEM_BYTES = 64 * 1024 * 1024
SUBLANES = 8
LANES = 128
BF16_ROWS = 16

TM = 512
TN = 512
TN_IN = IN_W // 4
TILES_PER_BATCH = SEQ // TM
TQ = 512
KEY_PREP_T = 256
SSM_T = 256
SSM_TC = SSM_T // SUBLANES
SSM_HALF = SSM_WIDTH // 2
SSM_HSTATE = SSM_GROUPS // 2 * SSM_STATE
SEQ_T = 256
SEQ_PAD = 16
MOE_BLOCK = 128
WEIGHT_DMA_PRIORITY = 1
ROUTE_T = 512


def _cparams(sem, vmem_mb):
    return pltpu.CompilerParams(dimension_semantics=sem, vmem_limit_bytes=vmem_mb * 1024 * 1024)


def _mod_id(tile):
    return jnp.minimum(tile // TILES_PER_BATCH, BATCH)


def _silu(x):
    return x * jax.nn.sigmoid(x)


def _ada_kernel(c_ref, w_ref, b_ref, o_ref):
    c = c_ref[...]
    a = _silu(c).astype(BF16)
    o_ref[0] = jnp.dot(a, w_ref[0].astype(BF16), preferred_element_type=F32) + b_ref[0]


def ada_modulation(cc, ada_w, ada_b):
    tn = 1024
    n6 = 6 * D_MODEL
    return pl.pallas_call(
        _ada_kernel,
        grid=(DEPTH, n6 // tn),
        in_specs=[
            pl.BlockSpec((SUBLANES, D_MODEL), lambda l, j: (0, 0)),
            pl.BlockSpec((1, D_MODEL, tn), lambda l, j: (l, 0, j)),
            pl.BlockSpec((1, 1, tn), lambda l, j: (l, 0, j)),
        ],
        out_specs=pl.BlockSpec((1, SUBLANES, tn), lambda l, j: (l, 0, j)),
        out_shape=jax.ShapeDtypeStruct((DEPTH, SUBLANES, n6), F32),
        compiler_params=_cparams(("arbitrary", "arbitrary"), 40),
        name="ada_modulation",
    )(cc, ada_w, ada_b.reshape(DEPTH, 1, n6))


def _inproj_kernel(x_ref, mod_ref, g_ref, w_ref, o_ref, h_ref):
    @pl.when(pl.program_id(1) == 0)
    def _():
        x = x_ref[...]
        ms = jnp.mean(x * x, axis=-1, keepdims=True)
        y = x * lax.rsqrt(ms + EPS) * g_ref[0]
        h = y * (1.0 + mod_ref[0, 1:2, :]) + mod_ref[0, 0:1, :]
        h_ref[...] = h.astype(BF16)

    o_ref[...] = jnp.dot(h_ref[...], w_ref[0], preferred_element_type=F32).astype(o_ref.dtype)


def in_projection(xa, mod, norm_g, w_in, layer, row_tile0, n_row_tiles, n_cols, tn):
    return pl.pallas_call(
        _inproj_kernel,
        grid=(n_row_tiles, n_cols // tn),
        in_specs=[
            pl.BlockSpec((TM, D_MODEL), lambda i, j: (i + row_tile0, 0)),
            pl.BlockSpec((1, 6, D_MODEL), lambda i, j: (_mod_id(i + row_tile0), 0, 0)),
            pl.BlockSpec((1, 1, D_MODEL), lambda i, j: (layer, 0, 0)),
            pl.BlockSpec((1, D_MODEL, tn), lambda i, j: (layer, 0, j)),
        ],
        out_specs=pl.BlockSpec((TM, tn), lambda i, j: (i, j)),
        out_shape=jax.ShapeDtypeStruct((n_row_tiles * TM, n_cols), BF16),
        scratch_shapes=[pltpu.VMEM((TM, D_MODEL), BF16)],
        compiler_params=_cparams(("arbitrary", "arbitrary"), 56),
        name="in_projection",
    )(xa, mod, norm_g.reshape(DEPTH, 1, D_MODEL), w_in)


def _rope_tables(n_ctx_rows):
    half = HEAD_DIM // 4
    inv_freq = ROPE_THETA ** (-jnp.arange(half, dtype=F32) / half)
    t = jnp.arange(SEQ)
    ang_r = (t // GRID_W).astype(F32)[:, None] * inv_freq[None, :]
    ang_c = (t % GRID_W).astype(F32)[:, None] * inv_freq[None, :]
    cos = jnp.concatenate([jnp.cos(ang_r), jnp.cos(ang_r), jnp.cos(ang_c), jnp.cos(ang_c)], axis=-1)
    sin = jnp.concatenate([-jnp.sin(ang_r), jnp.sin(ang_r), -jnp.sin(ang_c), jnp.sin(ang_c)], axis=-1)
    cos = jnp.concatenate([jnp.ones((n_ctx_rows, HEAD_DIM), F32), cos], axis=0)
    sin = jnp.concatenate([jnp.zeros((n_ctx_rows, HEAD_DIM), F32), sin], axis=0)
    return cos, sin


def _head_norm_rope(x, g, cos, sin):
    ms = jnp.mean(x * x, axis=-1, keepdims=True)
    y = x * lax.rsqrt(ms + EPS) * g
    lane = lax.broadcasted_iota(jnp.int32, y.shape, 1)
    first = (lane % (HEAD_DIM // 2)) < (HEAD_DIM // 4)
    partner = jnp.where(first, pltpu.roll(y, HEAD_DIM - HEAD_DIM // 4, 1), pltpu.roll(y, HEAD_DIM // 4, 1))
    return y * cos + partner * sin


def _attn_kernel(q_ref, k_ref, v_ref, cq_ref, sq_ref, ck_ref, sk_ref, gq_ref, gk_ref, o_ref, ks_ref, va_ref, *,
                 n_keys):
    @pl.when(pl.program_id(2) == 0)
    def _():
        def prep(c, carry):
            r0 = pl.multiple_of(c * KEY_PREP_T, KEY_PREP_T)
            rows = pl.ds(r0, KEY_PREP_T)
            kk = k_ref[0, rows, :].astype(F32)
            kn = _head_norm_rope(kk, gk_ref[0], ck_ref[rows, :], sk_ref[rows, :])
            ks_ref[rows, :] = kn.astype(BF16)
            return carry

        lax.fori_loop(0, n_keys // KEY_PREP_T, prep, 0)
        va_ref[:, 0:HEAD_DIM] = v_ref[0]
        va_ref[:, HEAD_DIM:2 * HEAD_DIM] = jnp.ones((n_keys, HEAD_DIM), BF16)

    k = ks_ref[...]
    va = va_ref[...]
    scale = HEAD_DIM ** -0.5 * math.log2(math.e)
    for hh in range(N_HEADS // N_KV_HEADS):
        cols = slice(hh * HEAD_DIM, (hh + 1) * HEAD_DIM)
        q = q_ref[:, cols].astype(F32)
        qn = _head_norm_rope(q, gq_ref[0], cq_ref[...], sq_ref[...]) * scale
        s = lax.dot_general(qn.astype(BF16), k, (((1,), (1,)), ((), ())), preferred_element_type=F32)
        m = jnp.max(s, axis=-1, keepdims=True)
        p = jnp.exp2(s - m)
        oa = jnp.dot(p.astype(BF16), va, preferred_element_type=F32)
        o = oa[:, 0:HEAD_DIM] / oa[:, HEAD_DIM:HEAD_DIM + 1]
        o_ref[:, cols] = o.astype(o_ref.dtype)


def attention(q_src, q_col0, q_row0, n_q, kv, cos_q, sin_q, cos_k, sin_k, q_norm_g, k_norm_g, layer):
    n_keys = kv.shape[1]
    grp_w = Q_W // N_KV_HEADS
    tq = min(TQ, n_q)
    qb = n_q // tq
    kern = functools.partial(_attn_kernel, n_keys=n_keys)
    return pl.pallas_call(
        kern,
        grid=(BATCH, N_KV_HEADS, qb),
        in_specs=[
            pl.BlockSpec((tq, grp_w), lambda b, g, i: (q_row0 // tq + b * qb + i, q_col0 // grp_w + g)),
            pl.BlockSpec((1, n_keys, HEAD_DIM), lambda b, g, i: (b, 0, g)),
            pl.BlockSpec((1, n_keys, HEAD_DIM), lambda b, g, i: (b, 0, N_KV_HEADS + g)),
            pl.BlockSpec((tq, HEAD_DIM), lambda b, g, i: (i, 0)),
            pl.BlockSpec((tq, HEAD_DIM), lambda b, g, i: (i, 0)),
            pl.BlockSpec((n_keys, HEAD_DIM), lambda b, g, i: (0, 0)),
            pl.BlockSpec((n_keys, HEAD_DIM), lambda b, g, i: (0, 0)),
            pl.BlockSpec((1, 1, HEAD_DIM), lambda b, g, i: (layer, 0, 0)),
            pl.BlockSpec((1, 1, HEAD_DIM), lambda b, g, i: (layer, 0, 0)),
        ],
        out_specs=pl.BlockSpec((tq, grp_w), lambda b, g, i: (b * qb + i, g)),
        out_shape=jax.ShapeDtypeStruct((BATCH * n_q, Q_W), BF16),
        scratch_shapes=[pltpu.VMEM((n_keys, HEAD_DIM), BF16), pltpu.VMEM((n_keys, 2 * HEAD_DIM), BF16)],
        compiler_params=_cparams(("arbitrary", "arbitrary", "arbitrary"), 56),
        name="attention",
    )(q_src, kv, kv, cos_q, sin_q, cos_k, sin_k,
      q_norm_g.reshape(DEPTH, 1, HEAD_DIM), k_norm_g.reshape(DEPTH, 1, HEAD_DIM))


def _cmul(ar, ai, br, bi):
    return ar * br - ai * bi, ar * bi + ai * br


SSM_CB = 512


def _ssm_kernel(u_ref, bm_ref, cm_ref, lam_ref, pw_ref, y_ref, bu_ref, yv_ref, carry_ref, *, reverse):
    @pl.when(pl.program_id(1) == 0)
    def _():
        carry_ref[...] = jnp.zeros_like(carry_ref)

    rr = lax.broadcasted_iota(jnp.int32, (SSM_T, SSM_T), 0)
    tt = lax.broadcasted_iota(jnp.int32, (SSM_T, SSM_T), 1)
    perm = jnp.where(tt == (rr % SUBLANES) * SSM_TC + rr // SUBLANES, 1.0, 0.0).astype(BF16)
    u_scan = jnp.dot(perm, u_ref[0], preferred_element_type=F32).astype(BF16)
    for h in range(2):
        bu_ref[h] = jnp.dot(u_scan[:, h * SSM_HALF:(h + 1) * SSM_HALF], bm_ref[h], preferred_element_type=F32)

    last_row = 0 if reverse else (SSM_T - SUBLANES)
    edge = (SUBLANES - 1) if reverse else 0
    toward = (lambda x, k: pltpu.roll(x, SUBLANES - k, 0)) if reverse else (lambda x, k: pltpu.roll(x, k, 0))

    def step_rows(j):
        jj = (SSM_TC - 1 - j) if reverse else j
        return pl.ds(pl.multiple_of(jj * SUBLANES, SUBLANES), SUBLANES)

    for h, cb in [(h, cb) for h in range(2) for cb in range(SSM_HSTATE // SSM_CB)]:
        re = slice(cb * SSM_CB, (cb + 1) * SSM_CB)
        im = slice(SSM_HSTATE + cb * SSM_CB, SSM_HSTATE + (cb + 1) * SSM_CB)
        lr = lam_ref[h, :, re]
        li = lam_ref[h, :, im]

        def local_scan(j, st, h=h, re=re, im=im, lr=lr, li=li):
            rows = step_rows(j)
            pr, pi = _cmul(lr, li, st[0], st[1])
            nr = pr + bu_ref[h, rows, re]
            ni = pi + bu_ref[h, rows, im]
            bu_ref[h, rows, re] = nr
            bu_ref[h, rows, im] = ni
            return nr, ni

        zero = jnp.zeros((SUBLANES, SSM_CB), F32)
        lax.fori_loop(0, SSM_TC, local_scan, (zero, zero), unroll=4)

        er = bu_ref[h, last_row:last_row + SUBLANES, re]
        ei = bu_ref[h, last_row:last_row + SUBLANES, im]
        row = lax.broadcasted_iota(jnp.int32, (SUBLANES, SSM_CB), 0)
        dist = (SUBLANES - 1 - row) if reverse else row
        xr = jnp.where(row == edge, carry_ref[h, :, re], toward(er, 1))
        xi = jnp.where(row == edge, carry_ref[h, :, im], toward(ei, 1))
        for lvl, k in enumerate((1, 2, 4)):
            ar = pw_ref[h, lvl, :, re]
            ai = pw_ref[h, lvl, :, im]
            sr = jnp.where(dist >= k, toward(xr, k), 0.0)
            si = jnp.where(dist >= k, toward(xi, k), 0.0)
            mr, mi = _cmul(ar, ai, sr, si)
            xr = xr + mr
            xi = xi + mi
        cr, ci = _cmul(pw_ref[h, 0, :, re], pw_ref[h, 0, :, im], xr, xi)
        carry_ref[h, :, re] = toward(cr + er, 1)
        carry_ref[h, :, im] = toward(ci + ei, 1)

        def add_carry(j, g, h=h, re=re, im=im, lr=lr, li=li):
            rows = step_rows(j)
            bu_ref[h, rows, re] = bu_ref[h, rows, re] + g[0]
            bu_ref[h, rows, im] = bu_ref[h, rows, im] + g[1]
            return _cmul(lr, li, g[0], g[1])

        lax.fori_loop(0, SSM_TC, add_carry, _cmul(lr, li, xr, xi), unroll=4)

    for h in range(2):
        y = jnp.dot(bu_ref[h].astype(BF16), cm_ref[h], preferred_element_type=F32)
        for cb in range(SSM_HALF // LANES):
            cols = slice(cb * LANES, (cb + 1) * LANES)
            out_cols = slice(h * SSM_HALF + cb * LANES, h * SSM_HALF + (cb + 1) * LANES)
            slab = h * (SSM_HALF // LANES) + cb
            yv_ref[slab] = y[:, cols]
            for s in range(SUBLANES):
                y_ref[0, s * SSM_TC:(s + 1) * SSM_TC, out_cols] = yv_ref[slab, pl.ds(s, SSM_TC, stride=SUBLANES), :]


def _ssm_params(a_re, a_im, log_dt, b_re, b_im, c_re, c_im):
    lam = lax.complex(a_re.astype(F32), a_im.astype(F32))
    dt = jnp.exp(log_dt.astype(F32))[..., None]
    lam_bar = jnp.exp(lam * dt)
    b_bar = ((lam_bar - 1.0) / lam)[..., None] * lax.complex(b_re.astype(F32), b_im.astype(F32))
    gh = SSM_GROUPS // 2
    eye = jnp.eye(gh, dtype=F32)

    def b_block(m):
        m = m.reshape(2, 2, gh, SSM_STATE, SSM_GROUP)
        return jnp.einsum('dhgpc,gk->dhgckp', m, eye).reshape(2, 2, gh * SSM_GROUP, gh * SSM_STATE)

    def c_block(m):
        m = m.reshape(2, 2, gh, SSM_GROUP, SSM_STATE)
        return jnp.einsum('dhgcp,gk->dhkpgc', m, eye).reshape(2, 2, gh * SSM_STATE, gh * SSM_GROUP)

    bm = jnp.concatenate([b_block(jnp.real(b_bar)), b_block(jnp.imag(b_bar))], axis=-1).astype(BF16)
    cm = jnp.concatenate([c_block(c_re.astype(F32)), -c_block(c_im.astype(F32))], axis=-2).astype(BF16)

    def table(z):
        z = z.reshape(2, 2, SSM_HSTATE)
        t = jnp.concatenate([jnp.real(z), jnp.imag(z)], axis=-1)
        return jnp.broadcast_to(t[:, :, None, :], (2, 2, SUBLANES, 2 * SSM_HSTATE))

    lam_t = table(lam_bar)
    pw_t = jnp.stack([table(jnp.exp(lam * dt * (SSM_TC * k))) for k in (1, 2, 4)], axis=2)
    return bm, cm, lam_t, pw_t


SSM_CHUNKS = S_ALL // SSM_T


def ssm_scan(u_chunks, bm, cm, lam_t, pw_t, reverse):
    hs2 = 2 * SSM_HSTATE

    def chunk(b, c):
        if reverse:
            c = jnp.where(c == 0, 0, SSM_CHUNKS - c)
        return b * SSM_CHUNKS + c

    kern = functools.partial(_ssm_kernel, reverse=reverse)
    return pl.pallas_call(
        kern,
        grid=(BATCH, SSM_CHUNKS),
        in_specs=[
            pl.BlockSpec((1, SSM_T, SSM_WIDTH), lambda b, c: (chunk(b, c), 0, 0)),
            pl.BlockSpec((2, SSM_HALF, hs2), lambda b, c: (0, 0, 0)),
            pl.BlockSpec((2, hs2, SSM_HALF), lambda b, c: (0, 0, 0)),
            pl.BlockSpec((2, SUBLANES, hs2), lambda b, c: (0, 0, 0)),
            pl.BlockSpec((2, 3, SUBLANES, hs2), lambda b, c: (0, 0, 0, 0)),
        ],
        out_specs=pl.BlockSpec((1, SSM_T, SSM_WIDTH), lambda b, c: (chunk(b, c), 0, 0)),
        out_shape=jax.ShapeDtypeStruct(u_chunks.shape, F32),
        scratch_shapes=[pltpu.VMEM((2, SSM_T, hs2), F32), pltpu.VMEM((SSM_WIDTH // LANES, SSM_T, LANES), F32),
                        pltpu.VMEM((2, SUBLANES, hs2), F32)],
        compiler_params=_cparams(("arbitrary",) * 2, 32),
        name="ssm_scan",
    )(u_chunks, bm, cm, lam_t, pw_t)


def _ssm_chunk_of_tile(i):
    lat_tiles = SEQ // SSM_T
    return jnp.where(i < BATCH * lat_tiles,
                     (i // lat_tiles) * SSM_CHUNKS + 1 + i % lat_tiles,
                     (i - BATCH * lat_tiles) * SSM_CHUNKS)


def _gelu_tanh(x):
    return 0.5 * x * (1.0 + jnp.tanh(math.sqrt(2.0 / math.pi) * (x + 0.044715 * (x * x * x))))


def _ssm_out_kernel(u_ref, yf_ref, yb_ref, d_ref, w_ref, b_ref, o_ref):
    y = d_ref[0] * u_ref[...].astype(F32) + yf_ref[0] + yb_ref[0]
    y = _gelu_tanh(y)
    z = jnp.dot(y.astype(BF16), w_ref[0], preferred_element_type=F32) + b_ref[0]
    o_ref[...] = (y * jax.nn.sigmoid(z)).astype(o_ref.dtype)


def ssm_output(p, yf, yb, ssm_d, glu_w, glu_b, layer):
    n = p.shape[0]
    yspec = pl.BlockSpec((1, SSM_T, SSM_WIDTH), lambda i: (_ssm_chunk_of_tile(i), 0, 0))
    return pl.pallas_call(
        _ssm_out_kernel,
        grid=(n // SSM_T,),
        in_specs=[
            pl.BlockSpec((SSM_T, SSM_WIDTH), lambda i: (i, COL_SSM // SSM_WIDTH)),
            yspec,
            yspec,
            pl.BlockSpec((1, 1, SSM_WIDTH), lambda i: (layer, 0, 0)),
            pl.BlockSpec((1, SSM_WIDTH, SSM_WIDTH), lambda i: (layer, 0, 0)),
            pl.BlockSpec((1, 1, SSM_WIDTH), lambda i: (layer, 0, 0)),
        ],
        out_specs=pl.BlockSpec((SSM_T, SSM_WIDTH), lambda i: (i, 0)),
        out_shape=jax.ShapeDtypeStruct((n, SSM_WIDTH), BF16),
        compiler_params=_cparams(("arbitrary",), 32),
        name="ssm_output",
    )(p, yf, yb, ssm_d.reshape(DEPTH, 1, SSM_WIDTH), glu_w, glu_b.reshape(DEPTH, 1, SSM_WIDTH))


def _pool_kernel(u_ref, w_ref, s_ref, o_ref, *, seq_len):
    t0 = pl.multiple_of(pl.program_id(1) * SEQ_T, SEQ_T)
    halo = u_ref[0, pl.ds(t0, SEQ_T + 2 * SEQ_PAD), :]
    centre = u_ref[0, pl.ds(t0 + SEQ_PAD, SEQ_T), :].astype(F32)
    tt = lax.broadcasted_iota(jnp.int32, (SEQ_T, SEQ_T + 2 * SEQ_PAD), 0)
    rr = lax.broadcasted_iota(jnp.int32, (SEQ_T, SEQ_T + 2 * SEQ_PAD), 1) - SEQ_PAD
    tg = t0 + lax.broadcasted_iota(jnp.int32, (SEQ_T, 1), 0)
    for gi, w in enumerate(POOL_WINDOWS):
        cols = slice(gi * POOL_GROUP, (gi + 1) * POOL_GROUP)
        band = ((rr >= tt - w // 2) & (rr <= tt + w // 2 - 1)).astype(F32).astype(BF16)
        wsum = jnp.dot(band, halo[:, cols], preferred_element_type=F32)
        cnt = jnp.minimum(tg - w // 2 + w, seq_len) - jnp.maximum(tg - w // 2, 0)
        pooled = wsum / cnt.astype(F32) - centre[:, cols]
        mixed = jnp.dot(pooled.astype(BF16), w_ref[0, gi], preferred_element_type=F32)
        o_ref[0, :, cols] = (mixed * s_ref[0, :, cols]).astype(o_ref.dtype)


def pool_mixer(u_pad, pool_w, pool_scale, layer):
    nseq, lp, _ = u_pad.shape
    seq_len = lp - 2 * SEQ_PAD
    kern = functools.partial(_pool_kernel, seq_len=seq_len)
    ng = len(POOL_WINDOWS)
    return pl.pallas_call(
        kern,
        grid=(nseq, seq_len // SEQ_T),
        in_specs=[
            pl.BlockSpec((1, lp, POOL_WIDTH), lambda s, t: (s, 0, 0)),
            pl.BlockSpec((1, ng, POOL_GROUP, POOL_GROUP), lambda s, t: (layer, 0, 0, 0)),
            pl.BlockSpec((1, 1, POOL_WIDTH), lambda s, t: (layer, 0, 0)),
        ],
        out_specs=pl.BlockSpec((1, SEQ_T, POOL_WIDTH), lambda s, t: (s, t, 0)),
        out_shape=jax.ShapeDtypeStruct((nseq, seq_len, POOL_WIDTH), BF16),
        compiler_params=_cparams(("arbitrary", "arbitrary"), 32),
        name="pool_mixer",
    )(u_pad, pool_w, pool_scale.reshape(DEPTH, 1, POOL_WIDTH))


CONV_RB = 64


def _conv_kernel(u_ref, w_ref, b_ref, g_ref, beta_ref, o_ref, glu_ref):
    t0 = pl.multiple_of(pl.program_id(1) * SEQ_T, SEQ_T)
    rows = SEQ_T + 2 * SEQ_PAD
    a = u_ref[0, pl.ds(t0, rows), 0:CONV_WIDTH].astype(F32)
    g = u_ref[0, pl.ds(t0, rows), CONV_WIDTH:2 * CONV_WIDTH].astype(F32)
    glu_ref[...] = a * jax.nn.sigmoid(g)
    off = SEQ_PAD - CONV_TAPS // 2
    for rb in range(SEQ_T // CONV_RB):
        parts = []
        for cb in range(CONV_WIDTH // LANES):
            cols = slice(cb * LANES, (cb + 1) * LANES)
            acc = jnp.zeros((CONV_RB, LANES), F32)
            for k in range(CONV_TAPS):
                acc = acc + glu_ref[rb * CONV_RB + k + off:rb * CONV_RB + k + off + CONV_RB, cols] * w_ref[0, k:k + 1, cols]
            parts.append(acc)
        y = jnp.concatenate(parts, axis=-1) + b_ref[0]
        yc = y - jnp.mean(y, axis=-1, keepdims=True)
        yn = yc * lax.rsqrt(jnp.mean(yc * yc, axis=-1, keepdims=True) + EPS)
        yn = yn * g_ref[0] + beta_ref[0]
        o_ref[0, rb * CONV_RB:(rb + 1) * CONV_RB, :] = _silu(yn).astype(o_ref.dtype)


def conv_mixer(u_pad, dw_w, dw_b, ln_g, ln_b, layer):
    nseq, lp, _ = u_pad.shape
    seq_len = lp - 2 * SEQ_PAD
    vec = lambda a: a.reshape(DEPTH, 1, CONV_WIDTH)
    vspec = pl.BlockSpec((1, 1, CONV_WIDTH), lambda s, t: (layer, 0, 0))
    return pl.pallas_call(
        _conv_kernel,
        grid=(nseq, seq_len // SEQ_T),
        in_specs=[
            pl.BlockSpec((1, lp, 2 * CONV_WIDTH), lambda s, t: (s, 0, 0)),
            pl.BlockSpec((1, CONV_TAPS, CONV_WIDTH), lambda s, t: (layer, 0, 0)),
            vspec, vspec, vspec,
        ],
        out_specs=pl.BlockSpec((1, SEQ_T, CONV_WIDTH), lambda s, t: (s, t, 0)),
        out_shape=jax.ShapeDtypeStruct((nseq, seq_len, CONV_WIDTH), BF16),
        scratch_shapes=[pltpu.VMEM((SEQ_T + 2 * SEQ_PAD, CONV_WIDTH), F32)],
        compiler_params=_cparams(("arbitrary", "arbitrary"), 40),
        name="conv_mixer",
    )(u_pad, dw_w, vec(dw_b), vec(ln_g), vec(ln_b))


def _merge_kernel(oa_ref, ys_ref, yp_ref, yc_ref, g0_ref, g1_ref, g2_ref, g3_ref,
                  wa_ref, ws_ref, wp_ref, wc_ref, o_ref):
    def branch(x_ref, w_ref, g_ref):
        up = jnp.dot(x_ref[...], w_ref[0, 0], preferred_element_type=F32)
        return jax.nn.sigmoid(g_ref[...].astype(F32)) * up

    m = (branch(oa_ref, wa_ref, g0_ref) + branch(ys_ref, ws_ref, g1_ref)
         + branch(yp_ref, wp_ref, g2_ref) + branch(yc_ref, wc_ref, g3_ref))
    o_ref[...] = m.astype(o_ref.dtype)


def merge_branches(p, o_attn, y_ssm, y_pool, y_conv, w_attn, w_ssm, w_pool, w_conv, layer):
    n = o_attn.shape[0]
    nct = D_MODEL // TN
    gate_tile0 = COL_GATE // TN

    def gate_spec(br):
        return pl.BlockSpec((TM, TN), lambda i, j: (i, gate_tile0 + br * nct + j))

    def x_spec(width):
        return pl.BlockSpec((TM, width), lambda i, j: (i, 0))

    def w_spec(width):
        return pl.BlockSpec((1, 1, width, TN), lambda i, j: (layer, j, 0, 0))

    return pl.pallas_call(
        _merge_kernel,
        grid=(n // TM, nct),
        in_specs=[x_spec(Q_W), x_spec(SSM_WIDTH), x_spec(POOL_WIDTH), x_spec(CONV_WIDTH),
                  gate_spec(0), gate_spec(1), gate_spec(2), gate_spec(3),
                  w_spec(Q_W), w_spec(SSM_WIDTH), w_spec(POOL_WIDTH), w_spec(CONV_WIDTH)],
        out_specs=pl.BlockSpec((TM, TN), lambda i, j: (i, j)),
        out_shape=jax.ShapeDtypeStruct((n, D_MODEL), BF16),
        compiler_params=_cparams(("arbitrary", "arbitrary"), 40),
        name="merge_branches",
    )(o_attn, y_ssm, y_pool, y_conv, p, p, p, p, w_attn, w_ssm, w_pool, w_conv)


def _split_bf16(x):
    hi = x.astype(BF16)
    lo = (x - hi.astype(F32)).astype(BF16)
    return hi, lo


SLAB = D_MODEL // LANES


def _store_row_slabs(ref, row0, x):
    n = x.shape[0]
    for j in range(SLAB):
        ref[pl.ds(row0 * SLAB + j, n, stride=SLAB), :] = x[:, j * LANES:(j + 1) * LANES]


def _load_row_slabs(ref, row0, n):
    return jnp.concatenate([ref[pl.ds(row0 * SLAB + j, n, stride=SLAB), :] for j in range(SLAB)], axis=-1)


def _outproj_kernel(m_ref, x_ref, mod_ref, g_ref, w_ref, rw_ref, xo_ref, h_ref, lg_ref):
    r_hi, r_lo = _split_bf16(rw_ref[...])
    nt = (((1,), (1,)), ((), ()))
    half = m_ref.shape[0] // 2
    for c in range(2):
        rows = slice(c * half, (c + 1) * half)
        mix = jnp.dot(m_ref[rows, :], w_ref[0], preferred_element_type=F32)
        x = x_ref[rows, :] + mod_ref[0, 2:3, :] * mix
        xo_ref[rows, :] = x
        ms = jnp.mean(x * x, axis=-1, keepdims=True)
        h = x * lax.rsqrt(ms + EPS) * g_ref[0]
        h = h * (1.0 + mod_ref[0, 4:5, :]) + mod_ref[0, 3:4, :]
        _store_row_slabs(h_ref, c * half, h)
        h_hi, h_lo = _split_bf16(h)
        lg_ref[:, rows] = (lax.dot_general(r_hi, h_hi, nt, preferred_element_type=F32)
                           + lax.dot_general(r_hi, h_lo, nt, preferred_element_type=F32)
                           + lax.dot_general(r_lo, h_hi, nt, preferred_element_type=F32))


def out_projection(m, xa, mod, norm2_g, w_out, router_wt, layer):
    n = m.shape[0]
    tmo = TM
    return pl.pallas_call(
        _outproj_kernel,
        grid=(n // tmo,),
        in_specs=[
            pl.BlockSpec((tmo, D_MODEL), lambda i: (i, 0)),
            pl.BlockSpec((tmo, D_MODEL), lambda i: (i, 0)),
            pl.BlockSpec((1, 6, D_MODEL), lambda i: (_mod_id(i), 0, 0)),
            pl.BlockSpec((1, 1, D_MODEL), lambda i: (layer, 0, 0)),
            pl.BlockSpec((1, D_MODEL, D_MODEL), lambda i: (layer, 0, 0)),
            pl.BlockSpec((N_EXPERTS, D_MODEL), lambda i: (0, 0)),
        ],
        out_specs=[
            pl.BlockSpec((tmo, D_MODEL), lambda i: (i, 0)),
            pl.BlockSpec((tmo * SLAB, LANES), lambda i: (i, 0)),
            pl.BlockSpec((N_EXPERTS, tmo), lambda i: (0, i)),
        ],
        out_shape=[
            jax.ShapeDtypeStruct((n, D_MODEL), F32),
            jax.ShapeDtypeStruct((n * SLAB, LANES), F32),
            jax.ShapeDtypeStruct((N_EXPERTS, n), F32),
        ],
        compiler_params=_cparams(("arbitrary",), 60),
        name="out_projection",
    )(m, xa, mod, norm2_g.reshape(DEPTH, 1, D_MODEL), w_out, router_wt)


def _first_argmax(blk, row):
    m = jnp.max(blk, axis=0, keepdims=True)
    idx = jnp.min(jnp.where(blk == m, row, EXPERTS_PER_GROUP), axis=0, keepdims=True)
    return m, idx


def _route_kernel(lg_ref, rb_ref, e_ref, w_ref, rank_ref, cnt_ref, run_ref):
    @pl.when(pl.program_id(0) == 0)
    def _():
        run_ref[...] = jnp.zeros_like(run_ref)

    t = lg_ref.shape[1]
    scores = jax.nn.sigmoid(lg_ref[...])
    sel = scores + rb_ref[...]
    row = lax.broadcasted_iota(jnp.int32, (EXPERTS_PER_GROUP, t), 0)
    neg = jnp.float32(-jnp.inf)

    best = None
    for g in range(N_EXPERT_GROUPS):
        blk = sel[g * EXPERTS_PER_GROUP:(g + 1) * EXPERTS_PER_GROUP, :]
        m1, i1 = _first_argmax(blk, row)
        m2 = jnp.max(jnp.where(row == i1, neg, blk), axis=0, keepdims=True)
        gs = m1 + m2
        if best is None:
            best, grp = gs, jnp.zeros((1, t), jnp.int32)
        else:
            better = gs > best
            best = jnp.where(better, gs, best)
            grp = jnp.where(better, g, grp)

    in_sel = jnp.zeros((EXPERTS_PER_GROUP, t), F32)
    in_sc = jnp.zeros((EXPERTS_PER_GROUP, t), F32)
    for g in range(N_EXPERT_GROUPS):
        rows = slice(g * EXPERTS_PER_GROUP, (g + 1) * EXPERTS_PER_GROUP)
        in_sel = jnp.where(grp == g, sel[rows, :], in_sel)
        in_sc = jnp.where(grp == g, scores[rows, :], in_sc)
    _, i1 = _first_argmax(in_sel, row)
    _, i2 = _first_argmax(jnp.where(row == i1, neg, in_sel), row)
    s1 = jnp.sum(jnp.where(row == i1, in_sc, 0.0), axis=0, keepdims=True)
    s2 = jnp.sum(jnp.where(row == i2, in_sc, 0.0), axis=0, keepdims=True)
    e1 = grp * EXPERTS_PER_GROUP + i1
    e2 = grp * EXPERTS_PER_GROUP + i2
    e_ref[0:1, :] = e1
    e_ref[1:2, :] = e2
    w_ref[0:1, :] = s1 / (s1 + s2)
    w_ref[1:2, :] = s2 / (s1 + s2)

    erow = lax.broadcasted_iota(jnp.int32, (N_EXPERTS, t), 0)
    oh1 = erow == e1
    oh2 = erow == e2
    cnt = jnp.where(oh1 | oh2, 1.0, 0.0)
    a = lax.broadcasted_iota(jnp.int32, (t, t), 0)
    b = lax.broadcasted_iota(jnp.int32, (t, t), 1)
    before = jnp.where(a < b, 1.0, 0.0).astype(BF16)
    excl = jnp.dot(cnt.astype(BF16), before, preferred_element_type=F32) + run_ref[...]
    rank_ref[0:1, :] = jnp.sum(jnp.where(oh1, excl, 0.0), axis=0, keepdims=True).astype(jnp.int32)
    rank_ref[1:2, :] = jnp.sum(jnp.where(oh2, excl, 0.0), axis=0, keepdims=True).astype(jnp.int32)
    run_ref[...] = run_ref[...] + jnp.sum(cnt, axis=1, keepdims=True)
    cnt_ref[...] = run_ref[...]


def route(logits_t, router_b):
    n = logits_t.shape[1]
    return pl.pallas_call(
        _route_kernel,
        grid=(n // ROUTE_T,),
        in_specs=[
            pl.BlockSpec((N_EXPERTS, ROUTE_T), lambda i: (0, i)),
            pl.BlockSpec((N_EXPERTS, 1), lambda i: (0, 0)),
        ],
        out_specs=[
            pl.BlockSpec((TOP_K, ROUTE_T), lambda i: (0, i)),
            pl.BlockSpec((TOP_K, ROUTE_T), lambda i: (0, i)),
            pl.BlockSpec((TOP_K, ROUTE_T), lambda i: (0, i)),
            pl.BlockSpec((N_EXPERTS, 1), lambda i: (0, 0)),
        ],
        out_shape=[
            jax.ShapeDtypeStruct((TOP_K, n), jnp.int32),
            jax.ShapeDtypeStruct((TOP_K, n), F32),
            jax.ShapeDtypeStruct((TOP_K, n), jnp.int32),
            jax.ShapeDtypeStruct((N_EXPERTS, 1), F32),
        ],
        scratch_shapes=[pltpu.VMEM((N_EXPERTS, 1), F32)],
        compiler_params=_cparams(("arbitrary",), 32),
        name="route",
    )(logits_t, router_b.reshape(N_EXPERTS, 1).astype(F32))


def _expert_kernel(be_ref, slot_ref, nxt_ref, nu_ref, tok_ref, dst_ref, h_hbm, wg_hbm, wu_hbm, wd_hbm, o_hbm,
                   xbuf_ref, ybuf_ref, wgf_ref, wuf_ref, wdf_ref, wgb_ref, wub_ref, wdb_ref, sem, xsem, ysem,
                   *, layer):
    b = pl.program_id(0)
    n_used = nu_ref[0]
    e = be_ref[b]
    fresh = (b == 0) | (e != be_ref[jnp.maximum(b - 1, 0)])
    live = b < n_used
    has_next = b + 1 < n_used
    cur = b % 2

    def slab(row):
        return pl.ds(pl.multiple_of(row * SLAB, SLAB), SLAB)

    def row_gathers(block, buf):
        return [pltpu.make_async_copy(h_hbm.at[slab(tok_ref[block * MOE_BLOCK + r])],
                                      xbuf_ref.at[buf, slab(r)], xsem.at[buf]) for r in range(MOE_BLOCK)]

    def row_scatters(block, buf):
        return [pltpu.make_async_copy(ybuf_ref.at[buf, slab(r)],
                                      o_hbm.at[slab(dst_ref[block * MOE_BLOCK + r])], ysem.at[buf])
                for r in range(MOE_BLOCK)]

    @pl.when(b == 0)
    def _():
        for cp in row_gathers(0, 0):
            cp.start()

    def weight_copies(expert, slot):
        return (pltpu.make_async_copy(wg_hbm.at[layer, expert], wgf_ref.at[slot], sem.at[slot]),
                pltpu.make_async_copy(wu_hbm.at[layer, expert], wuf_ref.at[slot], sem.at[slot]),
                pltpu.make_async_copy(wd_hbm.at[layer, expert], wdf_ref.at[slot], sem.at[slot]))

    @pl.when(b == 0)
    def _():
        for cp in weight_copies(e, 0):
            cp.start(priority=WEIGHT_DMA_PRIORITY)

    @pl.when(fresh & live)
    def _():
        slot = slot_ref[b]
        nxt = nxt_ref[b]

        @pl.when(nxt >= 0)
        def _():
            for cp in weight_copies(nxt, 1 - slot):
                cp.start(priority=WEIGHT_DMA_PRIORITY)

        for cp in weight_copies(e, slot):
            cp.wait()
        wgb_ref[...] = wgf_ref[slot].astype(BF16)
        wub_ref[...] = wuf_ref[slot].astype(BF16)
        wdb_ref[...] = wdf_ref[slot].astype(BF16)

    @pl.when(live & (b >= 2))
    def _():
        for cp in row_scatters(b - 2, cur):
            cp.wait()

    def block(scatter_prev, gather_next):
        for cp in row_gathers(b, cur):
            cp.wait()
        if scatter_prev:
            for cp in row_scatters(b - 1, 1 - cur):
                cp.start()
        if gather_next:
            for cp in row_gathers(b + 1, 1 - cur):
                cp.start()
        x = _load_row_slabs(xbuf_ref.at[cur], 0, MOE_BLOCK).astype(BF16)
        gate = jnp.dot(x, wgb_ref[...], preferred_element_type=F32)
        up = jnp.dot(x, wub_ref[...], preferred_element_type=F32)
        hid = (_silu(gate) * up).astype(BF16)
        _store_row_slabs(ybuf_ref.at[cur], 0, jnp.dot(hid, wdb_ref[...], preferred_element_type=F32))

    @pl.when(b == 0)
    def _():
        block(False, True)

    @pl.when(live & (b >= 1) & has_next)
    def _():
        block(True, True)

    @pl.when(live & (b >= 1) & jnp.logical_not(has_next))
    def _():
        block(True, False)

    @pl.when(b == n_used)
    def _():
        last = n_used - 1
        for cp in row_scatters(last, last % 2):
            cp.start()
        for cp in row_scatters(last - 1, 1 - last % 2):
            cp.wait()
        for cp in row_scatters(last, last % 2):
            cp.wait()

    @pl.when(jnp.logical_not(live))
    def _():
        ybuf_ref[0] = jnp.zeros((MOE_BLOCK * SLAB, LANES), F32)
        rows = pl.ds(pl.multiple_of(b * (MOE_BLOCK * SLAB), MOE_BLOCK * SLAB), MOE_BLOCK * SLAB)
        cp = pltpu.make_async_copy(ybuf_ref.at[0], o_hbm.at[rows], ysem.at[0])
        cp.start()
        cp.wait()


def _slot_maps(dest_flat, n, n_slots):
    i32 = jnp.int32
    owner = jnp.full((n_slots,), -1, i32).at[dest_flat].set(jnp.arange(TOP_K * n, dtype=i32), unique_indices=True)
    is_pad = owner < 0
    tok = jnp.where(is_pad, 0, owner % n)
    dst = jnp.where(is_pad, TOP_K * n + jnp.cumsum(is_pad.astype(i32)) - 1, owner)
    return tok.astype(i32), dst.astype(i32)


def experts(blk_e, blk_slot, blk_next, n_used, slot_tok, slot_dst, h2, w_gate, w_up, w_down, layer):
    n = h2.shape[0] // SLAB
    n_slots = slot_tok.shape[0]
    n_blocks = n_slots // MOE_BLOCK
    assert n_blocks - N_EXPERTS >= 2
    assert n_slots == TOP_K * n + N_EXPERTS * MOE_BLOCK
    grid_spec = pltpu.PrefetchScalarGridSpec(
        num_scalar_prefetch=6,
        grid=(n_blocks,),
        in_specs=[
            pl.BlockSpec(memory_space=pl.ANY),
            pl.BlockSpec(memory_space=pl.ANY),
            pl.BlockSpec(memory_space=pl.ANY),
            pl.BlockSpec(memory_space=pl.ANY),
        ],
        out_specs=pl.BlockSpec(memory_space=pl.ANY),
        scratch_shapes=[
            pltpu.VMEM((2, MOE_BLOCK * SLAB, LANES), F32),
            pltpu.VMEM((2, MOE_BLOCK * SLAB, LANES), F32),
            pltpu.VMEM((2, D_MODEL, EXPERT_FF), F32),
            pltpu.VMEM((2, D_MODEL, EXPERT_FF), F32),
            pltpu.VMEM((2, EXPERT_FF, D_MODEL), F32),
            pltpu.VMEM((D_MODEL, EXPERT_FF), BF16),
            pltpu.VMEM((D_MODEL, EXPERT_FF), BF16),
            pltpu.VMEM((EXPERT_FF, D_MODEL), BF16),
            pltpu.SemaphoreType.DMA((2,)),
            pltpu.SemaphoreType.DMA((2,)),
            pltpu.SemaphoreType.DMA((2,)),
        ],
    )
    return pl.pallas_call(
        functools.partial(_expert_kernel, layer=layer),
        grid_spec=grid_spec,
        out_shape=jax.ShapeDtypeStruct((n_slots * SLAB, LANES), F32),
        compiler_params=_cparams(("arbitrary",), 56),
        name="experts",
    )(blk_e, blk_slot, blk_next, n_used, slot_tok, slot_dst, h2, w_gate, w_up, w_down)


def _combine_kernel(x_ref, mod_ref, w_ref, fg_ref, y0_ref, y1_ref, o_ref, *, final_norm):
    f = w_ref[:, 0:1] * _load_row_slabs(y0_ref, 0, TM) + w_ref[:, 1:2] * _load_row_slabs(y1_ref, 0, TM)
    x = x_ref[...] + mod_ref[0, 5:6, :] * f
    if final_norm:
        ms = jnp.mean(x * x, axis=-1, keepdims=True)
        x = x * lax.rsqrt(ms + EPS) * fg_ref[...]
    o_ref[...] = x


def combine(xa, mod, wts_rows, ys, final_g, final_norm):
    n = xa.shape[0]
    n_tiles = n // TM
    kern = functools.partial(_combine_kernel, final_norm=final_norm)
    return pl.pallas_call(
        kern,
        grid=(n_tiles,),
        in_specs=[
            pl.BlockSpec((TM, D_MODEL), lambda i: (i, 0)),
            pl.BlockSpec((1, 6, D_MODEL), lambda i: (_mod_id(i), 0, 0)),
            pl.BlockSpec((TM, TOP_K), lambda i: (i, 0)),
            pl.BlockSpec((1, D_MODEL), lambda i: (0, 0)),
            pl.BlockSpec((TM * SLAB, LANES), lambda i: (i, 0)),
            pl.BlockSpec((TM * SLAB, LANES), lambda i: (n_tiles + i, 0)),
        ],
        out_specs=pl.BlockSpec((TM, D_MODEL), lambda i: (i, 0)),
        out_shape=jax.ShapeDtypeStruct((n, D_MODEL), F32),
        compiler_params=_cparams(("arbitrary",), 48),
        name="combine",
    )(xa, mod, wts_rows, final_g.reshape(1, D_MODEL), ys, ys)


def _slot_plan(e_idx, rank, counts, n_blocks):
    i32 = jnp.int32
    cnt = counts.reshape(N_EXPERTS).astype(i32)
    padded = (cnt + MOE_BLOCK - 1) // MOE_BLOCK * MOE_BLOCK
    pad_end = jnp.cumsum(padded)
    pad_start = pad_end - padded
    eids = jnp.arange(N_EXPERTS, dtype=i32)

    def lookup(table, idx):
        return jnp.sum(jnp.where(idx[..., None] == eids, table, 0), axis=-1).astype(i32)

    dest = lookup(pad_start, e_idx) + rank
    blk_start = jnp.arange(n_blocks, dtype=i32) * MOE_BLOCK
    blk_e = jnp.minimum(jnp.sum(pad_end[None, :] <= blk_start[:, None], axis=1), N_EXPERTS - 1).astype(i32)
    n_used = (pad_end[-1] // MOE_BLOCK).astype(i32).reshape(1)
    has = cnt > 0
    run = jnp.cumsum(has.astype(i32)) - 1
    first_from = lax.cummin(jnp.where(has, eids, N_EXPERTS)[::-1])[::-1]
    nxt = jnp.concatenate([first_from[1:], jnp.full((1,), N_EXPERTS, i32)])
    nxt = jnp.where(nxt >= N_EXPERTS, -1, nxt)
    return dest.reshape(-1).astype(i32), blk_e, lookup(run % 2, blk_e), lookup(nxt, blk_e), n_used


def _col_tiles(w):
    d, k, n = w.shape
    return w.astype(BF16).reshape(d, k, n // TN, TN).transpose(0, 2, 1, 3)


def _pad_rows(u):
    return jnp.pad(u, ((0, 0), (SEQ_PAD, SEQ_PAD), (0, 0)))


def _layer(layer, xa, mod, W, with_ctx, final_norm):
    n_lat_tiles = N_LAT // TM
    n_ctx_tiles = N_CTX // TM
    if with_ctx:
        p = in_projection(xa, mod, W['norm1_g'], W['w_in'], layer, 0, n_lat_tiles + n_ctx_tiles, IN_W, TN_IN)
        p_lat, p_ctx = p[:N_LAT], p[N_LAT:]
    else:
        p = in_projection(xa, mod, W['norm1_g'], W['w_in'], layer, 0, n_lat_tiles, IN_W, TN_IN)
        p_lat = p
        p_ctx = in_projection(xa, mod, W['norm1_g'], W['w_in'], layer, n_lat_tiles, n_ctx_tiles, CTX_STATE_W, TN)
    n_rows = p.shape[0]

    def seq(a, lo, hi, n_seq_rows):
        return a[:, lo:hi].reshape(BATCH, n_seq_rows, hi - lo)

    kv = jnp.concatenate([seq(p_ctx, 0, 2 * KV_W, CTX_LEN), seq(p_lat, 0, 2 * KV_W, SEQ)], axis=1)
    cos, sin = _rope_tables(CTX_LEN)
    o_attn = attention(p, COL_Q, 0, SEQ, kv, cos[CTX_LEN:], sin[CTX_LEN:], cos, sin,
                       W['q_norm_g'], W['k_norm_g'], layer)
    if with_ctx:
        kv_c = seq(p_ctx, 0, 2 * KV_W, CTX_LEN)
        ones, zeros = jnp.ones((CTX_LEN, HEAD_DIM), F32), jnp.zeros((CTX_LEN, HEAD_DIM), F32)
        o_attn_c = attention(p, COL_Q, N_LAT, CTX_LEN, kv_c, ones, zeros, ones, zeros,
                             W['q_norm_g'], W['k_norm_g'], layer)
        o_attn = jnp.concatenate([o_attn, o_attn_c], axis=0)

    u_seq = jnp.concatenate([seq(p_ctx, COL_SSM, COL_SSM + SSM_WIDTH, CTX_LEN),
                             seq(p_lat, COL_SSM, COL_SSM + SSM_WIDTH, SEQ)], axis=1)
    u_chunks = u_seq.reshape(BATCH * SSM_CHUNKS, SSM_T, SSM_WIDTH)
    bm, cm, lam_t, pw_t = _ssm_params(W['ssm_a_re'][layer], W['ssm_a_im'][layer], W['ssm_log_dt'][layer],
                                      W['ssm_b_re'][layer], W['ssm_b_im'][layer],
                                      W['ssm_c_re'][layer], W['ssm_c_im'][layer])
    yf = ssm_scan(u_chunks, bm[0], cm[0], lam_t[0], pw_t[0], reverse=False)
    yb = ssm_scan(u_chunks, bm[1], cm[1], lam_t[1], pw_t[1], reverse=True)
    y_ssm = ssm_output(p, yf, yb, W['ssm_d'], W['ssm_glu_w'], W['ssm_glu_b'], layer)

    def seq_mixers(src, n_seq_rows):
        up = _pad_rows(seq(src, COL_POOL, COL_POOL + POOL_WIDTH, n_seq_rows))
        uc = _pad_rows(seq(src, COL_CONV, COL_CONV + 2 * CONV_WIDTH, n_seq_rows))
        yp = pool_mixer(up, W['pool_w'], W['pool_scale'], layer)
        yc = conv_mixer(uc, W['conv_dw_w'], W['conv_dw_b'], W['conv_ln_g'], W['conv_ln_b'], layer)
        return yp.reshape(-1, POOL_WIDTH), yc.reshape(-1, CONV_WIDTH)

    y_pool, y_conv = seq_mixers(p_lat, SEQ)
    if with_ctx:
        yp_c, yc_c = seq_mixers(p_ctx, CTX_LEN)
        y_pool = jnp.concatenate([y_pool, yp_c], axis=0)
        y_conv = jnp.concatenate([y_conv, yc_c], axis=0)

    m = merge_branches(p, o_attn, y_ssm, y_pool, y_conv,
                       W['w_up_attn'], W['w_up_ssm'], W['w_up_pool'], W['w_up_conv'], layer)
    x1, h2, logits_t = out_projection(m, xa, mod, W['norm2_g'], W['w_out'], W['router_wt'], layer)

    e_idx, wts, rank, counts = route(logits_t, W['router_b'])
    n_blocks = -(-n_rows * TOP_K // MOE_BLOCK) + N_EXPERTS
    dest_flat, blk_e, blk_slot, blk_next, n_used = _slot_plan(e_idx, rank, counts, n_blocks)
    slot_tok, slot_dst = _slot_maps(dest_flat, n_rows, n_blocks * MOE_BLOCK)
    ys = experts(blk_e, blk_slot, blk_next, n_used, slot_tok, slot_dst, h2,
                 W['moe_w_gate'], W['moe_w_up'], W['moe_w_down'], layer)
    x2 = combine(x1, mod, wts.T, ys, W['final_g'], final_norm)
    if with_ctx:
        return x2
    return jnp.concatenate([x2, xa[N_LAT:]], axis=0)


def kernel(x, c, ctx, c_ctx, ada_w, ada_b, norm1_g, norm2_g, w_in, q_norm_g, k_norm_g, ssm_a_re, ssm_a_im, ssm_log_dt, ssm_b_re, ssm_b_im, ssm_c_re, ssm_c_im, ssm_d, ssm_glu_w, ssm_glu_b, pool_w, pool_scale, conv_dw_w, conv_dw_b, conv_ln_g, conv_ln_b, w_up_attn, w_up_ssm, w_up_pool, w_up_conv, w_out, router_w, router_b, moe_w_gate, moe_w_up, moe_w_down, final_g):
    W = dict(
        norm1_g=norm1_g, norm2_g=norm2_g, q_norm_g=q_norm_g, k_norm_g=k_norm_g,
        w_in=w_in.astype(BF16),
        ssm_a_re=ssm_a_re, ssm_a_im=ssm_a_im, ssm_log_dt=ssm_log_dt, ssm_b_re=ssm_b_re, ssm_b_im=ssm_b_im,
        ssm_c_re=ssm_c_re, ssm_c_im=ssm_c_im, ssm_d=ssm_d, ssm_glu_w=ssm_glu_w.astype(BF16), ssm_glu_b=ssm_glu_b,
        pool_w=pool_w.astype(BF16), pool_scale=pool_scale,
        conv_dw_w=conv_dw_w, conv_dw_b=conv_dw_b, conv_ln_g=conv_ln_g, conv_ln_b=conv_ln_b,
        w_up_attn=_col_tiles(w_up_attn), w_up_ssm=_col_tiles(w_up_ssm), w_up_pool=_col_tiles(w_up_pool),
        w_up_conv=_col_tiles(w_up_conv), w_out=w_out.astype(BF16),
        router_wt=router_w.T, router_b=router_b,
        moe_w_gate=moe_w_gate, moe_w_up=moe_w_up, moe_w_down=moe_w_down, final_g=final_g,
    )
    cc = jnp.concatenate([c, c_ctx[None, :], jnp.zeros((SUBLANES - BATCH - 1, D_MODEL), F32)], axis=0)
    mod_all = ada_modulation(cc, ada_w, ada_b)
    xa = jnp.concatenate([x.reshape(N_LAT, D_MODEL), ctx.reshape(N_CTX, D_MODEL)], axis=0)
    for layer in range(DEPTH):
        mod = mod_all[layer, :BATCH + 1].reshape(BATCH + 1, 6, D_MODEL)
        xa = _layer(layer, xa, mod, W, with_ctx=(layer < DEPTH - 1), final_norm=(layer == DEPTH - 1))
    return xa[:N_LAT].reshape(BATCH, SEQ, D_MODEL)
```

```python
import functools
import math

import jax
import jax.numpy as jnp
from jax import lax
from jax.experimental import pallas as pl
from jax.experimental.pallas import tpu as pltpu

F32 = jnp.float32
BF16 = jnp.bfloat16

D_MODEL = 2048
BATCH = 2
SEQ = 4096
DEPTH = 2
GRID_W = 64
CTX_LEN = 256
N_HEADS = 8
N_KV_HEADS = 2
HEAD_DIM = 128
ROPE_THETA = 10000.0
SSM_WIDTH = 512
SSM_GROUP = 16
SSM_GROUPS = SSM_WIDTH // SSM_GROUP
SSM_STATE = 64
POOL_WIDTH = 512
POOL_WINDOWS = (2, 4, 8, 16)
POOL_GROUP = POOL_WIDTH // len(POOL_WINDOWS)
CONV_WIDTH = 512
CONV_TAPS = 31
N_BRANCHES = 4
N_EXPERTS = 64
N_EXPERT_GROUPS = 8
EXPERTS_PER_GROUP = N_EXPERTS // N_EXPERT_GROUPS
TOP_K = 2
EXPERT_FF = 512
EPS = 1e-6

Q_W = N_HEADS * HEAD_DIM
KV_W = N_KV_HEADS * HEAD_DIM
IN_W = 2 * KV_W + SSM_WIDTH + Q_W + POOL_WIDTH + 2 * CONV_WIDTH + N_BRANCHES * D_MODEL
CTX_STATE_W = 2 * KV_W + SSM_WIDTH
COL_K, COL_V, COL_SSM, COL_Q = 0, KV_W, 2 * KV_W, 2 * KV_W + SSM_WIDTH
COL_POOL = COL_Q + Q_W
COL_CONV = COL_POOL + POOL_WIDTH
COL_GATE = COL_CONV + 2 * CONV_WIDTH

N_LAT = BATCH * SEQ
N_CTX = BATCH * CTX_LEN
S_ALL = CTX_LEN + SEQ

V7X_VMEM_BYTES = 64 * 1024 * 1024
SUBLANES = 8
LANES = 128
BF16_ROWS = 16

TM = 512
TN = 512
TN_IN = IN_W // 4
TILES_PER_BATCH = SEQ // TM
TQ = 512
KEY_PREP_T = 256
SSM_T = 256
SSM_TC = SSM_T // SUBLANES
SSM_HALF = SSM_WIDTH // 2
SSM_HSTATE = SSM_GROUPS // 2 * SSM_STATE
SEQ_T = 256
SEQ_PAD = 16
MOE_BLOCK = 128
WEIGHT_DMA_PRIORITY = 1
ROUTE_T = 512


def _cparams(sem, vmem_mb):
    return pltpu.CompilerParams(dimension_semantics=sem, vmem_limit_bytes=vmem_mb * 1024 * 1024)


def _mod_id(tile):
    return jnp.minimum(tile // TILES_PER_BATCH, BATCH)


def _silu(x):
    return x * jax.nn.sigmoid(x)


def _ada_kernel(c_ref, w_ref, b_ref, o_ref):
    c = c_ref[...]
    a = _silu(c).astype(BF16)
    o_ref[0] = jnp.dot(a, w_ref[0].astype(BF16), preferred_element_type=F32) + b_ref[0]


def ada_modulation(cc, ada_w, ada_b):
    tn = 1024
    n6 = 6 * D_MODEL
    return pl.pallas_call(
        _ada_kernel,
        grid=(DEPTH, n6 // tn),
        in_specs=[
            pl.BlockSpec((SUBLANES, D_MODEL), lambda l, j: (0, 0)),
            pl.BlockSpec((1, D_MODEL, tn), lambda l, j: (l, 0, j)),
            pl.BlockSpec((1, 1, tn), lambda l, j: (l, 0, j)),
        ],
        out_specs=pl.BlockSpec((1, SUBLANES, tn), lambda l, j: (l, 0, j)),
        out_shape=jax.ShapeDtypeStruct((DEPTH, SUBLANES, n6), F32),
        compiler_params=_cparams(("arbitrary", "arbitrary"), 40),
        name="ada_modulation",
    )(cc, ada_w, ada_b.reshape(DEPTH, 1, n6))


def _inproj_kernel(x_ref, mod_ref, g_ref, w_ref, o_ref, h_ref):
    @pl.when(pl.program_id(1) == 0)
    def _():
        x = x_ref[...]
        ms = jnp.mean(x * x, axis=-1, keepdims=True)
        y = x * lax.rsqrt(ms + EPS) * g_ref[0]
        h = y * (1.0 + mod_ref[0, 1:2, :]) + mod_ref[0, 0:1, :]
        h_ref[...] = h.astype(BF16)

    o_ref[...] = jnp.dot(h_ref[...], w_ref[0], preferred_element_type=F32).astype(o_ref.dtype)


def in_projection(xa, mod, norm_g, w_in, layer, row_tile0, n_row_tiles, n_cols, tn):
    return pl.pallas_call(
        _inproj_kernel,
        grid=(n_row_tiles, n_cols // tn),
        in_specs=[
            pl.BlockSpec((TM, D_MODEL), lambda i, j: (i + row_tile0, 0)),
            pl.BlockSpec((1, 6, D_MODEL), lambda i, j: (_mod_id(i + row_tile0), 0, 0)),
            pl.BlockSpec((1, 1, D_MODEL), lambda i, j: (layer, 0, 0)),
            pl.BlockSpec((1, D_MODEL, tn), lambda i, j: (layer, 0, j)),
        ],
        out_specs=pl.BlockSpec((TM, tn), lambda i, j: (i, j)),
        out_shape=jax.ShapeDtypeStruct((n_row_tiles * TM, n_cols), BF16),
        scratch_shapes=[pltpu.VMEM((TM, D_MODEL), BF16)],
        compiler_params=_cparams(("arbitrary", "arbitrary"), 56),
        name="in_projection",
    )(xa, mod, norm_g.reshape(DEPTH, 1, D_MODEL), w_in)


def _rope_tables(n_ctx_rows):
    half = HEAD_DIM // 4
    inv_freq = ROPE_THETA ** (-jnp.arange(half, dtype=F32) / half)
    t = jnp.arange(SEQ)
    ang_r = (t // GRID_W).astype(F32)[:, None] * inv_freq[None, :]
    ang_c = (t % GRID_W).astype(F32)[:, None] * inv_freq[None, :]
    cos = jnp.concatenate([jnp.cos(ang_r), jnp.cos(ang_r), jnp.cos(ang_c), jnp.cos(ang_c)], axis=-1)
    sin = jnp.concatenate([-jnp.sin(ang_r), jnp.sin(ang_r), -jnp.sin(ang_c), jnp.sin(ang_c)], axis=-1)
    cos = jnp.concatenate([jnp.ones((n_ctx_rows, HEAD_DIM), F32), cos], axis=0)
    sin = jnp.concatenate([jnp.zeros((n_ctx_rows, HEAD_DIM), F32), sin], axis=0)
    return cos, sin


def _head_norm_rope(x, g, cos, sin):
    ms = jnp.mean(x * x, axis=-1, keepdims=True)
    y = x * lax.rsqrt(ms + EPS) * g
    lane = lax.broadcasted_iota(jnp.int32, y.shape, 1)
    first = (lane % (HEAD_DIM // 2)) < (HEAD_DIM // 4)
    partner = jnp.where(first, pltpu.roll(y, HEAD_DIM - HEAD_DIM // 4, 1), pltpu.roll(y, HEAD_DIM // 4, 1))
    return y * cos + partner * sin


def _attn_kernel(q_ref, k_ref, v_ref, cq_ref, sq_ref, ck_ref, sk_ref, gq_ref, gk_ref, o_ref, ks_ref, va_ref, *,
                 n_keys):
    @pl.when(pl.program_id(2) == 0)
    def _():
        def prep(c, carry):
            r0 = pl.multiple_of(c * KEY_PREP_T, KEY_PREP_T)
            rows = pl.ds(r0, KEY_PREP_T)
            kk = k_ref[0, rows, :].astype(F32)
            kn = _head_norm_rope(kk, gk_ref[0], ck_ref[rows, :], sk_ref[rows, :])
            ks_ref[rows, :] = kn.astype(BF16)
            return carry

        lax.fori_loop(0, n_keys // KEY_PREP_T, prep, 0)
        va_ref[:, 0:HEAD_DIM] = v_ref[0]
        va_ref[:, HEAD_DIM:2 * HEAD_DIM] = jnp.ones((n_keys, HEAD_DIM), BF16)

    k = ks_ref[...]
    va = va_ref[...]
    scale = HEAD_DIM ** -0.5 * math.log2(math.e)
    for hh in range(N_HEADS // N_KV_HEADS):
        cols = slice(hh * HEAD_DIM, (hh + 1) * HEAD_DIM)
        q = q_ref[:, cols].astype(F32)
        qn = _head_norm_rope(q, gq_ref[0], cq_ref[...], sq_ref[...]) * scale
        s = lax.dot_general(qn.astype(BF16), k, (((1,), (1,)), ((), ())), preferred_element_type=F32)
        m = jnp.max(s, axis=-1, keepdims=True)
        p = jnp.exp2(s - m)
        oa = jnp.dot(p.astype(BF16), va, preferred_element_type=F32)
        o = oa[:, 0:HEAD_DIM] / oa[:, HEAD_DIM:HEAD_DIM + 1]
        o_ref[:, cols] = o.astype(o_ref.dtype)


def attention(q_src, q_col0, q_row0, n_q, kv, cos_q, sin_q, cos_k, sin_k, q_norm_g, k_norm_g, layer):
    n_keys = kv.shape[1]
    grp_w = Q_W // N_KV_HEADS
    tq = min(TQ, n_q)
    qb = n_q // tq
    kern = functools.partial(_attn_kernel, n_keys=n_keys)
    return pl.pallas_call(
        kern,
        grid=(BATCH, N_KV_HEADS, qb),
        in_specs=[
            pl.BlockSpec((tq, grp_w), lambda b, g, i: (q_row0 // tq + b * qb + i, q_col0 // grp_w + g)),
            pl.BlockSpec((1, n_keys, HEAD_DIM), lambda b, g, i: (b, 0, g)),
            pl.BlockSpec((1, n_keys, HEAD_DIM), lambda b, g, i: (b, 0, N_KV_HEADS + g)),
            pl.BlockSpec((tq, HEAD_DIM), lambda b, g, i: (i, 0)),
            pl.BlockSpec((tq, HEAD_DIM), lambda b, g, i: (i, 0)),
            pl.BlockSpec((n_keys, HEAD_DIM), lambda b, g, i: (0, 0)),
            pl.BlockSpec((n_keys, HEAD_DIM), lambda b, g, i: (0, 0)),
            pl.BlockSpec((1, 1, HEAD_DIM), lambda b, g, i: (layer, 0, 0)),
            pl.BlockSpec((1, 1, HEAD_DIM), lambda b, g, i: (layer, 0, 0)),
        ],
        out_specs=pl.BlockSpec((tq, grp_w), lambda b, g, i: (b * qb + i, g)),
        out_shape=jax.ShapeDtypeStruct((BATCH * n_q, Q_W), BF16),
        scratch_shapes=[pltpu.VMEM((n_keys, HEAD_DIM), BF16), pltpu.VMEM((n_keys, 2 * HEAD_DIM), BF16)],
        compiler_params=_cparams(("arbitrary", "arbitrary", "arbitrary"), 56),
        name="attention",
    )(q_src, kv, kv, cos_q, sin_q, cos_k, sin_k,
      q_norm_g.reshape(DEPTH, 1, HEAD_DIM), k_norm_g.reshape(DEPTH, 1, HEAD_DIM))


def _cmul(ar, ai, br, bi):
    return ar * br - ai * bi, ar * bi + ai * br


SSM_CB = 512


def _ssm_kernel(u_ref, bm_ref, cm_ref, lam_ref, pw_ref, y_ref, bu_ref, yv_ref, carry_ref, *, reverse):
    @pl.when(pl.program_id(1) == 0)
    def _():
        carry_ref[...] = jnp.zeros_like(carry_ref)

    rr = lax.broadcasted_iota(jnp.int32, (SSM_T, SSM_T), 0)
    tt = lax.broadcasted_iota(jnp.int32, (SSM_T, SSM_T), 1)
    perm = jnp.where(tt == (rr % SUBLANES) * SSM_TC + rr // SUBLANES, 1.0, 0.0).astype(BF16)
    u_scan = jnp.dot(perm, u_ref[0], preferred_element_type=F32).astype(BF16)
    for h in range(2):
        bu_ref[h] = jnp.dot(u_scan[:, h * SSM_HALF:(h + 1) * SSM_HALF], bm_ref[h], preferred_element_type=F32)

    last_row = 0 if reverse else (SSM_T - SUBLANES)
    edge = (SUBLANES - 1) if reverse else 0
    toward = (lambda x, k: pltpu.roll(x, SUBLANES - k, 0)) if reverse else (lambda x, k: pltpu.roll(x, k, 0))

    def step_rows(j):
        jj = (SSM_TC - 1 - j) if reverse else j
        return pl.ds(pl.multiple_of(jj * SUBLANES, SUBLANES), SUBLANES)

    for h, cb in [(h, cb) for h in range(2) for cb in range(SSM_HSTATE // SSM_CB)]:
        re = slice(cb * SSM_CB, (cb + 1) * SSM_CB)
        im = slice(SSM_HSTATE + cb * SSM_CB, SSM_HSTATE + (cb + 1) * SSM_CB)
        lr = lam_ref[h, :, re]
        li = lam_ref[h, :, im]

        def local_scan(j, st, h=h, re=re, im=im, lr=lr, li=li):
            rows = step_rows(j)
            pr, pi = _cmul(lr, li, st[0], st[1])
            nr = pr + bu_ref[h, rows, re]
            ni = pi + bu_ref[h, rows, im]
            bu_ref[h, rows, re] = nr
            bu_ref[h, rows, im] = ni
            return nr, ni

        zero = jnp.zeros((SUBLANES, SSM_CB), F32)
        lax.fori_loop(0, SSM_TC, local_scan, (zero, zero), unroll=4)

        er = bu_ref[h, last_row:last_row + SUBLANES, re]
        ei = bu_ref[h, last_row:last_row + SUBLANES, im]
        row = lax.broadcasted_iota(jnp.int32, (SUBLANES, SSM_CB), 0)
        dist = (SUBLANES - 1 - row) if reverse else row
        xr = jnp.where(row == edge, carry_ref[h, :, re], toward(er, 1))
        xi = jnp.where(row == edge, carry_ref[h, :, im], toward(ei, 1))
        for lvl, k in enumerate((1, 2, 4)):
            ar = pw_ref[h, lvl, :, re]
            ai = pw_ref[h, lvl, :, im]
            sr = jnp.where(dist >= k, toward(xr, k), 0.0)
            si = jnp.where(dist >= k, toward(xi, k), 0.0)
            mr, mi = _cmul(ar, ai, sr, si)
            xr = xr + mr
            xi = xi + mi
        cr, ci = _cmul(pw_ref[h, 0, :, re], pw_ref[h, 0, :, im], xr, xi)
        carry_ref[h, :, re] = toward(cr + er, 1)
        carry_ref[h, :, im] = toward(ci + ei, 1)

        def add_carry(j, g, h=h, re=re, im=im, lr=lr, li=li):
            rows = step_rows(j)
            bu_ref[h, rows, re] = bu_ref[h, rows, re] + g[0]
            bu_ref[h, rows, im] = bu_ref[h, rows, im] + g[1]
            return _cmul(lr, li, g[0], g[1])

        lax.fori_loop(0, SSM_TC, add_carry, _cmul(lr, li, xr, xi), unroll=4)

    for h in range(2):
        y = jnp.dot(bu_ref[h].astype(BF16), cm_ref[h], preferred_element_type=F32)
        for cb in range(SSM_HALF // LANES):
            cols = slice(cb * LANES, (cb + 1) * LANES)
            out_cols = slice(h * SSM_HALF + cb * LANES, h * SSM_HALF + (cb + 1) * LANES)
            slab = h * (SSM_HALF // LANES) + cb
            yv_ref[slab] = y[:, cols]
            for s in range(SUBLANES):
                y_ref[0, s * SSM_TC:(s + 1) * SSM_TC, out_cols] = yv_ref[slab, pl.ds(s, SSM_TC, stride=SUBLANES), :]


def _ssm_params(a_re, a_im, log_dt, b_re, b_im, c_re, c_im):
    lam = lax.complex(a_re.astype(F32), a_im.astype(F32))
    dt = jnp.exp(log_dt.astype(F32))[..., None]
    lam_bar = jnp.exp(lam * dt)
    b_bar = ((lam_bar - 1.0) / lam)[..., None] * lax.complex(b_re.astype(F32), b_im.astype(F32))
    gh = SSM_GROUPS // 2
    eye = jnp.eye(gh, dtype=F32)

    def b_block(m):
        m = m.reshape(2, 2, gh, SSM_STATE, SSM_GROUP)
        return jnp.einsum('dhgpc,gk->dhgckp', m, eye).reshape(2, 2, gh * SSM_GROUP, gh * SSM_STATE)

    def c_block(m):
        m = m.reshape(2, 2, gh, SSM_GROUP, SSM_STATE)
        return jnp.einsum('dhgcp,gk->dhkpgc', m, eye).reshape(2, 2, gh * SSM_STATE, gh * SSM_GROUP)

    bm = jnp.concatenate([b_block(jnp.real(b_bar)), b_block(jnp.imag(b_bar))], axis=-1).astype(BF16)
    cm = jnp.concatenate([c_block(c_re.astype(F32)), -c_block(c_im.astype(F32))], axis=-2).astype(BF16)

    def table(z):
        z = z.reshape(2, 2, SSM_HSTATE)
        t = jnp.concatenate([jnp.real(z), jnp.imag(z)], axis=-1)
        return jnp.broadcast_to(t[:, :, None, :], (2, 2, SUBLANES, 2 * SSM_HSTATE))

    lam_t = table(lam_bar)
    pw_t = jnp.stack([table(jnp.exp(lam * dt * (SSM_TC * k))) for k in (1, 2, 4)], axis=2)
    return bm, cm, lam_t, pw_t


SSM_CHUNKS = S_ALL // SSM_T


def ssm_scan(u_chunks, bm, cm, lam_t, pw_t, reverse):
    hs2 = 2 * SSM_HSTATE

    def chunk(b, c):
        if reverse:
            c = jnp.where(c == 0, 0, SSM_CHUNKS - c)
        return b * SSM_CHUNKS + c

    kern = functools.partial(_ssm_kernel, reverse=reverse)
    return pl.pallas_call(
        kern,
        grid=(BATCH, SSM_CHUNKS),
        in_specs=[
            pl.BlockSpec((1, SSM_T, SSM_WIDTH), lambda b, c: (chunk(b, c), 0, 0)),
            pl.BlockSpec((2, SSM_HALF, hs2), lambda b, c: (0, 0, 0)),
            pl.BlockSpec((2, hs2, SSM_HALF), lambda b, c: (0, 0, 0)),
            pl.BlockSpec((2, SUBLANES, hs2), lambda b, c: (0, 0, 0)),
            pl.BlockSpec((2, 3, SUBLANES, hs2), lambda b, c: (0, 0, 0, 0)),
        ],
        out_specs=pl.BlockSpec((1, SSM_T, SSM_WIDTH), lambda b, c: (chunk(b, c), 0, 0)),
        out_shape=jax.ShapeDtypeStruct(u_chunks.shape, F32),
        scratch_shapes=[pltpu.VMEM((2, SSM_T, hs2), F32), pltpu.VMEM((SSM_WIDTH // LANES, SSM_T, LANES), F32),
                        pltpu.VMEM((2, SUBLANES, hs2), F32)],
        compiler_params=_cparams(("arbitrary",) * 2, 32),
        name="ssm_scan",
    )(u_chunks, bm, cm, lam_t, pw_t)


def _ssm_chunk_of_tile(i):
    lat_tiles = SEQ // SSM_T
    return jnp.where(i < BATCH * lat_tiles,
                     (i // lat_tiles) * SSM_CHUNKS + 1 + i % lat_tiles,
                     (i - BATCH * lat_tiles) * SSM_CHUNKS)


def _gelu_tanh(x):
    return 0.5 * x * (1.0 + jnp.tanh(math.sqrt(2.0 / math.pi) * (x + 0.044715 * (x * x * x))))


def _ssm_out_kernel(u_ref, yf_ref, yb_ref, d_ref, w_ref, b_ref, o_ref):
    y = d_ref[0] * u_ref[...].astype(F32) + yf_ref[0] + yb_ref[0]
    y = _gelu_tanh(y)
    z = jnp.dot(y.astype(BF16), w_ref[0], preferred_element_type=F32) + b_ref[0]
    o_ref[...] = (y * jax.nn.sigmoid(z)).astype(o_ref.dtype)


def ssm_output(p, yf, yb, ssm_d, glu_w, glu_b, layer):
    n = p.shape[0]
    yspec = pl.BlockSpec((1, SSM_T, SSM_WIDTH), lambda i: (_ssm_chunk_of_tile(i), 0, 0))
    return pl.pallas_call(
        _ssm_out_kernel,
        grid=(n // SSM_T,),
        in_specs=[
            pl.BlockSpec((SSM_T, SSM_WIDTH), lambda i: (i, COL_SSM // SSM_WIDTH)),
            yspec,
            yspec,
            pl.BlockSpec((1, 1, SSM_WIDTH), lambda i: (layer, 0, 0)),
            pl.BlockSpec((1, SSM_WIDTH, SSM_WIDTH), lambda i: (layer, 0, 0)),
            pl.BlockSpec((1, 1, SSM_WIDTH), lambda i: (layer, 0, 0)),
        ],
        out_specs=pl.BlockSpec((SSM_T, SSM_WIDTH), lambda i: (i, 0)),
        out_shape=jax.ShapeDtypeStruct((n, SSM_WIDTH), BF16),
        compiler_params=_cparams(("arbitrary",), 32),
        name="ssm_output",
    )(p, yf, yb, ssm_d.reshape(DEPTH, 1, SSM_WIDTH), glu_w, glu_b.reshape(DEPTH, 1, SSM_WIDTH))


def _seq_tile(tile):
    lat_tiles = N_LAT // SEQ_T
    per_seq = SEQ // SEQ_T
    is_lat = tile < lat_tiles
    t = jnp.where(is_lat, tile % per_seq, (tile - lat_tiles) % (CTX_LEN // SEQ_T))
    last_t = jnp.where(is_lat, per_seq - 1, CTX_LEN // SEQ_T - 1)
    return t * SEQ_T, t == 0, t == last_t, jnp.where(is_lat, SEQ, CTX_LEN)


def _with_halo(prev_ref, cur_ref, next_ref, first, last):
    top = jnp.where(first, jnp.zeros((SEQ_PAD, cur_ref.shape[1]), cur_ref.dtype), prev_ref[SEQ_T - SEQ_PAD:SEQ_T, :])
    bot = jnp.where(last, jnp.zeros((SEQ_PAD, cur_ref.shape[1]), cur_ref.dtype), next_ref[0:SEQ_PAD, :])
    return jnp.concatenate([top, cur_ref[...], bot], axis=0)


def _halo_specs(width, col_block, n_tiles):
    return [pl.BlockSpec((SEQ_T, width), lambda g: (jnp.maximum(g - 1, 0), col_block)),
            pl.BlockSpec((SEQ_T, width), lambda g: (g, col_block)),
            pl.BlockSpec((SEQ_T, width), lambda g: (jnp.minimum(g + 1, n_tiles - 1), col_block))]


def _pool_kernel(prev_ref, cur_ref, next_ref, w_ref, s_ref, o_ref):
    t0, first, last, seq_len = _seq_tile(pl.program_id(0))
    halo = _with_halo(prev_ref, cur_ref, next_ref, first, last)
    centre = cur_ref[...].astype(F32)
    tt = lax.broadcasted_iota(jnp.int32, (SEQ_T, SEQ_T + 2 * SEQ_PAD), 0)
    rr = lax.broadcasted_iota(jnp.int32, (SEQ_T, SEQ_T + 2 * SEQ_PAD), 1) - SEQ_PAD
    tg = t0 + lax.broadcasted_iota(jnp.int32, (SEQ_T, 1), 0)
    for gi, w in enumerate(POOL_WINDOWS):
        cols = slice(gi * POOL_GROUP, (gi + 1) * POOL_GROUP)
        band = ((rr >= tt - w // 2) & (rr <= tt + w // 2 - 1)).astype(F32).astype(BF16)
        wsum = jnp.dot(band, halo[:, cols], preferred_element_type=F32)
        cnt = jnp.minimum(tg - w // 2 + w, seq_len) - jnp.maximum(tg - w // 2, 0)
        pooled = wsum / cnt.astype(F32) - centre[:, cols]
        mixed = jnp.dot(pooled.astype(BF16), w_ref[0, gi], preferred_element_type=F32)
        o_ref[:, cols] = (mixed * s_ref[0, :, cols]).astype(o_ref.dtype)


def pool_mixer(p, pool_w, pool_scale, layer):
    n_tiles = p.shape[0] // SEQ_T
    ng = len(POOL_WINDOWS)
    return pl.pallas_call(
        _pool_kernel,
        grid=(n_tiles,),
        in_specs=_halo_specs(POOL_WIDTH, COL_POOL // POOL_WIDTH, n_tiles) + [
            pl.BlockSpec((1, ng, POOL_GROUP, POOL_GROUP), lambda g: (layer, 0, 0, 0)),
            pl.BlockSpec((1, 1, POOL_WIDTH), lambda g: (layer, 0, 0)),
        ],
        out_specs=pl.BlockSpec((SEQ_T, POOL_WIDTH), lambda g: (g, 0)),
        out_shape=jax.ShapeDtypeStruct((p.shape[0], POOL_WIDTH), BF16),
        compiler_params=_cparams(("arbitrary",), 32),
        name="pool_mixer",
    )(p, p, p, pool_w, pool_scale.reshape(DEPTH, 1, POOL_WIDTH))


CONV_RB = 64


def _conv_kernel(ap_ref, ac_ref, an_ref, gp_ref, gc_ref, gn_ref, w_ref, b_ref, g_ref, beta_ref, o_ref, glu_ref):
    _, first, last, _ = _seq_tile(pl.program_id(0))
    a = _with_halo(ap_ref, ac_ref, an_ref, first, last).astype(F32)
    g = _with_halo(gp_ref, gc_ref, gn_ref, first, last).astype(F32)
    glu_ref[...] = a * jax.nn.sigmoid(g)
    off = SEQ_PAD - CONV_TAPS // 2
    for rb in range(SEQ_T // CONV_RB):
        parts = []
        for cb in range(CONV_WIDTH // LANES):
            cols = slice(cb * LANES, (cb + 1) * LANES)
            acc = jnp.zeros((CONV_RB, LANES), F32)
            for k in range(CONV_TAPS):
                acc = acc + glu_ref[rb * CONV_RB + k + off:rb * CONV_RB + k + off + CONV_RB, cols] * w_ref[0, k:k + 1, cols]
            parts.append(acc)
        y = jnp.concatenate(parts, axis=-1) + b_ref[0]
        yc = y - jnp.mean(y, axis=-1, keepdims=True)
        yn = yc * lax.rsqrt(jnp.mean(yc * yc, axis=-1, keepdims=True) + EPS)
        yn = yn * g_ref[0] + beta_ref[0]
        o_ref[rb * CONV_RB:(rb + 1) * CONV_RB, :] = _silu(yn).astype(o_ref.dtype)


def conv_mixer(p, dw_w, dw_b, ln_g, ln_b, layer):
    n_tiles = p.shape[0] // SEQ_T
    vec = lambda a: a.reshape(DEPTH, 1, CONV_WIDTH)
    vspec = pl.BlockSpec((1, 1, CONV_WIDTH), lambda g: (layer, 0, 0))
    a_block = COL_CONV // CONV_WIDTH
    return pl.pallas_call(
        _conv_kernel,
        grid=(n_tiles,),
        in_specs=_halo_specs(CONV_WIDTH, a_block, n_tiles) + _halo_specs(CONV_WIDTH, a_block + 1, n_tiles) + [
            pl.BlockSpec((1, CONV_TAPS, CONV_WIDTH), lambda g: (layer, 0, 0)),
            vspec, vspec, vspec,
        ],
        out_specs=pl.BlockSpec((SEQ_T, CONV_WIDTH), lambda g: (g, 0)),
        out_shape=jax.ShapeDtypeStruct((p.shape[0], CONV_WIDTH), BF16),
        scratch_shapes=[pltpu.VMEM((SEQ_T + 2 * SEQ_PAD, CONV_WIDTH), F32)],
        compiler_params=_cparams(("arbitrary",), 40),
        name="conv_mixer",
    )(p, p, p, p, p, p, dw_w, vec(dw_b), vec(ln_g), vec(ln_b))


def _merge_kernel(oa_ref, ys_ref, yp_ref, yc_ref, g0_ref, g1_ref, g2_ref, g3_ref,
                  wa_ref, ws_ref, wp_ref, wc_ref, o_ref):
    def branch(x_ref, w_ref, g_ref):
        up = jnp.dot(x_ref[...], w_ref[0, 0], preferred_element_type=F32)
        return jax.nn.sigmoid(g_ref[...].astype(F32)) * up

    m = (branch(oa_ref, wa_ref, g0_ref) + branch(ys_ref, ws_ref, g1_ref)
         + branch(yp_ref, wp_ref, g2_ref) + branch(yc_ref, wc_ref, g3_ref))
    o_ref[...] = m.astype(o_ref.dtype)


def merge_branches(p, o_attn, y_ssm, y_pool, y_conv, w_attn, w_ssm, w_pool, w_conv, layer):
    n = o_attn.shape[0]
    nct = D_MODEL // TN
    gate_tile0 = COL_GATE // TN

    def gate_spec(br):
        return pl.BlockSpec((TM, TN), lambda i, j: (i, gate_tile0 + br * nct + j))

    def x_spec(width):
        return pl.BlockSpec((TM, width), lambda i, j: (i, 0))

    def w_spec(width):
        return pl.BlockSpec((1, 1, width, TN), lambda i, j: (layer, j, 0, 0))

    return pl.pallas_call(
        _merge_kernel,
        grid=(n // TM, nct),
        in_specs=[x_spec(Q_W), x_spec(SSM_WIDTH), x_spec(POOL_WIDTH), x_spec(CONV_WIDTH),
                  gate_spec(0), gate_spec(1), gate_spec(2), gate_spec(3),
                  w_spec(Q_W), w_spec(SSM_WIDTH), w_spec(POOL_WIDTH), w_spec(CONV_WIDTH)],
        out_specs=pl.BlockSpec((TM, TN), lambda i, j: (i, j)),
        out_shape=jax.ShapeDtypeStruct((n, D_MODEL), BF16),
        compiler_params=_cparams(("arbitrary", "arbitrary"), 40),
        name="merge_branches",
    )(o_attn, y_ssm, y_pool, y_conv, p, p, p, p, w_attn, w_ssm, w_pool, w_conv)


def _split_bf16(x):
    hi = x.astype(BF16)
    lo = (x - hi.astype(F32)).astype(BF16)
    return hi, lo


SLAB = D_MODEL // LANES


def _store_row_slabs(ref, row0, x):
    n = x.shape[0]
    for j in range(SLAB):
        ref[pl.ds(row0 * SLAB + j, n, stride=SLAB), :] = x[:, j * LANES:(j + 1) * LANES]


def _load_row_slabs(ref, row0, n):
    return jnp.concatenate([ref[pl.ds(row0 * SLAB + j, n, stride=SLAB), :] for j in range(SLAB)], axis=-1)


def _outproj_kernel(m_ref, x_ref, mod_ref, g_ref, w_ref, rw_ref, xo_ref, h_ref, lg_ref):
    r_hi, r_lo = _split_bf16(rw_ref[...])
    nt = (((1,), (1,)), ((), ()))
    half = m_ref.shape[0] // 2
    for c in range(2):
        rows = slice(c * half, (c + 1) * half)
        mix = jnp.dot(m_ref[rows, :], w_ref[0], preferred_element_type=F32)
        x = x_ref[rows, :] + mod_ref[0, 2:3, :] * mix
        xo_ref[rows, :] = x
        ms = jnp.mean(x * x, axis=-1, keepdims=True)
        h = x * lax.rsqrt(ms + EPS) * g_ref[0]
        h = h * (1.0 + mod_ref[0, 4:5, :]) + mod_ref[0, 3:4, :]
        _store_row_slabs(h_ref, c * half, h)
        h_hi, h_lo = _split_bf16(h)
        lg_ref[:, rows] = (lax.dot_general(r_hi, h_hi, nt, preferred_element_type=F32)
                           + lax.dot_general(r_hi, h_lo, nt, preferred_element_type=F32)
                           + lax.dot_general(r_lo, h_hi, nt, preferred_element_type=F32))


def out_projection(m, xa, mod, norm2_g, w_out, router_wt, layer):
    n = m.shape[0]
    tmo = TM
    return pl.pallas_call(
        _outproj_kernel,
        grid=(n // tmo,),
        in_specs=[
            pl.BlockSpec((tmo, D_MODEL), lambda i: (i, 0)),
            pl.BlockSpec((tmo, D_MODEL), lambda i: (i, 0)),
            pl.BlockSpec((1, 6, D_MODEL), lambda i: (_mod_id(i), 0, 0)),
            pl.BlockSpec((1, 1, D_MODEL), lambda i: (layer, 0, 0)),
            pl.BlockSpec((1, D_MODEL, D_MODEL), lambda i: (layer, 0, 0)),
            pl.BlockSpec((N_EXPERTS, D_MODEL), lambda i: (0, 0)),
        ],
        out_specs=[
            pl.BlockSpec((tmo, D_MODEL), lambda i: (i, 0)),
            pl.BlockSpec((tmo * SLAB, LANES), lambda i: (i, 0)),
            pl.BlockSpec((N_EXPERTS, tmo), lambda i: (0, i)),
        ],
        out_shape=[
            jax.ShapeDtypeStruct((n, D_MODEL), F32),
            jax.ShapeDtypeStruct((n * SLAB, LANES), F32),
            jax.ShapeDtypeStruct((N_EXPERTS, n), F32),
        ],
        compiler_params=_cparams(("arbitrary",), 60),
        name="out_projection",
    )(m, xa, mod, norm2_g.reshape(DEPTH, 1, D_MODEL), w_out, router_wt)


def _first_argmax(blk, row):
    m = jnp.max(blk, axis=0, keepdims=True)
    idx = jnp.min(jnp.where(blk == m, row, EXPERTS_PER_GROUP), axis=0, keepdims=True)
    return m, idx


def _route_kernel(lg_ref, rb_ref, e_ref, w_ref, rank_ref, cnt_ref, run_ref):
    @pl.when(pl.program_id(0) == 0)
    def _():
        run_ref[...] = jnp.zeros_like(run_ref)

    t = lg_ref.shape[1]
    scores = jax.nn.sigmoid(lg_ref[...])
    sel = scores + rb_ref[...]
    row = lax.broadcasted_iota(jnp.int32, (EXPERTS_PER_GROUP, t), 0)
    neg = jnp.float32(-jnp.inf)

    best = None
    for g in range(N_EXPERT_GROUPS):
        blk = sel[g * EXPERTS_PER_GROUP:(g + 1) * EXPERTS_PER_GROUP, :]
        m1, i1 = _first_argmax(blk, row)
        m2 = jnp.max(jnp.where(row == i1, neg, blk), axis=0, keepdims=True)
        gs = m1 + m2
        if best is None:
            best, grp = gs, jnp.zeros((1, t), jnp.int32)
        else:
            better = gs > best
            best = jnp.where(better, gs, best)
            grp = jnp.where(better, g, grp)

    in_sel = jnp.zeros((EXPERTS_PER_GROUP, t), F32)
    in_sc = jnp.zeros((EXPERTS_PER_GROUP, t), F32)
    for g in range(N_EXPERT_GROUPS):
        rows = slice(g * EXPERTS_PER_GROUP, (g + 1) * EXPERTS_PER_GROUP)
        in_sel = jnp.where(grp == g, sel[rows, :], in_sel)
        in_sc = jnp.where(grp == g, scores[rows, :], in_sc)
    _, i1 = _first_argmax(in_sel, row)
    _, i2 = _first_argmax(jnp.where(row == i1, neg, in_sel), row)
    s1 = jnp.sum(jnp.where(row == i1, in_sc, 0.0), axis=0, keepdims=True)
    s2 = jnp.sum(jnp.where(row == i2, in_sc, 0.0), axis=0, keepdims=True)
    e1 = grp * EXPERTS_PER_GROUP + i1
    e2 = grp * EXPERTS_PER_GROUP + i2
    e_ref[0:1, :] = e1
    e_ref[1:2, :] = e2
    w_ref[0:1, :] = s1 / (s1 + s2)
    w_ref[1:2, :] = s2 / (s1 + s2)

    erow = lax.broadcasted_iota(jnp.int32, (N_EXPERTS, t), 0)
    oh1 = erow == e1
    oh2 = erow == e2
    cnt = jnp.where(oh1 | oh2, 1.0, 0.0)
    a = lax.broadcasted_iota(jnp.int32, (t, t), 0)
    b = lax.broadcasted_iota(jnp.int32, (t, t), 1)
    before = jnp.where(a < b, 1.0, 0.0).astype(BF16)
    excl = jnp.dot(cnt.astype(BF16), before, preferred_element_type=F32) + run_ref[...]
    rank_ref[0:1, :] = jnp.sum(jnp.where(oh1, excl, 0.0), axis=0, keepdims=True).astype(jnp.int32)
    rank_ref[1:2, :] = jnp.sum(jnp.where(oh2, excl, 0.0), axis=0, keepdims=True).astype(jnp.int32)
    run_ref[...] = run_ref[...] + jnp.sum(cnt, axis=1, keepdims=True)
    cnt_ref[...] = run_ref[...]


def route(logits_t, router_b):
    n = logits_t.shape[1]
    return pl.pallas_call(
        _route_kernel,
        grid=(n // ROUTE_T,),
        in_specs=[
            pl.BlockSpec((N_EXPERTS, ROUTE_T), lambda i: (0, i)),
            pl.BlockSpec((N_EXPERTS, 1), lambda i: (0, 0)),
        ],
        out_specs=[
            pl.BlockSpec((TOP_K, ROUTE_T), lambda i: (0, i)),
            pl.BlockSpec((TOP_K, ROUTE_T), lambda i: (0, i)),
            pl.BlockSpec((TOP_K, ROUTE_T), lambda i: (0, i)),
            pl.BlockSpec((N_EXPERTS, 1), lambda i: (0, 0)),
        ],
        out_shape=[
            jax.ShapeDtypeStruct((TOP_K, n), jnp.int32),
            jax.ShapeDtypeStruct((TOP_K, n), F32),
            jax.ShapeDtypeStruct((TOP_K, n), jnp.int32),
            jax.ShapeDtypeStruct((N_EXPERTS, 1), F32),
        ],
        scratch_shapes=[pltpu.VMEM((N_EXPERTS, 1), F32)],
        compiler_params=_cparams(("arbitrary",), 32),
        name="route",
    )(logits_t, router_b.reshape(N_EXPERTS, 1).astype(F32))


def _expert_kernel(be_ref, slot_ref, nxt_ref, nu_ref, tok_ref, dst_ref, h_hbm, wg_hbm, wu_hbm, wd_hbm, o_hbm,
                   xbuf_ref, ybuf_ref, wgf_ref, wuf_ref, wdf_ref, wgb_ref, wub_ref, wdb_ref, sem, xsem, ysem,
                   *, layer):
    b = pl.program_id(0)
    n_used = nu_ref[0]
    e = be_ref[b]
    fresh = (b == 0) | (e != be_ref[jnp.maximum(b - 1, 0)])
    live = b < n_used
    has_next = b + 1 < n_used
    cur = b % 2

    def slab(row):
        return pl.ds(pl.multiple_of(row * SLAB, SLAB), SLAB)

    def row_gathers(block, buf):
        return [pltpu.make_async_copy(h_hbm.at[slab(tok_ref[block * MOE_BLOCK + r])],
                                      xbuf_ref.at[buf, slab(r)], xsem.at[buf]) for r in range(MOE_BLOCK)]

    def row_scatters(block, buf):
        return [pltpu.make_async_copy(ybuf_ref.at[buf, slab(r)],
                                      o_hbm.at[slab(dst_ref[block * MOE_BLOCK + r])], ysem.at[buf])
                for r in range(MOE_BLOCK)]

    @pl.when(b == 0)
    def _():
        for cp in row_gathers(0, 0):
            cp.start()

    def weight_copies(expert, slot):
        return (pltpu.make_async_copy(wg_hbm.at[layer, expert], wgf_ref.at[slot], sem.at[slot]),
                pltpu.make_async_copy(wu_hbm.at[layer, expert], wuf_ref.at[slot], sem.at[slot]),
                pltpu.make_async_copy(wd_hbm.at[layer, expert], wdf_ref.at[slot], sem.at[slot]))

    @pl.when(b == 0)
    def _():
        for cp in weight_copies(e, 0):
            cp.start(priority=WEIGHT_DMA_PRIORITY)

    @pl.when(fresh & live)
    def _():
        slot = slot_ref[b]
        nxt = nxt_ref[b]

        @pl.when(nxt >= 0)
        def _():
            for cp in weight_copies(nxt, 1 - slot):
                cp.start(priority=WEIGHT_DMA_PRIORITY)

        for cp in weight_copies(e, slot):
            cp.wait()
        wgb_ref[...] = wgf_ref[slot].astype(BF16)
        wub_ref[...] = wuf_ref[slot].astype(BF16)
        wdb_ref[...] = wdf_ref[slot].astype(BF16)

    @pl.when(live & (b >= 2))
    def _():
        for cp in row_scatters(b - 2, cur):
            cp.wait()

    def block(scatter_prev, gather_next):
        for cp in row_gathers(b, cur):
            cp.wait()
        if scatter_prev:
            for cp in row_scatters(b - 1, 1 - cur):
                cp.start(priority=WEIGHT_DMA_PRIORITY)
        if gather_next:
            for cp in row_gathers(b + 1, 1 - cur):
                cp.start()
        x = _load_row_slabs(xbuf_ref.at[cur], 0, MOE_BLOCK).astype(BF16)
        gate = jnp.dot(x, wgb_ref[...], preferred_element_type=F32)
        up = jnp.dot(x, wub_ref[...], preferred_element_type=F32)
        hid = (_silu(gate) * up).astype(BF16)
        _store_row_slabs(ybuf_ref.at[cur], 0, jnp.dot(hid, wdb_ref[...], preferred_element_type=F32))

    @pl.when(b == 0)
    def _():
        block(False, True)

    @pl.when(live & (b >= 1) & has_next)
    def _():
        block(True, True)

    @pl.when(live & (b >= 1) & jnp.logical_not(has_next))
    def _():
        block(True, False)

    @pl.when(b == n_used)
    def _():
        last = n_used - 1
        for cp in row_scatters(last, last % 2):
            cp.start(priority=WEIGHT_DMA_PRIORITY)
        for cp in row_scatters(last - 1, 1 - last % 2):
            cp.wait()
        for cp in row_scatters(last, last % 2):
            cp.wait()

    @pl.when(jnp.logical_not(live))
    def _():
        ybuf_ref[0] = jnp.zeros((MOE_BLOCK * SLAB, LANES), F32)
        rows = pl.ds(pl.multiple_of(b * (MOE_BLOCK * SLAB), MOE_BLOCK * SLAB), MOE_BLOCK * SLAB)
        cp = pltpu.make_async_copy(ybuf_ref.at[0], o_hbm.at[rows], ysem.at[0])
        cp.start()
        cp.wait()


def _slot_maps(dest_flat, n, n_slots):
    i32 = jnp.int32
    owner = jnp.full((n_slots,), -1, i32).at[dest_flat].set(jnp.arange(TOP_K * n, dtype=i32), unique_indices=True)
    is_pad = owner < 0
    tok = jnp.where(is_pad, 0, owner % n)
    dst = jnp.where(is_pad, TOP_K * n + jnp.cumsum(is_pad.astype(i32)) - 1, owner)
    return tok.astype(i32), dst.astype(i32)


def experts(blk_e, blk_slot, blk_next, n_used, slot_tok, slot_dst, h2, w_gate, w_up, w_down, layer):
    n = h2.shape[0] // SLAB
    n_slots = slot_tok.shape[0]
    n_blocks = n_slots // MOE_BLOCK
    assert n_blocks - N_EXPERTS >= 2
    assert n_slots == TOP_K * n + N_EXPERTS * MOE_BLOCK
    grid_spec = pltpu.PrefetchScalarGridSpec(
        num_scalar_prefetch=6,
        grid=(n_blocks,),
        in_specs=[
            pl.BlockSpec(memory_space=pl.ANY),
            pl.BlockSpec(memory_space=pl.ANY),
            pl.BlockSpec(memory_space=pl.ANY),
            pl.BlockSpec(memory_space=pl.ANY),
        ],
        out_specs=pl.BlockSpec(memory_space=pl.ANY),
        scratch_shapes=[
            pltpu.VMEM((2, MOE_BLOCK * SLAB, LANES), F32),
            pltpu.VMEM((2, MOE_BLOCK * SLAB, LANES), F32),
            pltpu.VMEM((2, D_MODEL, EXPERT_FF), F32),
            pltpu.VMEM((2, D_MODEL, EXPERT_FF), F32),
            pltpu.VMEM((2, EXPERT_FF, D_MODEL), F32),
            pltpu.VMEM((D_MODEL, EXPERT_FF), BF16),
            pltpu.VMEM((D_MODEL, EXPERT_FF), BF16),
            pltpu.VMEM((EXPERT_FF, D_MODEL), BF16),
            pltpu.SemaphoreType.DMA((2,)),
            pltpu.SemaphoreType.DMA((2,)),
            pltpu.SemaphoreType.DMA((2,)),
        ],
    )
    return pl.pallas_call(
        functools.partial(_expert_kernel, layer=layer),
        grid_spec=grid_spec,
        out_shape=jax.ShapeDtypeStruct((n_slots * SLAB, LANES), F32),
        compiler_params=_cparams(("arbitrary",), 56),
        name="experts",
    )(blk_e, blk_slot, blk_next, n_used, slot_tok, slot_dst, h2, w_gate, w_up, w_down)


def _combine_kernel(x_ref, mod_ref, w_ref, fg_ref, y0_ref, y1_ref, o_ref, *, final_norm):
    f = w_ref[:, 0:1] * _load_row_slabs(y0_ref, 0, TM) + w_ref[:, 1:2] * _load_row_slabs(y1_ref, 0, TM)
    x = x_ref[...] + mod_ref[0, 5:6, :] * f
    if final_norm:
        ms = jnp.mean(x * x, axis=-1, keepdims=True)
        x = x * lax.rsqrt(ms + EPS) * fg_ref[...]
    o_ref[...] = x


def combine(xa, mod, wts_rows, ys, final_g, final_norm):
    n = xa.shape[0]
    n_tiles = n // TM
    kern = functools.partial(_combine_kernel, final_norm=final_norm)
    return pl.pallas_call(
        kern,
        grid=(n_tiles,),
        in_specs=[
            pl.BlockSpec((TM, D_MODEL), lambda i: (i, 0)),
            pl.BlockSpec((1, 6, D_MODEL), lambda i: (_mod_id(i), 0, 0)),
            pl.BlockSpec((TM, TOP_K), lambda i: (i, 0)),
            pl.BlockSpec((1, D_MODEL), lambda i: (0, 0)),
            pl.BlockSpec((TM * SLAB, LANES), lambda i: (i, 0)),
            pl.BlockSpec((TM * SLAB, LANES), lambda i: (n_tiles + i, 0)),
        ],
        out_specs=pl.BlockSpec((TM, D_MODEL), lambda i: (i, 0)),
        out_shape=jax.ShapeDtypeStruct((n, D_MODEL), F32),
        compiler_params=_cparams(("arbitrary",), 48),
        name="combine",
    )(xa, mod, wts_rows, final_g.reshape(1, D_MODEL), ys, ys)


def _slot_plan(e_idx, rank, counts, n_blocks):
    i32 = jnp.int32
    cnt = counts.reshape(N_EXPERTS).astype(i32)
    padded = (cnt + MOE_BLOCK - 1) // MOE_BLOCK * MOE_BLOCK
    pad_end = jnp.cumsum(padded)
    pad_start = pad_end - padded
    eids = jnp.arange(N_EXPERTS, dtype=i32)

    def lookup(table, idx):
        return jnp.sum(jnp.where(idx[..., None] == eids, table, 0), axis=-1).astype(i32)

    dest = lookup(pad_start, e_idx) + rank
    blk_start = jnp.arange(n_blocks, dtype=i32) * MOE_BLOCK
    blk_e = jnp.minimum(jnp.sum(pad_end[None, :] <= blk_start[:, None], axis=1), N_EXPERTS - 1).astype(i32)
    n_used = (pad_end[-1] // MOE_BLOCK).astype(i32).reshape(1)
    has = cnt > 0
    run = jnp.cumsum(has.astype(i32)) - 1
    first_from = lax.cummin(jnp.where(has, eids, N_EXPERTS)[::-1])[::-1]
    nxt = jnp.concatenate([first_from[1:], jnp.full((1,), N_EXPERTS, i32)])
    nxt = jnp.where(nxt >= N_EXPERTS, -1, nxt)
    return dest.reshape(-1).astype(i32), blk_e, lookup(run % 2, blk_e), lookup(nxt, blk_e), n_used


def _col_tiles(w):
    d, k, n = w.shape
    return w.astype(BF16).reshape(d, k, n // TN, TN).transpose(0, 2, 1, 3)


def _layer(layer, xa, mod, W, with_ctx, final_norm):
    n_lat_tiles = N_LAT // TM
    n_ctx_tiles = N_CTX // TM
    if with_ctx:
        p = in_projection(xa, mod, W['norm1_g'], W['w_in'], layer, 0, n_lat_tiles + n_ctx_tiles, IN_W, TN_IN)
        p_lat, p_ctx = p[:N_LAT], p[N_LAT:]
    else:
        p = in_projection(xa, mod, W['norm1_g'], W['w_in'], layer, 0, n_lat_tiles, IN_W, TN_IN)
        p_lat = p
        p_ctx = in_projection(xa, mod, W['norm1_g'], W['w_in'], layer, n_lat_tiles, n_ctx_tiles, CTX_STATE_W, TN)
    n_rows = p.shape[0]

    def seq(a, lo, hi, n_seq_rows):
        return a[:, lo:hi].reshape(BATCH, n_seq_rows, hi - lo)

    kv = jnp.concatenate([seq(p_ctx, 0, 2 * KV_W, CTX_LEN), seq(p_lat, 0, 2 * KV_W, SEQ)], axis=1)
    cos, sin = _rope_tables(CTX_LEN)
    o_attn = attention(p, COL_Q, 0, SEQ, kv, cos[CTX_LEN:], sin[CTX_LEN:], cos, sin,
                       W['q_norm_g'], W['k_norm_g'], layer)
    if with_ctx:
        kv_c = seq(p_ctx, 0, 2 * KV_W, CTX_LEN)
        ones, zeros = jnp.ones((CTX_LEN, HEAD_DIM), F32), jnp.zeros((CTX_LEN, HEAD_DIM), F32)
        o_attn_c = attention(p, COL_Q, N_LAT, CTX_LEN, kv_c, ones, zeros, ones, zeros,
                             W['q_norm_g'], W['k_norm_g'], layer)
        o_attn = jnp.concatenate([o_attn, o_attn_c], axis=0)

    u_seq = jnp.concatenate([seq(p_ctx, COL_SSM, COL_SSM + SSM_WIDTH, CTX_LEN),
                             seq(p_lat, COL_SSM, COL_SSM + SSM_WIDTH, SEQ)], axis=1)
    u_chunks = u_seq.reshape(BATCH * SSM_CHUNKS, SSM_T, SSM_WIDTH)
    bm, cm, lam_t, pw_t = _ssm_params(W['ssm_a_re'][layer], W['ssm_a_im'][layer], W['ssm_log_dt'][layer],
                                      W['ssm_b_re'][layer], W['ssm_b_im'][layer],
                                      W['ssm_c_re'][layer], W['ssm_c_im'][layer])
    yf = ssm_scan(u_chunks, bm[0], cm[0], lam_t[0], pw_t[0], reverse=False)
    yb = ssm_scan(u_chunks, bm[1], cm[1], lam_t[1], pw_t[1], reverse=True)
    y_ssm = ssm_output(p, yf, yb, W['ssm_d'], W['ssm_glu_w'], W['ssm_glu_b'], layer)

    y_pool = pool_mixer(p, W['pool_w'], W['pool_scale'], layer)
    y_conv = conv_mixer(p, W['conv_dw_w'], W['conv_dw_b'], W['conv_ln_g'], W['conv_ln_b'], layer)

    m = merge_branches(p, o_attn, y_ssm, y_pool, y_conv,
                       W['w_up_attn'], W['w_up_ssm'], W['w_up_pool'], W['w_up_conv'], layer)
    x1, h2, logits_t = out_projection(m, xa, mod, W['norm2_g'], W['w_out'], W['router_wt'], layer)

    e_idx, wts, rank, counts = route(logits_t, W['router_b'])
    n_blocks = -(-n_rows * TOP_K // MOE_BLOCK) + N_EXPERTS
    dest_flat, blk_e, blk_slot, blk_next, n_used = _slot_plan(e_idx, rank, counts, n_blocks)
    slot_tok, slot_dst = _slot_maps(dest_flat, n_rows, n_blocks * MOE_BLOCK)
    ys = experts(blk_e, blk_slot, blk_next, n_used, slot_tok, slot_dst, h2,
                 W['moe_w_gate'], W['moe_w_up'], W['moe_w_down'], layer)
    x2 = combine(x1, mod, wts.T, ys, W['final_g'], final_norm)
    if with_ctx:
        return x2
    return jnp.concatenate([x2, xa[N_LAT:]], axis=0)


def kernel(x, c, ctx, c_ctx, ada_w, ada_b, norm1_g, norm2_g, w_in, q_norm_g, k_norm_g, ssm_a_re, ssm_a_im, ssm_log_dt, ssm_b_re, ssm_b_im, ssm_c_re, ssm_c_im, ssm_d, ssm_glu_w, ssm_glu_b, pool_w, pool_scale, conv_dw_w, conv_dw_b, conv_ln_g, conv_ln_b, w_up_attn, w_up_ssm, w_up_pool, w_up_conv, w_out, router_w, router_b, moe_w_gate, moe_w_up, moe_w_down, final_g):
    W = dict(
        norm1_g=norm1_g, norm2_g=norm2_g, q_norm_g=q_norm_g, k_norm_g=k_norm_g,
        w_in=w_in.astype(BF16),
        ssm_a_re=ssm_a_re, ssm_a_im=ssm_a_im, ssm_log_dt=ssm_log_dt, ssm_b_re=ssm_b_re, ssm_b_im=ssm_b_im,
        ssm_c_re=ssm_c_re, ssm_c_im=ssm_c_im, ssm_d=ssm_d, ssm_glu_w=ssm_glu_w.astype(BF16), ssm_glu_b=ssm_glu_b,
        pool_w=pool_w.astype(BF16), pool_scale=pool_scale,
        conv_dw_w=conv_dw_w, conv_dw_b=conv_dw_b, conv_ln_g=conv_ln_g, conv_ln_b=conv_ln_b,
        w_up_attn=_col_tiles(w_up_attn), w_up_ssm=_col_tiles(w_up_ssm), w_up_pool=_col_tiles(w_up_pool),
        w_up_conv=_col_tiles(w_up_conv), w_out=w_out.astype(BF16),
        router_wt=router_w.T, router_b=router_b,
        moe_w_gate=moe_w_gate, moe_w_up=moe_w_up, moe_w_down=moe_w_down, final_g=final_g,
    )
    cc = jnp.concatenate([c, c_ctx[None, :], jnp.zeros((SUBLANES - BATCH - 1, D_MODEL), F32)], axis=0)
    mod_all = ada_modulation(cc, ada_w, ada_b)
    xa = jnp.concatenate([x.reshape(N_LAT, D_MODEL), ctx.reshape(N_CTX, D_MODEL)], axis=0)
    for layer in range(DEPTH):
        mod = mod_all[layer, :BATCH + 1].reshape(BATCH + 1, 6, D_MODEL)
        xa = _layer(layer, xa, mod, W, with_ctx=(layer < DEPTH - 1), final_norm=(layer == DEPTH - 1))
    return xa[:N_LAT].reshape(BATCH, SEQ, D_MODEL)
```

```python
import functools
import math

import jax
import jax.numpy as jnp
from jax import lax
from jax.experimental import pallas as pl
from jax.experimental.pallas import tpu as pltpu

F32 = jnp.float32
BF16 = jnp.bfloat16

D_MODEL = 2048
BATCH = 2
SEQ = 4096
DEPTH = 2
GRID_W = 64
CTX_LEN = 256
N_HEADS = 8
N_KV_HEADS = 2
HEAD_DIM = 128
ROPE_THETA = 10000.0
SSM_WIDTH = 512
SSM_GROUP = 16
SSM_GROUPS = SSM_WIDTH // SSM_GROUP
SSM_STATE = 64
POOL_WIDTH = 512
POOL_WINDOWS = (2, 4, 8, 16)
POOL_GROUP = POOL_WIDTH // len(POOL_WINDOWS)
CONV_WIDTH = 512
CONV_TAPS = 31
N_BRANCHES = 4
N_EXPERTS = 64
N_EXPERT_GROUPS = 8
EXPERTS_PER_GROUP = N_EXPERTS // N_EXPERT_GROUPS
TOP_K = 2
EXPERT_FF = 512
EPS = 1e-6

Q_W = N_HEADS * HEAD_DIM
KV_W = N_KV_HEADS * HEAD_DIM
IN_W = 2 * KV_W + SSM_WIDTH + Q_W + POOL_WIDTH + 2 * CONV_WIDTH + N_BRANCHES * D_MODEL
CTX_STATE_W = 2 * KV_W + SSM_WIDTH
COL_K, COL_V, COL_SSM, COL_Q = 0, KV_W, 2 * KV_W, 2 * KV_W + SSM_WIDTH
COL_POOL = COL_Q + Q_W
COL_CONV = COL_POOL + POOL_WIDTH
COL_GATE = COL_CONV + 2 * CONV_WIDTH

N_LAT = BATCH * SEQ
N_CTX = BATCH * CTX_LEN
S_ALL = CTX_LEN + SEQ

V7X_VMEM_BYTES = 64 * 1024 * 1024
SUBLANES = 8
LANES = 128
BF16_ROWS = 16

TM = 512
TN = 512
TN_IN = IN_W // 4
TILES_PER_BATCH = SEQ // TM
TQ = 512
KEY_PREP_T = 256
SSM_T = 256
SSM_TC = SSM_T // SUBLANES
SSM_HALF = SSM_WIDTH // 2
SSM_HSTATE = SSM_GROUPS // 2 * SSM_STATE
SEQ_T = 256
SEQ_PAD = 16
MOE_BLOCK = 128
WEIGHT_SLOTS = 3
WEIGHT_DMA_PRIORITY = 1
ROUTE_T = 512


def _cparams(sem, vmem_mb):
    return pltpu.CompilerParams(dimension_semantics=sem, vmem_limit_bytes=vmem_mb * 1024 * 1024)


def _mod_id(tile):
    return jnp.minimum(tile // TILES_PER_BATCH, BATCH)


def _silu(x):
    return x * jax.nn.sigmoid(x)


def _ada_kernel(c_ref, w_ref, b_ref, o_ref):
    c = c_ref[...]
    a = _silu(c).astype(BF16)
    o_ref[0] = jnp.dot(a, w_ref[0].astype(BF16), preferred_element_type=F32) + b_ref[0]


def ada_modulation(cc, ada_w, ada_b):
    tn = 1024
    n6 = 6 * D_MODEL
    return pl.pallas_call(
        _ada_kernel,
        grid=(DEPTH, n6 // tn),
        in_specs=[
            pl.BlockSpec((SUBLANES, D_MODEL), lambda l, j: (0, 0)),
            pl.BlockSpec((1, D_MODEL, tn), lambda l, j: (l, 0, j)),
            pl.BlockSpec((1, 1, tn), lambda l, j: (l, 0, j)),
        ],
        out_specs=pl.BlockSpec((1, SUBLANES, tn), lambda l, j: (l, 0, j)),
        out_shape=jax.ShapeDtypeStruct((DEPTH, SUBLANES, n6), F32),
        compiler_params=_cparams(("arbitrary", "arbitrary"), 40),
        name="ada_modulation",
    )(cc, ada_w, ada_b.reshape(DEPTH, 1, n6))


def _inproj_kernel(x_ref, mod_ref, g_ref, w_ref, o_ref, h_ref):
    @pl.when(pl.program_id(1) == 0)
    def _():
        x = x_ref[...]
        ms = jnp.mean(x * x, axis=-1, keepdims=True)
        y = x * lax.rsqrt(ms + EPS) * g_ref[0]
        h = y * (1.0 + mod_ref[0, 1:2, :]) + mod_ref[0, 0:1, :]
        h_ref[...] = h.astype(BF16)

    o_ref[...] = jnp.dot(h_ref[...], w_ref[0], preferred_element_type=F32).astype(o_ref.dtype)


def in_projection(xa, mod, norm_g, w_in, layer, row_tile0, n_row_tiles, n_cols, tn):
    return pl.pallas_call(
        _inproj_kernel,
        grid=(n_row_tiles, n_cols // tn),
        in_specs=[
            pl.BlockSpec((TM, D_MODEL), lambda i, j: (i + row_tile0, 0)),
            pl.BlockSpec((1, 6, D_MODEL), lambda i, j: (_mod_id(i + row_tile0), 0, 0)),
            pl.BlockSpec((1, 1, D_MODEL), lambda i, j: (layer, 0, 0)),
            pl.BlockSpec((1, D_MODEL, tn), lambda i, j: (layer, 0, j)),
        ],
        out_specs=pl.BlockSpec((TM, tn), lambda i, j: (i, j)),
        out_shape=jax.ShapeDtypeStruct((n_row_tiles * TM, n_cols), BF16),
        scratch_shapes=[pltpu.VMEM((TM, D_MODEL), BF16)],
        compiler_params=_cparams(("arbitrary", "arbitrary"), 56),
        name="in_projection",
    )(xa, mod, norm_g.reshape(DEPTH, 1, D_MODEL), w_in)


def _rope_tables(n_ctx_rows):
    half = HEAD_DIM // 4
    inv_freq = ROPE_THETA ** (-jnp.arange(half, dtype=F32) / half)
    t = jnp.arange(SEQ)
    ang_r = (t // GRID_W).astype(F32)[:, None] * inv_freq[None, :]
    ang_c = (t % GRID_W).astype(F32)[:, None] * inv_freq[None, :]
    cos = jnp.concatenate([jnp.cos(ang_r), jnp.cos(ang_r), jnp.cos(ang_c), jnp.cos(ang_c)], axis=-1)
    sin = jnp.concatenate([-jnp.sin(ang_r), jnp.sin(ang_r), -jnp.sin(ang_c), jnp.sin(ang_c)], axis=-1)
    cos = jnp.concatenate([jnp.ones((n_ctx_rows, HEAD_DIM), F32), cos], axis=0)
    sin = jnp.concatenate([jnp.zeros((n_ctx_rows, HEAD_DIM), F32), sin], axis=0)
    return cos, sin


def _head_norm_rope(x, g, cos, sin):
    ms = jnp.mean(x * x, axis=-1, keepdims=True)
    y = x * lax.rsqrt(ms + EPS) * g
    lane = lax.broadcasted_iota(jnp.int32, y.shape, 1)
    first = (lane % (HEAD_DIM // 2)) < (HEAD_DIM // 4)
    partner = jnp.where(first, pltpu.roll(y, HEAD_DIM - HEAD_DIM // 4, 1), pltpu.roll(y, HEAD_DIM // 4, 1))
    return y * cos + partner * sin


def _attn_kernel(q_ref, k_ref, v_ref, cq_ref, sq_ref, ck_ref, sk_ref, gq_ref, gk_ref, o_ref, ks_ref, va_ref, *,
                 n_keys):
    @pl.when(pl.program_id(2) == 0)
    def _():
        def prep(c, carry):
            r0 = pl.multiple_of(c * KEY_PREP_T, KEY_PREP_T)
            rows = pl.ds(r0, KEY_PREP_T)
            kk = k_ref[0, rows, :].astype(F32)
            kn = _head_norm_rope(kk, gk_ref[0], ck_ref[rows, :], sk_ref[rows, :])
            ks_ref[rows, :] = kn.astype(BF16)
            return carry

        lax.fori_loop(0, n_keys // KEY_PREP_T, prep, 0)
        va_ref[:, 0:HEAD_DIM] = v_ref[0]
        va_ref[:, HEAD_DIM:2 * HEAD_DIM] = jnp.ones((n_keys, HEAD_DIM), BF16)

    k = ks_ref[...]
    va = va_ref[...]
    scale = HEAD_DIM ** -0.5 * math.log2(math.e)
    for hh in range(N_HEADS // N_KV_HEADS):
        cols = slice(hh * HEAD_DIM, (hh + 1) * HEAD_DIM)
        q = q_ref[:, cols].astype(F32)
        qn = _head_norm_rope(q, gq_ref[0], cq_ref[...], sq_ref[...]) * scale
        s = lax.dot_general(qn.astype(BF16), k, (((1,), (1,)), ((), ())), preferred_element_type=F32)
        m = jnp.max(s, axis=-1, keepdims=True)
        p = jnp.exp2(s - m)
        oa = jnp.dot(p.astype(BF16), va, preferred_element_type=F32)
        o = oa[:, 0:HEAD_DIM] / oa[:, HEAD_DIM:HEAD_DIM + 1]
        o_ref[:, cols] = o.astype(o_ref.dtype)


def attention(q_src, q_col0, q_row0, n_q, kv, cos_q, sin_q, cos_k, sin_k, q_norm_g, k_norm_g, layer):
    n_keys = kv.shape[1]
    grp_w = Q_W // N_KV_HEADS
    tq = min(TQ, n_q)
    qb = n_q // tq
    kern = functools.partial(_attn_kernel, n_keys=n_keys)
    return pl.pallas_call(
        kern,
        grid=(BATCH, N_KV_HEADS, qb),
        in_specs=[
            pl.BlockSpec((tq, grp_w), lambda b, g, i: (q_row0 // tq + b * qb + i, q_col0 // grp_w + g)),
            pl.BlockSpec((1, n_keys, HEAD_DIM), lambda b, g, i: (b, 0, g)),
            pl.BlockSpec((1, n_keys, HEAD_DIM), lambda b, g, i: (b, 0, N_KV_HEADS + g)),
            pl.BlockSpec((tq, HEAD_DIM), lambda b, g, i: (i, 0)),
            pl.BlockSpec((tq, HEAD_DIM), lambda b, g, i: (i, 0)),
            pl.BlockSpec((n_keys, HEAD_DIM), lambda b, g, i: (0, 0)),
            pl.BlockSpec((n_keys, HEAD_DIM), lambda b, g, i: (0, 0)),
            pl.BlockSpec((1, 1, HEAD_DIM), lambda b, g, i: (layer, 0, 0)),
            pl.BlockSpec((1, 1, HEAD_DIM), lambda b, g, i: (layer, 0, 0)),
        ],
        out_specs=pl.BlockSpec((tq, grp_w), lambda b, g, i: (b * qb + i, g)),
        out_shape=jax.ShapeDtypeStruct((BATCH * n_q, Q_W), BF16),
        scratch_shapes=[pltpu.VMEM((n_keys, HEAD_DIM), BF16), pltpu.VMEM((n_keys, 2 * HEAD_DIM), BF16)],
        compiler_params=_cparams(("arbitrary", "arbitrary", "arbitrary"), 56),
        name="attention",
    )(q_src, kv, kv, cos_q, sin_q, cos_k, sin_k,
      q_norm_g.reshape(DEPTH, 1, HEAD_DIM), k_norm_g.reshape(DEPTH, 1, HEAD_DIM))


def _cmul(ar, ai, br, bi):
    return ar * br - ai * bi, ar * bi + ai * br


SSM_CB = 512


def _ssm_kernel(u_ref, bm_ref, cm_ref, lam_ref, pw_ref, y_ref, bu_ref, yv_ref, carry_ref, *, reverse):
    @pl.when(pl.program_id(1) == 0)
    def _():
        carry_ref[...] = jnp.zeros_like(carry_ref)

    rr = lax.broadcasted_iota(jnp.int32, (SSM_T, SSM_T), 0)
    tt = lax.broadcasted_iota(jnp.int32, (SSM_T, SSM_T), 1)
    perm = jnp.where(tt == (rr % SUBLANES) * SSM_TC + rr // SUBLANES, 1.0, 0.0).astype(BF16)
    u_scan = jnp.dot(perm, u_ref[0], preferred_element_type=F32).astype(BF16)
    for h in range(2):
        bu_ref[h] = jnp.dot(u_scan[:, h * SSM_HALF:(h + 1) * SSM_HALF], bm_ref[h], preferred_element_type=F32)

    last_row = 0 if reverse else (SSM_T - SUBLANES)
    edge = (SUBLANES - 1) if reverse else 0
    toward = (lambda x, k: pltpu.roll(x, SUBLANES - k, 0)) if reverse else (lambda x, k: pltpu.roll(x, k, 0))

    def step_rows(j):
        jj = (SSM_TC - 1 - j) if reverse else j
        return pl.ds(pl.multiple_of(jj * SUBLANES, SUBLANES), SUBLANES)

    for h, cb in [(h, cb) for h in range(2) for cb in range(SSM_HSTATE // SSM_CB)]:
        re = slice(cb * SSM_CB, (cb + 1) * SSM_CB)
        im = slice(SSM_HSTATE + cb * SSM_CB, SSM_HSTATE + (cb + 1) * SSM_CB)
        lr = lam_ref[h, :, re]
        li = lam_ref[h, :, im]

        def local_scan(j, st, h=h, re=re, im=im, lr=lr, li=li):
            rows = step_rows(j)
            pr, pi = _cmul(lr, li, st[0], st[1])
            nr = pr + bu_ref[h, rows, re]
            ni = pi + bu_ref[h, rows, im]
            bu_ref[h, rows, re] = nr
            bu_ref[h, rows, im] = ni
            return nr, ni

        zero = jnp.zeros((SUBLANES, SSM_CB), F32)
        lax.fori_loop(0, SSM_TC, local_scan, (zero, zero), unroll=4)

        er = bu_ref[h, last_row:last_row + SUBLANES, re]
        ei = bu_ref[h, last_row:last_row + SUBLANES, im]
        row = lax.broadcasted_iota(jnp.int32, (SUBLANES, SSM_CB), 0)
        dist = (SUBLANES - 1 - row) if reverse else row
        xr = jnp.where(row == edge, carry_ref[h, :, re], toward(er, 1))
        xi = jnp.where(row == edge, carry_ref[h, :, im], toward(ei, 1))
        for lvl, k in enumerate((1, 2, 4)):
            ar = pw_ref[h, lvl, :, re]
            ai = pw_ref[h, lvl, :, im]
            sr = jnp.where(dist >= k, toward(xr, k), 0.0)
            si = jnp.where(dist >= k, toward(xi, k), 0.0)
            mr, mi = _cmul(ar, ai, sr, si)
            xr = xr + mr
            xi = xi + mi
        cr, ci = _cmul(pw_ref[h, 0, :, re], pw_ref[h, 0, :, im], xr, xi)
        carry_ref[h, :, re] = toward(cr + er, 1)
        carry_ref[h, :, im] = toward(ci + ei, 1)

        def add_carry(j, g, h=h, re=re, im=im, lr=lr, li=li):
            rows = step_rows(j)
            bu_ref[h, rows, re] = bu_ref[h, rows, re] + g[0]
            bu_ref[h, rows, im] = bu_ref[h, rows, im] + g[1]
            return _cmul(lr, li, g[0], g[1])

        lax.fori_loop(0, SSM_TC, add_carry, _cmul(lr, li, xr, xi), unroll=4)

    for h in range(2):
        y = jnp.dot(bu_ref[h].astype(BF16), cm_ref[h], preferred_element_type=F32)
        for cb in range(SSM_HALF // LANES):
            cols = slice(cb * LANES, (cb + 1) * LANES)
            out_cols = slice(h * SSM_HALF + cb * LANES, h * SSM_HALF + (cb + 1) * LANES)
            slab = h * (SSM_HALF // LANES) + cb
            yv_ref[slab] = y[:, cols]
            for s in range(SUBLANES):
                y_ref[0, s * SSM_TC:(s + 1) * SSM_TC, out_cols] = yv_ref[slab, pl.ds(s, SSM_TC, stride=SUBLANES), :]


def _ssm_params(a_re, a_im, log_dt, b_re, b_im, c_re, c_im):
    lam = lax.complex(a_re.astype(F32), a_im.astype(F32))
    dt = jnp.exp(log_dt.astype(F32))[..., None]
    lam_bar = jnp.exp(lam * dt)
    b_bar = ((lam_bar - 1.0) / lam)[..., None] * lax.complex(b_re.astype(F32), b_im.astype(F32))
    gh = SSM_GROUPS // 2
    eye = jnp.eye(gh, dtype=F32)

    def b_block(m):
        m = m.reshape(2, 2, gh, SSM_STATE, SSM_GROUP)
        return jnp.einsum('dhgpc,gk->dhgckp', m, eye).reshape(2, 2, gh * SSM_GROUP, gh * SSM_STATE)

    def c_block(m):
        m = m.reshape(2, 2, gh, SSM_GROUP, SSM_STATE)
        return jnp.einsum('dhgcp,gk->dhkpgc', m, eye).reshape(2, 2, gh * SSM_STATE, gh * SSM_GROUP)

    bm = jnp.concatenate([b_block(jnp.real(b_bar)), b_block(jnp.imag(b_bar))], axis=-1).astype(BF16)
    cm = jnp.concatenate([c_block(c_re.astype(F32)), -c_block(c_im.astype(F32))], axis=-2).astype(BF16)

    def table(z):
        z = z.reshape(2, 2, SSM_HSTATE)
        t = jnp.concatenate([jnp.real(z), jnp.imag(z)], axis=-1)
        return jnp.broadcast_to(t[:, :, None, :], (2, 2, SUBLANES, 2 * SSM_HSTATE))

    lam_t = table(lam_bar)
    pw_t = jnp.stack([table(jnp.exp(lam * dt * (SSM_TC * k))) for k in (1, 2, 4)], axis=2)
    return bm, cm, lam_t, pw_t


SSM_CHUNKS = S_ALL // SSM_T


def ssm_scan(u_chunks, bm, cm, lam_t, pw_t, reverse):
    hs2 = 2 * SSM_HSTATE

    def chunk(b, c):
        if reverse:
            c = jnp.where(c == 0, 0, SSM_CHUNKS - c)
        return b * SSM_CHUNKS + c

    kern = functools.partial(_ssm_kernel, reverse=reverse)
    return pl.pallas_call(
        kern,
        grid=(BATCH, SSM_CHUNKS),
        in_specs=[
            pl.BlockSpec((1, SSM_T, SSM_WIDTH), lambda b, c: (chunk(b, c), 0, 0)),
            pl.BlockSpec((2, SSM_HALF, hs2), lambda b, c: (0, 0, 0)),
            pl.BlockSpec((2, hs2, SSM_HALF), lambda b, c: (0, 0, 0)),
            pl.BlockSpec((2, SUBLANES, hs2), lambda b, c: (0, 0, 0)),
            pl.BlockSpec((2, 3, SUBLANES, hs2), lambda b, c: (0, 0, 0, 0)),
        ],
        out_specs=pl.BlockSpec((1, SSM_T, SSM_WIDTH), lambda b, c: (chunk(b, c), 0, 0)),
        out_shape=jax.ShapeDtypeStruct(u_chunks.shape, F32),
        scratch_shapes=[pltpu.VMEM((2, SSM_T, hs2), F32), pltpu.VMEM((SSM_WIDTH // LANES, SSM_T, LANES), F32),
                        pltpu.VMEM((2, SUBLANES, hs2), F32)],
        compiler_params=_cparams(("arbitrary",) * 2, 32),
        name="ssm_scan",
    )(u_chunks, bm, cm, lam_t, pw_t)


def _ssm_chunk_of_tile(i):
    lat_tiles = SEQ // SSM_T
    return jnp.where(i < BATCH * lat_tiles,
                     (i // lat_tiles) * SSM_CHUNKS + 1 + i % lat_tiles,
                     (i - BATCH * lat_tiles) * SSM_CHUNKS)


def _gelu_tanh(x):
    return 0.5 * x * (1.0 + jnp.tanh(math.sqrt(2.0 / math.pi) * (x + 0.044715 * (x * x * x))))


def _ssm_out_kernel(u_ref, yf_ref, yb_ref, d_ref, w_ref, b_ref, o_ref):
    y = d_ref[0] * u_ref[...].astype(F32) + yf_ref[0] + yb_ref[0]
    y = _gelu_tanh(y)
    z = jnp.dot(y.astype(BF16), w_ref[0], preferred_element_type=F32) + b_ref[0]
    o_ref[...] = (y * jax.nn.sigmoid(z)).astype(o_ref.dtype)


def ssm_output(p, yf, yb, ssm_d, glu_w, glu_b, layer):
    n = p.shape[0]
    yspec = pl.BlockSpec((1, SSM_T, SSM_WIDTH), lambda i: (_ssm_chunk_of_tile(i), 0, 0))
    return pl.pallas_call(
        _ssm_out_kernel,
        grid=(n // SSM_T,),
        in_specs=[
            pl.BlockSpec((SSM_T, SSM_WIDTH), lambda i: (i, COL_SSM // SSM_WIDTH)),
            yspec,
            yspec,
            pl.BlockSpec((1, 1, SSM_WIDTH), lambda i: (layer, 0, 0)),
            pl.BlockSpec((1, SSM_WIDTH, SSM_WIDTH), lambda i: (layer, 0, 0)),
            pl.BlockSpec((1, 1, SSM_WIDTH), lambda i: (layer, 0, 0)),
        ],
        out_specs=pl.BlockSpec((SSM_T, SSM_WIDTH), lambda i: (i, 0)),
        out_shape=jax.ShapeDtypeStruct((n, SSM_WIDTH), BF16),
        compiler_params=_cparams(("arbitrary",), 32),
        name="ssm_output",
    )(p, yf, yb, ssm_d.reshape(DEPTH, 1, SSM_WIDTH), glu_w, glu_b.reshape(DEPTH, 1, SSM_WIDTH))


def _seq_tile(tile):
    lat_tiles = N_LAT // SEQ_T
    per_seq = SEQ // SEQ_T
    is_lat = tile < lat_tiles
    t = jnp.where(is_lat, tile % per_seq, (tile - lat_tiles) % (CTX_LEN // SEQ_T))
    last_t = jnp.where(is_lat, per_seq - 1, CTX_LEN // SEQ_T - 1)
    return t * SEQ_T, t == 0, t == last_t, jnp.where(is_lat, SEQ, CTX_LEN)


def _with_halo(prev_ref, cur_ref, next_ref, first, last):
    top = jnp.where(first, jnp.zeros((SEQ_PAD, cur_ref.shape[1]), cur_ref.dtype), prev_ref[SEQ_T - SEQ_PAD:SEQ_T, :])
    bot = jnp.where(last, jnp.zeros((SEQ_PAD, cur_ref.shape[1]), cur_ref.dtype), next_ref[0:SEQ_PAD, :])
    return jnp.concatenate([top, cur_ref[...], bot], axis=0)


def _halo_specs(width, col_block, n_tiles):
    return [pl.BlockSpec((SEQ_T, width), lambda g: (jnp.maximum(g - 1, 0), col_block)),
            pl.BlockSpec((SEQ_T, width), lambda g: (g, col_block)),
            pl.BlockSpec((SEQ_T, width), lambda g: (jnp.minimum(g + 1, n_tiles - 1), col_block))]


def _pool_kernel(prev_ref, cur_ref, next_ref, w_ref, s_ref, o_ref):
    t0, first, last, seq_len = _seq_tile(pl.program_id(0))
    halo = _with_halo(prev_ref, cur_ref, next_ref, first, last)
    centre = cur_ref[...].astype(F32)
    tt = lax.broadcasted_iota(jnp.int32, (SEQ_T, SEQ_T + 2 * SEQ_PAD), 0)
    rr = lax.broadcasted_iota(jnp.int32, (SEQ_T, SEQ_T + 2 * SEQ_PAD), 1) - SEQ_PAD
    tg = t0 + lax.broadcasted_iota(jnp.int32, (SEQ_T, 1), 0)
    for gi, w in enumerate(POOL_WINDOWS):
        cols = slice(gi * POOL_GROUP, (gi + 1) * POOL_GROUP)
        band = ((rr >= tt - w // 2) & (rr <= tt + w // 2 - 1)).astype(F32).astype(BF16)
        wsum = jnp.dot(band, halo[:, cols], preferred_element_type=F32)
        cnt = jnp.minimum(tg - w // 2 + w, seq_len) - jnp.maximum(tg - w // 2, 0)
        pooled = wsum / cnt.astype(F32) - centre[:, cols]
        mixed = jnp.dot(pooled.astype(BF16), w_ref[0, gi], preferred_element_type=F32)
        o_ref[:, cols] = (mixed * s_ref[0, :, cols]).astype(o_ref.dtype)


def pool_mixer(p, pool_w, pool_scale, layer):
    n_tiles = p.shape[0] // SEQ_T
    ng = len(POOL_WINDOWS)
    return pl.pallas_call(
        _pool_kernel,
        grid=(n_tiles,),
        in_specs=_halo_specs(POOL_WIDTH, COL_POOL // POOL_WIDTH, n_tiles) + [
            pl.BlockSpec((1, ng, POOL_GROUP, POOL_GROUP), lambda g: (layer, 0, 0, 0)),
            pl.BlockSpec((1, 1, POOL_WIDTH), lambda g: (layer, 0, 0)),
        ],
        out_specs=pl.BlockSpec((SEQ_T, POOL_WIDTH), lambda g: (g, 0)),
        out_shape=jax.ShapeDtypeStruct((p.shape[0], POOL_WIDTH), BF16),
        compiler_params=_cparams(("arbitrary",), 32),
        name="pool_mixer",
    )(p, p, p, pool_w, pool_scale.reshape(DEPTH, 1, POOL_WIDTH))


CONV_RB = 64


def _conv_kernel(ap_ref, ac_ref, an_ref, gp_ref, gc_ref, gn_ref, w_ref, b_ref, g_ref, beta_ref, o_ref, glu_ref):
    _, first, last, _ = _seq_tile(pl.program_id(0))
    a = _with_halo(ap_ref, ac_ref, an_ref, first, last).astype(F32)
    g = _with_halo(gp_ref, gc_ref, gn_ref, first, last).astype(F32)
    glu_ref[...] = a * jax.nn.sigmoid(g)
    off = SEQ_PAD - CONV_TAPS // 2
    for rb in range(SEQ_T // CONV_RB):
        parts = []
        for cb in range(CONV_WIDTH // LANES):
            cols = slice(cb * LANES, (cb + 1) * LANES)
            acc = jnp.zeros((CONV_RB, LANES), F32)
            for k in range(CONV_TAPS):
                acc = acc + glu_ref[rb * CONV_RB + k + off:rb * CONV_RB + k + off + CONV_RB, cols] * w_ref[0, k:k + 1, cols]
            parts.append(acc)
        y = jnp.concatenate(parts, axis=-1) + b_ref[0]
        yc = y - jnp.mean(y, axis=-1, keepdims=True)
        yn = yc * lax.rsqrt(jnp.mean(yc * yc, axis=-1, keepdims=True) + EPS)
        yn = yn * g_ref[0] + beta_ref[0]
        o_ref[rb * CONV_RB:(rb + 1) * CONV_RB, :] = _silu(yn).astype(o_ref.dtype)


def conv_mixer(p, dw_w, dw_b, ln_g, ln_b, layer):
    n_tiles = p.shape[0] // SEQ_T
    vec = lambda a: a.reshape(DEPTH, 1, CONV_WIDTH)
    vspec = pl.BlockSpec((1, 1, CONV_WIDTH), lambda g: (layer, 0, 0))
    a_block = COL_CONV // CONV_WIDTH
    return pl.pallas_call(
        _conv_kernel,
        grid=(n_tiles,),
        in_specs=_halo_specs(CONV_WIDTH, a_block, n_tiles) + _halo_specs(CONV_WIDTH, a_block + 1, n_tiles) + [
            pl.BlockSpec((1, CONV_TAPS, CONV_WIDTH), lambda g: (layer, 0, 0)),
            vspec, vspec, vspec,
        ],
        out_specs=pl.BlockSpec((SEQ_T, CONV_WIDTH), lambda g: (g, 0)),
        out_shape=jax.ShapeDtypeStruct((p.shape[0], CONV_WIDTH), BF16),
        scratch_shapes=[pltpu.VMEM((SEQ_T + 2 * SEQ_PAD, CONV_WIDTH), F32)],
        compiler_params=_cparams(("arbitrary",), 40),
        name="conv_mixer",
    )(p, p, p, p, p, p, dw_w, vec(dw_b), vec(ln_g), vec(ln_b))


def _merge_kernel(oa_ref, ys_ref, yp_ref, yc_ref, g0_ref, g1_ref, g2_ref, g3_ref,
                  wa_ref, ws_ref, wp_ref, wc_ref, o_ref):
    def branch(x_ref, w_ref, g_ref):
        up = jnp.dot(x_ref[...], w_ref[0, 0], preferred_element_type=F32)
        return jax.nn.sigmoid(g_ref[...].astype(F32)) * up

    m = (branch(oa_ref, wa_ref, g0_ref) + branch(ys_ref, ws_ref, g1_ref)
         + branch(yp_ref, wp_ref, g2_ref) + branch(yc_ref, wc_ref, g3_ref))
    o_ref[...] = m.astype(o_ref.dtype)


def merge_branches(p, o_attn, y_ssm, y_pool, y_conv, w_attn, w_ssm, w_pool, w_conv, layer):
    n = o_attn.shape[0]
    nct = D_MODEL // TN
    gate_tile0 = COL_GATE // TN

    def gate_spec(br):
        return pl.BlockSpec((TM, TN), lambda i, j: (i, gate_tile0 + br * nct + j))

    def x_spec(width):
        return pl.BlockSpec((TM, width), lambda i, j: (i, 0))

    def w_spec(width):
        return pl.BlockSpec((1, 1, width, TN), lambda i, j: (layer, j, 0, 0))

    return pl.pallas_call(
        _merge_kernel,
        grid=(n // TM, nct),
        in_specs=[x_spec(Q_W), x_spec(SSM_WIDTH), x_spec(POOL_WIDTH), x_spec(CONV_WIDTH),
                  gate_spec(0), gate_spec(1), gate_spec(2), gate_spec(3),
                  w_spec(Q_W), w_spec(SSM_WIDTH), w_spec(POOL_WIDTH), w_spec(CONV_WIDTH)],
        out_specs=pl.BlockSpec((TM, TN), lambda i, j: (i, j)),
        out_shape=jax.ShapeDtypeStruct((n, D_MODEL), BF16),
        compiler_params=_cparams(("arbitrary", "arbitrary"), 40),
        name="merge_branches",
    )(o_attn, y_ssm, y_pool, y_conv, p, p, p, p, w_attn, w_ssm, w_pool, w_conv)


def _split_bf16(x):
    hi = x.astype(BF16)
    lo = (x - hi.astype(F32)).astype(BF16)
    return hi, lo


SLAB = D_MODEL // LANES


def _store_row_slabs(ref, row0, x):
    n = x.shape[0]
    for j in range(SLAB):
        ref[pl.ds(row0 * SLAB + j, n, stride=SLAB), :] = x[:, j * LANES:(j + 1) * LANES]


def _load_row_slabs(ref, row0, n):
    return jnp.concatenate([ref[pl.ds(row0 * SLAB + j, n, stride=SLAB), :] for j in range(SLAB)], axis=-1)


def _outproj_kernel(m_ref, x_ref, mod_ref, g_ref, w_ref, rw_ref, xo_ref, h_ref, lg_ref):
    r_hi, r_lo = _split_bf16(rw_ref[...])
    nt = (((1,), (1,)), ((), ()))
    half = m_ref.shape[0] // 2
    for c in range(2):
        rows = slice(c * half, (c + 1) * half)
        mix = jnp.dot(m_ref[rows, :], w_ref[0], preferred_element_type=F32)
        x = x_ref[rows, :] + mod_ref[0, 2:3, :] * mix
        xo_ref[rows, :] = x
        ms = jnp.mean(x * x, axis=-1, keepdims=True)
        h = x * lax.rsqrt(ms + EPS) * g_ref[0]
        h = h * (1.0 + mod_ref[0, 4:5, :]) + mod_ref[0, 3:4, :]
        _store_row_slabs(h_ref, c * half, h)
        h_hi, h_lo = _split_bf16(h)
        lg_ref[:, rows] = (lax.dot_general(r_hi, h_hi, nt, preferred_element_type=F32)
                           + lax.dot_general(r_hi, h_lo, nt, preferred_element_type=F32)
                           + lax.dot_general(r_lo, h_hi, nt, preferred_element_type=F32))


def out_projection(m, xa, mod, norm2_g, w_out, router_wt, layer):
    n = m.shape[0]
    tmo = TM
    return pl.pallas_call(
        _outproj_kernel,
        grid=(n // tmo,),
        in_specs=[
            pl.BlockSpec((tmo, D_MODEL), lambda i: (i, 0)),
            pl.BlockSpec((tmo, D_MODEL), lambda i: (i, 0)),
            pl.BlockSpec((1, 6, D_MODEL), lambda i: (_mod_id(i), 0, 0)),
            pl.BlockSpec((1, 1, D_MODEL), lambda i: (layer, 0, 0)),
            pl.BlockSpec((1, D_MODEL, D_MODEL), lambda i: (layer, 0, 0)),
            pl.BlockSpec((N_EXPERTS, D_MODEL), lambda i: (0, 0)),
        ],
        out_specs=[
            pl.BlockSpec((tmo, D_MODEL), lambda i: (i, 0)),
            pl.BlockSpec((tmo * SLAB, LANES), lambda i: (i, 0)),
            pl.BlockSpec((N_EXPERTS, tmo), lambda i: (0, i)),
        ],
        out_shape=[
            jax.ShapeDtypeStruct((n, D_MODEL), F32),
            jax.ShapeDtypeStruct((n * SLAB, LANES), F32),
            jax.ShapeDtypeStruct((N_EXPERTS, n), F32),
        ],
        compiler_params=_cparams(("arbitrary",), 60),
        name="out_projection",
    )(m, xa, mod, norm2_g.reshape(DEPTH, 1, D_MODEL), w_out, router_wt)


def _first_argmax(blk, row):
    m = jnp.max(blk, axis=0, keepdims=True)
    idx = jnp.min(jnp.where(blk == m, row, EXPERTS_PER_GROUP), axis=0, keepdims=True)
    return m, idx


def _route_kernel(lg_ref, rb_ref, e_ref, w_ref, rank_ref, cnt_ref, run_ref):
    @pl.when(pl.program_id(0) == 0)
    def _():
        run_ref[...] = jnp.zeros_like(run_ref)

    t = lg_ref.shape[1]
    scores = jax.nn.sigmoid(lg_ref[...])
    sel = scores + rb_ref[...]
    row = lax.broadcasted_iota(jnp.int32, (EXPERTS_PER_GROUP, t), 0)
    neg = jnp.float32(-jnp.inf)

    best = None
    for g in range(N_EXPERT_GROUPS):
        blk = sel[g * EXPERTS_PER_GROUP:(g + 1) * EXPERTS_PER_GROUP, :]
        m1, i1 = _first_argmax(blk, row)
        m2 = jnp.max(jnp.where(row == i1, neg, blk), axis=0, keepdims=True)
        gs = m1 + m2
        if best is None:
            best, grp = gs, jnp.zeros((1, t), jnp.int32)
        else:
            better = gs > best
            best = jnp.where(better, gs, best)
            grp = jnp.where(better, g, grp)

    in_sel = jnp.zeros((EXPERTS_PER_GROUP, t), F32)
    in_sc = jnp.zeros((EXPERTS_PER_GROUP, t), F32)
    for g in range(N_EXPERT_GROUPS):
        rows = slice(g * EXPERTS_PER_GROUP, (g + 1) * EXPERTS_PER_GROUP)
        in_sel = jnp.where(grp == g, sel[rows, :], in_sel)
        in_sc = jnp.where(grp == g, scores[rows, :], in_sc)
    _, i1 = _first_argmax(in_sel, row)
    _, i2 = _first_argmax(jnp.where(row == i1, neg, in_sel), row)
    s1 = jnp.sum(jnp.where(row == i1, in_sc, 0.0), axis=0, keepdims=True)
    s2 = jnp.sum(jnp.where(row == i2, in_sc, 0.0), axis=0, keepdims=True)
    e1 = grp * EXPERTS_PER_GROUP + i1
    e2 = grp * EXPERTS_PER_GROUP + i2
    e_ref[0:1, :] = e1
    e_ref[1:2, :] = e2
    w_ref[0:1, :] = s1 / (s1 + s2)
    w_ref[1:2, :] = s2 / (s1 + s2)

    erow = lax.broadcasted_iota(jnp.int32, (N_EXPERTS, t), 0)
    oh1 = erow == e1
    oh2 = erow == e2
    cnt = jnp.where(oh1 | oh2, 1.0, 0.0)
    a = lax.broadcasted_iota(jnp.int32, (t, t), 0)
    b = lax.broadcasted_iota(jnp.int32, (t, t), 1)
    before = jnp.where(a < b, 1.0, 0.0).astype(BF16)
    excl = jnp.dot(cnt.astype(BF16), before, preferred_element_type=F32) + run_ref[...]
    rank_ref[0:1, :] = jnp.sum(jnp.where(oh1, excl, 0.0), axis=0, keepdims=True).astype(jnp.int32)
    rank_ref[1:2, :] = jnp.sum(jnp.where(oh2, excl, 0.0), axis=0, keepdims=True).astype(jnp.int32)
    run_ref[...] = run_ref[...] + jnp.sum(cnt, axis=1, keepdims=True)
    cnt_ref[...] = run_ref[...]


def route(logits_t, router_b):
    n = logits_t.shape[1]
    return pl.pallas_call(
        _route_kernel,
        grid=(n // ROUTE_T,),
        in_specs=[
            pl.BlockSpec((N_EXPERTS, ROUTE_T), lambda i: (0, i)),
            pl.BlockSpec((N_EXPERTS, 1), lambda i: (0, 0)),
        ],
        out_specs=[
            pl.BlockSpec((TOP_K, ROUTE_T), lambda i: (0, i)),
            pl.BlockSpec((TOP_K, ROUTE_T), lambda i: (0, i)),
            pl.BlockSpec((TOP_K, ROUTE_T), lambda i: (0, i)),
            pl.BlockSpec((N_EXPERTS, 1), lambda i: (0, 0)),
        ],
        out_shape=[
            jax.ShapeDtypeStruct((TOP_K, n), jnp.int32),
            jax.ShapeDtypeStruct((TOP_K, n), F32),
            jax.ShapeDtypeStruct((TOP_K, n), jnp.int32),
            jax.ShapeDtypeStruct((N_EXPERTS, 1), F32),
        ],
        scratch_shapes=[pltpu.VMEM((N_EXPERTS, 1), F32)],
        compiler_params=_cparams(("arbitrary",), 32),
        name="route",
    )(logits_t, router_b.reshape(N_EXPERTS, 1).astype(F32))


def _expert_kernel(be_ref, slot_ref, nxt_ref, nu_ref, tok_ref, dst_ref, h_hbm, wg_hbm, wu_hbm, wd_hbm, o_hbm,
                   xbuf_ref, ybuf_ref, wgf_ref, wuf_ref, wdf_ref, wgb_ref, wub_ref, wdb_ref, sem, xsem, ysem,
                   *, layer):
    b = pl.program_id(0)
    n_used = nu_ref[0]
    e = be_ref[b]
    fresh = (b == 0) | (e != be_ref[jnp.maximum(b - 1, 0)])
    live = b < n_used
    has_next = b + 1 < n_used
    cur = b % 2

    def slab(row):
        return pl.ds(pl.multiple_of(row * SLAB, SLAB), SLAB)

    def row_gathers(block, buf):
        return [pltpu.make_async_copy(h_hbm.at[slab(tok_ref[block * MOE_BLOCK + r])],
                                      xbuf_ref.at[buf, slab(r)], xsem.at[buf]) for r in range(MOE_BLOCK)]

    def row_scatters(block, buf):
        return [pltpu.make_async_copy(ybuf_ref.at[buf, slab(r)],
                                      o_hbm.at[slab(dst_ref[block * MOE_BLOCK + r])], ysem.at[buf])
                for r in range(MOE_BLOCK)]

    @pl.when(b == 0)
    def _():
        for cp in row_gathers(0, 0):
            cp.start()

    def weight_copies(expert, slot):
        return (pltpu.make_async_copy(wg_hbm.at[layer, expert], wgf_ref.at[slot], sem.at[slot]),
                pltpu.make_async_copy(wu_hbm.at[layer, expert], wuf_ref.at[slot], sem.at[slot]),
                pltpu.make_async_copy(wd_hbm.at[layer, expert], wdf_ref.at[slot], sem.at[slot]))

    @pl.when(b == 0)
    def _():
        for cp in weight_copies(e, 0):
            cp.start(priority=WEIGHT_DMA_PRIORITY)

        @pl.when(nxt_ref[0] >= 0)
        def _():
            for cp in weight_copies(nxt_ref[0], 1):
                cp.start(priority=WEIGHT_DMA_PRIORITY)

    @pl.when(fresh & live)
    def _():
        slot = slot_ref[b]
        ahead = nxt_ref[pl.num_programs(0) + b]

        @pl.when(ahead >= 0)
        def _():
            ahead_slot = jnp.where(slot + 2 >= WEIGHT_SLOTS, slot + 2 - WEIGHT_SLOTS, slot + 2)
            for cp in weight_copies(ahead, ahead_slot):
                cp.start(priority=WEIGHT_DMA_PRIORITY)

        for cp in weight_copies(e, slot):
            cp.wait()
        wgb_ref[...] = wgf_ref[slot].astype(BF16)
        wub_ref[...] = wuf_ref[slot].astype(BF16)
        wdb_ref[...] = wdf_ref[slot].astype(BF16)

    @pl.when(live & (b >= 2))
    def _():
        for cp in row_scatters(b - 2, cur):
            cp.wait()

    def block(scatter_prev, gather_next):
        for cp in row_gathers(b, cur):
            cp.wait()
        if scatter_prev:
            for cp in row_scatters(b - 1, 1 - cur):
                cp.start(priority=WEIGHT_DMA_PRIORITY)
        if gather_next:
            for cp in row_gathers(b + 1, 1 - cur):
                cp.start()
        x = _load_row_slabs(xbuf_ref.at[cur], 0, MOE_BLOCK).astype(BF16)
        gate = jnp.dot(x, wgb_ref[...], preferred_element_type=F32)
        up = jnp.dot(x, wub_ref[...], preferred_element_type=F32)
        hid = (_silu(gate) * up).astype(BF16)
        _store_row_slabs(ybuf_ref.at[cur], 0, jnp.dot(hid, wdb_ref[...], preferred_element_type=F32))

    @pl.when(b == 0)
    def _():
        block(False, True)

    @pl.when(live & (b >= 1) & has_next)
    def _():
        block(True, True)

    @pl.when(live & (b >= 1) & jnp.logical_not(has_next))
    def _():
        block(True, False)

    @pl.when(b == n_used)
    def _():
        last = n_used - 1
        for cp in row_scatters(last, last % 2):
            cp.start(priority=WEIGHT_DMA_PRIORITY)
        for cp in row_scatters(last - 1, 1 - last % 2):
            cp.wait()
        for cp in row_scatters(last, last % 2):
            cp.wait()

    @pl.when(jnp.logical_not(live))
    def _():
        ybuf_ref[0] = jnp.zeros((MOE_BLOCK * SLAB, LANES), F32)
        rows = pl.ds(pl.multiple_of(b * (MOE_BLOCK * SLAB), MOE_BLOCK * SLAB), MOE_BLOCK * SLAB)
        cp = pltpu.make_async_copy(ybuf_ref.at[0], o_hbm.at[rows], ysem.at[0])
        cp.start()
        cp.wait()


def _slot_maps(dest_flat, n, n_slots):
    i32 = jnp.int32
    owner = jnp.full((n_slots,), -1, i32).at[dest_flat].set(jnp.arange(TOP_K * n, dtype=i32), unique_indices=True)
    is_pad = owner < 0
    tok = jnp.where(is_pad, 0, owner % n)
    dst = jnp.where(is_pad, TOP_K * n + jnp.cumsum(is_pad.astype(i32)) - 1, owner)
    return tok.astype(i32), dst.astype(i32)


def experts(blk_e, blk_slot, blk_next, n_used, slot_tok, slot_dst, h2, w_gate, w_up, w_down, layer):
    n = h2.shape[0] // SLAB
    n_slots = slot_tok.shape[0]
    n_blocks = n_slots // MOE_BLOCK
    assert n_blocks - N_EXPERTS >= 2
    assert n_slots == TOP_K * n + N_EXPERTS * MOE_BLOCK
    grid_spec = pltpu.PrefetchScalarGridSpec(
        num_scalar_prefetch=6,
        grid=(n_blocks,),
        in_specs=[
            pl.BlockSpec(memory_space=pl.ANY),
            pl.BlockSpec(memory_space=pl.ANY),
            pl.BlockSpec(memory_space=pl.ANY),
            pl.BlockSpec(memory_space=pl.ANY),
        ],
        out_specs=pl.BlockSpec(memory_space=pl.ANY),
        scratch_shapes=[
            pltpu.VMEM((2, MOE_BLOCK * SLAB, LANES), F32),
            pltpu.VMEM((2, MOE_BLOCK * SLAB, LANES), F32),
            pltpu.VMEM((WEIGHT_SLOTS, D_MODEL, EXPERT_FF), F32),
            pltpu.VMEM((WEIGHT_SLOTS, D_MODEL, EXPERT_FF), F32),
            pltpu.VMEM((WEIGHT_SLOTS, EXPERT_FF, D_MODEL), F32),
            pltpu.VMEM((D_MODEL, EXPERT_FF), BF16),
            pltpu.VMEM((D_MODEL, EXPERT_FF), BF16),
            pltpu.VMEM((EXPERT_FF, D_MODEL), BF16),
            pltpu.SemaphoreType.DMA((WEIGHT_SLOTS,)),
            pltpu.SemaphoreType.DMA((2,)),
            pltpu.SemaphoreType.DMA((2,)),
        ],
    )
    return pl.pallas_call(
        functools.partial(_expert_kernel, layer=layer),
        grid_spec=grid_spec,
        out_shape=jax.ShapeDtypeStruct((n_slots * SLAB, LANES), F32),
        compiler_params=_cparams(("arbitrary",), 60),
        name="experts",
    )(blk_e, blk_slot, blk_next, n_used, slot_tok, slot_dst, h2, w_gate, w_up, w_down)


def _combine_kernel(x_ref, mod_ref, w_ref, fg_ref, y0_ref, y1_ref, o_ref, *, final_norm):
    f = w_ref[:, 0:1] * _load_row_slabs(y0_ref, 0, TM) + w_ref[:, 1:2] * _load_row_slabs(y1_ref, 0, TM)
    x = x_ref[...] + mod_ref[0, 5:6, :] * f
    if final_norm:
        ms = jnp.mean(x * x, axis=-1, keepdims=True)
        x = x * lax.rsqrt(ms + EPS) * fg_ref[...]
    o_ref[...] = x


def combine(xa, mod, wts_rows, ys, final_g, final_norm):
    n = xa.shape[0]
    n_tiles = n // TM
    kern = functools.partial(_combine_kernel, final_norm=final_norm)
    return pl.pallas_call(
        kern,
        grid=(n_tiles,),
        in_specs=[
            pl.BlockSpec((TM, D_MODEL), lambda i: (i, 0)),
            pl.BlockSpec((1, 6, D_MODEL), lambda i: (_mod_id(i), 0, 0)),
            pl.BlockSpec((TM, TOP_K), lambda i: (i, 0)),
            pl.BlockSpec((1, D_MODEL), lambda i: (0, 0)),
            pl.BlockSpec((TM * SLAB, LANES), lambda i: (i, 0)),
            pl.BlockSpec((TM * SLAB, LANES), lambda i: (n_tiles + i, 0)),
        ],
        out_specs=pl.BlockSpec((TM, D_MODEL), lambda i: (i, 0)),
        out_shape=jax.ShapeDtypeStruct((n, D_MODEL), F32),
        compiler_params=_cparams(("arbitrary",), 48),
        name="combine",
    )(xa, mod, wts_rows, final_g.reshape(1, D_MODEL), ys, ys)


def _slot_plan(e_idx, rank, counts, n_blocks):
    i32 = jnp.int32
    cnt = counts.reshape(N_EXPERTS).astype(i32)
    padded = (cnt + MOE_BLOCK - 1) // MOE_BLOCK * MOE_BLOCK
    pad_end = jnp.cumsum(padded)
    pad_start = pad_end - padded
    eids = jnp.arange(N_EXPERTS, dtype=i32)

    def lookup(table, idx):
        return jnp.sum(jnp.where(idx[..., None] == eids, table, 0), axis=-1).astype(i32)

    dest = lookup(pad_start, e_idx) + rank
    blk_start = jnp.arange(n_blocks, dtype=i32) * MOE_BLOCK
    blk_e = jnp.minimum(jnp.sum(pad_end[None, :] <= blk_start[:, None], axis=1), N_EXPERTS - 1).astype(i32)
    n_used = (pad_end[-1] // MOE_BLOCK).astype(i32).reshape(1)
    has = cnt > 0
    run = jnp.cumsum(has.astype(i32)) - 1
    first_from = lax.cummin(jnp.where(has, eids, N_EXPERTS)[::-1])[::-1]
    nxt = jnp.concatenate([first_from[1:], jnp.full((1,), N_EXPERTS, i32)])
    nxt = jnp.where(nxt >= N_EXPERTS, -1, nxt)
    nxt2 = jnp.where(nxt >= 0, lookup(nxt, jnp.maximum(nxt, 0)), -1)
    blk_ahead = jnp.stack([lookup(nxt, blk_e), lookup(nxt2, blk_e)], axis=0).reshape(-1)
    return dest.reshape(-1).astype(i32), blk_e, lookup(run % WEIGHT_SLOTS, blk_e), blk_ahead, n_used


def _col_tiles(w):
    d, k, n = w.shape
    return w.astype(BF16).reshape(d, k, n // TN, TN).transpose(0, 2, 1, 3)


def _layer(layer, xa, mod, W, with_ctx, final_norm):
    n_lat_tiles = N_LAT // TM
    n_ctx_tiles = N_CTX // TM
    if with_ctx:
        p = in_projection(xa, mod, W['norm1_g'], W['w_in'], layer, 0, n_lat_tiles + n_ctx_tiles, IN_W, TN_IN)
        p_lat, p_ctx = p[:N_LAT], p[N_LAT:]
    else:
        p = in_projection(xa, mod, W['norm1_g'], W['w_in'], layer, 0, n_lat_tiles, IN_W, TN_IN)
        p_lat = p
        p_ctx = in_projection(xa, mod, W['norm1_g'], W['w_in'], layer, n_lat_tiles, n_ctx_tiles, CTX_STATE_W, TN)
    n_rows = p.shape[0]

    def seq(a, lo, hi, n_seq_rows):
        return a[:, lo:hi].reshape(BATCH, n_seq_rows, hi - lo)

    kv = jnp.concatenate([seq(p_ctx, 0, 2 * KV_W, CTX_LEN), seq(p_lat, 0, 2 * KV_W, SEQ)], axis=1)
    cos, sin = _rope_tables(CTX_LEN)
    o_attn = attention(p, COL_Q, 0, SEQ, kv, cos[CTX_LEN:], sin[CTX_LEN:], cos, sin,
                       W['q_norm_g'], W['k_norm_g'], layer)
    if with_ctx:
        kv_c = seq(p_ctx, 0, 2 * KV_W, CTX_LEN)
        ones, zeros = jnp.ones((CTX_LEN, HEAD_DIM), F32), jnp.zeros((CTX_LEN, HEAD_DIM), F32)
        o_attn_c = attention(p, COL_Q, N_LAT, CTX_LEN, kv_c, ones, zeros, ones, zeros,
                             W['q_norm_g'], W['k_norm_g'], layer)
        o_attn = jnp.concatenate([o_attn, o_attn_c], axis=0)

    u_seq = jnp.concatenate([seq(p_ctx, COL_SSM, COL_SSM + SSM_WIDTH, CTX_LEN),
                             seq(p_lat, COL_SSM, COL_SSM + SSM_WIDTH, SEQ)], axis=1)
    u_chunks = u_seq.reshape(BATCH * SSM_CHUNKS, SSM_T, SSM_WIDTH)
    bm, cm, lam_t, pw_t = _ssm_params(W['ssm_a_re'][layer], W['ssm_a_im'][layer], W['ssm_log_dt'][layer],
                                      W['ssm_b_re'][layer], W['ssm_b_im'][layer],
                                      W['ssm_c_re'][layer], W['ssm_c_im'][layer])
    yf = ssm_scan(u_chunks, bm[0], cm[0], lam_t[0], pw_t[0], reverse=False)
    yb = ssm_scan(u_chunks, bm[1], cm[1], lam_t[1], pw_t[1], reverse=True)
    y_ssm = ssm_output(p, yf, yb, W['ssm_d'], W['ssm_glu_w'], W['ssm_glu_b'], layer)

    y_pool = pool_mixer(p, W['pool_w'], W['pool_scale'], layer)
    y_conv = conv_mixer(p, W['conv_dw_w'], W['conv_dw_b'], W['conv_ln_g'], W['conv_ln_b'], layer)

    m = merge_branches(p, o_attn, y_ssm, y_pool, y_conv,
                       W['w_up_attn'], W['w_up_ssm'], W['w_up_pool'], W['w_up_conv'], layer)
    x1, h2, logits_t = out_projection(m, xa, mod, W['norm2_g'], W['w_out'], W['router_wt'], layer)

    e_idx, wts, rank, counts = route(logits_t, W['router_b'])
    n_blocks = -(-n_rows * TOP_K // MOE_BLOCK) + N_EXPERTS
    dest_flat, blk_e, blk_slot, blk_next, n_used = _slot_plan(e_idx, rank, counts, n_blocks)
    slot_tok, slot_dst = _slot_maps(dest_flat, n_rows, n_blocks * MOE_BLOCK)
    ys = experts(blk_e, blk_slot, blk_next, n_used, slot_tok, slot_dst, h2,
                 W['moe_w_gate'], W['moe_w_up'], W['moe_w_down'], layer)
    x2 = combine(x1, mod, wts.T, ys, W['final_g'], final_norm)
    if with_ctx:
        return x2
    return jnp.concatenate([x2, xa[N_LAT:]], axis=0)


def kernel(x, c, ctx, c_ctx, ada_w, ada_b, norm1_g, norm2_g, w_in, q_norm_g, k_norm_g, ssm_a_re, ssm_a_im, ssm_log_dt, ssm_b_re, ssm_b_im, ssm_c_re, ssm_c_im, ssm_d, ssm_glu_w, ssm_glu_b, pool_w, pool_scale, conv_dw_w, conv_dw_b, conv_ln_g, conv_ln_b, w_up_attn, w_up_ssm, w_up_pool, w_up_conv, w_out, router_w, router_b, moe_w_gate, moe_w_up, moe_w_down, final_g):
    W = dict(
        norm1_g=norm1_g, norm2_g=norm2_g, q_norm_g=q_norm_g, k_norm_g=k_norm_g,
        w_in=w_in.astype(BF16),
        ssm_a_re=ssm_a_re, ssm_a_im=ssm_a_im, ssm_log_dt=ssm_log_dt, ssm_b_re=ssm_b_re, ssm_b_im=ssm_b_im,
        ssm_c_re=ssm_c_re, ssm_c_im=ssm_c_im, ssm_d=ssm_d, ssm_glu_w=ssm_glu_w.astype(BF16), ssm_glu_b=ssm_glu_b,
        pool_w=pool_w.astype(BF16), pool_scale=pool_scale,
        conv_dw_w=conv_dw_w, conv_dw_b=conv_dw_b, conv_ln_g=conv_ln_g, conv_ln_b=conv_ln_b,
        w_up_attn=_col_tiles(w_up_attn), w_up_ssm=_col_tiles(w_up_ssm), w_up_pool=_col_tiles(w_up_pool),
        w_up_conv=_col_tiles(w_up_conv), w_out=w_out.astype(BF16),
        router_wt=router_w.T, router_b=router_b,
        moe_w_gate=moe_w_gate, moe_w_up=moe_w_up, moe_w_down=moe_w_down, final_g=final_g,
    )
    cc = jnp.concatenate([c, c_ctx[None, :], jnp.zeros((SUBLANES - BATCH - 1, D_MODEL), F32)], axis=0)
    mod_all = ada_modulation(cc, ada_w, ada_b)
    xa = jnp.concatenate([x.reshape(N_LAT, D_MODEL), ctx.reshape(N_CTX, D_MODEL)], axis=0)
    for layer in range(DEPTH):
        mod = mod_all[layer, :BATCH + 1].reshape(BATCH + 1, 6, D_MODEL)
        xa = _layer(layer, xa, mod, W, with_ctx=(layer < DEPTH - 1), final_norm=(layer == DEPTH - 1))
    return xa[:N_LAT].reshape(BATCH, SEQ, D_MODEL)
```

```python
import functools
import math

import jax
import jax.numpy as jnp
from jax import lax
from jax.experimental import pallas as pl
from jax.experimental.pallas import tpu as pltpu

F32 = jnp.float32
BF16 = jnp.bfloat16

D_MODEL = 2048
BATCH = 2
SEQ = 4096
DEPTH = 2
GRID_W = 64
CTX_LEN = 256
N_HEADS = 8
N_KV_HEADS = 2
HEAD_DIM = 128
ROPE_THETA = 10000.0
SSM_WIDTH = 512
SSM_GROUP = 16
SSM_GROUPS = SSM_WIDTH // SSM_GROUP
SSM_STATE = 64
POOL_WIDTH = 512
POOL_WINDOWS = (2, 4, 8, 16)
POOL_GROUP = POOL_WIDTH // len(POOL_WINDOWS)
CONV_WIDTH = 512
CONV_TAPS = 31
N_BRANCHES = 4
N_EXPERTS = 64
N_EXPERT_GROUPS = 8
EXPERTS_PER_GROUP = N_EXPERTS // N_EXPERT_GROUPS
TOP_K = 2
EXPERT_FF = 512
EPS = 1e-6

Q_W = N_HEADS * HEAD_DIM
KV_W = N_KV_HEADS * HEAD_DIM
IN_W = 2 * KV_W + SSM_WIDTH + Q_W + POOL_WIDTH + 2 * CONV_WIDTH + N_BRANCHES * D_MODEL
CTX_STATE_W = 2 * KV_W + SSM_WIDTH
COL_K, COL_V, COL_SSM, COL_Q = 0, KV_W, 2 * KV_W, 2 * KV_W + SSM_WIDTH
COL_POOL = COL_Q + Q_W
COL_CONV = COL_POOL + POOL_WIDTH
COL_GATE = COL_CONV + 2 * CONV_WIDTH

N_LAT = BATCH * SEQ
N_CTX = BATCH * CTX_LEN
S_ALL = CTX_LEN + SEQ

V7X_VMEM_BYTES = 64 * 1024 * 1024
SUBLANES = 8
LANES = 128
BF16_ROWS = 16

TM = 512
TN = 512
TN_IN = IN_W // 4
TILES_PER_BATCH = SEQ // TM
TQ = 512
KEY_PREP_T = 256
SSM_T = 256
SSM_TC = SSM_T // SUBLANES
SSM_HALF = SSM_WIDTH // 2
SSM_HSTATE = SSM_GROUPS // 2 * SSM_STATE
SEQ_T = 256
SEQ_PAD = 16
MOE_BLOCK = 128
WEIGHT_SLOTS = 3
WEIGHT_DMA_PRIORITY = 1
ROUTE_T = 512


def _cparams(sem, vmem_mb):
    return pltpu.CompilerParams(dimension_semantics=sem, vmem_limit_bytes=vmem_mb * 1024 * 1024)


def _mod_id(tile):
    return jnp.minimum(tile // TILES_PER_BATCH, BATCH)


def _silu(x):
    return x * jax.nn.sigmoid(x)


def _ada_kernel(c_ref, w_ref, b_ref, o_ref):
    c = c_ref[...]
    a = _silu(c).astype(BF16)
    o_ref[0] = jnp.dot(a, w_ref[0].astype(BF16), preferred_element_type=F32) + b_ref[0]


def ada_modulation(cc, ada_w, ada_b):
    tn = 1024
    n6 = 6 * D_MODEL
    return pl.pallas_call(
        _ada_kernel,
        grid=(DEPTH, n6 // tn),
        in_specs=[
            pl.BlockSpec((SUBLANES, D_MODEL), lambda l, j: (0, 0)),
            pl.BlockSpec((1, D_MODEL, tn), lambda l, j: (l, 0, j)),
            pl.BlockSpec((1, 1, tn), lambda l, j: (l, 0, j)),
        ],
        out_specs=pl.BlockSpec((1, SUBLANES, tn), lambda l, j: (l, 0, j)),
        out_shape=jax.ShapeDtypeStruct((DEPTH, SUBLANES, n6), F32),
        compiler_params=_cparams(("arbitrary", "arbitrary"), 40),
        name="ada_modulation",
    )(cc, ada_w, ada_b.reshape(DEPTH, 1, n6))


def _inproj_kernel(x_ref, mod_ref, g_ref, w_ref, o_ref, h_ref):
    @pl.when(pl.program_id(1) == 0)
    def _():
        x = x_ref[...]
        ms = jnp.mean(x * x, axis=-1, keepdims=True)
        y = x * lax.rsqrt(ms + EPS) * g_ref[0]
        h = y * (1.0 + mod_ref[0, 1:2, :]) + mod_ref[0, 0:1, :]
        h_ref[...] = h.astype(BF16)

    o_ref[...] = jnp.dot(h_ref[...], w_ref[0], preferred_element_type=F32).astype(o_ref.dtype)


def in_projection(xa, mod, norm_g, w_in, layer, row_tile0, n_row_tiles, n_cols, tn):
    return pl.pallas_call(
        _inproj_kernel,
        grid=(n_row_tiles, n_cols // tn),
        in_specs=[
            pl.BlockSpec((TM, D_MODEL), lambda i, j: (i + row_tile0, 0)),
            pl.BlockSpec((1, 6, D_MODEL), lambda i, j: (_mod_id(i + row_tile0), 0, 0)),
            pl.BlockSpec((1, 1, D_MODEL), lambda i, j: (layer, 0, 0)),
            pl.BlockSpec((1, D_MODEL, tn), lambda i, j: (layer, 0, j)),
        ],
        out_specs=pl.BlockSpec((TM, tn), lambda i, j: (i, j)),
        out_shape=jax.ShapeDtypeStruct((n_row_tiles * TM, n_cols), BF16),
        scratch_shapes=[pltpu.VMEM((TM, D_MODEL), BF16)],
        compiler_params=_cparams(("arbitrary", "arbitrary"), 56),
        name="in_projection",
    )(xa, mod, norm_g.reshape(DEPTH, 1, D_MODEL), w_in)


def _rope_tables(n_ctx_rows):
    half = HEAD_DIM // 4
    inv_freq = ROPE_THETA ** (-jnp.arange(half, dtype=F32) / half)
    t = jnp.arange(SEQ)
    ang_r = (t // GRID_W).astype(F32)[:, None] * inv_freq[None, :]
    ang_c = (t % GRID_W).astype(F32)[:, None] * inv_freq[None, :]
    cos = jnp.concatenate([jnp.cos(ang_r), jnp.cos(ang_r), jnp.cos(ang_c), jnp.cos(ang_c)], axis=-1)
    sin = jnp.concatenate([-jnp.sin(ang_r), jnp.sin(ang_r), -jnp.sin(ang_c), jnp.sin(ang_c)], axis=-1)
    cos = jnp.concatenate([jnp.ones((n_ctx_rows, HEAD_DIM), F32), cos], axis=0)
    sin = jnp.concatenate([jnp.zeros((n_ctx_rows, HEAD_DIM), F32), sin], axis=0)
    return cos, sin


def _head_norm_rope(x, g, cos, sin):
    ms = jnp.mean(x * x, axis=-1, keepdims=True)
    y = x * lax.rsqrt(ms + EPS) * g
    lane = lax.broadcasted_iota(jnp.int32, y.shape, 1)
    first = (lane % (HEAD_DIM // 2)) < (HEAD_DIM // 4)
    partner = jnp.where(first, pltpu.roll(y, HEAD_DIM - HEAD_DIM // 4, 1), pltpu.roll(y, HEAD_DIM // 4, 1))
    return y * cos + partner * sin


def _attn_kernel(q_ref, k_ref, v_ref, cq_ref, sq_ref, ck_ref, sk_ref, gq_ref, gk_ref, o_ref, ks_ref, va_ref, *,
                 n_keys):
    @pl.when(pl.program_id(2) == 0)
    def _():
        def prep(c, carry):
            r0 = pl.multiple_of(c * KEY_PREP_T, KEY_PREP_T)
            rows = pl.ds(r0, KEY_PREP_T)
            kk = k_ref[0, rows, :].astype(F32)
            kn = _head_norm_rope(kk, gk_ref[0], ck_ref[rows, :], sk_ref[rows, :])
            ks_ref[rows, :] = kn.astype(BF16)
            return carry

        lax.fori_loop(0, n_keys // KEY_PREP_T, prep, 0)
        va_ref[:, 0:HEAD_DIM] = v_ref[0]
        va_ref[:, HEAD_DIM:2 * HEAD_DIM] = jnp.ones((n_keys, HEAD_DIM), BF16)

    k = ks_ref[...]
    va = va_ref[...]
    scale = HEAD_DIM ** -0.5 * math.log2(math.e)
    for hh in range(N_HEADS // N_KV_HEADS):
        cols = slice(hh * HEAD_DIM, (hh + 1) * HEAD_DIM)
        q = q_ref[:, cols].astype(F32)
        qn = _head_norm_rope(q, gq_ref[0], cq_ref[...], sq_ref[...]) * scale
        s = lax.dot_general(qn.astype(BF16), k, (((1,), (1,)), ((), ())), preferred_element_type=F32)
        m = jnp.max(s, axis=-1, keepdims=True)
        p = jnp.exp2(s - m)
        oa = jnp.dot(p.astype(BF16), va, preferred_element_type=F32)
        o = oa[:, 0:HEAD_DIM] / oa[:, HEAD_DIM:HEAD_DIM + 1]
        o_ref[:, cols] = o.astype(o_ref.dtype)


def attention(q_src, q_col0, q_row0, n_q, kv, cos_q, sin_q, cos_k, sin_k, q_norm_g, k_norm_g, layer):
    n_keys = kv.shape[1]
    grp_w = Q_W // N_KV_HEADS
    tq = min(TQ, n_q)
    qb = n_q // tq
    kern = functools.partial(_attn_kernel, n_keys=n_keys)
    return pl.pallas_call(
        kern,
        grid=(BATCH, N_KV_HEADS, qb),
        in_specs=[
            pl.BlockSpec((tq, grp_w), lambda b, g, i: (q_row0 // tq + b * qb + i, q_col0 // grp_w + g)),
            pl.BlockSpec((1, n_keys, HEAD_DIM), lambda b, g, i: (b, 0, g)),
            pl.BlockSpec((1, n_keys, HEAD_DIM), lambda b, g, i: (b, 0, N_KV_HEADS + g)),
            pl.BlockSpec((tq, HEAD_DIM), lambda b, g, i: (i, 0)),
            pl.BlockSpec((tq, HEAD_DIM), lambda b, g, i: (i, 0)),
            pl.BlockSpec((n_keys, HEAD_DIM), lambda b, g, i: (0, 0)),
            pl.BlockSpec((n_keys, HEAD_DIM), lambda b, g, i: (0, 0)),
            pl.BlockSpec((1, 1, HEAD_DIM), lambda b, g, i: (layer, 0, 0)),
            pl.BlockSpec((1, 1, HEAD_DIM), lambda b, g, i: (layer, 0, 0)),
        ],
        out_specs=pl.BlockSpec((tq, grp_w), lambda b, g, i: (b * qb + i, g)),
        out_shape=jax.ShapeDtypeStruct((BATCH * n_q, Q_W), BF16),
        scratch_shapes=[pltpu.VMEM((n_keys, HEAD_DIM), BF16), pltpu.VMEM((n_keys, 2 * HEAD_DIM), BF16)],
        compiler_params=_cparams(("arbitrary", "arbitrary", "arbitrary"), 56),
        name="attention",
    )(q_src, kv, kv, cos_q, sin_q, cos_k, sin_k,
      q_norm_g.reshape(DEPTH, 1, HEAD_DIM), k_norm_g.reshape(DEPTH, 1, HEAD_DIM))


def _cmul(ar, ai, br, bi):
    return ar * br - ai * bi, ar * bi + ai * br


SSM_CB = 512


def _ssm_kernel(u_ref, bm_ref, cm_ref, lam_ref, pw_ref, y_ref, bu_ref, yv_ref, carry_ref, *, reverse):
    @pl.when(pl.program_id(1) == 0)
    def _():
        carry_ref[...] = jnp.zeros_like(carry_ref)

    rr = lax.broadcasted_iota(jnp.int32, (SSM_T, SSM_T), 0)
    tt = lax.broadcasted_iota(jnp.int32, (SSM_T, SSM_T), 1)
    perm = jnp.where(tt == (rr % SUBLANES) * SSM_TC + rr // SUBLANES, 1.0, 0.0).astype(BF16)
    u_scan = jnp.dot(perm, u_ref[0], preferred_element_type=F32).astype(BF16)
    for h in range(2):
        bu_ref[h] = jnp.dot(u_scan[:, h * SSM_HALF:(h + 1) * SSM_HALF], bm_ref[h], preferred_element_type=F32)

    last_row = 0 if reverse else (SSM_T - SUBLANES)
    edge = (SUBLANES - 1) if reverse else 0
    toward = (lambda x, k: pltpu.roll(x, SUBLANES - k, 0)) if reverse else (lambda x, k: pltpu.roll(x, k, 0))

    def step_rows(j):
        jj = (SSM_TC - 1 - j) if reverse else j
        return pl.ds(pl.multiple_of(jj * SUBLANES, SUBLANES), SUBLANES)

    for h, cb in [(h, cb) for h in range(2) for cb in range(SSM_HSTATE // SSM_CB)]:
        re = slice(cb * SSM_CB, (cb + 1) * SSM_CB)
        im = slice(SSM_HSTATE + cb * SSM_CB, SSM_HSTATE + (cb + 1) * SSM_CB)
        lr = lam_ref[h, :, re]
        li = lam_ref[h, :, im]

        def local_scan(j, st, h=h, re=re, im=im, lr=lr, li=li):
            rows = step_rows(j)
            pr, pi = _cmul(lr, li, st[0], st[1])
            nr = pr + bu_ref[h, rows, re]
            ni = pi + bu_ref[h, rows, im]
            bu_ref[h, rows, re] = nr
            bu_ref[h, rows, im] = ni
            return nr, ni

        zero = jnp.zeros((SUBLANES, SSM_CB), F32)
        lax.fori_loop(0, SSM_TC, local_scan, (zero, zero), unroll=4)

        er = bu_ref[h, last_row:last_row + SUBLANES, re]
        ei = bu_ref[h, last_row:last_row + SUBLANES, im]
        row = lax.broadcasted_iota(jnp.int32, (SUBLANES, SSM_CB), 0)
        dist = (SUBLANES - 1 - row) if reverse else row
        xr = jnp.where(row == edge, carry_ref[h, :, re], toward(er, 1))
        xi = jnp.where(row == edge, carry_ref[h, :, im], toward(ei, 1))
        for lvl, k in enumerate((1, 2, 4)):
            ar = pw_ref[h, lvl, :, re]
            ai = pw_ref[h, lvl, :, im]
            sr = jnp.where(dist >= k, toward(xr, k), 0.0)
            si = jnp.where(dist >= k, toward(xi, k), 0.0)
            mr, mi = _cmul(ar, ai, sr, si)
            xr = xr + mr
            xi = xi + mi
        cr, ci = _cmul(pw_ref[h, 0, :, re], pw_ref[h, 0, :, im], xr, xi)
        carry_ref[h, :, re] = toward(cr + er, 1)
        carry_ref[h, :, im] = toward(ci + ei, 1)

        def add_carry(j, g, h=h, re=re, im=im, lr=lr, li=li):
            rows = step_rows(j)
            bu_ref[h, rows, re] = bu_ref[h, rows, re] + g[0]
            bu_ref[h, rows, im] = bu_ref[h, rows, im] + g[1]
            return _cmul(lr, li, g[0], g[1])

        lax.fori_loop(0, SSM_TC, add_carry, _cmul(lr, li, xr, xi), unroll=4)

    for h in range(2):
        y = jnp.dot(bu_ref[h].astype(BF16), cm_ref[h], preferred_element_type=F32)
        for cb in range(SSM_HALF // LANES):
            cols = slice(cb * LANES, (cb + 1) * LANES)
            out_cols = slice(h * SSM_HALF + cb * LANES, h * SSM_HALF + (cb + 1) * LANES)
            slab = h * (SSM_HALF // LANES) + cb
            yv_ref[slab] = y[:, cols]
            for s in range(SUBLANES):
                y_ref[0, s * SSM_TC:(s + 1) * SSM_TC, out_cols] = yv_ref[slab, pl.ds(s, SSM_TC, stride=SUBLANES), :]


def _ssm_params(a_re, a_im, log_dt, b_re, b_im, c_re, c_im):
    lam = lax.complex(a_re.astype(F32), a_im.astype(F32))
    dt = jnp.exp(log_dt.astype(F32))[..., None]
    lam_bar = jnp.exp(lam * dt)
    b_bar = ((lam_bar - 1.0) / lam)[..., None] * lax.complex(b_re.astype(F32), b_im.astype(F32))
    gh = SSM_GROUPS // 2
    eye = jnp.eye(gh, dtype=F32)

    def b_block(m):
        m = m.reshape(2, 2, gh, SSM_STATE, SSM_GROUP)
        return jnp.einsum('dhgpc,gk->dhgckp', m, eye).reshape(2, 2, gh * SSM_GROUP, gh * SSM_STATE)

    def c_block(m):
        m = m.reshape(2, 2, gh, SSM_GROUP, SSM_STATE)
        return jnp.einsum('dhgcp,gk->dhkpgc', m, eye).reshape(2, 2, gh * SSM_STATE, gh * SSM_GROUP)

    bm = jnp.concatenate([b_block(jnp.real(b_bar)), b_block(jnp.imag(b_bar))], axis=-1).astype(BF16)
    cm = jnp.concatenate([c_block(c_re.astype(F32)), -c_block(c_im.astype(F32))], axis=-2).astype(BF16)

    def table(z):
        z = z.reshape(2, 2, SSM_HSTATE)
        t = jnp.concatenate([jnp.real(z), jnp.imag(z)], axis=-1)
        return jnp.broadcast_to(t[:, :, None, :], (2, 2, SUBLANES, 2 * SSM_HSTATE))

    lam_t = table(lam_bar)
    pw_t = jnp.stack([table(jnp.exp(lam * dt * (SSM_TC * k))) for k in (1, 2, 4)], axis=2)
    return bm, cm, lam_t, pw_t


SSM_CHUNKS = S_ALL // SSM_T


def ssm_scan(u_chunks, bm, cm, lam_t, pw_t, reverse):
    hs2 = 2 * SSM_HSTATE

    def chunk(b, c):
        if reverse:
            c = jnp.where(c == 0, 0, SSM_CHUNKS - c)
        return b * SSM_CHUNKS + c

    kern = functools.partial(_ssm_kernel, reverse=reverse)
    return pl.pallas_call(
        kern,
        grid=(BATCH, SSM_CHUNKS),
        in_specs=[
            pl.BlockSpec((1, SSM_T, SSM_WIDTH), lambda b, c: (chunk(b, c), 0, 0)),
            pl.BlockSpec((2, SSM_HALF, hs2), lambda b, c: (0, 0, 0)),
            pl.BlockSpec((2, hs2, SSM_HALF), lambda b, c: (0, 0, 0)),
            pl.BlockSpec((2, SUBLANES, hs2), lambda b, c: (0, 0, 0)),
            pl.BlockSpec((2, 3, SUBLANES, hs2), lambda b, c: (0, 0, 0, 0)),
        ],
        out_specs=pl.BlockSpec((1, SSM_T, SSM_WIDTH), lambda b, c: (chunk(b, c), 0, 0)),
        out_shape=jax.ShapeDtypeStruct(u_chunks.shape, F32),
        scratch_shapes=[pltpu.VMEM((2, SSM_T, hs2), F32), pltpu.VMEM((SSM_WIDTH // LANES, SSM_T, LANES), F32),
                        pltpu.VMEM((2, SUBLANES, hs2), F32)],
        compiler_params=_cparams(("arbitrary",) * 2, 32),
        name="ssm_scan",
    )(u_chunks, bm, cm, lam_t, pw_t)


def _ssm_chunk_of_tile(i):
    lat_tiles = SEQ // SSM_T
    return jnp.where(i < BATCH * lat_tiles,
                     (i // lat_tiles) * SSM_CHUNKS + 1 + i % lat_tiles,
                     (i - BATCH * lat_tiles) * SSM_CHUNKS)


def _gelu_tanh(x):
    return 0.5 * x * (1.0 + jnp.tanh(math.sqrt(2.0 / math.pi) * (x + 0.044715 * (x * x * x))))


def _ssm_out_kernel(u_ref, yf_ref, yb_ref, d_ref, w_ref, b_ref, o_ref):
    y = d_ref[0] * u_ref[...].astype(F32) + yf_ref[0] + yb_ref[0]
    y = _gelu_tanh(y)
    z = jnp.dot(y.astype(BF16), w_ref[0], preferred_element_type=F32) + b_ref[0]
    o_ref[...] = (y * jax.nn.sigmoid(z)).astype(o_ref.dtype)


def ssm_output(p, yf, yb, ssm_d, glu_w, glu_b, layer):
    n = p.shape[0]
    yspec = pl.BlockSpec((1, SSM_T, SSM_WIDTH), lambda i: (_ssm_chunk_of_tile(i), 0, 0))
    return pl.pallas_call(
        _ssm_out_kernel,
        grid=(n // SSM_T,),
        in_specs=[
            pl.BlockSpec((SSM_T, SSM_WIDTH), lambda i: (i, COL_SSM // SSM_WIDTH)),
            yspec,
            yspec,
            pl.BlockSpec((1, 1, SSM_WIDTH), lambda i: (layer, 0, 0)),
            pl.BlockSpec((1, SSM_WIDTH, SSM_WIDTH), lambda i: (layer, 0, 0)),
            pl.BlockSpec((1, 1, SSM_WIDTH), lambda i: (layer, 0, 0)),
        ],
        out_specs=pl.BlockSpec((SSM_T, SSM_WIDTH), lambda i: (i, 0)),
        out_shape=jax.ShapeDtypeStruct((n, SSM_WIDTH), BF16),
        compiler_params=_cparams(("arbitrary",), 32),
        name="ssm_output",
    )(p, yf, yb, ssm_d.reshape(DEPTH, 1, SSM_WIDTH), glu_w, glu_b.reshape(DEPTH, 1, SSM_WIDTH))


def _seq_tile(tile):
    lat_tiles = N_LAT // SEQ_T
    per_seq = SEQ // SEQ_T
    is_lat = tile < lat_tiles
    t = jnp.where(is_lat, tile % per_seq, (tile - lat_tiles) % (CTX_LEN // SEQ_T))
    last_t = jnp.where(is_lat, per_seq - 1, CTX_LEN // SEQ_T - 1)
    return t * SEQ_T, t == 0, t == last_t, jnp.where(is_lat, SEQ, CTX_LEN)


def _with_halo(prev_ref, cur_ref, next_ref, first, last):
    top = jnp.where(first, jnp.zeros((SEQ_PAD, cur_ref.shape[1]), cur_ref.dtype), prev_ref[SEQ_T - SEQ_PAD:SEQ_T, :])
    bot = jnp.where(last, jnp.zeros((SEQ_PAD, cur_ref.shape[1]), cur_ref.dtype), next_ref[0:SEQ_PAD, :])
    return jnp.concatenate([top, cur_ref[...], bot], axis=0)


def _halo_specs(width, col_block, n_tiles):
    return [pl.BlockSpec((SEQ_T, width), lambda g: (jnp.maximum(g - 1, 0), col_block)),
            pl.BlockSpec((SEQ_T, width), lambda g: (g, col_block)),
            pl.BlockSpec((SEQ_T, width), lambda g: (jnp.minimum(g + 1, n_tiles - 1), col_block))]


def _pool_kernel(prev_ref, cur_ref, next_ref, w_ref, s_ref, o_ref):
    t0, first, last, seq_len = _seq_tile(pl.program_id(0))
    halo = _with_halo(prev_ref, cur_ref, next_ref, first, last)
    centre = cur_ref[...].astype(F32)
    tt = lax.broadcasted_iota(jnp.int32, (SEQ_T, SEQ_T + 2 * SEQ_PAD), 0)
    rr = lax.broadcasted_iota(jnp.int32, (SEQ_T, SEQ_T + 2 * SEQ_PAD), 1) - SEQ_PAD
    tg = t0 + lax.broadcasted_iota(jnp.int32, (SEQ_T, 1), 0)
    for gi, w in enumerate(POOL_WINDOWS):
        cols = slice(gi * POOL_GROUP, (gi + 1) * POOL_GROUP)
        band = ((rr >= tt - w // 2) & (rr <= tt + w // 2 - 1)).astype(F32).astype(BF16)
        wsum = jnp.dot(band, halo[:, cols], preferred_element_type=F32)
        cnt = jnp.minimum(tg - w // 2 + w, seq_len) - jnp.maximum(tg - w // 2, 0)
        pooled = wsum / cnt.astype(F32) - centre[:, cols]
        mixed = jnp.dot(pooled.astype(BF16), w_ref[0, gi], preferred_element_type=F32)
        o_ref[:, cols] = (mixed * s_ref[0, :, cols]).astype(o_ref.dtype)


def pool_mixer(p, pool_w, pool_scale, layer):
    n_tiles = p.shape[0] // SEQ_T
    ng = len(POOL_WINDOWS)
    return pl.pallas_call(
        _pool_kernel,
        grid=(n_tiles,),
        in_specs=_halo_specs(POOL_WIDTH, COL_POOL // POOL_WIDTH, n_tiles) + [
            pl.BlockSpec((1, ng, POOL_GROUP, POOL_GROUP), lambda g: (layer, 0, 0, 0)),
            pl.BlockSpec((1, 1, POOL_WIDTH), lambda g: (layer, 0, 0)),
        ],
        out_specs=pl.BlockSpec((SEQ_T, POOL_WIDTH), lambda g: (g, 0)),
        out_shape=jax.ShapeDtypeStruct((p.shape[0], POOL_WIDTH), BF16),
        compiler_params=_cparams(("arbitrary",), 32),
        name="pool_mixer",
    )(p, p, p, pool_w, pool_scale.reshape(DEPTH, 1, POOL_WIDTH))


CONV_RB = 64


def _conv_kernel(ap_ref, ac_ref, an_ref, gp_ref, gc_ref, gn_ref, w_ref, b_ref, g_ref, beta_ref, o_ref, glu_ref):
    _, first, last, _ = _seq_tile(pl.program_id(0))
    a = _with_halo(ap_ref, ac_ref, an_ref, first, last).astype(F32)
    g = _with_halo(gp_ref, gc_ref, gn_ref, first, last).astype(F32)
    glu_ref[...] = a * jax.nn.sigmoid(g)
    off = SEQ_PAD - CONV_TAPS // 2
    for rb in range(SEQ_T // CONV_RB):
        parts = []
        for cb in range(CONV_WIDTH // LANES):
            cols = slice(cb * LANES, (cb + 1) * LANES)
            acc = jnp.zeros((CONV_RB, LANES), F32)
            for k in range(CONV_TAPS):
                acc = acc + glu_ref[rb * CONV_RB + k + off:rb * CONV_RB + k + off + CONV_RB, cols] * w_ref[0, k:k + 1, cols]
            parts.append(acc)
        y = jnp.concatenate(parts, axis=-1) + b_ref[0]
        yc = y - jnp.mean(y, axis=-1, keepdims=True)
        yn = yc * lax.rsqrt(jnp.mean(yc * yc, axis=-1, keepdims=True) + EPS)
        yn = yn * g_ref[0] + beta_ref[0]
        o_ref[rb * CONV_RB:(rb + 1) * CONV_RB, :] = _silu(yn).astype(o_ref.dtype)


def conv_mixer(p, dw_w, dw_b, ln_g, ln_b, layer):
    n_tiles = p.shape[0] // SEQ_T
    vec = lambda a: a.reshape(DEPTH, 1, CONV_WIDTH)
    vspec = pl.BlockSpec((1, 1, CONV_WIDTH), lambda g: (layer, 0, 0))
    a_block = COL_CONV // CONV_WIDTH
    return pl.pallas_call(
        _conv_kernel,
        grid=(n_tiles,),
        in_specs=_halo_specs(CONV_WIDTH, a_block, n_tiles) + _halo_specs(CONV_WIDTH, a_block + 1, n_tiles) + [
            pl.BlockSpec((1, CONV_TAPS, CONV_WIDTH), lambda g: (layer, 0, 0)),
            vspec, vspec, vspec,
        ],
        out_specs=pl.BlockSpec((SEQ_T, CONV_WIDTH), lambda g: (g, 0)),
        out_shape=jax.ShapeDtypeStruct((p.shape[0], CONV_WIDTH), BF16),
        scratch_shapes=[pltpu.VMEM((SEQ_T + 2 * SEQ_PAD, CONV_WIDTH), F32)],
        compiler_params=_cparams(("arbitrary",), 40),
        name="conv_mixer",
    )(p, p, p, p, p, p, dw_w, vec(dw_b), vec(ln_g), vec(ln_b))


def _merge_kernel(oa_ref, ys_ref, yp_ref, yc_ref, g0_ref, g1_ref, g2_ref, g3_ref,
                  wa_ref, ws_ref, wp_ref, wc_ref, o_ref):
    def branch(x_ref, w_ref, g_ref):
        up = jnp.dot(x_ref[...], w_ref[0, 0], preferred_element_type=F32)
        return jax.nn.sigmoid(g_ref[...].astype(F32)) * up

    m = (branch(oa_ref, wa_ref, g0_ref) + branch(ys_ref, ws_ref, g1_ref)
         + branch(yp_ref, wp_ref, g2_ref) + branch(yc_ref, wc_ref, g3_ref))
    o_ref[...] = m.astype(o_ref.dtype)


def merge_branches(p, o_attn, y_ssm, y_pool, y_conv, w_attn, w_ssm, w_pool, w_conv, layer):
    n = o_attn.shape[0]
    nct = D_MODEL // TN
    gate_tile0 = COL_GATE // TN

    def gate_spec(br):
        return pl.BlockSpec((TM, TN), lambda i, j: (i, gate_tile0 + br * nct + j))

    def x_spec(width):
        return pl.BlockSpec((TM, width), lambda i, j: (i, 0))

    def w_spec(width):
        return pl.BlockSpec((1, 1, width, TN), lambda i, j: (layer, j, 0, 0))

    return pl.pallas_call(
        _merge_kernel,
        grid=(n // TM, nct),
        in_specs=[x_spec(Q_W), x_spec(SSM_WIDTH), x_spec(POOL_WIDTH), x_spec(CONV_WIDTH),
                  gate_spec(0), gate_spec(1), gate_spec(2), gate_spec(3),
                  w_spec(Q_W), w_spec(SSM_WIDTH), w_spec(POOL_WIDTH), w_spec(CONV_WIDTH)],
        out_specs=pl.BlockSpec((TM, TN), lambda i, j: (i, j)),
        out_shape=jax.ShapeDtypeStruct((n, D_MODEL), BF16),
        compiler_params=_cparams(("arbitrary", "arbitrary"), 40),
        name="merge_branches",
    )(o_attn, y_ssm, y_pool, y_conv, p, p, p, p, w_attn, w_ssm, w_pool, w_conv)


def _split_bf16(x):
    hi = x.astype(BF16)
    lo = (x - hi.astype(F32)).astype(BF16)
    return hi, lo


SLAB = D_MODEL // LANES


def _store_row_slabs(ref, row0, x):
    n = x.shape[0]
    for j in range(SLAB):
        ref[pl.ds(row0 * SLAB + j, n, stride=SLAB), :] = x[:, j * LANES:(j + 1) * LANES]


def _load_row_slabs(ref, row0, n):
    return jnp.concatenate([ref[pl.ds(row0 * SLAB + j, n, stride=SLAB), :] for j in range(SLAB)], axis=-1)


def _outproj_kernel(m_ref, x_ref, mod_ref, g_ref, w_ref, rw_ref, xo_ref, h_ref, lg_ref):
    r_hi, r_lo = _split_bf16(rw_ref[...])
    nt = (((1,), (1,)), ((), ()))
    half = m_ref.shape[0] // 2
    for c in range(2):
        rows = slice(c * half, (c + 1) * half)
        mix = jnp.dot(m_ref[rows, :], w_ref[0], preferred_element_type=F32)
        x = x_ref[rows, :] + mod_ref[0, 2:3, :] * mix
        xo_ref[rows, :] = x
        ms = jnp.mean(x * x, axis=-1, keepdims=True)
        h = x * lax.rsqrt(ms + EPS) * g_ref[0]
        h = h * (1.0 + mod_ref[0, 4:5, :]) + mod_ref[0, 3:4, :]
        _store_row_slabs(h_ref, c * half, h)
        h_hi, h_lo = _split_bf16(h)
        lg_ref[:, rows] = (lax.dot_general(r_hi, h_hi, nt, preferred_element_type=F32)
                           + lax.dot_general(r_hi, h_lo, nt, preferred_element_type=F32)
                           + lax.dot_general(r_lo, h_hi, nt, preferred_element_type=F32))


def out_projection(m, xa, mod, norm2_g, w_out, router_wt, layer):
    n = m.shape[0]
    tmo = TM
    return pl.pallas_call(
        _outproj_kernel,
        grid=(n // tmo,),
        in_specs=[
            pl.BlockSpec((tmo, D_MODEL), lambda i: (i, 0)),
            pl.BlockSpec((tmo, D_MODEL), lambda i: (i, 0)),
            pl.BlockSpec((1, 6, D_MODEL), lambda i: (_mod_id(i), 0, 0)),
            pl.BlockSpec((1, 1, D_MODEL), lambda i: (layer, 0, 0)),
            pl.BlockSpec((1, D_MODEL, D_MODEL), lambda i: (layer, 0, 0)),
            pl.BlockSpec((N_EXPERTS, D_MODEL), lambda i: (0, 0)),
        ],
        out_specs=[
            pl.BlockSpec((tmo, D_MODEL), lambda i: (i, 0)),
            pl.BlockSpec((tmo * SLAB, LANES), lambda i: (i, 0)),
            pl.BlockSpec((N_EXPERTS, tmo), lambda i: (0, i)),
        ],
        out_shape=[
            jax.ShapeDtypeStruct((n, D_MODEL), F32),
            jax.ShapeDtypeStruct((n * SLAB, LANES), F32),
            jax.ShapeDtypeStruct((N_EXPERTS, n), F32),
        ],
        compiler_params=_cparams(("arbitrary",), 60),
        name="out_projection",
    )(m, xa, mod, norm2_g.reshape(DEPTH, 1, D_MODEL), w_out, router_wt)


def _first_argmax(blk, row):
    m = jnp.max(blk, axis=0, keepdims=True)
    idx = jnp.min(jnp.where(blk == m, row, EXPERTS_PER_GROUP), axis=0, keepdims=True)
    return m, idx


def _route_kernel(lg_ref, rb_ref, e_ref, w_ref, rank_ref, cnt_ref, run_ref):
    @pl.when(pl.program_id(0) == 0)
    def _():
        run_ref[...] = jnp.zeros_like(run_ref)

    t = lg_ref.shape[1]
    scores = jax.nn.sigmoid(lg_ref[...])
    sel = scores + rb_ref[...]
    row = lax.broadcasted_iota(jnp.int32, (EXPERTS_PER_GROUP, t), 0)
    neg = jnp.float32(-jnp.inf)

    best = None
    for g in range(N_EXPERT_GROUPS):
        blk = sel[g * EXPERTS_PER_GROUP:(g + 1) * EXPERTS_PER_GROUP, :]
        m1, i1 = _first_argmax(blk, row)
        m2 = jnp.max(jnp.where(row == i1, neg, blk), axis=0, keepdims=True)
        gs = m1 + m2
        if best is None:
            best, grp = gs, jnp.zeros((1, t), jnp.int32)
        else:
            better = gs > best
            best = jnp.where(better, gs, best)
            grp = jnp.where(better, g, grp)

    in_sel = jnp.zeros((EXPERTS_PER_GROUP, t), F32)
    in_sc = jnp.zeros((EXPERTS_PER_GROUP, t), F32)
    for g in range(N_EXPERT_GROUPS):
        rows = slice(g * EXPERTS_PER_GROUP, (g + 1) * EXPERTS_PER_GROUP)
        in_sel = jnp.where(grp == g, sel[rows, :], in_sel)
        in_sc = jnp.where(grp == g, scores[rows, :], in_sc)
    _, i1 = _first_argmax(in_sel, row)
    _, i2 = _first_argmax(jnp.where(row == i1, neg, in_sel), row)
    s1 = jnp.sum(jnp.where(row == i1, in_sc, 0.0), axis=0, keepdims=True)
    s2 = jnp.sum(jnp.where(row == i2, in_sc, 0.0), axis=0, keepdims=True)
    e1 = grp * EXPERTS_PER_GROUP + i1
    e2 = grp * EXPERTS_PER_GROUP + i2
    e_ref[0:1, :] = e1
    e_ref[1:2, :] = e2
    w_ref[0:1, :] = s1 / (s1 + s2)
    w_ref[1:2, :] = s2 / (s1 + s2)

    erow = lax.broadcasted_iota(jnp.int32, (N_EXPERTS, t), 0)
    oh1 = erow == e1
    oh2 = erow == e2
    cnt = jnp.where(oh1 | oh2, 1.0, 0.0)
    a = lax.broadcasted_iota(jnp.int32, (t, t), 0)
    b = lax.broadcasted_iota(jnp.int32, (t, t), 1)
    before = jnp.where(a < b, 1.0, 0.0).astype(BF16)
    excl = jnp.dot(cnt.astype(BF16), before, preferred_element_type=F32) + run_ref[...]
    rank_ref[0:1, :] = jnp.sum(jnp.where(oh1, excl, 0.0), axis=0, keepdims=True).astype(jnp.int32)
    rank_ref[1:2, :] = jnp.sum(jnp.where(oh2, excl, 0.0), axis=0, keepdims=True).astype(jnp.int32)
    run_ref[...] = run_ref[...] + jnp.sum(cnt, axis=1, keepdims=True)
    cnt_ref[...] = run_ref[...]


def route(logits_t, router_b):
    n = logits_t.shape[1]
    return pl.pallas_call(
        _route_kernel,
        grid=(n // ROUTE_T,),
        in_specs=[
            pl.BlockSpec((N_EXPERTS, ROUTE_T), lambda i: (0, i)),
            pl.BlockSpec((N_EXPERTS, 1), lambda i: (0, 0)),
        ],
        out_specs=[
            pl.BlockSpec((TOP_K, ROUTE_T), lambda i: (0, i)),
            pl.BlockSpec((TOP_K, ROUTE_T), lambda i: (0, i)),
            pl.BlockSpec((TOP_K, ROUTE_T), lambda i: (0, i)),
            pl.BlockSpec((N_EXPERTS, 1), lambda i: (0, 0)),
        ],
        out_shape=[
            jax.ShapeDtypeStruct((TOP_K, n), jnp.int32),
            jax.ShapeDtypeStruct((TOP_K, n), F32),
            jax.ShapeDtypeStruct((TOP_K, n), jnp.int32),
            jax.ShapeDtypeStruct((N_EXPERTS, 1), F32),
        ],
        scratch_shapes=[pltpu.VMEM((N_EXPERTS, 1), F32)],
        compiler_params=_cparams(("arbitrary",), 32),
        name="route",
    )(logits_t, router_b.reshape(N_EXPERTS, 1).astype(F32))


def _expert_kernel(be_ref, slot_ref, nxt_ref, nu_ref, tok_ref, h_hbm, wg_hbm, wu_hbm, wd_hbm, o_ref,
                   xbuf_ref, wgf_ref, wuf_ref, wdf_ref, wgb_ref, wub_ref, wdb_ref, sem, xsem, *, layer):
    b = pl.program_id(0)
    n_used = nu_ref[0]
    e = be_ref[b]
    fresh = (b == 0) | (e != be_ref[jnp.maximum(b - 1, 0)])
    live = b < n_used
    has_next = b + 1 < n_used
    cur = b % 2

    def slab(row):
        return pl.ds(pl.multiple_of(row * SLAB, SLAB), SLAB)

    def row_gathers(block, buf):
        return [pltpu.make_async_copy(h_hbm.at[slab(tok_ref[block * MOE_BLOCK + r])],
                                      xbuf_ref.at[buf, slab(r)], xsem.at[buf]) for r in range(MOE_BLOCK)]

    @pl.when(b == 0)
    def _():
        for cp in row_gathers(0, 0):
            cp.start()

    def weight_copies(expert, slot):
        return (pltpu.make_async_copy(wg_hbm.at[layer, expert], wgf_ref.at[slot], sem.at[slot]),
                pltpu.make_async_copy(wu_hbm.at[layer, expert], wuf_ref.at[slot], sem.at[slot]),
                pltpu.make_async_copy(wd_hbm.at[layer, expert], wdf_ref.at[slot], sem.at[slot]))

    @pl.when(b == 0)
    def _():
        for cp in weight_copies(e, 0):
            cp.start(priority=WEIGHT_DMA_PRIORITY)

        @pl.when(nxt_ref[0] >= 0)
        def _():
            for cp in weight_copies(nxt_ref[0], 1):
                cp.start(priority=WEIGHT_DMA_PRIORITY)

    @pl.when(fresh & live)
    def _():
        slot = slot_ref[b]
        ahead = nxt_ref[pl.num_programs(0) + b]

        @pl.when(ahead >= 0)
        def _():
            ahead_slot = jnp.where(slot + 2 >= WEIGHT_SLOTS, slot + 2 - WEIGHT_SLOTS, slot + 2)
            for cp in weight_copies(ahead, ahead_slot):
                cp.start(priority=WEIGHT_DMA_PRIORITY)

        for cp in weight_copies(e, slot):
            cp.wait()
        wgb_ref[...] = wgf_ref[slot].astype(BF16)
        wub_ref[...] = wuf_ref[slot].astype(BF16)
        wdb_ref[...] = wdf_ref[slot].astype(BF16)

    def block(gather_next):
        for cp in row_gathers(b, cur):
            cp.wait()
        if gather_next:
            for cp in row_gathers(b + 1, 1 - cur):
                cp.start()
        x = _load_row_slabs(xbuf_ref.at[cur], 0, MOE_BLOCK).astype(BF16)
        gate = jnp.dot(x, wgb_ref[...], preferred_element_type=F32)
        up = jnp.dot(x, wub_ref[...], preferred_element_type=F32)
        hid = (_silu(gate) * up).astype(BF16)
        _store_row_slabs(o_ref, 0, jnp.dot(hid, wdb_ref[...], preferred_element_type=F32))

    @pl.when(live & has_next)
    def _():
        block(True)

    @pl.when(live & jnp.logical_not(has_next))
    def _():
        block(False)

    @pl.when(jnp.logical_not(live))
    def _():
        o_ref[...] = jnp.zeros_like(o_ref)


def _slot_tokens(dest_flat, n, n_slots):
    i32 = jnp.int32
    owner = jnp.full((n_slots,), -1, i32).at[dest_flat].set(jnp.arange(TOP_K * n, dtype=i32), unique_indices=True)
    return jnp.where(owner < 0, 0, owner % n).astype(i32)


def experts(blk_e, blk_slot, blk_next, n_used, slot_tok, h2, w_gate, w_up, w_down, layer):
    n_slots = slot_tok.shape[0]
    n_blocks = n_slots // MOE_BLOCK
    grid_spec = pltpu.PrefetchScalarGridSpec(
        num_scalar_prefetch=5,
        grid=(n_blocks,),
        in_specs=[
            pl.BlockSpec(memory_space=pl.ANY),
            pl.BlockSpec(memory_space=pl.ANY),
            pl.BlockSpec(memory_space=pl.ANY),
            pl.BlockSpec(memory_space=pl.ANY),
        ],
        out_specs=pl.BlockSpec((MOE_BLOCK * SLAB, LANES), lambda b, be, sl, nx, nu, tk: (b, 0)),
        scratch_shapes=[
            pltpu.VMEM((2, MOE_BLOCK * SLAB, LANES), F32),
            pltpu.VMEM((WEIGHT_SLOTS, D_MODEL, EXPERT_FF), F32),
            pltpu.VMEM((WEIGHT_SLOTS, D_MODEL, EXPERT_FF), F32),
            pltpu.VMEM((WEIGHT_SLOTS, EXPERT_FF, D_MODEL), F32),
            pltpu.VMEM((D_MODEL, EXPERT_FF), BF16),
            pltpu.VMEM((D_MODEL, EXPERT_FF), BF16),
            pltpu.VMEM((EXPERT_FF, D_MODEL), BF16),
            pltpu.SemaphoreType.DMA((WEIGHT_SLOTS,)),
            pltpu.SemaphoreType.DMA((2,)),
        ],
    )
    return pl.pallas_call(
        functools.partial(_expert_kernel, layer=layer),
        grid_spec=grid_spec,
        out_shape=jax.ShapeDtypeStruct((n_slots * SLAB, LANES), F32),
        compiler_params=_cparams(("arbitrary",), 60),
        name="experts",
    )(blk_e, blk_slot, blk_next, n_used, slot_tok, h2, w_gate, w_up, w_down)


COMBINE_T = 256


def _combine_kernel(dest_ref, x_ref, mod_ref, w_ref, fg_ref, ys_hbm, o_ref, buf_ref, sem, *, final_norm):
    n = dest_ref.shape[0] // TOP_K
    base = pl.program_id(0) * COMBINE_T

    def row_copy(t, k):
        src = pl.ds(pl.multiple_of(dest_ref[k * n + base + t] * SLAB, SLAB), SLAB)
        dst = pl.ds(pl.multiple_of(t * SLAB, SLAB), SLAB)
        return pltpu.make_async_copy(ys_hbm.at[src], buf_ref.at[k, dst], sem)

    def start(t, carry):
        for k in range(TOP_K):
            row_copy(t, k).start()
        return carry

    def wait(t, carry):
        for k in range(TOP_K):
            row_copy(t, k).wait()
        return carry

    lax.fori_loop(0, COMBINE_T, start, 0)
    lax.fori_loop(0, COMBINE_T, wait, 0)
    f = (w_ref[:, 0:1] * _load_row_slabs(buf_ref.at[0], 0, COMBINE_T)
         + w_ref[:, 1:2] * _load_row_slabs(buf_ref.at[1], 0, COMBINE_T))
    x = x_ref[...] + mod_ref[0, 5:6, :] * f
    if final_norm:
        ms = jnp.mean(x * x, axis=-1, keepdims=True)
        x = x * lax.rsqrt(ms + EPS) * fg_ref[...]
    o_ref[...] = x


def combine(dest_flat, xa, mod, wts_rows, ys, final_g, final_norm):
    n = xa.shape[0]
    tiles_per_tm = TM // COMBINE_T
    grid_spec = pltpu.PrefetchScalarGridSpec(
        num_scalar_prefetch=1,
        grid=(n // COMBINE_T,),
        in_specs=[
            pl.BlockSpec((COMBINE_T, D_MODEL), lambda i, d: (i, 0)),
            pl.BlockSpec((1, 6, D_MODEL), lambda i, d: (_mod_id(i // tiles_per_tm), 0, 0)),
            pl.BlockSpec((COMBINE_T, TOP_K), lambda i, d: (i, 0)),
            pl.BlockSpec((1, D_MODEL), lambda i, d: (0, 0)),
            pl.BlockSpec(memory_space=pl.ANY),
        ],
        out_specs=pl.BlockSpec((COMBINE_T, D_MODEL), lambda i, d: (i, 0)),
        scratch_shapes=[pltpu.VMEM((TOP_K, COMBINE_T * SLAB, LANES), F32), pltpu.SemaphoreType.DMA(())],
    )
    kern = functools.partial(_combine_kernel, final_norm=final_norm)
    return pl.pallas_call(
        kern,
        grid_spec=grid_spec,
        out_shape=jax.ShapeDtypeStruct((n, D_MODEL), F32),
        compiler_params=_cparams(("arbitrary",), 32),
        name="combine",
    )(dest_flat, xa, mod, wts_rows, final_g.reshape(1, D_MODEL), ys)


def _slot_plan(e_idx, rank, counts, n_blocks):
    i32 = jnp.int32
    cnt = counts.reshape(N_EXPERTS).astype(i32)
    padded = (cnt + MOE_BLOCK - 1) // MOE_BLOCK * MOE_BLOCK
    pad_end = jnp.cumsum(padded)
    pad_start = pad_end - padded
    eids = jnp.arange(N_EXPERTS, dtype=i32)

    def lookup(table, idx):
        return jnp.sum(jnp.where(idx[..., None] == eids, table, 0), axis=-1).astype(i32)

    dest = lookup(pad_start, e_idx) + rank
    blk_start = jnp.arange(n_blocks, dtype=i32) * MOE_BLOCK
    blk_e = jnp.minimum(jnp.sum(pad_end[None, :] <= blk_start[:, None], axis=1), N_EXPERTS - 1).astype(i32)
    n_used = (pad_end[-1] // MOE_BLOCK).astype(i32).reshape(1)
    has = cnt > 0
    run = jnp.cumsum(has.astype(i32)) - 1
    first_from = lax.cummin(jnp.where(has, eids, N_EXPERTS)[::-1])[::-1]
    nxt = jnp.concatenate([first_from[1:], jnp.full((1,), N_EXPERTS, i32)])
    nxt = jnp.where(nxt >= N_EXPERTS, -1, nxt)
    nxt2 = jnp.where(nxt >= 0, lookup(nxt, jnp.maximum(nxt, 0)), -1)
    blk_ahead = jnp.stack([lookup(nxt, blk_e), lookup(nxt2, blk_e)], axis=0).reshape(-1)
    return dest.reshape(-1).astype(i32), blk_e, lookup(run % WEIGHT_SLOTS, blk_e), blk_ahead, n_used


def _col_tiles(w):
    d, k, n = w.shape
    return w.astype(BF16).reshape(d, k, n // TN, TN).transpose(0, 2, 1, 3)


def _layer(layer, xa, mod, W, with_ctx, final_norm):
    n_lat_tiles = N_LAT // TM
    n_ctx_tiles = N_CTX // TM
    if with_ctx:
        p = in_projection(xa, mod, W['norm1_g'], W['w_in'], layer, 0, n_lat_tiles + n_ctx_tiles, IN_W, TN_IN)
        p_lat, p_ctx = p[:N_LAT], p[N_LAT:]
    else:
        p = in_projection(xa, mod, W['norm1_g'], W['w_in'], layer, 0, n_lat_tiles, IN_W, TN_IN)
        p_lat = p
        p_ctx = in_projection(xa, mod, W['norm1_g'], W['w_in'], layer, n_lat_tiles, n_ctx_tiles, CTX_STATE_W, TN)
    n_rows = p.shape[0]

    def seq(a, lo, hi, n_seq_rows):
        return a[:, lo:hi].reshape(BATCH, n_seq_rows, hi - lo)

    kv = jnp.concatenate([seq(p_ctx, 0, 2 * KV_W, CTX_LEN), seq(p_lat, 0, 2 * KV_W, SEQ)], axis=1)
    cos, sin = _rope_tables(CTX_LEN)
    o_attn = attention(p, COL_Q, 0, SEQ, kv, cos[CTX_LEN:], sin[CTX_LEN:], cos, sin,
                       W['q_norm_g'], W['k_norm_g'], layer)
    if with_ctx:
        kv_c = seq(p_ctx, 0, 2 * KV_W, CTX_LEN)
        ones, zeros = jnp.ones((CTX_LEN, HEAD_DIM), F32), jnp.zeros((CTX_LEN, HEAD_DIM), F32)
        o_attn_c = attention(p, COL_Q, N_LAT, CTX_LEN, kv_c, ones, zeros, ones, zeros,
                             W['q_norm_g'], W['k_norm_g'], layer)
        o_attn = jnp.concatenate([o_attn, o_attn_c], axis=0)

    u_seq = jnp.concatenate([seq(p_ctx, COL_SSM, COL_SSM + SSM_WIDTH, CTX_LEN),
                             seq(p_lat, COL_SSM, COL_SSM + SSM_WIDTH, SEQ)], axis=1)
    u_chunks = u_seq.reshape(BATCH * SSM_CHUNKS, SSM_T, SSM_WIDTH)
    bm, cm, lam_t, pw_t = _ssm_params(W['ssm_a_re'][layer], W['ssm_a_im'][layer], W['ssm_log_dt'][layer],
                                      W['ssm_b_re'][layer], W['ssm_b_im'][layer],
                                      W['ssm_c_re'][layer], W['ssm_c_im'][layer])
    yf = ssm_scan(u_chunks, bm[0], cm[0], lam_t[0], pw_t[0], reverse=False)
    yb = ssm_scan(u_chunks, bm[1], cm[1], lam_t[1], pw_t[1], reverse=True)
    y_ssm = ssm_output(p, yf, yb, W['ssm_d'], W['ssm_glu_w'], W['ssm_glu_b'], layer)

    y_pool = pool_mixer(p, W['pool_w'], W['pool_scale'], layer)
    y_conv = conv_mixer(p, W['conv_dw_w'], W['conv_dw_b'], W['conv_ln_g'], W['conv_ln_b'], layer)

    m = merge_branches(p, o_attn, y_ssm, y_pool, y_conv,
                       W['w_up_attn'], W['w_up_ssm'], W['w_up_pool'], W['w_up_conv'], layer)
    x1, h2, logits_t = out_projection(m, xa, mod, W['norm2_g'], W['w_out'], W['router_wt'], layer)

    e_idx, wts, rank, counts = route(logits_t, W['router_b'])
    n_blocks = -(-n_rows * TOP_K // MOE_BLOCK) + N_EXPERTS
    dest_flat, blk_e, blk_slot, blk_next, n_used = _slot_plan(e_idx, rank, counts, n_blocks)
    slot_tok = _slot_tokens(dest_flat, n_rows, n_blocks * MOE_BLOCK)
    ys = experts(blk_e, blk_slot, blk_next, n_used, slot_tok, h2,
                 W['moe_w_gate'], W['moe_w_up'], W['moe_w_down'], layer)
    x2 = combine(dest_flat, x1, mod, wts.T, ys, W['final_g'], final_norm)
    if with_ctx:
        return x2
    return jnp.concatenate([x2, xa[N_LAT:]], axis=0)


def kernel(x, c, ctx, c_ctx, ada_w, ada_b, norm1_g, norm2_g, w_in, q_norm_g, k_norm_g, ssm_a_re, ssm_a_im, ssm_log_dt, ssm_b_re, ssm_b_im, ssm_c_re, ssm_c_im, ssm_d, ssm_glu_w, ssm_glu_b, pool_w, pool_scale, conv_dw_w, conv_dw_b, conv_ln_g, conv_ln_b, w_up_attn, w_up_ssm, w_up_pool, w_up_conv, w_out, router_w, router_b, moe_w_gate, moe_w_up, moe_w_down, final_g):
    W = dict(
        norm1_g=norm1_g, norm2_g=norm2_g, q_norm_g=q_norm_g, k_norm_g=k_norm_g,
        w_in=w_in.astype(BF16),
        ssm_a_re=ssm_a_re, ssm_a_im=ssm_a_im, ssm_log_dt=ssm_log_dt, ssm_b_re=ssm_b_re, ssm_b_im=ssm_b_im,
        ssm_c_re=ssm_c_re, ssm_c_im=ssm_c_im, ssm_d=ssm_d, ssm_glu_w=ssm_glu_w.astype(BF16), ssm_glu_b=ssm_glu_b,
        pool_w=pool_w.astype(BF16), pool_scale=pool_scale,
        conv_dw_w=conv_dw_w, conv_dw_b=conv_dw_b, conv_ln_g=conv_ln_g, conv_ln_b=conv_ln_b,
        w_up_attn=_col_tiles(w_up_attn), w_up_ssm=_col_tiles(w_up_ssm), w_up_pool=_col_tiles(w_up_pool),
        w_up_conv=_col_tiles(w_up_conv), w_out=w_out.astype(BF16),
        router_wt=router_w.T, router_b=router_b,
        moe_w_gate=moe_w_gate, moe_w_up=moe_w_up, moe_w_down=moe_w_down, final_g=final_g,
    )
    cc = jnp.concatenate([c, c_ctx[None, :], jnp.zeros((SUBLANES - BATCH - 1, D_MODEL), F32)], axis=0)
    mod_all = ada_modulation(cc, ada_w, ada_b)
    xa = jnp.concatenate([x.reshape(N_LAT, D_MODEL), ctx.reshape(N_CTX, D_MODEL)], axis=0)
    for layer in range(DEPTH):
        mod = mod_all[layer, :BATCH + 1].reshape(BATCH + 1, 6, D_MODEL)
        xa = _layer(layer, xa, mod, W, with_ctx=(layer < DEPTH - 1), final_norm=(layer == DEPTH - 1))
    return xa[:N_LAT].reshape(BATCH, SEQ, D_MODEL)
```

```python
import functools
import math

import jax
import jax.numpy as jnp
from jax import lax
from jax.experimental import pallas as pl
from jax.experimental.pallas import tpu as pltpu

F32 = jnp.float32
BF16 = jnp.bfloat16

D_MODEL = 2048
BATCH = 2
SEQ = 4096
DEPTH = 2
GRID_W = 64
CTX_LEN = 256
N_HEADS = 8
N_KV_HEADS = 2
HEAD_DIM = 128
ROPE_THETA = 10000.0
SSM_WIDTH = 512
SSM_GROUP = 16
SSM_GROUPS = SSM_WIDTH // SSM_GROUP
SSM_STATE = 64
POOL_WIDTH = 512
POOL_WINDOWS = (2, 4, 8, 16)
POOL_GROUP = POOL_WIDTH // len(POOL_WINDOWS)
CONV_WIDTH = 512
CONV_TAPS = 31
N_BRANCHES = 4
N_EXPERTS = 64
N_EXPERT_GROUPS = 8
EXPERTS_PER_GROUP = N_EXPERTS // N_EXPERT_GROUPS
TOP_K = 2
EXPERT_FF = 512
EPS = 1e-6

Q_W = N_HEADS * HEAD_DIM
KV_W = N_KV_HEADS * HEAD_DIM
IN_W = 2 * KV_W + SSM_WIDTH + Q_W + POOL_WIDTH + 2 * CONV_WIDTH + N_BRANCHES * D_MODEL
CTX_STATE_W = 2 * KV_W + SSM_WIDTH
COL_K, COL_V, COL_SSM, COL_Q = 0, KV_W, 2 * KV_W, 2 * KV_W + SSM_WIDTH
COL_POOL = COL_Q + Q_W
COL_CONV = COL_POOL + POOL_WIDTH
COL_GATE = COL_CONV + 2 * CONV_WIDTH

N_LAT = BATCH * SEQ
N_CTX = BATCH * CTX_LEN
S_ALL = CTX_LEN + SEQ

V7X_VMEM_BYTES = 64 * 1024 * 1024
SUBLANES = 8
LANES = 128
BF16_ROWS = 16

TM = 512
TN = 512
TN_IN = IN_W // 4
TILES_PER_BATCH = SEQ // TM
TQ = 512
KEY_PREP_T = 256
SSM_T = 256
SSM_TC = SSM_T // SUBLANES
SSM_HALF = SSM_WIDTH // 2
SSM_HSTATE = SSM_GROUPS // 2 * SSM_STATE
SEQ_T = 256
SEQ_PAD = 16
MOE_BLOCK = 128
WEIGHT_SLOTS = 3
WEIGHT_DMA_PRIORITY = 1
ROUTE_T = 512


def _cparams(sem, vmem_mb):
    return pltpu.CompilerParams(dimension_semantics=sem, vmem_limit_bytes=vmem_mb * 1024 * 1024)


def _mod_id(tile):
    return jnp.minimum(tile // TILES_PER_BATCH, BATCH)


def _silu(x):
    return x * jax.nn.sigmoid(x)


def _ada_kernel(c_ref, w_ref, b_ref, o_ref):
    c = c_ref[...]
    a = _silu(c).astype(BF16)
    o_ref[0] = jnp.dot(a, w_ref[0].astype(BF16), preferred_element_type=F32) + b_ref[0]


def ada_modulation(cc, ada_w, ada_b):
    tn = 1024
    n6 = 6 * D_MODEL
    return pl.pallas_call(
        _ada_kernel,
        grid=(DEPTH, n6 // tn),
        in_specs=[
            pl.BlockSpec((SUBLANES, D_MODEL), lambda l, j: (0, 0)),
            pl.BlockSpec((1, D_MODEL, tn), lambda l, j: (l, 0, j)),
            pl.BlockSpec((1, 1, tn), lambda l, j: (l, 0, j)),
        ],
        out_specs=pl.BlockSpec((1, SUBLANES, tn), lambda l, j: (l, 0, j)),
        out_shape=jax.ShapeDtypeStruct((DEPTH, SUBLANES, n6), F32),
        compiler_params=_cparams(("arbitrary", "arbitrary"), 40),
        name="ada_modulation",
    )(cc, ada_w, ada_b.reshape(DEPTH, 1, n6))


def _inproj_kernel(x_ref, mod_ref, g_ref, w_ref, o_ref, h_ref):
    @pl.when(pl.program_id(1) == 0)
    def _():
        x = x_ref[...]
        ms = jnp.mean(x * x, axis=-1, keepdims=True)
        y = x * lax.rsqrt(ms + EPS) * g_ref[0]
        h = y * (1.0 + mod_ref[0, 1:2, :]) + mod_ref[0, 0:1, :]
        h_ref[...] = h.astype(BF16)

    o_ref[...] = jnp.dot(h_ref[...], w_ref[0], preferred_element_type=F32).astype(o_ref.dtype)


def in_projection(xa, mod, norm_g, w_in, layer, row_tile0, n_row_tiles, n_cols, tn):
    return pl.pallas_call(
        _inproj_kernel,
        grid=(n_row_tiles, n_cols // tn),
        in_specs=[
            pl.BlockSpec((TM, D_MODEL), lambda i, j: (i + row_tile0, 0)),
            pl.BlockSpec((1, 6, D_MODEL), lambda i, j: (_mod_id(i + row_tile0), 0, 0)),
            pl.BlockSpec((1, 1, D_MODEL), lambda i, j: (layer, 0, 0)),
            pl.BlockSpec((1, D_MODEL, tn), lambda i, j: (layer, 0, j)),
        ],
        out_specs=pl.BlockSpec((TM, tn), lambda i, j: (i, j)),
        out_shape=jax.ShapeDtypeStruct((n_row_tiles * TM, n_cols), BF16),
        scratch_shapes=[pltpu.VMEM((TM, D_MODEL), BF16)],
        compiler_params=_cparams(("arbitrary", "arbitrary"), 56),
        name="in_projection",
    )(xa, mod, norm_g.reshape(DEPTH, 1, D_MODEL), w_in)


def _rope_tables(n_ctx_rows):
    half = HEAD_DIM // 4
    inv_freq = ROPE_THETA ** (-jnp.arange(half, dtype=F32) / half)
    t = jnp.arange(SEQ)
    ang_r = (t // GRID_W).astype(F32)[:, None] * inv_freq[None, :]
    ang_c = (t % GRID_W).astype(F32)[:, None] * inv_freq[None, :]
    cos = jnp.concatenate([jnp.cos(ang_r), jnp.cos(ang_r), jnp.cos(ang_c), jnp.cos(ang_c)], axis=-1)
    sin = jnp.concatenate([-jnp.sin(ang_r), jnp.sin(ang_r), -jnp.sin(ang_c), jnp.sin(ang_c)], axis=-1)
    cos = jnp.concatenate([jnp.ones((n_ctx_rows, HEAD_DIM), F32), cos], axis=0)
    sin = jnp.concatenate([jnp.zeros((n_ctx_rows, HEAD_DIM), F32), sin], axis=0)
    return cos, sin


def _head_norm_rope(x, g, cos, sin):
    ms = jnp.mean(x * x, axis=-1, keepdims=True)
    y = x * lax.rsqrt(ms + EPS) * g
    lane = lax.broadcasted_iota(jnp.int32, y.shape, 1)
    first = (lane % (HEAD_DIM // 2)) < (HEAD_DIM // 4)
    partner = jnp.where(first, pltpu.roll(y, HEAD_DIM - HEAD_DIM // 4, 1), pltpu.roll(y, HEAD_DIM // 4, 1))
    return y * cos + partner * sin


def _attn_kernel(q_ref, k_ref, v_ref, cq_ref, sq_ref, ck_ref, sk_ref, gq_ref, gk_ref, o_ref, ks_ref, va_ref, *,
                 n_keys):
    @pl.when(pl.program_id(2) == 0)
    def _():
        def prep(c, carry):
            r0 = pl.multiple_of(c * KEY_PREP_T, KEY_PREP_T)
            rows = pl.ds(r0, KEY_PREP_T)
            kk = k_ref[0, rows, :].astype(F32)
            kn = _head_norm_rope(kk, gk_ref[0], ck_ref[rows, :], sk_ref[rows, :])
            ks_ref[rows, :] = kn.astype(BF16)
            return carry

        lax.fori_loop(0, n_keys // KEY_PREP_T, prep, 0)
        va_ref[:, 0:HEAD_DIM] = v_ref[0]
        va_ref[:, HEAD_DIM:2 * HEAD_DIM] = jnp.ones((n_keys, HEAD_DIM), BF16)

    k = ks_ref[...]
    va = va_ref[...]
    scale = HEAD_DIM ** -0.5 * math.log2(math.e)
    for hh in range(N_HEADS // N_KV_HEADS):
        cols = slice(hh * HEAD_DIM, (hh + 1) * HEAD_DIM)
        q = q_ref[:, cols].astype(F32)
        qn = _head_norm_rope(q, gq_ref[0], cq_ref[...], sq_ref[...]) * scale
        s = lax.dot_general(qn.astype(BF16), k, (((1,), (1,)), ((), ())), preferred_element_type=F32)
        m = jnp.max(s, axis=-1, keepdims=True)
        p = jnp.exp2(s - m)
        oa = jnp.dot(p.astype(BF16), va, preferred_element_type=F32)
        o = oa[:, 0:HEAD_DIM] / oa[:, HEAD_DIM:HEAD_DIM + 1]
        o_ref[:, cols] = o.astype(o_ref.dtype)


def attention(q_src, q_col0, q_row0, n_q, kv, cos_q, sin_q, cos_k, sin_k, q_norm_g, k_norm_g, layer):
    n_keys = kv.shape[1]
    grp_w = Q_W // N_KV_HEADS
    tq = min(TQ, n_q)
    qb = n_q // tq
    kern = functools.partial(_attn_kernel, n_keys=n_keys)
    return pl.pallas_call(
        kern,
        grid=(BATCH, N_KV_HEADS, qb),
        in_specs=[
            pl.BlockSpec((tq, grp_w), lambda b, g, i: (q_row0 // tq + b * qb + i, q_col0 // grp_w + g)),
            pl.BlockSpec((1, n_keys, HEAD_DIM), lambda b, g, i: (b, 0, g)),
            pl.BlockSpec((1, n_keys, HEAD_DIM), lambda b, g, i: (b, 0, N_KV_HEADS + g)),
            pl.BlockSpec((tq, HEAD_DIM), lambda b, g, i: (i, 0)),
            pl.BlockSpec((tq, HEAD_DIM), lambda b, g, i: (i, 0)),
            pl.BlockSpec((n_keys, HEAD_DIM), lambda b, g, i: (0, 0)),
            pl.BlockSpec((n_keys, HEAD_DIM), lambda b, g, i: (0, 0)),
            pl.BlockSpec((1, 1, HEAD_DIM), lambda b, g, i: (layer, 0, 0)),
            pl.BlockSpec((1, 1, HEAD_DIM), lambda b, g, i: (layer, 0, 0)),
        ],
        out_specs=pl.BlockSpec((tq, grp_w), lambda b, g, i: (b * qb + i, g)),
        out_shape=jax.ShapeDtypeStruct((BATCH * n_q, Q_W), BF16),
        scratch_shapes=[pltpu.VMEM((n_keys, HEAD_DIM), BF16), pltpu.VMEM((n_keys, 2 * HEAD_DIM), BF16)],
        compiler_params=_cparams(("arbitrary", "arbitrary", "arbitrary"), 56),
        name="attention",
    )(q_src, kv, kv, cos_q, sin_q, cos_k, sin_k,
      q_norm_g.reshape(DEPTH, 1, HEAD_DIM), k_norm_g.reshape(DEPTH, 1, HEAD_DIM))


def _cmul(ar, ai, br, bi):
    return ar * br - ai * bi, ar * bi + ai * br


SSM_CB = 512


def _ssm_kernel(u_ref, bm_ref, cm_ref, lam_ref, pw_ref, y_ref, bu_ref, yv_ref, carry_ref, *, reverse):
    @pl.when(pl.program_id(1) == 0)
    def _():
        carry_ref[...] = jnp.zeros_like(carry_ref)

    rr = lax.broadcasted_iota(jnp.int32, (SSM_T, SSM_T), 0)
    tt = lax.broadcasted_iota(jnp.int32, (SSM_T, SSM_T), 1)
    perm = jnp.where(tt == (rr % SUBLANES) * SSM_TC + rr // SUBLANES, 1.0, 0.0).astype(BF16)
    u_scan = jnp.dot(perm, u_ref[0], preferred_element_type=F32).astype(BF16)
    for h in range(2):
        bu_ref[h] = jnp.dot(u_scan[:, h * SSM_HALF:(h + 1) * SSM_HALF], bm_ref[h], preferred_element_type=F32)

    last_row = 0 if reverse else (SSM_T - SUBLANES)
    edge = (SUBLANES - 1) if reverse else 0
    toward = (lambda x, k: pltpu.roll(x, SUBLANES - k, 0)) if reverse else (lambda x, k: pltpu.roll(x, k, 0))

    def step_rows(j):
        jj = (SSM_TC - 1 - j) if reverse else j
        return pl.ds(pl.multiple_of(jj * SUBLANES, SUBLANES), SUBLANES)

    for h, cb in [(h, cb) for h in range(2) for cb in range(SSM_HSTATE // SSM_CB)]:
        re = slice(cb * SSM_CB, (cb + 1) * SSM_CB)
        im = slice(SSM_HSTATE + cb * SSM_CB, SSM_HSTATE + (cb + 1) * SSM_CB)
        lr = lam_ref[h, :, re]
        li = lam_ref[h, :, im]

        def local_scan(j, st, h=h, re=re, im=im, lr=lr, li=li):
            rows = step_rows(j)
            pr, pi = _cmul(lr, li, st[0], st[1])
            nr = pr + bu_ref[h, rows, re]
            ni = pi + bu_ref[h, rows, im]
            bu_ref[h, rows, re] = nr
            bu_ref[h, rows, im] = ni
            return nr, ni

        zero = jnp.zeros((SUBLANES, SSM_CB), F32)
        lax.fori_loop(0, SSM_TC, local_scan, (zero, zero), unroll=4)

        er = bu_ref[h, last_row:last_row + SUBLANES, re]
        ei = bu_ref[h, last_row:last_row + SUBLANES, im]
        row = lax.broadcasted_iota(jnp.int32, (SUBLANES, SSM_CB), 0)
        dist = (SUBLANES - 1 - row) if reverse else row
        xr = jnp.where(row == edge, carry_ref[h, :, re], toward(er, 1))
        xi = jnp.where(row == edge, carry_ref[h, :, im], toward(ei, 1))
        for lvl, k in enumerate((1, 2, 4)):
            ar = pw_ref[h, lvl, :, re]
            ai = pw_ref[h, lvl, :, im]
            sr = jnp.where(dist >= k, toward(xr, k), 0.0)
            si = jnp.where(dist >= k, toward(xi, k), 0.0)
            mr, mi = _cmul(ar, ai, sr, si)
            xr = xr + mr
            xi = xi + mi
        cr, ci = _cmul(pw_ref[h, 0, :, re], pw_ref[h, 0, :, im], xr, xi)
        carry_ref[h, :, re] = toward(cr + er, 1)
        carry_ref[h, :, im] = toward(ci + ei, 1)

        def add_carry(j, g, h=h, re=re, im=im, lr=lr, li=li):
            rows = step_rows(j)
            bu_ref[h, rows, re] = bu_ref[h, rows, re] + g[0]
            bu_ref[h, rows, im] = bu_ref[h, rows, im] + g[1]
            return _cmul(lr, li, g[0], g[1])

        lax.fori_loop(0, SSM_TC, add_carry, _cmul(lr, li, xr, xi), unroll=4)

    for h in range(2):
        y = jnp.dot(bu_ref[h].astype(BF16), cm_ref[h], preferred_element_type=F32)
        for cb in range(SSM_HALF // LANES):
            cols = slice(cb * LANES, (cb + 1) * LANES)
            out_cols = slice(h * SSM_HALF + cb * LANES, h * SSM_HALF + (cb + 1) * LANES)
            slab = h * (SSM_HALF // LANES) + cb
            yv_ref[slab] = y[:, cols]
            for s in range(SUBLANES):
                y_ref[0, s * SSM_TC:(s + 1) * SSM_TC, out_cols] = yv_ref[slab, pl.ds(s, SSM_TC, stride=SUBLANES), :]


def _ssm_params(a_re, a_im, log_dt, b_re, b_im, c_re, c_im):
    lam = lax.complex(a_re.astype(F32), a_im.astype(F32))
    dt = jnp.exp(log_dt.astype(F32))[..., None]
    lam_bar = jnp.exp(lam * dt)
    b_bar = ((lam_bar - 1.0) / lam)[..., None] * lax.complex(b_re.astype(F32), b_im.astype(F32))
    gh = SSM_GROUPS // 2
    eye = jnp.eye(gh, dtype=F32)

    def b_block(m):
        m = m.reshape(2, 2, gh, SSM_STATE, SSM_GROUP)
        return jnp.einsum('dhgpc,gk->dhgckp', m, eye).reshape(2, 2, gh * SSM_GROUP, gh * SSM_STATE)

    def c_block(m):
        m = m.reshape(2, 2, gh, SSM_GROUP, SSM_STATE)
        return jnp.einsum('dhgcp,gk->dhkpgc', m, eye).reshape(2, 2, gh * SSM_STATE, gh * SSM_GROUP)

    bm = jnp.concatenate([b_block(jnp.real(b_bar)), b_block(jnp.imag(b_bar))], axis=-1).astype(BF16)
    cm = jnp.concatenate([c_block(c_re.astype(F32)), -c_block(c_im.astype(F32))], axis=-2).astype(BF16)

    def table(z):
        z = z.reshape(2, 2, SSM_HSTATE)
        t = jnp.concatenate([jnp.real(z), jnp.imag(z)], axis=-1)
        return jnp.broadcast_to(t[:, :, None, :], (2, 2, SUBLANES, 2 * SSM_HSTATE))

    lam_t = table(lam_bar)
    pw_t = jnp.stack([table(jnp.exp(lam * dt * (SSM_TC * k))) for k in (1, 2, 4)], axis=2)
    return bm, cm, lam_t, pw_t


SSM_CHUNKS = S_ALL // SSM_T


def ssm_scan(u_chunks, bm, cm, lam_t, pw_t, reverse):
    hs2 = 2 * SSM_HSTATE

    def chunk(b, c):
        if reverse:
            c = jnp.where(c == 0, 0, SSM_CHUNKS - c)
        return b * SSM_CHUNKS + c

    kern = functools.partial(_ssm_kernel, reverse=reverse)
    return pl.pallas_call(
        kern,
        grid=(BATCH, SSM_CHUNKS),
        in_specs=[
            pl.BlockSpec((1, SSM_T, SSM_WIDTH), lambda b, c: (chunk(b, c), 0, 0)),
            pl.BlockSpec((2, SSM_HALF, hs2), lambda b, c: (0, 0, 0)),
            pl.BlockSpec((2, hs2, SSM_HALF), lambda b, c: (0, 0, 0)),
            pl.BlockSpec((2, SUBLANES, hs2), lambda b, c: (0, 0, 0)),
            pl.BlockSpec((2, 3, SUBLANES, hs2), lambda b, c: (0, 0, 0, 0)),
        ],
        out_specs=pl.BlockSpec((1, SSM_T, SSM_WIDTH), lambda b, c: (chunk(b, c), 0, 0)),
        out_shape=jax.ShapeDtypeStruct(u_chunks.shape, F32),
        scratch_shapes=[pltpu.VMEM((2, SSM_T, hs2), F32), pltpu.VMEM((SSM_WIDTH // LANES, SSM_T, LANES), F32),
                        pltpu.VMEM((2, SUBLANES, hs2), F32)],
        compiler_params=_cparams(("arbitrary",) * 2, 32),
        name="ssm_scan",
    )(u_chunks, bm, cm, lam_t, pw_t)


def _ssm_chunk_of_tile(i):
    lat_tiles = SEQ // SSM_T
    return jnp.where(i < BATCH * lat_tiles,
                     (i // lat_tiles) * SSM_CHUNKS + 1 + i % lat_tiles,
                     (i - BATCH * lat_tiles) * SSM_CHUNKS)


def _gelu_tanh(x):
    return 0.5 * x * (1.0 + jnp.tanh(math.sqrt(2.0 / math.pi) * (x + 0.044715 * (x * x * x))))


def _ssm_out_kernel(u_ref, yf_ref, yb_ref, d_ref, w_ref, b_ref, o_ref):
    y = d_ref[0] * u_ref[...].astype(F32) + yf_ref[0] + yb_ref[0]
    y = _gelu_tanh(y)
    z = jnp.dot(y.astype(BF16), w_ref[0], preferred_element_type=F32) + b_ref[0]
    o_ref[...] = (y * jax.nn.sigmoid(z)).astype(o_ref.dtype)


def ssm_output(p, yf, yb, ssm_d, glu_w, glu_b, layer):
    n = p.shape[0]
    yspec = pl.BlockSpec((1, SSM_T, SSM_WIDTH), lambda i: (_ssm_chunk_of_tile(i), 0, 0))
    return pl.pallas_call(
        _ssm_out_kernel,
        grid=(n // SSM_T,),
        in_specs=[
            pl.BlockSpec((SSM_T, SSM_WIDTH), lambda i: (i, COL_SSM // SSM_WIDTH)),
            yspec,
            yspec,
            pl.BlockSpec((1, 1, SSM_WIDTH), lambda i: (layer, 0, 0)),
            pl.BlockSpec((1, SSM_WIDTH, SSM_WIDTH), lambda i: (layer, 0, 0)),
            pl.BlockSpec((1, 1, SSM_WIDTH), lambda i: (layer, 0, 0)),
        ],
        out_specs=pl.BlockSpec((SSM_T, SSM_WIDTH), lambda i: (i, 0)),
        out_shape=jax.ShapeDtypeStruct((n, SSM_WIDTH), BF16),
        compiler_params=_cparams(("arbitrary",), 32),
        name="ssm_output",
    )(p, yf, yb, ssm_d.reshape(DEPTH, 1, SSM_WIDTH), glu_w, glu_b.reshape(DEPTH, 1, SSM_WIDTH))


def _seq_tile(tile):
    lat_tiles = N_LAT // SEQ_T
    per_seq = SEQ // SEQ_T
    is_lat = tile < lat_tiles
    t = jnp.where(is_lat, tile % per_seq, (tile - lat_tiles) % (CTX_LEN // SEQ_T))
    last_t = jnp.where(is_lat, per_seq - 1, CTX_LEN // SEQ_T - 1)
    return t * SEQ_T, t == 0, t == last_t, jnp.where(is_lat, SEQ, CTX_LEN)


def _with_halo(prev_ref, cur_ref, next_ref, first, last):
    top = jnp.where(first, jnp.zeros((SEQ_PAD, cur_ref.shape[1]), cur_ref.dtype), prev_ref[SEQ_T - SEQ_PAD:SEQ_T, :])
    bot = jnp.where(last, jnp.zeros((SEQ_PAD, cur_ref.shape[1]), cur_ref.dtype), next_ref[0:SEQ_PAD, :])
    return jnp.concatenate([top, cur_ref[...], bot], axis=0)


def _halo_specs(width, col_block, n_tiles):
    return [pl.BlockSpec((SEQ_T, width), lambda g: (jnp.maximum(g - 1, 0), col_block)),
            pl.BlockSpec((SEQ_T, width), lambda g: (g, col_block)),
            pl.BlockSpec((SEQ_T, width), lambda g: (jnp.minimum(g + 1, n_tiles - 1), col_block))]


def _pool_kernel(prev_ref, cur_ref, next_ref, w_ref, s_ref, o_ref):
    t0, first, last, seq_len = _seq_tile(pl.program_id(0))
    halo = _with_halo(prev_ref, cur_ref, next_ref, first, last)
    centre = cur_ref[...].astype(F32)
    tt = lax.broadcasted_iota(jnp.int32, (SEQ_T, SEQ_T + 2 * SEQ_PAD), 0)
    rr = lax.broadcasted_iota(jnp.int32, (SEQ_T, SEQ_T + 2 * SEQ_PAD), 1) - SEQ_PAD
    tg = t0 + lax.broadcasted_iota(jnp.int32, (SEQ_T, 1), 0)
    for gi, w in enumerate(POOL_WINDOWS):
        cols = slice(gi * POOL_GROUP, (gi + 1) * POOL_GROUP)
        band = ((rr >= tt - w // 2) & (rr <= tt + w // 2 - 1)).astype(F32).astype(BF16)
        wsum = jnp.dot(band, halo[:, cols], preferred_element_type=F32)
        cnt = jnp.minimum(tg - w // 2 + w, seq_len) - jnp.maximum(tg - w // 2, 0)
        pooled = wsum / cnt.astype(F32) - centre[:, cols]
        mixed = jnp.dot(pooled.astype(BF16), w_ref[0, gi], preferred_element_type=F32)
        o_ref[:, cols] = (mixed * s_ref[0, :, cols]).astype(o_ref.dtype)


def pool_mixer(p, pool_w, pool_scale, layer):
    n_tiles = p.shape[0] // SEQ_T
    ng = len(POOL_WINDOWS)
    return pl.pallas_call(
        _pool_kernel,
        grid=(n_tiles,),
        in_specs=_halo_specs(POOL_WIDTH, COL_POOL // POOL_WIDTH, n_tiles) + [
            pl.BlockSpec((1, ng, POOL_GROUP, POOL_GROUP), lambda g: (layer, 0, 0, 0)),
            pl.BlockSpec((1, 1, POOL_WIDTH), lambda g: (layer, 0, 0)),
        ],
        out_specs=pl.BlockSpec((SEQ_T, POOL_WIDTH), lambda g: (g, 0)),
        out_shape=jax.ShapeDtypeStruct((p.shape[0], POOL_WIDTH), BF16),
        compiler_params=_cparams(("arbitrary",), 32),
        name="pool_mixer",
    )(p, p, p, pool_w, pool_scale.reshape(DEPTH, 1, POOL_WIDTH))


CONV_RB = 64


def _conv_kernel(ap_ref, ac_ref, an_ref, gp_ref, gc_ref, gn_ref, w_ref, b_ref, g_ref, beta_ref, o_ref, glu_ref):
    _, first, last, _ = _seq_tile(pl.program_id(0))
    a = _with_halo(ap_ref, ac_ref, an_ref, first, last).astype(F32)
    g = _with_halo(gp_ref, gc_ref, gn_ref, first, last).astype(F32)
    glu_ref[...] = a * jax.nn.sigmoid(g)
    off = SEQ_PAD - CONV_TAPS // 2
    for rb in range(SEQ_T // CONV_RB):
        parts = []
        for cb in range(CONV_WIDTH // LANES):
            cols = slice(cb * LANES, (cb + 1) * LANES)
            acc = jnp.zeros((CONV_RB, LANES), F32)
            for k in range(CONV_TAPS):
                acc = acc + glu_ref[rb * CONV_RB + k + off:rb * CONV_RB + k + off + CONV_RB, cols] * w_ref[0, k:k + 1, cols]
            parts.append(acc)
        y = jnp.concatenate(parts, axis=-1) + b_ref[0]
        yc = y - jnp.mean(y, axis=-1, keepdims=True)
        yn = yc * lax.rsqrt(jnp.mean(yc * yc, axis=-1, keepdims=True) + EPS)
        yn = yn * g_ref[0] + beta_ref[0]
        o_ref[rb * CONV_RB:(rb + 1) * CONV_RB, :] = _silu(yn).astype(o_ref.dtype)


def conv_mixer(p, dw_w, dw_b, ln_g, ln_b, layer):
    n_tiles = p.shape[0] // SEQ_T
    vec = lambda a: a.reshape(DEPTH, 1, CONV_WIDTH)
    vspec = pl.BlockSpec((1, 1, CONV_WIDTH), lambda g: (layer, 0, 0))
    a_block = COL_CONV // CONV_WIDTH
    return pl.pallas_call(
        _conv_kernel,
        grid=(n_tiles,),
        in_specs=_halo_specs(CONV_WIDTH, a_block, n_tiles) + _halo_specs(CONV_WIDTH, a_block + 1, n_tiles) + [
            pl.BlockSpec((1, CONV_TAPS, CONV_WIDTH), lambda g: (layer, 0, 0)),
            vspec, vspec, vspec,
        ],
        out_specs=pl.BlockSpec((SEQ_T, CONV_WIDTH), lambda g: (g, 0)),
        out_shape=jax.ShapeDtypeStruct((p.shape[0], CONV_WIDTH), BF16),
        scratch_shapes=[pltpu.VMEM((SEQ_T + 2 * SEQ_PAD, CONV_WIDTH), F32)],
        compiler_params=_cparams(("arbitrary",), 40),
        name="conv_mixer",
    )(p, p, p, p, p, p, dw_w, vec(dw_b), vec(ln_g), vec(ln_b))


def _merge_kernel(oa_ref, ys_ref, yp_ref, yc_ref, g0_ref, g1_ref, g2_ref, g3_ref,
                  wa_ref, ws_ref, wp_ref, wc_ref, o_ref):
    def branch(x_ref, w_ref, g_ref):
        up = jnp.dot(x_ref[...], w_ref[0, 0], preferred_element_type=F32)
        return jax.nn.sigmoid(g_ref[...].astype(F32)) * up

    m = (branch(oa_ref, wa_ref, g0_ref) + branch(ys_ref, ws_ref, g1_ref)
         + branch(yp_ref, wp_ref, g2_ref) + branch(yc_ref, wc_ref, g3_ref))
    o_ref[...] = m.astype(o_ref.dtype)


def merge_branches(p, o_attn, y_ssm, y_pool, y_conv, w_attn, w_ssm, w_pool, w_conv, layer):
    n = o_attn.shape[0]
    nct = D_MODEL // TN
    gate_tile0 = COL_GATE // TN

    def gate_spec(br):
        return pl.BlockSpec((TM, TN), lambda i, j: (i, gate_tile0 + br * nct + j))

    def x_spec(width):
        return pl.BlockSpec((TM, width), lambda i, j: (i, 0))

    def w_spec(width):
        return pl.BlockSpec((1, 1, width, TN), lambda i, j: (layer, j, 0, 0))

    return pl.pallas_call(
        _merge_kernel,
        grid=(n // TM, nct),
        in_specs=[x_spec(Q_W), x_spec(SSM_WIDTH), x_spec(POOL_WIDTH), x_spec(CONV_WIDTH),
                  gate_spec(0), gate_spec(1), gate_spec(2), gate_spec(3),
                  w_spec(Q_W), w_spec(SSM_WIDTH), w_spec(POOL_WIDTH), w_spec(CONV_WIDTH)],
        out_specs=pl.BlockSpec((TM, TN), lambda i, j: (i, j)),
        out_shape=jax.ShapeDtypeStruct((n, D_MODEL), BF16),
        compiler_params=_cparams(("arbitrary", "arbitrary"), 40),
        name="merge_branches",
    )(o_attn, y_ssm, y_pool, y_conv, p, p, p, p, w_attn, w_ssm, w_pool, w_conv)


def _split_bf16(x):
    hi = x.astype(BF16)
    lo = (x - hi.astype(F32)).astype(BF16)
    return hi, lo


SLAB = D_MODEL // LANES


def _store_row_slabs(ref, row0, x):
    n = x.shape[0]
    for j in range(SLAB):
        ref[pl.ds(row0 * SLAB + j, n, stride=SLAB), :] = x[:, j * LANES:(j + 1) * LANES]


def _load_row_slabs(ref, row0, n):
    return jnp.concatenate([ref[pl.ds(row0 * SLAB + j, n, stride=SLAB), :] for j in range(SLAB)], axis=-1)


def _outproj_kernel(m_ref, x_ref, mod_ref, g_ref, w_ref, rw_ref, xo_ref, h_ref, lg_ref):
    r_hi, r_lo = _split_bf16(rw_ref[...])
    nt = (((1,), (1,)), ((), ()))
    half = m_ref.shape[0] // 2
    for c in range(2):
        rows = slice(c * half, (c + 1) * half)
        mix = jnp.dot(m_ref[rows, :], w_ref[0], preferred_element_type=F32)
        x = x_ref[rows, :] + mod_ref[0, 2:3, :] * mix
        xo_ref[rows, :] = x
        ms = jnp.mean(x * x, axis=-1, keepdims=True)
        h = x * lax.rsqrt(ms + EPS) * g_ref[0]
        h = h * (1.0 + mod_ref[0, 4:5, :]) + mod_ref[0, 3:4, :]
        _store_row_slabs(h_ref, c * half, h)
        h_hi, h_lo = _split_bf16(h)
        lg_ref[:, rows] = (lax.dot_general(r_hi, h_hi, nt, preferred_element_type=F32)
                           + lax.dot_general(r_hi, h_lo, nt, preferred_element_type=F32)
                           + lax.dot_general(r_lo, h_hi, nt, preferred_element_type=F32))


def out_projection(m, xa, mod, norm2_g, w_out, router_wt, layer):
    n = m.shape[0]
    tmo = TM
    return pl.pallas_call(
        _outproj_kernel,
        grid=(n // tmo,),
        in_specs=[
            pl.BlockSpec((tmo, D_MODEL), lambda i: (i, 0)),
            pl.BlockSpec((tmo, D_MODEL), lambda i: (i, 0)),
            pl.BlockSpec((1, 6, D_MODEL), lambda i: (_mod_id(i), 0, 0)),
            pl.BlockSpec((1, 1, D_MODEL), lambda i: (layer, 0, 0)),
            pl.BlockSpec((1, D_MODEL, D_MODEL), lambda i: (layer, 0, 0)),
            pl.BlockSpec((N_EXPERTS, D_MODEL), lambda i: (0, 0)),
        ],
        out_specs=[
            pl.BlockSpec((tmo, D_MODEL), lambda i: (i, 0)),
            pl.BlockSpec((tmo * SLAB, LANES), lambda i: (i, 0)),
            pl.BlockSpec((N_EXPERTS, tmo), lambda i: (0, i)),
        ],
        out_shape=[
            jax.ShapeDtypeStruct((n, D_MODEL), F32),
            jax.ShapeDtypeStruct((n * SLAB, LANES), F32),
            jax.ShapeDtypeStruct((N_EXPERTS, n), F32),
        ],
        compiler_params=_cparams(("arbitrary",), 60),
        name="out_projection",
    )(m, xa, mod, norm2_g.reshape(DEPTH, 1, D_MODEL), w_out, router_wt)


def _first_argmax(blk, row):
    m = jnp.max(blk, axis=0, keepdims=True)
    idx = jnp.min(jnp.where(blk == m, row, EXPERTS_PER_GROUP), axis=0, keepdims=True)
    return m, idx


def _route_kernel(lg_ref, rb_ref, e_ref, w_ref, rank_ref, cnt_ref, run_ref):
    @pl.when(pl.program_id(0) == 0)
    def _():
        run_ref[...] = jnp.zeros_like(run_ref)

    t = lg_ref.shape[1]
    scores = jax.nn.sigmoid(lg_ref[...])
    sel = scores + rb_ref[...]
    row = lax.broadcasted_iota(jnp.int32, (EXPERTS_PER_GROUP, t), 0)
    neg = jnp.float32(-jnp.inf)

    best = None
    for g in range(N_EXPERT_GROUPS):
        blk = sel[g * EXPERTS_PER_GROUP:(g + 1) * EXPERTS_PER_GROUP, :]
        m1, i1 = _first_argmax(blk, row)
        m2 = jnp.max(jnp.where(row == i1, neg, blk), axis=0, keepdims=True)
        gs = m1 + m2
        if best is None:
            best, grp = gs, jnp.zeros((1, t), jnp.int32)
        else:
            better = gs > best
            best = jnp.where(better, gs, best)
            grp = jnp.where(better, g, grp)

    in_sel = jnp.zeros((EXPERTS_PER_GROUP, t), F32)
    in_sc = jnp.zeros((EXPERTS_PER_GROUP, t), F32)
    for g in range(N_EXPERT_GROUPS):
        rows = slice(g * EXPERTS_PER_GROUP, (g + 1) * EXPERTS_PER_GROUP)
        in_sel = jnp.where(grp == g, sel[rows, :], in_sel)
        in_sc = jnp.where(grp == g, scores[rows, :], in_sc)
    _, i1 = _first_argmax(in_sel, row)
    _, i2 = _first_argmax(jnp.where(row == i1, neg, in_sel), row)
    s1 = jnp.sum(jnp.where(row == i1, in_sc, 0.0), axis=0, keepdims=True)
    s2 = jnp.sum(jnp.where(row == i2, in_sc, 0.0), axis=0, keepdims=True)
    e1 = grp * EXPERTS_PER_GROUP + i1
    e2 = grp * EXPERTS_PER_GROUP + i2
    e_ref[0:1, :] = e1
    e_ref[1:2, :] = e2
    w_ref[0:1, :] = s1 / (s1 + s2)
    w_ref[1:2, :] = s2 / (s1 + s2)

    erow = lax.broadcasted_iota(jnp.int32, (N_EXPERTS, t), 0)
    oh1 = erow == e1
    oh2 = erow == e2
    cnt = jnp.where(oh1 | oh2, 1.0, 0.0)
    a = lax.broadcasted_iota(jnp.int32, (t, t), 0)
    b = lax.broadcasted_iota(jnp.int32, (t, t), 1)
    before = jnp.where(a < b, 1.0, 0.0).astype(BF16)
    excl = jnp.dot(cnt.astype(BF16), before, preferred_element_type=F32) + run_ref[...]
    rank_ref[0:1, :] = jnp.sum(jnp.where(oh1, excl, 0.0), axis=0, keepdims=True).astype(jnp.int32)
    rank_ref[1:2, :] = jnp.sum(jnp.where(oh2, excl, 0.0), axis=0, keepdims=True).astype(jnp.int32)
    run_ref[...] = run_ref[...] + jnp.sum(cnt, axis=1, keepdims=True)
    cnt_ref[...] = run_ref[...]


def route(logits_t, router_b):
    n = logits_t.shape[1]
    return pl.pallas_call(
        _route_kernel,
        grid=(n // ROUTE_T,),
        in_specs=[
            pl.BlockSpec((N_EXPERTS, ROUTE_T), lambda i: (0, i)),
            pl.BlockSpec((N_EXPERTS, 1), lambda i: (0, 0)),
        ],
        out_specs=[
            pl.BlockSpec((TOP_K, ROUTE_T), lambda i: (0, i)),
            pl.BlockSpec((TOP_K, ROUTE_T), lambda i: (0, i)),
            pl.BlockSpec((TOP_K, ROUTE_T), lambda i: (0, i)),
            pl.BlockSpec((N_EXPERTS, 1), lambda i: (0, 0)),
        ],
        out_shape=[
            jax.ShapeDtypeStruct((TOP_K, n), jnp.int32),
            jax.ShapeDtypeStruct((TOP_K, n), F32),
            jax.ShapeDtypeStruct((TOP_K, n), jnp.int32),
            jax.ShapeDtypeStruct((N_EXPERTS, 1), F32),
        ],
        scratch_shapes=[pltpu.VMEM((N_EXPERTS, 1), F32)],
        compiler_params=_cparams(("arbitrary",), 32),
        name="route",
    )(logits_t, router_b.reshape(N_EXPERTS, 1).astype(F32))


def _expert_kernel(be_ref, slot_ref, nxt_ref, nu_ref, tok_ref, h_hbm, wg_hbm, wu_hbm, wd_hbm, o_ref,
                   xbuf_ref, wgf_ref, wuf_ref, wdf_ref, wgb_ref, wub_ref, wdb_ref, sem, xsem, *, layer):
    b = pl.program_id(0)
    n_used = nu_ref[0]
    e = be_ref[b]
    fresh = (b == 0) | (e != be_ref[jnp.maximum(b - 1, 0)])
    live = b < n_used
    has_next = b + 1 < n_used
    cur = b % 2

    def slab(row):
        return pl.ds(pl.multiple_of(row * SLAB, SLAB), SLAB)

    def row_gathers(block, buf):
        return [pltpu.make_async_copy(h_hbm.at[slab(tok_ref[block * MOE_BLOCK + r])],
                                      xbuf_ref.at[buf, slab(r)], xsem.at[buf]) for r in range(MOE_BLOCK)]

    @pl.when(b == 0)
    def _():
        for cp in row_gathers(0, 0):
            cp.start()

    def weight_copies(expert, slot):
        return (pltpu.make_async_copy(wg_hbm.at[layer, expert], wgf_ref.at[slot], sem.at[slot]),
                pltpu.make_async_copy(wu_hbm.at[layer, expert], wuf_ref.at[slot], sem.at[slot]),
                pltpu.make_async_copy(wd_hbm.at[layer, expert], wdf_ref.at[slot], sem.at[slot]))

    @pl.when(b == 0)
    def _():
        for cp in weight_copies(e, 0):
            cp.start(priority=WEIGHT_DMA_PRIORITY)

        @pl.when(nxt_ref[0] >= 0)
        def _():
            for cp in weight_copies(nxt_ref[0], 1):
                cp.start(priority=WEIGHT_DMA_PRIORITY)

    @pl.when(fresh & live)
    def _():
        slot = slot_ref[b]
        ahead = nxt_ref[pl.num_programs(0) + b]

        @pl.when(ahead >= 0)
        def _():
            ahead_slot = jnp.where(slot + 2 >= WEIGHT_SLOTS, slot + 2 - WEIGHT_SLOTS, slot + 2)
            for cp in weight_copies(ahead, ahead_slot):
                cp.start(priority=WEIGHT_DMA_PRIORITY)

        for cp in weight_copies(e, slot):
            cp.wait()
        wgb_ref[...] = wgf_ref[slot].astype(BF16)
        wub_ref[...] = wuf_ref[slot].astype(BF16)
        wdb_ref[...] = wdf_ref[slot].astype(BF16)

    @pl.when(live & has_next)
    def _():
        for cp in row_gathers(b + 1, 1 - cur):
            cp.start()

    @pl.when(live)
    def _():
        for cp in row_gathers(b, cur):
            cp.wait()
        x = _load_row_slabs(xbuf_ref.at[cur], 0, MOE_BLOCK).astype(BF16)
        gate = jnp.dot(x, wgb_ref[...], preferred_element_type=F32)
        up = jnp.dot(x, wub_ref[...], preferred_element_type=F32)
        hid = (_silu(gate) * up).astype(BF16)
        _store_row_slabs(o_ref, 0, jnp.dot(hid, wdb_ref[...], preferred_element_type=F32))

    @pl.when(jnp.logical_not(live))
    def _():
        o_ref[...] = jnp.zeros_like(o_ref)


def _slot_tokens(dest_flat, n, n_slots):
    i32 = jnp.int32
    owner = jnp.full((n_slots,), -1, i32).at[dest_flat].set(jnp.arange(TOP_K * n, dtype=i32), unique_indices=True)
    return jnp.where(owner < 0, 0, owner % n).astype(i32)


def experts(blk_e, blk_slot, blk_next, n_used, slot_tok, h2, w_gate, w_up, w_down, layer):
    n_slots = slot_tok.shape[0]
    n_blocks = n_slots // MOE_BLOCK
    grid_spec = pltpu.PrefetchScalarGridSpec(
        num_scalar_prefetch=5,
        grid=(n_blocks,),
        in_specs=[
            pl.BlockSpec(memory_space=pl.ANY),
            pl.BlockSpec(memory_space=pl.ANY),
            pl.BlockSpec(memory_space=pl.ANY),
            pl.BlockSpec(memory_space=pl.ANY),
        ],
        out_specs=pl.BlockSpec((MOE_BLOCK * SLAB, LANES), lambda b, be, sl, nx, nu, tk: (b, 0)),
        scratch_shapes=[
            pltpu.VMEM((2, MOE_BLOCK * SLAB, LANES), F32),
            pltpu.VMEM((WEIGHT_SLOTS, D_MODEL, EXPERT_FF), F32),
            pltpu.VMEM((WEIGHT_SLOTS, D_MODEL, EXPERT_FF), F32),
            pltpu.VMEM((WEIGHT_SLOTS, EXPERT_FF, D_MODEL), F32),
            pltpu.VMEM((D_MODEL, EXPERT_FF), BF16),
            pltpu.VMEM((D_MODEL, EXPERT_FF), BF16),
            pltpu.VMEM((EXPERT_FF, D_MODEL), BF16),
            pltpu.SemaphoreType.DMA((WEIGHT_SLOTS,)),
            pltpu.SemaphoreType.DMA((2,)),
        ],
    )
    return pl.pallas_call(
        functools.partial(_expert_kernel, layer=layer),
        grid_spec=grid_spec,
        out_shape=jax.ShapeDtypeStruct((n_slots * SLAB, LANES), F32),
        compiler_params=_cparams(("arbitrary",), 60),
        name="experts",
    )(blk_e, blk_slot, blk_next, n_used, slot_tok, h2, w_gate, w_up, w_down)


COMBINE_T = 256


def _combine_kernel(dest_ref, x_ref, mod_ref, w_ref, fg_ref, ys_hbm, o_ref, buf_ref, sem, *, final_norm):
    n = dest_ref.shape[0] // TOP_K
    base = pl.program_id(0) * COMBINE_T

    def row_copy(t, k):
        src = pl.ds(pl.multiple_of(dest_ref[k * n + base + t] * SLAB, SLAB), SLAB)
        dst = pl.ds(pl.multiple_of(t * SLAB, SLAB), SLAB)
        return pltpu.make_async_copy(ys_hbm.at[src], buf_ref.at[k, dst], sem)

    def start(t, carry):
        for k in range(TOP_K):
            row_copy(t, k).start()
        return carry

    def wait(t, carry):
        for k in range(TOP_K):
            row_copy(t, k).wait()
        return carry

    lax.fori_loop(0, COMBINE_T, start, 0)
    lax.fori_loop(0, COMBINE_T, wait, 0)
    f = (w_ref[:, 0:1] * _load_row_slabs(buf_ref.at[0], 0, COMBINE_T)
         + w_ref[:, 1:2] * _load_row_slabs(buf_ref.at[1], 0, COMBINE_T))
    x = x_ref[...] + mod_ref[0, 5:6, :] * f
    if final_norm:
        ms = jnp.mean(x * x, axis=-1, keepdims=True)
        x = x * lax.rsqrt(ms + EPS) * fg_ref[...]
    o_ref[...] = x


def combine(dest_flat, xa, mod, wts_rows, ys, final_g, final_norm):
    n = xa.shape[0]
    tiles_per_tm = TM // COMBINE_T
    grid_spec = pltpu.PrefetchScalarGridSpec(
        num_scalar_prefetch=1,
        grid=(n // COMBINE_T,),
        in_specs=[
            pl.BlockSpec((COMBINE_T, D_MODEL), lambda i, d: (i, 0)),
            pl.BlockSpec((1, 6, D_MODEL), lambda i, d: (_mod_id(i // tiles_per_tm), 0, 0)),
            pl.BlockSpec((COMBINE_T, TOP_K), lambda i, d: (i, 0)),
            pl.BlockSpec((1, D_MODEL), lambda i, d: (0, 0)),
            pl.BlockSpec(memory_space=pl.ANY),
        ],
        out_specs=pl.BlockSpec((COMBINE_T, D_MODEL), lambda i, d: (i, 0)),
        scratch_shapes=[pltpu.VMEM((TOP_K, COMBINE_T * SLAB, LANES), F32), pltpu.SemaphoreType.DMA(())],
    )
    kern = functools.partial(_combine_kernel, final_norm=final_norm)
    return pl.pallas_call(
        kern,
        grid_spec=grid_spec,
        out_shape=jax.ShapeDtypeStruct((n, D_MODEL), F32),
        compiler_params=_cparams(("arbitrary",), 32),
        name="combine",
    )(dest_flat, xa, mod, wts_rows, final_g.reshape(1, D_MODEL), ys)


def _slot_plan(e_idx, rank, counts, n_blocks):
    i32 = jnp.int32
    cnt = counts.reshape(N_EXPERTS).astype(i32)
    padded = (cnt + MOE_BLOCK - 1) // MOE_BLOCK * MOE_BLOCK
    pad_end = jnp.cumsum(padded)
    pad_start = pad_end - padded
    eids = jnp.arange(N_EXPERTS, dtype=i32)

    def lookup(table, idx):
        return jnp.sum(jnp.where(idx[..., None] == eids, table, 0), axis=-1).astype(i32)

    dest = lookup(pad_start, e_idx) + rank
    blk_start = jnp.arange(n_blocks, dtype=i32) * MOE_BLOCK
    blk_e = jnp.minimum(jnp.sum(pad_end[None, :] <= blk_start[:, None], axis=1), N_EXPERTS - 1).astype(i32)
    n_used = (pad_end[-1] // MOE_BLOCK).astype(i32).reshape(1)
    has = cnt > 0
    run = jnp.cumsum(has.astype(i32)) - 1
    first_from = lax.cummin(jnp.where(has, eids, N_EXPERTS)[::-1])[::-1]
    nxt = jnp.concatenate([first_from[1:], jnp.full((1,), N_EXPERTS, i32)])
    nxt = jnp.where(nxt >= N_EXPERTS, -1, nxt)
    nxt2 = jnp.where(nxt >= 0, lookup(nxt, jnp.maximum(nxt, 0)), -1)
    blk_ahead = jnp.stack([lookup(nxt, blk_e), lookup(nxt2, blk_e)], axis=0).reshape(-1)
    return dest.reshape(-1).astype(i32), blk_e, lookup(run % WEIGHT_SLOTS, blk_e), blk_ahead, n_used


def _col_tiles(w):
    d, k, n = w.shape
    return w.astype(BF16).reshape(d, k, n // TN, TN).transpose(0, 2, 1, 3)


def _layer(layer, xa, mod, W, with_ctx, final_norm):
    n_lat_tiles = N_LAT // TM
    n_ctx_tiles = N_CTX // TM
    if with_ctx:
        p = in_projection(xa, mod, W['norm1_g'], W['w_in'], layer, 0, n_lat_tiles + n_ctx_tiles, IN_W, TN_IN)
        p_lat, p_ctx = p[:N_LAT], p[N_LAT:]
    else:
        p = in_projection(xa, mod, W['norm1_g'], W['w_in'], layer, 0, n_lat_tiles, IN_W, TN_IN)
        p_lat = p
        p_ctx = in_projection(xa, mod, W['norm1_g'], W['w_in'], layer, n_lat_tiles, n_ctx_tiles, CTX_STATE_W, TN)
    n_rows = p.shape[0]

    def seq(a, lo, hi, n_seq_rows):
        return a[:, lo:hi].reshape(BATCH, n_seq_rows, hi - lo)

    kv = jnp.concatenate([seq(p_ctx, 0, 2 * KV_W, CTX_LEN), seq(p_lat, 0, 2 * KV_W, SEQ)], axis=1)
    cos, sin = _rope_tables(CTX_LEN)
    o_attn = attention(p, COL_Q, 0, SEQ, kv, cos[CTX_LEN:], sin[CTX_LEN:], cos, sin,
                       W['q_norm_g'], W['k_norm_g'], layer)
    if with_ctx:
        kv_c = seq(p_ctx, 0, 2 * KV_W, CTX_LEN)
        ones, zeros = jnp.ones((CTX_LEN, HEAD_DIM), F32), jnp.zeros((CTX_LEN, HEAD_DIM), F32)
        o_attn_c = attention(p, COL_Q, N_LAT, CTX_LEN, kv_c, ones, zeros, ones, zeros,
                             W['q_norm_g'], W['k_norm_g'], layer)
        o_attn = jnp.concatenate([o_attn, o_attn_c], axis=0)

    u_seq = jnp.concatenate([seq(p_ctx, COL_SSM, COL_SSM + SSM_WIDTH, CTX_LEN),
                             seq(p_lat, COL_SSM, COL_SSM + SSM_WIDTH, SEQ)], axis=1)
    u_chunks = u_seq.reshape(BATCH * SSM_CHUNKS, SSM_T, SSM_WIDTH)
    bm, cm, lam_t, pw_t = _ssm_params(W['ssm_a_re'][layer], W['ssm_a_im'][layer], W['ssm_log_dt'][layer],
                                      W['ssm_b_re'][layer], W['ssm_b_im'][layer],
                                      W['ssm_c_re'][layer], W['ssm_c_im'][layer])
    yf = ssm_scan(u_chunks, bm[0], cm[0], lam_t[0], pw_t[0], reverse=False)
    yb = ssm_scan(u_chunks, bm[1], cm[1], lam_t[1], pw_t[1], reverse=True)
    y_ssm = ssm_output(p, yf, yb, W['ssm_d'], W['ssm_glu_w'], W['ssm_glu_b'], layer)

    y_pool = pool_mixer(p, W['pool_w'], W['pool_scale'], layer)
    y_conv = conv_mixer(p, W['conv_dw_w'], W['conv_dw_b'], W['conv_ln_g'], W['conv_ln_b'], layer)

    m = merge_branches(p, o_attn, y_ssm, y_pool, y_conv,
                       W['w_up_attn'], W['w_up_ssm'], W['w_up_pool'], W['w_up_conv'], layer)
    x1, h2, logits_t = out_projection(m, xa, mod, W['norm2_g'], W['w_out'], W['router_wt'], layer)

    e_idx, wts, rank, counts = route(logits_t, W['router_b'])
    n_blocks = -(-n_rows * TOP_K // MOE_BLOCK) + N_EXPERTS
    dest_flat, blk_e, blk_slot, blk_next, n_used = _slot_plan(e_idx, rank, counts, n_blocks)
    slot_tok = _slot_tokens(dest_flat, n_rows, n_blocks * MOE_BLOCK)
    ys = experts(blk_e, blk_slot, blk_next, n_used, slot_tok, h2,
                 W['moe_w_gate'], W['moe_w_up'], W['moe_w_down'], layer)
    x2 = combine(dest_flat, x1, mod, wts.T, ys, W['final_g'], final_norm)
    if with_ctx:
        return x2
    return jnp.concatenate([x2, xa[N_LAT:]], axis=0)


def kernel(x, c, ctx, c_ctx, ada_w, ada_b, norm1_g, norm2_g, w_in, q_norm_g, k_norm_g, ssm_a_re, ssm_a_im, ssm_log_dt, ssm_b_re, ssm_b_im, ssm_c_re, ssm_c_im, ssm_d, ssm_glu_w, ssm_glu_b, pool_w, pool_scale, conv_dw_w, conv_dw_b, conv_ln_g, conv_ln_b, w_up_attn, w_up_ssm, w_up_pool, w_up_conv, w_out, router_w, router_b, moe_w_gate, moe_w_up, moe_w_down, final_g):
    W = dict(
        norm1_g=norm1_g, norm2_g=norm2_g, q_norm_g=q_norm_g, k_norm_g=k_norm_g,
        w_in=w_in.astype(BF16),
        ssm_a_re=ssm_a_re, ssm_a_im=ssm_a_im, ssm_log_dt=ssm_log_dt, ssm_b_re=ssm_b_re, ssm_b_im=ssm_b_im,
        ssm_c_re=ssm_c_re, ssm_c_im=ssm_c_im, ssm_d=ssm_d, ssm_glu_w=ssm_glu_w.astype(BF16), ssm_glu_b=ssm_glu_b,
        pool_w=pool_w.astype(BF16), pool_scale=pool_scale,
        conv_dw_w=conv_dw_w, conv_dw_b=conv_dw_b, conv_ln_g=conv_ln_g, conv_ln_b=conv_ln_b,
        w_up_attn=_col_tiles(w_up_attn), w_up_ssm=_col_tiles(w_up_ssm), w_up_pool=_col_tiles(w_up_pool),
        w_up_conv=_col_tiles(w_up_conv), w_out=w_out.astype(BF16),
        router_wt=router_w.T, router_b=router_b,
        moe_w_gate=moe_w_gate, moe_w_up=moe_w_up, moe_w_down=moe_w_down, final_g=final_g,
    )
    cc = jnp.concatenate([c, c_ctx[None, :], jnp.zeros((SUBLANES - BATCH - 1, D_MODEL), F32)], axis=0)
    mod_all = ada_modulation(cc, ada_w, ada_b)
    xa = jnp.concatenate([x.reshape(N_LAT, D_MODEL), ctx.reshape(N_CTX, D_MODEL)], axis=0)
    for layer in range(DEPTH):
        mod = mod_all[layer, :BATCH + 1].reshape(BATCH + 1, 6, D_MODEL)
        xa = _layer(layer, xa, mod, W, with_ctx=(layer < DEPTH - 1), final_norm=(layer == DEPTH - 1))
    return xa[:N_LAT].reshape(BATCH, SEQ, D_MODEL)
```

```python
import functools
import math

import jax
import jax.numpy as jnp
from jax import lax
from jax.experimental import pallas as pl
from jax.experimental.pallas import tpu as pltpu

F32 = jnp.float32
BF16 = jnp.bfloat16

D_MODEL = 2048
BATCH = 2
SEQ = 4096
DEPTH = 2
GRID_W = 64
CTX_LEN = 256
N_HEADS = 8
N_KV_HEADS = 2
HEAD_DIM = 128
ROPE_THETA = 10000.0
SSM_WIDTH = 512
SSM_GROUP = 16
SSM_GROUPS = SSM_WIDTH // SSM_GROUP
SSM_STATE = 64
POOL_WIDTH = 512
POOL_WINDOWS = (2, 4, 8, 16)
POOL_GROUP = POOL_WIDTH // len(POOL_WINDOWS)
CONV_WIDTH = 512
CONV_TAPS = 31
N_BRANCHES = 4
N_EXPERTS = 64
N_EXPERT_GROUPS = 8
EXPERTS_PER_GROUP = N_EXPERTS // N_EXPERT_GROUPS
TOP_K = 2
EXPERT_FF = 512
EPS = 1e-6

Q_W = N_HEADS * HEAD_DIM
KV_W = N_KV_HEADS * HEAD_DIM
IN_W = 2 * KV_W + SSM_WIDTH + Q_W + POOL_WIDTH + 2 * CONV_WIDTH + N_BRANCHES * D_MODEL
CTX_STATE_W = 2 * KV_W + SSM_WIDTH
COL_K, COL_V, COL_SSM, COL_Q = 0, KV_W, 2 * KV_W, 2 * KV_W + SSM_WIDTH
COL_POOL = COL_Q + Q_W
COL_CONV = COL_POOL + POOL_WIDTH
COL_GATE = COL_CONV + 2 * CONV_WIDTH

N_LAT = BATCH * SEQ
N_CTX = BATCH * CTX_LEN
S_ALL = CTX_LEN + SEQ

V7X_VMEM_BYTES = 64 * 1024 * 1024
SUBLANES = 8
LANES = 128
BF16_ROWS = 16

TM = 512
TN = 512
TN_IN = IN_W // 4
TILES_PER_BATCH = SEQ // TM
TQ = 512
KEY_PREP_T = 256
SSM_T = 256
SSM_TC = SSM_T // SUBLANES
SSM_HALF = SSM_WIDTH // 2
SSM_HSTATE = SSM_GROUPS // 2 * SSM_STATE
SEQ_T = 256
SEQ_PAD = 16
MOE_BLOCK = 128
WEIGHT_SLOTS = 3
WEIGHT_DMA_PRIORITY = 1
ROUTE_T = 512


def _cparams(sem, vmem_mb):
    return pltpu.CompilerParams(dimension_semantics=sem, vmem_limit_bytes=vmem_mb * 1024 * 1024)


def _mod_id(tile):
    return jnp.minimum(tile // TILES_PER_BATCH, BATCH)


def _silu(x):
    return x * jax.nn.sigmoid(x)


def _ada_kernel(c_ref, w_ref, b_ref, o_ref):
    c = c_ref[...]
    a = _silu(c).astype(BF16)
    o_ref[0] = jnp.dot(a, w_ref[0].astype(BF16), preferred_element_type=F32) + b_ref[0]


def ada_modulation(cc, ada_w, ada_b):
    tn = 1024
    n6 = 6 * D_MODEL
    return pl.pallas_call(
        _ada_kernel,
        grid=(DEPTH, n6 // tn),
        in_specs=[
            pl.BlockSpec((SUBLANES, D_MODEL), lambda l, j: (0, 0)),
            pl.BlockSpec((1, D_MODEL, tn), lambda l, j: (l, 0, j)),
            pl.BlockSpec((1, 1, tn), lambda l, j: (l, 0, j)),
        ],
        out_specs=pl.BlockSpec((1, SUBLANES, tn), lambda l, j: (l, 0, j)),
        out_shape=jax.ShapeDtypeStruct((DEPTH, SUBLANES, n6), F32),
        compiler_params=_cparams(("arbitrary", "arbitrary"), 40),
        name="ada_modulation",
    )(cc, ada_w, ada_b.reshape(DEPTH, 1, n6))


def _inproj_kernel(x_ref, mod_ref, g_ref, w_ref, o_ref, h_ref):
    @pl.when(pl.program_id(1) == 0)
    def _():
        x = x_ref[...]
        ms = jnp.mean(x * x, axis=-1, keepdims=True)
        y = x * lax.rsqrt(ms + EPS) * g_ref[0]
        h = y * (1.0 + mod_ref[0, 1:2, :]) + mod_ref[0, 0:1, :]
        h_ref[...] = h.astype(BF16)

    o_ref[...] = jnp.dot(h_ref[...], w_ref[0], preferred_element_type=F32).astype(o_ref.dtype)


def in_projection(xa, mod, norm_g, w_in, layer, row_tile0, n_row_tiles, n_cols, tn):
    return pl.pallas_call(
        _inproj_kernel,
        grid=(n_row_tiles, n_cols // tn),
        in_specs=[
            pl.BlockSpec((TM, D_MODEL), lambda i, j: (i + row_tile0, 0)),
            pl.BlockSpec((1, 6, D_MODEL), lambda i, j: (_mod_id(i + row_tile0), 0, 0)),
            pl.BlockSpec((1, 1, D_MODEL), lambda i, j: (layer, 0, 0)),
            pl.BlockSpec((1, D_MODEL, tn), lambda i, j: (layer, 0, j)),
        ],
        out_specs=pl.BlockSpec((TM, tn), lambda i, j: (i, j)),
        out_shape=jax.ShapeDtypeStruct((n_row_tiles * TM, n_cols), BF16),
        scratch_shapes=[pltpu.VMEM((TM, D_MODEL), BF16)],
        compiler_params=_cparams(("arbitrary", "arbitrary"), 56),
        name="in_projection",
    )(xa, mod, norm_g.reshape(DEPTH, 1, D_MODEL), w_in)


def _rope_tables(n_ctx_rows):
    half = HEAD_DIM // 4
    inv_freq = ROPE_THETA ** (-jnp.arange(half, dtype=F32) / half)
    t = jnp.arange(SEQ)
    ang_r = (t // GRID_W).astype(F32)[:, None] * inv_freq[None, :]
    ang_c = (t % GRID_W).astype(F32)[:, None] * inv_freq[None, :]
    cos = jnp.concatenate([jnp.cos(ang_r), jnp.cos(ang_r), jnp.cos(ang_c), jnp.cos(ang_c)], axis=-1)
    sin = jnp.concatenate([-jnp.sin(ang_r), jnp.sin(ang_r), -jnp.sin(ang_c), jnp.sin(ang_c)], axis=-1)
    cos = jnp.concatenate([jnp.ones((n_ctx_rows, HEAD_DIM), F32), cos], axis=0)
    sin = jnp.concatenate([jnp.zeros((n_ctx_rows, HEAD_DIM), F32), sin], axis=0)
    return cos, sin


def _head_norm_rope(x, g, cos, sin):
    ms = jnp.mean(x * x, axis=-1, keepdims=True)
    y = x * lax.rsqrt(ms + EPS) * g
    lane = lax.broadcasted_iota(jnp.int32, y.shape, 1)
    first = (lane % (HEAD_DIM // 2)) < (HEAD_DIM // 4)
    partner = jnp.where(first, pltpu.roll(y, HEAD_DIM - HEAD_DIM // 4, 1), pltpu.roll(y, HEAD_DIM // 4, 1))
    return y * cos + partner * sin


def _attn_kernel(q_ref, k_ref, v_ref, cq_ref, sq_ref, ck_ref, sk_ref, gq_ref, gk_ref, o_ref, ks_ref, va_ref, *,
                 n_keys):
    @pl.when(pl.program_id(2) == 0)
    def _():
        def prep(c, carry):
            r0 = pl.multiple_of(c * KEY_PREP_T, KEY_PREP_T)
            rows = pl.ds(r0, KEY_PREP_T)
            kk = k_ref[0, rows, :].astype(F32)
            kn = _head_norm_rope(kk, gk_ref[0], ck_ref[rows, :], sk_ref[rows, :])
            ks_ref[rows, :] = kn.astype(BF16)
            return carry

        lax.fori_loop(0, n_keys // KEY_PREP_T, prep, 0)
        va_ref[:, 0:HEAD_DIM] = v_ref[0]
        va_ref[:, HEAD_DIM:2 * HEAD_DIM] = jnp.ones((n_keys, HEAD_DIM), BF16)

    k = ks_ref[...]
    va = va_ref[...]
    scale = HEAD_DIM ** -0.5 * math.log2(math.e)
    for hh in range(N_HEADS // N_KV_HEADS):
        cols = slice(hh * HEAD_DIM, (hh + 1) * HEAD_DIM)
        q = q_ref[:, cols].astype(F32)
        qn = _head_norm_rope(q, gq_ref[0], cq_ref[...], sq_ref[...]) * scale
        s = lax.dot_general(qn.astype(BF16), k, (((1,), (1,)), ((), ())), preferred_element_type=F32)
        m = jnp.max(s, axis=-1, keepdims=True)
        p = jnp.exp2(s - m)
        oa = jnp.dot(p.astype(BF16), va, preferred_element_type=F32)
        o = oa[:, 0:HEAD_DIM] / oa[:, HEAD_DIM:HEAD_DIM + 1]
        o_ref[:, cols] = o.astype(o_ref.dtype)


def attention(q_src, q_col0, q_row0, n_q, kv, cos_q, sin_q, cos_k, sin_k, q_norm_g, k_norm_g, layer):
    n_keys = kv.shape[1]
    grp_w = Q_W // N_KV_HEADS
    tq = min(TQ, n_q)
    qb = n_q // tq
    kern = functools.partial(_attn_kernel, n_keys=n_keys)
    return pl.pallas_call(
        kern,
        grid=(BATCH, N_KV_HEADS, qb),
        in_specs=[
            pl.BlockSpec((tq, grp_w), lambda b, g, i: (q_row0 // tq + b * qb + i, q_col0 // grp_w + g)),
            pl.BlockSpec((1, n_keys, HEAD_DIM), lambda b, g, i: (b, 0, g)),
            pl.BlockSpec((1, n_keys, HEAD_DIM), lambda b, g, i: (b, 0, N_KV_HEADS + g)),
            pl.BlockSpec((tq, HEAD_DIM), lambda b, g, i: (i, 0)),
            pl.BlockSpec((tq, HEAD_DIM), lambda b, g, i: (i, 0)),
            pl.BlockSpec((n_keys, HEAD_DIM), lambda b, g, i: (0, 0)),
            pl.BlockSpec((n_keys, HEAD_DIM), lambda b, g, i: (0, 0)),
            pl.BlockSpec((1, 1, HEAD_DIM), lambda b, g, i: (layer, 0, 0)),
            pl.BlockSpec((1, 1, HEAD_DIM), lambda b, g, i: (layer, 0, 0)),
        ],
        out_specs=pl.BlockSpec((tq, grp_w), lambda b, g, i: (b * qb + i, g)),
        out_shape=jax.ShapeDtypeStruct((BATCH * n_q, Q_W), BF16),
        scratch_shapes=[pltpu.VMEM((n_keys, HEAD_DIM), BF16), pltpu.VMEM((n_keys, 2 * HEAD_DIM), BF16)],
        compiler_params=_cparams(("arbitrary", "arbitrary", "arbitrary"), 56),
        name="attention",
    )(q_src, kv, kv, cos_q, sin_q, cos_k, sin_k,
      q_norm_g.reshape(DEPTH, 1, HEAD_DIM), k_norm_g.reshape(DEPTH, 1, HEAD_DIM))


def _cmul(ar, ai, br, bi):
    return ar * br - ai * bi, ar * bi + ai * br


SSM_CB = 512


def _ssm_kernel(u_ref, bm_ref, cm_ref, lam_ref, pw_ref, y_ref, bu_ref, yv_ref, carry_ref, *, reverse):
    @pl.when(pl.program_id(1) == 0)
    def _():
        carry_ref[...] = jnp.zeros_like(carry_ref)

    rr = lax.broadcasted_iota(jnp.int32, (SSM_T, SSM_T), 0)
    tt = lax.broadcasted_iota(jnp.int32, (SSM_T, SSM_T), 1)
    perm = jnp.where(tt == (rr % SUBLANES) * SSM_TC + rr // SUBLANES, 1.0, 0.0).astype(BF16)
    u_scan = jnp.dot(perm, u_ref[0], preferred_element_type=F32).astype(BF16)
    for h in range(2):
        bu_ref[h] = jnp.dot(u_scan[:, h * SSM_HALF:(h + 1) * SSM_HALF], bm_ref[h], preferred_element_type=F32)

    last_row = 0 if reverse else (SSM_T - SUBLANES)
    edge = (SUBLANES - 1) if reverse else 0
    toward = (lambda x, k: pltpu.roll(x, SUBLANES - k, 0)) if reverse else (lambda x, k: pltpu.roll(x, k, 0))

    def step_rows(j):
        jj = (SSM_TC - 1 - j) if reverse else j
        return pl.ds(pl.multiple_of(jj * SUBLANES, SUBLANES), SUBLANES)

    for h, cb in [(h, cb) for h in range(2) for cb in range(SSM_HSTATE // SSM_CB)]:
        re = slice(cb * SSM_CB, (cb + 1) * SSM_CB)
        im = slice(SSM_HSTATE + cb * SSM_CB, SSM_HSTATE + (cb + 1) * SSM_CB)
        lr = lam_ref[h, :, re]
        li = lam_ref[h, :, im]

        def local_scan(j, st, h=h, re=re, im=im, lr=lr, li=li):
            rows = step_rows(j)
            pr, pi = _cmul(lr, li, st[0], st[1])
            nr = pr + bu_ref[h, rows, re]
            ni = pi + bu_ref[h, rows, im]
            bu_ref[h, rows, re] = nr
            bu_ref[h, rows, im] = ni
            return nr, ni

        zero = jnp.zeros((SUBLANES, SSM_CB), F32)
        lax.fori_loop(0, SSM_TC, local_scan, (zero, zero), unroll=4)

        er = bu_ref[h, last_row:last_row + SUBLANES, re]
        ei = bu_ref[h, last_row:last_row + SUBLANES, im]
        row = lax.broadcasted_iota(jnp.int32, (SUBLANES, SSM_CB), 0)
        dist = (SUBLANES - 1 - row) if reverse else row
        xr = jnp.where(row == edge, carry_ref[h, :, re], toward(er, 1))
        xi = jnp.where(row == edge, carry_ref[h, :, im], toward(ei, 1))
        for lvl, k in enumerate((1, 2, 4)):
            ar = pw_ref[h, lvl, :, re]
            ai = pw_ref[h, lvl, :, im]
            sr = jnp.where(dist >= k, toward(xr, k), 0.0)
            si = jnp.where(dist >= k, toward(xi, k), 0.0)
            mr, mi = _cmul(ar, ai, sr, si)
            xr = xr + mr
            xi = xi + mi
        cr, ci = _cmul(pw_ref[h, 0, :, re], pw_ref[h, 0, :, im], xr, xi)
        carry_ref[h, :, re] = toward(cr + er, 1)
        carry_ref[h, :, im] = toward(ci + ei, 1)

        def add_carry(j, g, h=h, re=re, im=im, lr=lr, li=li):
            rows = step_rows(j)
            bu_ref[h, rows, re] = bu_ref[h, rows, re] + g[0]
            bu_ref[h, rows, im] = bu_ref[h, rows, im] + g[1]
            return _cmul(lr, li, g[0], g[1])

        lax.fori_loop(0, SSM_TC, add_carry, _cmul(lr, li, xr, xi), unroll=4)

    for h in range(2):
        y = jnp.dot(bu_ref[h].astype(BF16), cm_ref[h], preferred_element_type=F32)
        for cb in range(SSM_HALF // LANES):
            cols = slice(cb * LANES, (cb + 1) * LANES)
            out_cols = slice(h * SSM_HALF + cb * LANES, h * SSM_HALF + (cb + 1) * LANES)
            slab = h * (SSM_HALF // LANES) + cb
            yv_ref[slab] = y[:, cols]
            for s in range(SUBLANES):
                y_ref[0, s * SSM_TC:(s + 1) * SSM_TC, out_cols] = yv_ref[slab, pl.ds(s, SSM_TC, stride=SUBLANES), :]


def _ssm_params(a_re, a_im, log_dt, b_re, b_im, c_re, c_im):
    lam = lax.complex(a_re.astype(F32), a_im.astype(F32))
    dt = jnp.exp(log_dt.astype(F32))[..., None]
    lam_bar = jnp.exp(lam * dt)
    b_bar = ((lam_bar - 1.0) / lam)[..., None] * lax.complex(b_re.astype(F32), b_im.astype(F32))
    gh = SSM_GROUPS // 2
    eye = jnp.eye(gh, dtype=F32)

    def b_block(m):
        m = m.reshape(2, 2, gh, SSM_STATE, SSM_GROUP)
        return jnp.einsum('dhgpc,gk->dhgckp', m, eye).reshape(2, 2, gh * SSM_GROUP, gh * SSM_STATE)

    def c_block(m):
        m = m.reshape(2, 2, gh, SSM_GROUP, SSM_STATE)
        return jnp.einsum('dhgcp,gk->dhkpgc', m, eye).reshape(2, 2, gh * SSM_STATE, gh * SSM_GROUP)

    bm = jnp.concatenate([b_block(jnp.real(b_bar)), b_block(jnp.imag(b_bar))], axis=-1).astype(BF16)
    cm = jnp.concatenate([c_block(c_re.astype(F32)), -c_block(c_im.astype(F32))], axis=-2).astype(BF16)

    def table(z):
        z = z.reshape(2, 2, SSM_HSTATE)
        t = jnp.concatenate([jnp.real(z), jnp.imag(z)], axis=-1)
        return jnp.broadcast_to(t[:, :, None, :], (2, 2, SUBLANES, 2 * SSM_HSTATE))

    lam_t = table(lam_bar)
    pw_t = jnp.stack([table(jnp.exp(lam * dt * (SSM_TC * k))) for k in (1, 2, 4)], axis=2)
    return bm, cm, lam_t, pw_t


SSM_CHUNKS = S_ALL // SSM_T


def ssm_scan(u_chunks, bm, cm, lam_t, pw_t, reverse):
    hs2 = 2 * SSM_HSTATE

    def chunk(b, c):
        if reverse:
            c = jnp.where(c == 0, 0, SSM_CHUNKS - c)
        return b * SSM_CHUNKS + c

    kern = functools.partial(_ssm_kernel, reverse=reverse)
    return pl.pallas_call(
        kern,
        grid=(BATCH, SSM_CHUNKS),
        in_specs=[
            pl.BlockSpec((1, SSM_T, SSM_WIDTH), lambda b, c: (chunk(b, c), 0, 0)),
            pl.BlockSpec((2, SSM_HALF, hs2), lambda b, c: (0, 0, 0)),
            pl.BlockSpec((2, hs2, SSM_HALF), lambda b, c: (0, 0, 0)),
            pl.BlockSpec((2, SUBLANES, hs2), lambda b, c: (0, 0, 0)),
            pl.BlockSpec((2, 3, SUBLANES, hs2), lambda b, c: (0, 0, 0, 0)),
        ],
        out_specs=pl.BlockSpec((1, SSM_T, SSM_WIDTH), lambda b, c: (chunk(b, c), 0, 0)),
        out_shape=jax.ShapeDtypeStruct(u_chunks.shape, F32),
        scratch_shapes=[pltpu.VMEM((2, SSM_T, hs2), F32), pltpu.VMEM((SSM_WIDTH // LANES, SSM_T, LANES), F32),
                        pltpu.VMEM((2, SUBLANES, hs2), F32)],
        compiler_params=_cparams(("arbitrary",) * 2, 32),
        name="ssm_scan",
    )(u_chunks, bm, cm, lam_t, pw_t)


def _ssm_chunk_of_tile(i):
    lat_tiles = SEQ // SSM_T
    return jnp.where(i < BATCH * lat_tiles,
                     (i // lat_tiles) * SSM_CHUNKS + 1 + i % lat_tiles,
                     (i - BATCH * lat_tiles) * SSM_CHUNKS)


def _gelu_tanh(x):
    return 0.5 * x * (1.0 + jnp.tanh(math.sqrt(2.0 / math.pi) * (x + 0.044715 * (x * x * x))))


def _ssm_out_kernel(u_ref, yf_ref, yb_ref, d_ref, w_ref, b_ref, o_ref):
    y = d_ref[0] * u_ref[...].astype(F32) + yf_ref[0] + yb_ref[0]
    y = _gelu_tanh(y)
    z = jnp.dot(y.astype(BF16), w_ref[0], preferred_element_type=F32) + b_ref[0]
    o_ref[...] = (y * jax.nn.sigmoid(z)).astype(o_ref.dtype)


def ssm_output(p, yf, yb, ssm_d, glu_w, glu_b, layer):
    n = p.shape[0]
    yspec = pl.BlockSpec((1, SSM_T, SSM_WIDTH), lambda i: (_ssm_chunk_of_tile(i), 0, 0))
    return pl.pallas_call(
        _ssm_out_kernel,
        grid=(n // SSM_T,),
        in_specs=[
            pl.BlockSpec((SSM_T, SSM_WIDTH), lambda i: (i, COL_SSM // SSM_WIDTH)),
            yspec,
            yspec,
            pl.BlockSpec((1, 1, SSM_WIDTH), lambda i: (layer, 0, 0)),
            pl.BlockSpec((1, SSM_WIDTH, SSM_WIDTH), lambda i: (layer, 0, 0)),
            pl.BlockSpec((1, 1, SSM_WIDTH), lambda i: (layer, 0, 0)),
        ],
        out_specs=pl.BlockSpec((SSM_T, SSM_WIDTH), lambda i: (i, 0)),
        out_shape=jax.ShapeDtypeStruct((n, SSM_WIDTH), BF16),
        compiler_params=_cparams(("arbitrary",), 32),
        name="ssm_output",
    )(p, yf, yb, ssm_d.reshape(DEPTH, 1, SSM_WIDTH), glu_w, glu_b.reshape(DEPTH, 1, SSM_WIDTH))


def _seq_tile(tile):
    lat_tiles = N_LAT // SEQ_T
    per_seq = SEQ // SEQ_T
    is_lat = tile < lat_tiles
    t = jnp.where(is_lat, tile % per_seq, (tile - lat_tiles) % (CTX_LEN // SEQ_T))
    last_t = jnp.where(is_lat, per_seq - 1, CTX_LEN // SEQ_T - 1)
    return t * SEQ_T, t == 0, t == last_t, jnp.where(is_lat, SEQ, CTX_LEN)


def _with_halo(prev_ref, cur_ref, next_ref, first, last):
    top = jnp.where(first, jnp.zeros((SEQ_PAD, cur_ref.shape[1]), cur_ref.dtype), prev_ref[SEQ_T - SEQ_PAD:SEQ_T, :])
    bot = jnp.where(last, jnp.zeros((SEQ_PAD, cur_ref.shape[1]), cur_ref.dtype), next_ref[0:SEQ_PAD, :])
    return jnp.concatenate([top, cur_ref[...], bot], axis=0)


def _halo_specs(width, col_block, n_tiles):
    return [pl.BlockSpec((SEQ_T, width), lambda g: (jnp.maximum(g - 1, 0), col_block)),
            pl.BlockSpec((SEQ_T, width), lambda g: (g, col_block)),
            pl.BlockSpec((SEQ_T, width), lambda g: (jnp.minimum(g + 1, n_tiles - 1), col_block))]


def _pool_kernel(prev_ref, cur_ref, next_ref, w_ref, s_ref, o_ref):
    t0, first, last, seq_len = _seq_tile(pl.program_id(0))
    halo = _with_halo(prev_ref, cur_ref, next_ref, first, last)
    centre = cur_ref[...].astype(F32)
    tt = lax.broadcasted_iota(jnp.int32, (SEQ_T, SEQ_T + 2 * SEQ_PAD), 0)
    rr = lax.broadcasted_iota(jnp.int32, (SEQ_T, SEQ_T + 2 * SEQ_PAD), 1) - SEQ_PAD
    tg = t0 + lax.broadcasted_iota(jnp.int32, (SEQ_T, 1), 0)
    for gi, w in enumerate(POOL_WINDOWS):
        cols = slice(gi * POOL_GROUP, (gi + 1) * POOL_GROUP)
        band = ((rr >= tt - w // 2) & (rr <= tt + w // 2 - 1)).astype(F32).astype(BF16)
        wsum = jnp.dot(band, halo[:, cols], preferred_element_type=F32)
        cnt = jnp.minimum(tg - w // 2 + w, seq_len) - jnp.maximum(tg - w // 2, 0)
        pooled = wsum / cnt.astype(F32) - centre[:, cols]
        mixed = jnp.dot(pooled.astype(BF16), w_ref[0, gi], preferred_element_type=F32)
        o_ref[:, cols] = (mixed * s_ref[0, :, cols]).astype(o_ref.dtype)


def pool_mixer(p, pool_w, pool_scale, layer):
    n_tiles = p.shape[0] // SEQ_T
    ng = len(POOL_WINDOWS)
    return pl.pallas_call(
        _pool_kernel,
        grid=(n_tiles,),
        in_specs=_halo_specs(POOL_WIDTH, COL_POOL // POOL_WIDTH, n_tiles) + [
            pl.BlockSpec((1, ng, POOL_GROUP, POOL_GROUP), lambda g: (layer, 0, 0, 0)),
            pl.BlockSpec((1, 1, POOL_WIDTH), lambda g: (layer, 0, 0)),
        ],
        out_specs=pl.BlockSpec((SEQ_T, POOL_WIDTH), lambda g: (g, 0)),
        out_shape=jax.ShapeDtypeStruct((p.shape[0], POOL_WIDTH), BF16),
        compiler_params=_cparams(("arbitrary",), 32),
        name="pool_mixer",
    )(p, p, p, pool_w, pool_scale.reshape(DEPTH, 1, POOL_WIDTH))


CONV_RB = 64


def _conv_kernel(ap_ref, ac_ref, an_ref, gp_ref, gc_ref, gn_ref, w_ref, b_ref, g_ref, beta_ref, o_ref, glu_ref):
    _, first, last, _ = _seq_tile(pl.program_id(0))
    a = _with_halo(ap_ref, ac_ref, an_ref, first, last).astype(F32)
    g = _with_halo(gp_ref, gc_ref, gn_ref, first, last).astype(F32)
    glu_ref[0] = a * jax.nn.sigmoid(g)
    shifted_rows = SEQ_T + 2 * SEQ_PAD - SUBLANES
    for r in range(1, SUBLANES):
        glu_ref[r, 0:shifted_rows, :] = glu_ref[0, r:r + shifted_rows, :]
    off = SEQ_PAD - CONV_TAPS // 2
    for rb in range(SEQ_T // CONV_RB):
        parts = []
        for cb in range(CONV_WIDTH // LANES):
            cols = slice(cb * LANES, (cb + 1) * LANES)
            acc = jnp.zeros((CONV_RB, LANES), F32)
            for k in range(CONV_TAPS):
                r = (k + off) % SUBLANES
                row0 = rb * CONV_RB + (k + off) - r
                acc = acc + glu_ref[r, row0:row0 + CONV_RB, cols] * w_ref[0, k:k + 1, cols]
            parts.append(acc)
        y = jnp.concatenate(parts, axis=-1) + b_ref[0]
        yc = y - jnp.mean(y, axis=-1, keepdims=True)
        yn = yc * lax.rsqrt(jnp.mean(yc * yc, axis=-1, keepdims=True) + EPS)
        yn = yn * g_ref[0] + beta_ref[0]
        o_ref[rb * CONV_RB:(rb + 1) * CONV_RB, :] = _silu(yn).astype(o_ref.dtype)


def conv_mixer(p, dw_w, dw_b, ln_g, ln_b, layer):
    n_tiles = p.shape[0] // SEQ_T
    vec = lambda a: a.reshape(DEPTH, 1, CONV_WIDTH)
    vspec = pl.BlockSpec((1, 1, CONV_WIDTH), lambda g: (layer, 0, 0))
    a_block = COL_CONV // CONV_WIDTH
    return pl.pallas_call(
        _conv_kernel,
        grid=(n_tiles,),
        in_specs=_halo_specs(CONV_WIDTH, a_block, n_tiles) + _halo_specs(CONV_WIDTH, a_block + 1, n_tiles) + [
            pl.BlockSpec((1, CONV_TAPS, CONV_WIDTH), lambda g: (layer, 0, 0)),
            vspec, vspec, vspec,
        ],
        out_specs=pl.BlockSpec((SEQ_T, CONV_WIDTH), lambda g: (g, 0)),
        out_shape=jax.ShapeDtypeStruct((p.shape[0], CONV_WIDTH), BF16),
        scratch_shapes=[pltpu.VMEM((SUBLANES, SEQ_T + 2 * SEQ_PAD, CONV_WIDTH), F32)],
        compiler_params=_cparams(("arbitrary",), 40),
        name="conv_mixer",
    )(p, p, p, p, p, p, dw_w, vec(dw_b), vec(ln_g), vec(ln_b))


def _merge_kernel(oa_ref, ys_ref, yp_ref, yc_ref, g0_ref, g1_ref, g2_ref, g3_ref,
                  wa_ref, ws_ref, wp_ref, wc_ref, o_ref):
    def branch(x_ref, w_ref, g_ref):
        up = jnp.dot(x_ref[...], w_ref[0, 0], preferred_element_type=F32)
        return jax.nn.sigmoid(g_ref[...].astype(F32)) * up

    m = (branch(oa_ref, wa_ref, g0_ref) + branch(ys_ref, ws_ref, g1_ref)
         + branch(yp_ref, wp_ref, g2_ref) + branch(yc_ref, wc_ref, g3_ref))
    o_ref[...] = m.astype(o_ref.dtype)


def merge_branches(p, o_attn, y_ssm, y_pool, y_conv, w_attn, w_ssm, w_pool, w_conv, layer):
    n = o_attn.shape[0]
    nct = D_MODEL // TN
    gate_tile0 = COL_GATE // TN

    def gate_spec(br):
        return pl.BlockSpec((TM, TN), lambda i, j: (i, gate_tile0 + br * nct + j))

    def x_spec(width):
        return pl.BlockSpec((TM, width), lambda i, j: (i, 0))

    def w_spec(width):
        return pl.BlockSpec((1, 1, width, TN), lambda i, j: (layer, j, 0, 0))

    return pl.pallas_call(
        _merge_kernel,
        grid=(n // TM, nct),
        in_specs=[x_spec(Q_W), x_spec(SSM_WIDTH), x_spec(POOL_WIDTH), x_spec(CONV_WIDTH),
                  gate_spec(0), gate_spec(1), gate_spec(2), gate_spec(3),
                  w_spec(Q_W), w_spec(SSM_WIDTH), w_spec(POOL_WIDTH), w_spec(CONV_WIDTH)],
        out_specs=pl.BlockSpec((TM, TN), lambda i, j: (i, j)),
        out_shape=jax.ShapeDtypeStruct((n, D_MODEL), BF16),
        compiler_params=_cparams(("arbitrary", "arbitrary"), 40),
        name="merge_branches",
    )(o_attn, y_ssm, y_pool, y_conv, p, p, p, p, w_attn, w_ssm, w_pool, w_conv)


def _split_bf16(x):
    hi = x.astype(BF16)
    lo = (x - hi.astype(F32)).astype(BF16)
    return hi, lo


SLAB = D_MODEL // LANES


def _store_row_slabs(ref, row0, x):
    n = x.shape[0]
    for j in range(SLAB):
        ref[pl.ds(row0 * SLAB + j, n, stride=SLAB), :] = x[:, j * LANES:(j + 1) * LANES]


def _load_row_slabs(ref, row0, n):
    return jnp.concatenate([ref[pl.ds(row0 * SLAB + j, n, stride=SLAB), :] for j in range(SLAB)], axis=-1)


def _outproj_kernel(m_ref, x_ref, mod_ref, g_ref, w_ref, rw_ref, xo_ref, h_ref, lg_ref):
    r_hi, r_lo = _split_bf16(rw_ref[...])
    nt = (((1,), (1,)), ((), ()))
    half = m_ref.shape[0] // 2
    for c in range(2):
        rows = slice(c * half, (c + 1) * half)
        mix = jnp.dot(m_ref[rows, :], w_ref[0], preferred_element_type=F32)
        x = x_ref[rows, :] + mod_ref[0, 2:3, :] * mix
        xo_ref[rows, :] = x
        ms = jnp.mean(x * x, axis=-1, keepdims=True)
        h = x * lax.rsqrt(ms + EPS) * g_ref[0]
        h = h * (1.0 + mod_ref[0, 4:5, :]) + mod_ref[0, 3:4, :]
        _store_row_slabs(h_ref, c * half, h)
        h_hi, h_lo = _split_bf16(h)
        lg_ref[:, rows] = (lax.dot_general(r_hi, h_hi, nt, preferred_element_type=F32)
                           + lax.dot_general(r_hi, h_lo, nt, preferred_element_type=F32)
                           + lax.dot_general(r_lo, h_hi, nt, preferred_element_type=F32))


def out_projection(m, xa, mod, norm2_g, w_out, router_wt, layer):
    n = m.shape[0]
    tmo = TM
    return pl.pallas_call(
        _outproj_kernel,
        grid=(n // tmo,),
        in_specs=[
            pl.BlockSpec((tmo, D_MODEL), lambda i: (i, 0)),
            pl.BlockSpec((tmo, D_MODEL), lambda i: (i, 0)),
            pl.BlockSpec((1, 6, D_MODEL), lambda i: (_mod_id(i), 0, 0)),
            pl.BlockSpec((1, 1, D_MODEL), lambda i: (layer, 0, 0)),
            pl.BlockSpec((1, D_MODEL, D_MODEL), lambda i: (layer, 0, 0)),
            pl.BlockSpec((N_EXPERTS, D_MODEL), lambda i: (0, 0)),
        ],
        out_specs=[
            pl.BlockSpec((tmo, D_MODEL), lambda i: (i, 0)),
            pl.BlockSpec((tmo * SLAB, LANES), lambda i: (i, 0)),
            pl.BlockSpec((N_EXPERTS, tmo), lambda i: (0, i)),
        ],
        out_shape=[
            jax.ShapeDtypeStruct((n, D_MODEL), F32),
            jax.ShapeDtypeStruct((n * SLAB, LANES), F32),
            jax.ShapeDtypeStruct((N_EXPERTS, n), F32),
        ],
        compiler_params=_cparams(("arbitrary",), 60),
        name="out_projection",
    )(m, xa, mod, norm2_g.reshape(DEPTH, 1, D_MODEL), w_out, router_wt)


def _first_argmax(blk, row):
    m = jnp.max(blk, axis=0, keepdims=True)
    idx = jnp.min(jnp.where(blk == m, row, EXPERTS_PER_GROUP), axis=0, keepdims=True)
    return m, idx


def _route_kernel(lg_ref, rb_ref, e_ref, w_ref, rank_ref, cnt_ref, run_ref):
    @pl.when(pl.program_id(0) == 0)
    def _():
        run_ref[...] = jnp.zeros_like(run_ref)

    t = lg_ref.shape[1]
    scores = jax.nn.sigmoid(lg_ref[...])
    sel = scores + rb_ref[...]
    row = lax.broadcasted_iota(jnp.int32, (EXPERTS_PER_GROUP, t), 0)
    neg = jnp.float32(-jnp.inf)

    best = None
    for g in range(N_EXPERT_GROUPS):
        blk = sel[g * EXPERTS_PER_GROUP:(g + 1) * EXPERTS_PER_GROUP, :]
        m1, i1 = _first_argmax(blk, row)
        m2 = jnp.max(jnp.where(row == i1, neg, blk), axis=0, keepdims=True)
        gs = m1 + m2
        if best is None:
            best, grp = gs, jnp.zeros((1, t), jnp.int32)
        else:
            better = gs > best
            best = jnp.where(better, gs, best)
            grp = jnp.where(better, g, grp)

    in_sel = jnp.zeros((EXPERTS_PER_GROUP, t), F32)
    in_sc = jnp.zeros((EXPERTS_PER_GROUP, t), F32)
    for g in range(N_EXPERT_GROUPS):
        rows = slice(g * EXPERTS_PER_GROUP, (g + 1) * EXPERTS_PER_GROUP)
        in_sel = jnp.where(grp == g, sel[rows, :], in_sel)
        in_sc = jnp.where(grp == g, scores[rows, :], in_sc)
    _, i1 = _first_argmax(in_sel, row)
    _, i2 = _first_argmax(jnp.where(row == i1, neg, in_sel), row)
    s1 = jnp.sum(jnp.where(row == i1, in_sc, 0.0), axis=0, keepdims=True)
    s2 = jnp.sum(jnp.where(row == i2, in_sc, 0.0), axis=0, keepdims=True)
    e1 = grp * EXPERTS_PER_GROUP + i1
    e2 = grp * EXPERTS_PER_GROUP + i2
    e_ref[0:1, :] = e1
    e_ref[1:2, :] = e2
    w_ref[0:1, :] = s1 / (s1 + s2)
    w_ref[1:2, :] = s2 / (s1 + s2)

    erow = lax.broadcasted_iota(jnp.int32, (N_EXPERTS, t), 0)
    oh1 = erow == e1
    oh2 = erow == e2
    cnt = jnp.where(oh1 | oh2, 1.0, 0.0)
    a = lax.broadcasted_iota(jnp.int32, (t, t), 0)
    b = lax.broadcasted_iota(jnp.int32, (t, t), 1)
    before = jnp.where(a < b, 1.0, 0.0).astype(BF16)
    excl = jnp.dot(cnt.astype(BF16), before, preferred_element_type=F32) + run_ref[...]
    rank_ref[0:1, :] = jnp.sum(jnp.where(oh1, excl, 0.0), axis=0, keepdims=True).astype(jnp.int32)
    rank_ref[1:2, :] = jnp.sum(jnp.where(oh2, excl, 0.0), axis=0, keepdims=True).astype(jnp.int32)
    run_ref[...] = run_ref[...] + jnp.sum(cnt, axis=1, keepdims=True)
    cnt_ref[...] = run_ref[...]


def route(logits_t, router_b):
    n = logits_t.shape[1]
    return pl.pallas_call(
        _route_kernel,
        grid=(n // ROUTE_T,),
        in_specs=[
            pl.BlockSpec((N_EXPERTS, ROUTE_T), lambda i: (0, i)),
            pl.BlockSpec((N_EXPERTS, 1), lambda i: (0, 0)),
        ],
        out_specs=[
            pl.BlockSpec((TOP_K, ROUTE_T), lambda i: (0, i)),
            pl.BlockSpec((TOP_K, ROUTE_T), lambda i: (0, i)),
            pl.BlockSpec((TOP_K, ROUTE_T), lambda i: (0, i)),
            pl.BlockSpec((N_EXPERTS, 1), lambda i: (0, 0)),
        ],
        out_shape=[
            jax.ShapeDtypeStruct((TOP_K, n), jnp.int32),
            jax.ShapeDtypeStruct((TOP_K, n), F32),
            jax.ShapeDtypeStruct((TOP_K, n), jnp.int32),
            jax.ShapeDtypeStruct((N_EXPERTS, 1), F32),
        ],
        scratch_shapes=[pltpu.VMEM((N_EXPERTS, 1), F32)],
        compiler_params=_cparams(("arbitrary",), 32),
        name="route",
    )(logits_t, router_b.reshape(N_EXPERTS, 1).astype(F32))


def _expert_kernel(be_ref, slot_ref, nxt_ref, nu_ref, tok_ref, dst_ref, h_hbm, wg_hbm, wu_hbm, wd_hbm, o_hbm,
                   xbuf_ref, ybuf_ref, wgf_ref, wuf_ref, wdf_ref, wgb_ref, wub_ref, wdb_ref, sem, xsem, ysem,
                   *, layer):
    b = pl.program_id(0)
    n_used = nu_ref[0]
    e = be_ref[b]
    fresh = (b == 0) | (e != be_ref[jnp.maximum(b - 1, 0)])
    live = b < n_used
    has_next = b + 1 < n_used
    cur = b % 2

    def slab(row):
        return pl.ds(pl.multiple_of(row * SLAB, SLAB), SLAB)

    def row_gathers(block, buf):
        return [pltpu.make_async_copy(h_hbm.at[slab(tok_ref[block * MOE_BLOCK + r])],
                                      xbuf_ref.at[buf, slab(r)], xsem.at[buf]) for r in range(MOE_BLOCK)]

    def row_scatters(block, buf):
        return [pltpu.make_async_copy(ybuf_ref.at[buf, slab(r)],
                                      o_hbm.at[slab(dst_ref[block * MOE_BLOCK + r])], ysem.at[buf])
                for r in range(MOE_BLOCK)]

    @pl.when(b == 0)
    def _():
        for cp in row_gathers(0, 0):
            cp.start()

    def weight_copies(expert, slot):
        return (pltpu.make_async_copy(wg_hbm.at[layer, expert], wgf_ref.at[slot], sem.at[slot]),
                pltpu.make_async_copy(wu_hbm.at[layer, expert], wuf_ref.at[slot], sem.at[slot]),
                pltpu.make_async_copy(wd_hbm.at[layer, expert], wdf_ref.at[slot], sem.at[slot]))

    @pl.when(b == 0)
    def _():
        for cp in weight_copies(e, 0):
            cp.start(priority=WEIGHT_DMA_PRIORITY)

        @pl.when(nxt_ref[0] >= 0)
        def _():
            for cp in weight_copies(nxt_ref[0], 1):
                cp.start(priority=WEIGHT_DMA_PRIORITY)

    @pl.when(fresh & live)
    def _():
        slot = slot_ref[b]
        ahead = nxt_ref[pl.num_programs(0) + b]

        @pl.when(ahead >= 0)
        def _():
            ahead_slot = jnp.where(slot + 2 >= WEIGHT_SLOTS, slot + 2 - WEIGHT_SLOTS, slot + 2)
            for cp in weight_copies(ahead, ahead_slot):
                cp.start(priority=WEIGHT_DMA_PRIORITY)

        for cp in weight_copies(e, slot):
            cp.wait()
        wgb_ref[...] = wgf_ref[slot].astype(BF16)
        wub_ref[...] = wuf_ref[slot].astype(BF16)
        wdb_ref[...] = wdf_ref[slot].astype(BF16)

    @pl.when(live & (b >= 2))
    def _():
        for cp in row_scatters(b - 2, cur):
            cp.wait()

    def block(scatter_prev, gather_next):
        for cp in row_gathers(b, cur):
            cp.wait()
        if scatter_prev:
            for cp in row_scatters(b - 1, 1 - cur):
                cp.start(priority=WEIGHT_DMA_PRIORITY)
        if gather_next:
            for cp in row_gathers(b + 1, 1 - cur):
                cp.start()
        x = _load_row_slabs(xbuf_ref.at[cur], 0, MOE_BLOCK).astype(BF16)
        gate = jnp.dot(x, wgb_ref[...], preferred_element_type=F32)
        up = jnp.dot(x, wub_ref[...], preferred_element_type=F32)
        hid = (_silu(gate) * up).astype(BF16)
        _store_row_slabs(ybuf_ref.at[cur], 0, jnp.dot(hid, wdb_ref[...], preferred_element_type=F32))

    @pl.when(b == 0)
    def _():
        block(False, True)

    @pl.when(live & (b >= 1) & has_next)
    def _():
        block(True, True)

    @pl.when(live & (b >= 1) & jnp.logical_not(has_next))
    def _():
        block(True, False)

    @pl.when(b == n_used)
    def _():
        last = n_used - 1
        for cp in row_scatters(last, last % 2):
            cp.start(priority=WEIGHT_DMA_PRIORITY)
        for cp in row_scatters(last - 1, 1 - last % 2):
            cp.wait()
        for cp in row_scatters(last, last % 2):
            cp.wait()

    @pl.when(jnp.logical_not(live))
    def _():
        ybuf_ref[0] = jnp.zeros((MOE_BLOCK * SLAB, LANES), F32)
        rows = pl.ds(pl.multiple_of(b * (MOE_BLOCK * SLAB), MOE_BLOCK * SLAB), MOE_BLOCK * SLAB)
        cp = pltpu.make_async_copy(ybuf_ref.at[0], o_hbm.at[rows], ysem.at[0])
        cp.start()
        cp.wait()


def _slot_maps(dest_flat, n, n_slots):
    i32 = jnp.int32
    owner = jnp.full((n_slots,), -1, i32).at[dest_flat].set(jnp.arange(TOP_K * n, dtype=i32), unique_indices=True)
    is_pad = owner < 0
    tok = jnp.where(is_pad, 0, owner % n)
    dst = jnp.where(is_pad, TOP_K * n + jnp.cumsum(is_pad.astype(i32)) - 1, owner)
    return tok.astype(i32), dst.astype(i32)


def experts(blk_e, blk_slot, blk_next, n_used, slot_tok, slot_dst, h2, w_gate, w_up, w_down, layer):
    n = h2.shape[0] // SLAB
    n_slots = slot_tok.shape[0]
    n_blocks = n_slots // MOE_BLOCK
    assert n_blocks - N_EXPERTS >= 2
    assert n_slots == TOP_K * n + N_EXPERTS * MOE_BLOCK
    grid_spec = pltpu.PrefetchScalarGridSpec(
        num_scalar_prefetch=6,
        grid=(n_blocks,),
        in_specs=[
            pl.BlockSpec(memory_space=pl.ANY),
            pl.BlockSpec(memory_space=pl.ANY),
            pl.BlockSpec(memory_space=pl.ANY),
            pl.BlockSpec(memory_space=pl.ANY),
        ],
        out_specs=pl.BlockSpec(memory_space=pl.ANY),
        scratch_shapes=[
            pltpu.VMEM((2, MOE_BLOCK * SLAB, LANES), F32),
            pltpu.VMEM((2, MOE_BLOCK * SLAB, LANES), F32),
            pltpu.VMEM((WEIGHT_SLOTS, D_MODEL, EXPERT_FF), F32),
            pltpu.VMEM((WEIGHT_SLOTS, D_MODEL, EXPERT_FF), F32),
            pltpu.VMEM((WEIGHT_SLOTS, EXPERT_FF, D_MODEL), F32),
            pltpu.VMEM((D_MODEL, EXPERT_FF), BF16),
            pltpu.VMEM((D_MODEL, EXPERT_FF), BF16),
            pltpu.VMEM((EXPERT_FF, D_MODEL), BF16),
            pltpu.SemaphoreType.DMA((WEIGHT_SLOTS,)),
            pltpu.SemaphoreType.DMA((2,)),
            pltpu.SemaphoreType.DMA((2,)),
        ],
    )
    return pl.pallas_call(
        functools.partial(_expert_kernel, layer=layer),
        grid_spec=grid_spec,
        out_shape=jax.ShapeDtypeStruct((n_slots * SLAB, LANES), F32),
        compiler_params=_cparams(("arbitrary",), 60),
        name="experts",
    )(blk_e, blk_slot, blk_next, n_used, slot_tok, slot_dst, h2, w_gate, w_up, w_down)


def _combine_kernel(x_ref, mod_ref, w_ref, fg_ref, y0_ref, y1_ref, o_ref, *, final_norm):
    f = w_ref[:, 0:1] * _load_row_slabs(y0_ref, 0, TM) + w_ref[:, 1:2] * _load_row_slabs(y1_ref, 0, TM)
    x = x_ref[...] + mod_ref[0, 5:6, :] * f
    if final_norm:
        ms = jnp.mean(x * x, axis=-1, keepdims=True)
        x = x * lax.rsqrt(ms + EPS) * fg_ref[...]
    o_ref[...] = x


def combine(xa, mod, wts_rows, ys, final_g, final_norm):
    n = xa.shape[0]
    n_tiles = n // TM
    kern = functools.partial(_combine_kernel, final_norm=final_norm)
    return pl.pallas_call(
        kern,
        grid=(n_tiles,),
        in_specs=[
            pl.BlockSpec((TM, D_MODEL), lambda i: (i, 0)),
            pl.BlockSpec((1, 6, D_MODEL), lambda i: (_mod_id(i), 0, 0)),
            pl.BlockSpec((TM, TOP_K), lambda i: (i, 0)),
            pl.BlockSpec((1, D_MODEL), lambda i: (0, 0)),
            pl.BlockSpec((TM * SLAB, LANES), lambda i: (i, 0)),
            pl.BlockSpec((TM * SLAB, LANES), lambda i: (n_tiles + i, 0)),
        ],
        out_specs=pl.BlockSpec((TM, D_MODEL), lambda i: (i, 0)),
        out_shape=jax.ShapeDtypeStruct((n, D_MODEL), F32),
        compiler_params=_cparams(("arbitrary",), 48),
        name="combine",
    )(xa, mod, wts_rows, final_g.reshape(1, D_MODEL), ys, ys)


def _slot_plan(e_idx, rank, counts, n_blocks):
    i32 = jnp.int32
    cnt = counts.reshape(N_EXPERTS).astype(i32)
    padded = (cnt + MOE_BLOCK - 1) // MOE_BLOCK * MOE_BLOCK
    pad_end = jnp.cumsum(padded)
    pad_start = pad_end - padded
    eids = jnp.arange(N_EXPERTS, dtype=i32)

    def lookup(table, idx):
        return jnp.sum(jnp.where(idx[..., None] == eids, table, 0), axis=-1).astype(i32)

    dest = lookup(pad_start, e_idx) + rank
    blk_start = jnp.arange(n_blocks, dtype=i32) * MOE_BLOCK
    blk_e = jnp.minimum(jnp.sum(pad_end[None, :] <= blk_start[:, None], axis=1), N_EXPERTS - 1).astype(i32)
    n_used = (pad_end[-1] // MOE_BLOCK).astype(i32).reshape(1)
    has = cnt > 0
    run = jnp.cumsum(has.astype(i32)) - 1
    first_from = lax.cummin(jnp.where(has, eids, N_EXPERTS)[::-1])[::-1]
    nxt = jnp.concatenate([first_from[1:], jnp.full((1,), N_EXPERTS, i32)])
    nxt = jnp.where(nxt >= N_EXPERTS, -1, nxt)
    nxt2 = jnp.where(nxt >= 0, lookup(nxt, jnp.maximum(nxt, 0)), -1)
    blk_ahead = jnp.stack([lookup(nxt, blk_e), lookup(nxt2, blk_e)], axis=0).reshape(-1)
    return dest.reshape(-1).astype(i32), blk_e, lookup(run % WEIGHT_SLOTS, blk_e), blk_ahead, n_used


def _col_tiles(w):
    d, k, n = w.shape
    return w.astype(BF16).reshape(d, k, n // TN, TN).transpose(0, 2, 1, 3)


def _layer(layer, xa, mod, W, with_ctx, final_norm):
    n_lat_tiles = N_LAT // TM
    n_ctx_tiles = N_CTX // TM
    if with_ctx:
        p = in_projection(xa, mod, W['norm1_g'], W['w_in'], layer, 0, n_lat_tiles + n_ctx_tiles, IN_W, TN_IN)
        p_lat, p_ctx = p[:N_LAT], p[N_LAT:]
    else:
        p = in_projection(xa, mod, W['norm1_g'], W['w_in'], layer, 0, n_lat_tiles, IN_W, TN_IN)
        p_lat = p
        p_ctx = in_projection(xa, mod, W['norm1_g'], W['w_in'], layer, n_lat_tiles, n_ctx_tiles, CTX_STATE_W, TN)
    n_rows = p.shape[0]

    def seq(a, lo, hi, n_seq_rows):
        return a[:, lo:hi].reshape(BATCH, n_seq_rows, hi - lo)

    kv = jnp.concatenate([seq(p_ctx, 0, 2 * KV_W, CTX_LEN), seq(p_lat, 0, 2 * KV_W, SEQ)], axis=1)
    cos, sin = _rope_tables(CTX_LEN)
    o_attn = attention(p, COL_Q, 0, SEQ, kv, cos[CTX_LEN:], sin[CTX_LEN:], cos, sin,
                       W['q_norm_g'], W['k_norm_g'], layer)
    if with_ctx:
        kv_c = seq(p_ctx, 0, 2 * KV_W, CTX_LEN)
        ones, zeros = jnp.ones((CTX_LEN, HEAD_DIM), F32), jnp.zeros((CTX_LEN, HEAD_DIM), F32)
        o_attn_c = attention(p, COL_Q, N_LAT, CTX_LEN, kv_c, ones, zeros, ones, zeros,
                             W['q_norm_g'], W['k_norm_g'], layer)
        o_attn = jnp.concatenate([o_attn, o_attn_c], axis=0)

    u_seq = jnp.concatenate([seq(p_ctx, COL_SSM, COL_SSM + SSM_WIDTH, CTX_LEN),
                             seq(p_lat, COL_SSM, COL_SSM + SSM_WIDTH, SEQ)], axis=1)
    u_chunks = u_seq.reshape(BATCH * SSM_CHUNKS, SSM_T, SSM_WIDTH)
    bm, cm, lam_t, pw_t = _ssm_params(W['ssm_a_re'][layer], W['ssm_a_im'][layer], W['ssm_log_dt'][layer],
                                      W['ssm_b_re'][layer], W['ssm_b_im'][layer],
                                      W['ssm_c_re'][layer], W['ssm_c_im'][layer])
    yf = ssm_scan(u_chunks, bm[0], cm[0], lam_t[0], pw_t[0], reverse=False)
    yb = ssm_scan(u_chunks, bm[1], cm[1], lam_t[1], pw_t[1], reverse=True)
    y_ssm = ssm_output(p, yf, yb, W['ssm_d'], W['ssm_glu_w'], W['ssm_glu_b'], layer)

    y_pool = pool_mixer(p, W['pool_w'], W['pool_scale'], layer)
    y_conv = conv_mixer(p, W['conv_dw_w'], W['conv_dw_b'], W['conv_ln_g'], W['conv_ln_b'], layer)

    m = merge_branches(p, o_attn, y_ssm, y_pool, y_conv,
                       W['w_up_attn'], W['w_up_ssm'], W['w_up_pool'], W['w_up_conv'], layer)
    x1, h2, logits_t = out_projection(m, xa, mod, W['norm2_g'], W['w_out'], W['router_wt'], layer)

    e_idx, wts, rank, counts = route(logits_t, W['router_b'])
    n_blocks = -(-n_rows * TOP_K // MOE_BLOCK) + N_EXPERTS
    dest_flat, blk_e, blk_slot, blk_next, n_used = _slot_plan(e_idx, rank, counts, n_blocks)
    slot_tok, slot_dst = _slot_maps(dest_flat, n_rows, n_blocks * MOE_BLOCK)
    ys = experts(blk_e, blk_slot, blk_next, n_used, slot_tok, slot_dst, h2,
                 W['moe_w_gate'], W['moe_w_up'], W['moe_w_down'], layer)
    x2 = combine(x1, mod, wts.T, ys, W['final_g'], final_norm)
    if with_ctx:
        return x2
    return jnp.concatenate([x2, xa[N_LAT:]], axis=0)


def kernel(x, c, ctx, c_ctx, ada_w, ada_b, norm1_g, norm2_g, w_in, q_norm_g, k_norm_g, ssm_a_re, ssm_a_im, ssm_log_dt, ssm_b_re, ssm_b_im, ssm_c_re, ssm_c_im, ssm_d, ssm_glu_w, ssm_glu_b, pool_w, pool_scale, conv_dw_w, conv_dw_b, conv_ln_g, conv_ln_b, w_up_attn, w_up_ssm, w_up_pool, w_up_conv, w_out, router_w, router_b, moe_w_gate, moe_w_up, moe_w_down, final_g):
    W = dict(
        norm1_g=norm1_g, norm2_g=norm2_g, q_norm_g=q_norm_g, k_norm_g=k_norm_g,
        w_in=w_in.astype(BF16),
        ssm_a_re=ssm_a_re, ssm_a_im=ssm_a_im, ssm_log_dt=ssm_log_dt, ssm_b_re=ssm_b_re, ssm_b_im=ssm_b_im,
        ssm_c_re=ssm_c_re, ssm_c_im=ssm_c_im, ssm_d=ssm_d, ssm_glu_w=ssm_glu_w.astype(BF16), ssm_glu_b=ssm_glu_b,
        pool_w=pool_w.astype(BF16), pool_scale=pool_scale,
        conv_dw_w=conv_dw_w, conv_dw_b=conv_dw_b, conv_ln_g=conv_ln_g, conv_ln_b=conv_ln_b,
        w_up_attn=_col_tiles(w_up_attn), w_up_ssm=_col_tiles(w_up_ssm), w_up_pool=_col_tiles(w_up_pool),
        w_up_conv=_col_tiles(w_up_conv), w_out=w_out.astype(BF16),
        router_wt=router_w.T, router_b=router_b,
        moe_w_gate=moe_w_gate, moe_w_up=moe_w_up, moe_w_down=moe_w_down, final_g=final_g,
    )
    cc = jnp.concatenate([c, c_ctx[None, :], jnp.zeros((SUBLANES - BATCH - 1, D_MODEL), F32)], axis=0)
    mod_all = ada_modulation(cc, ada_w, ada_b)
    xa = jnp.concatenate([x.reshape(N_LAT, D_MODEL), ctx.reshape(N_CTX, D_MODEL)], axis=0)
    for layer in range(DEPTH):
        mod = mod_all[layer, :BATCH + 1].reshape(BATCH + 1, 6, D_MODEL)
        xa = _layer(layer, xa, mod, W, with_ctx=(layer < DEPTH - 1), final_norm=(layer == DEPTH - 1))
    return xa[:N_LAT].reshape(BATCH, SEQ, D_MODEL)
```
